```python
import math
import jax
import jax.numpy as jnp
from jax import lax
import numpy as np

D_MODEL = 1024
BATCH = 4
SEQ = 4096
DEPTH = 4
DEC_BATCH = 128
DEC_SEQ = 8
PAST_LEN = 2048
PAGE_SIZE = 128

N_MIXERS = 3
RMS_EPS = 1e-6
SSD_EXPAND = 2
SSD_D_INNER = SSD_EXPAND * D_MODEL
SSD_HEAD_DIM = 64
SSD_HEADS = SSD_D_INNER // SSD_HEAD_DIM
SSD_GROUPS = 4
SSD_HPG = SSD_HEADS // SSD_GROUPS
SSD_STATE = 128
SSD_CONV = 4
SSD_CONV_CH = SSD_D_INNER + 2 * SSD_GROUPS * SSD_STATE
SSD_IN = 2 * SSD_D_INNER + 2 * SSD_GROUPS * SSD_STATE + SSD_HEADS
SSD_CHUNK = 128
FOX_HEAD_DIM = 64
FOX_HEADS = D_MODEL // FOX_HEAD_DIM
FOX_INNER = FOX_HEADS * FOX_HEAD_DIM
FOX_BLOCK = 128
RWKV_HEAD = 64
RWKV_HEADS = D_MODEL // RWKV_HEAD
RWKV_DECAY_LORA = 64
RWKV_AAA_LORA = 64
RWKV_GATE_LORA = 128
RWKV_GN_EPS = 64e-5
D_FF = 2816
FFN_CONV = 3
N_SSD = len(range(0, DEPTH, N_MIXERS))
N_FOX = len(range(1, DEPTH, N_MIXERS))
N_RWKV = len(range(2, DEPTH, N_MIXERS))

kernel_name = "hybrid_ssd_fox_rwkv7_convglu_adaln_step"

F32 = jnp.float32


def rmsnorm(x, g):
    xf = x.astype(F32)
    y = xf * lax.rsqrt(jnp.mean(xf * xf, axis=-1, keepdims=True) + RMS_EPS)
    return (y * g.astype(F32)).astype(x.dtype)


def ada_mod(c, w, b):
    return jnp.split(jax.nn.silu(c) @ w + b, 6, axis=-1)


def modulate(h, shift, scale):
    return h * (1 + scale[:, None, :]) + shift[:, None, :]


def causal_dwconv(x_ext, w, b):
    width = w.shape[0]
    T = x_ext.shape[1] - (width - 1)
    out = x_ext[:, width - 1:width - 1 + T] * w[width - 1]
    for k in range(width - 1):
        out = out + x_ext[:, k:k + T] * w[k]
    return out + b


def segsum(a):
    T = a.shape[-1]
    rep = jnp.broadcast_to(a[..., :, None], a.shape + (T,))
    strict = jnp.tril(jnp.ones((T, T), dtype=bool), -1)
    cs = jnp.cumsum(jnp.where(strict, rep, 0.0), axis=-2)
    return jnp.where(jnp.tril(jnp.ones((T, T), dtype=bool)), cs, -jnp.inf)


def ssd_chunked(xh, dt, A, Bm, Cm, h0):
    b, T = xh.shape[:2]
    Lc = min(SSD_CHUNK, T)
    nc = T // Lc
    X = (xh * dt[..., None]).reshape(b, nc, Lc, SSD_GROUPS, SSD_HPG, SSD_HEAD_DIM)
    a = jnp.moveaxis((dt * A).reshape(b, nc, Lc, SSD_GROUPS, SSD_HPG), 2, -1)
    a_cs = jnp.cumsum(a, axis=-1)
    Bc = Bm.reshape(b, nc, Lc, SSD_GROUPS, SSD_STATE)
    Cc = Cm.reshape(b, nc, Lc, SSD_GROUPS, SSD_STATE)
    decay_in = jnp.exp(segsum(a))
    CB = jnp.einsum('bclgn,bcsgn->bcgls', Cc, Bc)
    y_diag = jnp.einsum('bcgjls,bcsgjp->bclgjp', CB[:, :, :, None] * decay_in, X)
    decay_to_end = jnp.moveaxis(jnp.exp(a_cs[..., -1:] - a_cs), -1, 2)
    chunk_states = jnp.einsum('bclgn,bclgjp->bcgjpn', Bc, X * decay_to_end[..., None])
    chunk_decay = jnp.exp(a_cs[..., -1])

    def carry_step(h, inp):
        dec, st = inp
        return h * dec[..., None, None] + st, h

    h_last, h_in = lax.scan(carry_step, h0,
                            (jnp.moveaxis(chunk_decay, 1, 0), jnp.moveaxis(chunk_states, 1, 0)))
    h_in = jnp.moveaxis(h_in, 0, 1)
    decay_from_start = jnp.moveaxis(jnp.exp(a_cs), -1, 2)
    y_off = jnp.einsum('bclgn,bcgjpn->bclgjp', Cc, h_in) * decay_from_start[..., None]
    return (y_diag + y_off).reshape(b, T, SSD_GROUPS, SSD_HPG, SSD_HEAD_DIM), h_last


def ssd_mixer(h, conv_state, ssm_state, in_w, conv_w, conv_b, dt_bias, a_log, d_skip, norm_w, out_w):
    b, T, _ = h.shape
    z, xbc, dt = jnp.split(h @ in_w, [SSD_D_INNER, SSD_D_INNER + SSD_CONV_CH], axis=-1)
    xbc_ext = jnp.concatenate([conv_state.astype(xbc.dtype), xbc], axis=1)
    new_conv = xbc_ext[:, -(SSD_CONV - 1):]
    xbc = jax.nn.silu(causal_dwconv(xbc_ext, conv_w, conv_b))
    xs, Bm, Cm = jnp.split(xbc, [SSD_D_INNER, SSD_D_INNER + SSD_GROUPS * SSD_STATE], axis=-1)
    xs = xs.reshape(b, T, SSD_GROUPS, SSD_HPG, SSD_HEAD_DIM).astype(F32)
    Bm = Bm.reshape(b, T, SSD_GROUPS, SSD_STATE).astype(F32)
    Cm = Cm.reshape(b, T, SSD_GROUPS, SSD_STATE).astype(F32)
    dt = jax.nn.softplus(dt.astype(F32) + dt_bias.astype(F32)).reshape(b, T, SSD_GROUPS, SSD_HPG)
    A = -jnp.exp(a_log.astype(F32)).reshape(SSD_GROUPS, SSD_HPG)
    h0 = ssm_state.astype(F32).reshape(b, SSD_GROUPS, SSD_HPG, SSD_HEAD_DIM, SSD_STATE)
    y, h_last = ssd_chunked(xs, dt, A, Bm, Cm, h0)
    y = y + d_skip.astype(F32).reshape(SSD_GROUPS, SSD_HPG, 1) * xs
    yg = (y.reshape(b, T, SSD_D_INNER) * jax.nn.silu(z.astype(F32))).reshape(b, T, SSD_GROUPS, -1)
    yg = yg * lax.rsqrt(jnp.mean(yg * yg, axis=-1, keepdims=True) + RMS_EPS)
    y = (yg.reshape(b, T, SSD_D_INNER) * norm_w.astype(F32)).astype(h.dtype)
    new_ssm = h_last.reshape(b, SSD_HEADS, SSD_HEAD_DIM, SSD_STATE).astype(h.dtype)
    return y @ out_w, new_conv, new_ssm


def fox_project(h, in_w, f_b):
    b, T, _ = h.shape
    q, k, v, f = jnp.split(h @ in_w, [FOX_INNER, 2 * FOX_INNER, 3 * FOX_INNER], axis=-1)
    q = q.reshape(b, T, FOX_HEADS, FOX_HEAD_DIM)
    k = k.reshape(b, T, FOX_HEADS, FOX_HEAD_DIM)
    v = v.reshape(b, T, FOX_HEADS, FOX_HEAD_DIM)
    logf = jax.nn.log_sigmoid(f.astype(F32) + f_b.astype(F32))
    return q, k, v, logf


def fox_prompt(h, in_w, f_b, out_w):
    b, T, _ = h.shape
    q, k, v, logf = fox_project(h, in_w, f_b)
    cum = jnp.moveaxis(jnp.cumsum(logf, axis=1), 1, 2)
    kpos = jnp.arange(T)
    qoff = jnp.arange(FOX_BLOCK)
    scale = FOX_HEAD_DIM ** -0.5

    def one_block(start):
        qb = lax.dynamic_slice_in_dim(q, start, FOX_BLOCK, axis=1)
        cq = lax.dynamic_slice_in_dim(cum, start, FOX_BLOCK, axis=2)
        s = jnp.einsum('bqhd,bkhd->bhqk', qb, k).astype(F32) * scale
        s = s + cq[..., :, None] - cum[..., None, :]
        s = jnp.where(kpos[None, :] <= (start + qoff)[:, None], s, -jnp.inf)
        p = jax.nn.softmax(s, axis=-1).astype(v.dtype)
        return jnp.einsum('bhqk,bkhd->bqhd', p, v)

    o = lax.map(one_block, jnp.arange(T // FOX_BLOCK) * FOX_BLOCK)
    o = jnp.moveaxis(o, 0, 1).reshape(b, T, FOX_INNER)
    return o @ out_w, k, v, logf.astype(h.dtype)


def fox_sample(h, cache_k, cache_v, cache_logf, page_table, in_w, f_b, out_w):
    b, T, _ = h.shape
    q, k, v, logf = fox_project(h, in_w, f_b)
    past = page_table.shape[1] * cache_k.shape[1]
    k_past = cache_k[page_table].reshape(b, past, FOX_HEADS, FOX_HEAD_DIM)
    v_past = cache_v[page_table].reshape(b, past, FOX_HEADS, FOX_HEAD_DIM)
    cum_past = jnp.cumsum(cache_logf[page_table].reshape(b, past, FOX_HEADS).astype(F32), axis=1)
    cum_new = jnp.moveaxis(jnp.cumsum(logf, axis=1), 1, 2)
    bias_past = jnp.moveaxis(cum_past[:, -1:] - cum_past, 1, 2)
    scale = FOX_HEAD_DIM ** -0.5
    s_past = (jnp.einsum('bqhd,bkhd->bhqk', q, k_past).astype(F32) * scale
              + cum_new[..., :, None] + bias_past[..., None, :])
    s_new = (jnp.einsum('bqhd,bkhd->bhqk', q, k).astype(F32) * scale
             + cum_new[..., :, None] - cum_new[..., None, :])
    s_new = jnp.where(jnp.tril(jnp.ones((T, T), dtype=bool)), s_new, -jnp.inf)
    p = jax.nn.softmax(jnp.concatenate([s_past, s_new], axis=-1), axis=-1).astype(v.dtype)
    o = (jnp.einsum('bhqk,bkhd->bqhd', p[..., :past], v_past)
         + jnp.einsum('bhqk,bkhd->bqhd', p[..., past:], v))
    return o.reshape(b, T, FOX_INNER) @ out_w, k, v, logf.astype(h.dtype)


def wkv_scan(r, w, k, v, a, bvec, S0):
    def step(S, inp):
        r_t, w_t, k_t, v_t, a_t, b_t = inp
        sa = jnp.einsum('bhij,bhj->bhi', S, a_t)
        S = S * w_t[:, :, None, :] + sa[..., None] * b_t[:, :, None, :] + v_t[..., None] * k_t[:, :, None, :]
        return S, jnp.einsum('bhij,bhj->bhi', S, r_t)

    xs = tuple(jnp.moveaxis(z, 1, 0) for z in (r, w, k, v, a, bvec))
    S, ys = lax.scan(step, S0, xs)
    return S, jnp.moveaxis(ys, 0, 1)


def rwkv7_mixer(h, shift_state, wkv_state, mu, w_rkv, w0, w1, w2, a0, a1, a2, g1, g2,
                k_k, k_a, r_k, ln_w, ln_b, out_w):
    b, T, D = h.shape
    heads = lambda z: z.reshape(b, T, RWKV_HEADS, RWKV_HEAD).astype(F32)
    prev = jnp.concatenate([shift_state[:, None].astype(h.dtype), h[:, :-1]], axis=1)
    xmix = h[None] + (prev - h)[None] * mu[:, None, None, :]
    r, k, v = jnp.einsum('sbtd,sde->sbte', xmix[:3], w_rkv)
    w_log = -jax.nn.softplus(-(w0 + jnp.tanh(xmix[3] @ w1) @ w2).astype(F32)) - 0.5
    decay = jnp.exp(-jnp.exp(w_log))
    a = jax.nn.sigmoid((a0 + (xmix[4] @ a1) @ a2).astype(F32))
    g = jax.nn.sigmoid(xmix[5] @ g1) @ g2
    kk = heads(k * k_k)
    kk = kk / jnp.maximum(jnp.sqrt(jnp.sum(kk * kk, axis=-1, keepdims=True)), 1e-12)
    k = k.astype(F32) * (1 + (a - 1) * k_a.astype(F32))
    rh, kh, vh, wh, ah = heads(r), heads(k), heads(v), heads(decay), heads(a)
    S, y = wkv_scan(rh, wh, kh, vh, -kk, kk * ah, wkv_state.astype(F32))
    mean = jnp.mean(y, axis=-1, keepdims=True)
    var = jnp.mean(jnp.square(y - mean), axis=-1, keepdims=True)
    yn = ((y - mean) * lax.rsqrt(var + RWKV_GN_EPS)).reshape(b, T, D) * ln_w.astype(F32) + ln_b.astype(F32)
    bonus = jnp.sum(rh * kh * r_k.astype(F32), axis=-1, keepdims=True) * vh
    out = ((yn + bonus.reshape(b, T, D)) * g.astype(F32)).astype(h.dtype)
    return out @ out_w, h[:, -1], S.astype(h.dtype)


def conv_ffn(h, conv_state, up_w, conv_w, conv_b, down_w):
    gate, up = jnp.split(h @ up_w, 2, axis=-1)
    g_ext = jnp.concatenate([conv_state.astype(gate.dtype), gate], axis=1)
    new_state = g_ext[:, -(FFN_CONV - 1):]
    gate = causal_dwconv(g_ext, conv_w, conv_b)
    return (jax.nn.silu(gate) * up) @ down_w, new_state


def setup_inputs(seed: int = 0) -> dict:
    key = jax.random.key(seed)
    keys = iter(jax.random.split(key, 80))

    def nrm(shape, scale=1.0):
        return jax.random.normal(next(keys), shape, F32) * scale

    def unif(shape, lo, hi):
        return jax.random.uniform(next(keys), shape, F32, lo, hi)

    n_pages = PAST_LEN // PAGE_SIZE
    n_used = DEC_BATCH * n_pages
    n_pool = n_used + max(1, n_used // 4)
    page_table = jax.random.permutation(next(keys), n_pool)[:n_used].reshape(DEC_BATCH, n_pages).astype(jnp.int32)
    dt_init = jnp.exp(unif((N_SSD, SSD_HEADS), math.log(1e-3), math.log(1e-1)))
    dt_bias = dt_init + jnp.log(-jnp.expm1(-dt_init))
    d_in = D_MODEL ** -0.5
    return {
        "x_prompt": nrm((BATCH, SEQ, D_MODEL)),
        "x_sample": nrm((DEC_BATCH, DEC_SEQ, D_MODEL)),
        "c_prompt": nrm((BATCH, D_MODEL)),
        "c_sample": nrm((DEC_BATCH, D_MODEL)),
        "state_ssm": nrm((N_SSD, DEC_BATCH, SSD_HEADS, SSD_HEAD_DIM, SSD_STATE), 0.1),
        "state_ssd_conv": nrm((N_SSD, DEC_BATCH, SSD_CONV - 1, SSD_CONV_CH)),
        "cache_k": nrm((N_FOX, n_pool, PAGE_SIZE, FOX_HEADS, FOX_HEAD_DIM)),
        "cache_v": nrm((N_FOX, n_pool, PAGE_SIZE, FOX_HEADS, FOX_HEAD_DIM)),
        "cache_logf": jax.nn.log_sigmoid(unif((N_FOX, n_pool, PAGE_SIZE, FOX_HEADS), 1.0, 5.0)
                                         + nrm((N_FOX, n_pool, PAGE_SIZE, FOX_HEADS))),
        "page_table": page_table,
        "state_wkv": nrm((N_RWKV, DEC_BATCH, RWKV_HEADS, RWKV_HEAD, RWKV_HEAD), 0.5),
        "state_shift": nrm((N_RWKV, DEC_BATCH, D_MODEL)),
        "state_ffn_conv": nrm((DEPTH, DEC_BATCH, FFN_CONV - 1, D_FF)),
        "ada_w": nrm((DEPTH, D_MODEL, 6 * D_MODEL), 0.5 * d_in),
        "ada_b": nrm((DEPTH, 6 * D_MODEL), 0.02),
        "norm_mix": 1.0 + nrm((DEPTH, D_MODEL), 0.02),
        "norm_ffn": 1.0 + nrm((DEPTH, D_MODEL), 0.02),
        "norm_final": 1.0 + nrm((D_MODEL,), 0.02),
        "ssd_in_w": nrm((N_SSD, D_MODEL, SSD_IN), d_in),
        "ssd_conv_w": nrm((N_SSD, SSD_CONV, SSD_CONV_CH), SSD_CONV ** -0.5),
        "ssd_conv_b": nrm((N_SSD, SSD_CONV_CH), 0.02),
        "ssd_dt_bias": dt_bias,
        "ssd_a_log": jnp.log(unif((N_SSD, SSD_HEADS), 1.0, 16.0)),
        "ssd_d": 1.0 + nrm((N_SSD, SSD_HEADS), 0.1),
        "ssd_norm_w": 1.0 + nrm((N_SSD, SSD_D_INNER), 0.02),
        "ssd_out_w": nrm((N_SSD, SSD_D_INNER, D_MODEL), SSD_D_INNER ** -0.5),
        "fox_in_w": nrm((N_FOX, D_MODEL, 3 * FOX_INNER + FOX_HEADS), d_in),
        "fox_f_b": unif((N_FOX, FOX_HEADS), 1.0, 5.0),
        "fox_out_w": nrm((N_FOX, FOX_INNER, D_MODEL), FOX_INNER ** -0.5),
        "rwkv_mu": unif((N_RWKV, 6, D_MODEL), 0.0, 1.0),
        "rwkv_w_rkv": nrm((N_RWKV, 3, D_MODEL, D_MODEL), d_in),
        "rwkv_w0": unif((N_RWKV, D_MODEL), -6.0, 1.0),
        "rwkv_w1": nrm((N_RWKV, D_MODEL, RWKV_DECAY_LORA), d_in),
        "rwkv_w2": nrm((N_RWKV, RWKV_DECAY_LORA, D_MODEL), 0.1 * RWKV_DECAY_LORA ** -0.5),
        "rwkv_a0": nrm((N_RWKV, D_MODEL), 0.1),
        "rwkv_a1": nrm((N_RWKV, D_MODEL, RWKV_AAA_LORA), d_in),
        "rwkv_a2": nrm((N_RWKV, RWKV_AAA_LORA, D_MODEL), 0.1 * RWKV_AAA_LORA ** -0.5),
        "rwkv_g1": nrm((N_RWKV, D_MODEL, RWKV_GATE_LORA), d_in),
        "rwkv_g2": nrm((N_RWKV, RWKV_GATE_LORA, D_MODEL), RWKV_GATE_LORA ** -0.5),
        "rwkv_k_k": 0.85 + nrm((N_RWKV, D_MODEL), 0.05),
        "rwkv_k_a": 1.0 + nrm((N_RWKV, D_MODEL), 0.05),
        "rwkv_r_k": nrm((N_RWKV, RWKV_HEADS, RWKV_HEAD), 0.1),
        "rwkv_ln_w": 1.0 + nrm((N_RWKV, D_MODEL), 0.02),
        "rwkv_ln_b": nrm((N_RWKV, D_MODEL), 0.02),
        "rwkv_out_w": nrm((N_RWKV, D_MODEL, D_MODEL), d_in),
        "ffn_up_w": nrm((DEPTH, D_MODEL, 2 * D_FF), d_in),
        "ffn_conv_w": nrm((DEPTH, FFN_CONV, D_FF), FFN_CONV ** -0.5),
        "ffn_conv_b": nrm((DEPTH, D_FF), 0.02),
        "ffn_down_w": nrm((DEPTH, D_FF, D_MODEL), D_FF ** -0.5),
    }


def reference(x_prompt, x_sample, c_prompt, c_sample,
              state_ssm, state_ssd_conv, cache_k, cache_v, cache_logf, page_table,
              state_wkv, state_shift, state_ffn_conv,
              ada_w, ada_b, norm_mix, norm_ffn, norm_final,
              ssd_in_w, ssd_conv_w, ssd_conv_b, ssd_dt_bias, ssd_a_log, ssd_d, ssd_norm_w, ssd_out_w,
              fox_in_w, fox_f_b, fox_out_w,
              rwkv_mu, rwkv_w_rkv, rwkv_w0, rwkv_w1, rwkv_w2, rwkv_a0, rwkv_a1, rwkv_a2,
              rwkv_g1, rwkv_g2, rwkv_k_k, rwkv_k_a, rwkv_r_k, rwkv_ln_w, rwkv_ln_b, rwkv_out_w,
              ffn_up_w, ffn_conv_w, ffn_conv_b, ffn_down_w):
    xp, xs = x_prompt, x_sample
    bp, dtype = xp.shape[0], xp.dtype
    ssm_p, ssm_s, sconv_p, sconv_s = [], [], [], []
    k_p, k_s, v_p, v_s, lf_p, lf_s = [], [], [], [], [], []
    wkv_p, wkv_s, shift_p, shift_s = [], [], [], []
    fconv_p, fconv_s = [], []
    for i in range(DEPTH):
        kind, j = i % N_MIXERS, i // N_MIXERS
        mp = ada_mod(c_prompt, ada_w[i], ada_b[i])
        ms = ada_mod(c_sample, ada_w[i], ada_b[i])
        hp = modulate(rmsnorm(xp, norm_mix[i]), mp[0], mp[1])
        hs = modulate(rmsnorm(xs, norm_mix[i]), ms[0], ms[1])
        if kind == 0:
            w = (ssd_in_w[j], ssd_conv_w[j], ssd_conv_b[j], ssd_dt_bias[j], ssd_a_log[j],
                 ssd_d[j], ssd_norm_w[j], ssd_out_w[j])
            op, cp, sp = ssd_mixer(hp, jnp.zeros((bp, SSD_CONV - 1, SSD_CONV_CH), dtype),
                                   jnp.zeros((bp, SSD_HEADS, SSD_HEAD_DIM, SSD_STATE), dtype), *w)
            os_, cs, ss = ssd_mixer(hs, state_ssd_conv[j], state_ssm[j], *w)
            ssm_p.append(sp); ssm_s.append(ss); sconv_p.append(cp); sconv_s.append(cs)
        elif kind == 1:
            op, kp, vp, lp = fox_prompt(hp, fox_in_w[j], fox_f_b[j], fox_out_w[j])
            os_, ks, vs, ls = fox_sample(hs, cache_k[j], cache_v[j], cache_logf[j], page_table,
                                         fox_in_w[j], fox_f_b[j], fox_out_w[j])
            k_p.append(kp); k_s.append(ks); v_p.append(vp); v_s.append(vs); lf_p.append(lp); lf_s.append(ls)
        else:
            w = (rwkv_mu[j], rwkv_w_rkv[j], rwkv_w0[j], rwkv_w1[j], rwkv_w2[j], rwkv_a0[j], rwkv_a1[j],
                 rwkv_a2[j], rwkv_g1[j], rwkv_g2[j], rwkv_k_k[j], rwkv_k_a[j], rwkv_r_k[j],
                 rwkv_ln_w[j], rwkv_ln_b[j], rwkv_out_w[j])
            op, shp, wp = rwkv7_mixer(hp, jnp.zeros((bp, D_MODEL), dtype),
                                      jnp.zeros((bp, RWKV_HEADS, RWKV_HEAD, RWKV_HEAD), dtype), *w)
            os_, shs, ws = rwkv7_mixer(hs, state_shift[j], state_wkv[j], *w)
            wkv_p.append(wp); wkv_s.append(ws); shift_p.append(shp); shift_s.append(shs)
        xp = xp + mp[2][:, None, :] * op
        xs = xs + ms[2][:, None, :] * os_
        hp = modulate(rmsnorm(xp, norm_ffn[i]), mp[3], mp[4])
        hs = modulate(rmsnorm(xs, norm_ffn[i]), ms[3], ms[4])
        fw = (ffn_up_w[i], ffn_conv_w[i], ffn_conv_b[i], ffn_down_w[i])
        fp, fcp = conv_ffn(hp, jnp.zeros((bp, FFN_CONV - 1, D_FF), dtype), *fw)
        fs, fcs = conv_ffn(hs, state_ffn_conv[i], *fw)
        fconv_p.append(fcp); fconv_s.append(fcs)
        xp = xp + mp[5][:, None, :] * fp
        xs = xs + ms[5][:, None, :] * fs
    y_prompt = rmsnorm(xp, norm_final)
    y_sample = rmsnorm(xs, norm_final)
    ssm_prompt, ssm_sample = jnp.stack(ssm_p), jnp.stack(ssm_s)
    ssd_conv_prompt, ssd_conv_sample = jnp.stack(sconv_p), jnp.stack(sconv_s)
    k_prompt, k_sample = jnp.stack(k_p), jnp.stack(k_s)
    v_prompt, v_sample = jnp.stack(v_p), jnp.stack(v_s)
    logf_prompt, logf_sample = jnp.stack(lf_p), jnp.stack(lf_s)
    wkv_prompt, wkv_sample = jnp.stack(wkv_p), jnp.stack(wkv_s)
    shift_prompt, shift_sample = jnp.stack(shift_p), jnp.stack(shift_s)
    ffn_conv_prompt, ffn_conv_sample = jnp.stack(fconv_p), jnp.stack(fconv_s)
    return (y_prompt, y_sample, ssm_prompt, ssm_sample, ssd_conv_prompt, ssd_conv_sample,
            k_prompt, k_sample, v_prompt, v_sample, logf_prompt, logf_sample,
            wkv_prompt, wkv_sample, shift_prompt, shift_sample, ffn_conv_prompt, ffn_conv_sample)
```

```python
import functools
import math

import jax
import jax.numpy as jnp
from jax import lax
from jax.experimental import pallas as pl
from jax.experimental.pallas import tpu as pltpu

F32 = jnp.float32
BF16 = jnp.bfloat16

D_MODEL = 1024
DEPTH = 4
N_MIXERS = 3
RMS_EPS = 1e-6
SSD_D_INNER = 2048
SSD_HEAD_DIM = 64
SSD_HEADS = 32
SSD_GROUPS = 4
SSD_HPG = 8
SSD_STATE = 128
SSD_CONV = 4
SSD_CONV_CH = 3072
SSD_CHUNK = 128
FOX_HEAD_DIM = 64
FOX_HEADS = 16
FOX_INNER = 1024
FOX_BLOCK = 128
RWKV_HEAD = 64
RWKV_HEADS = 16
RWKV_GN_EPS = 64e-5
D_FF = 2816
FFN_CONV = 3

SUBLANES = 8
LANES = 128
VMEM_LIMIT = 56 * 1024 * 1024


def _cparams(n_grid):
    return pltpu.CompilerParams(dimension_semantics=("arbitrary",) * n_grid,
                                vmem_limit_bytes=VMEM_LIMIT)


def _modnorm(x, g, sc, sh, T):
    xn = x * lax.rsqrt(jnp.mean(x * x, axis=-1, keepdims=True) + RMS_EPS) * g
    if sc.ndim == 2:
        return xn * (1.0 + sc) + sh
    tm, D = x.shape
    x3 = xn.reshape(tm // T, T, D)
    return (x3 * (1.0 + sc) + sh).reshape(tm, D)


def _gate_mul(y, gt, T):
    if gt.ndim == 2:
        return y * gt
    tm, D = y.shape
    return (y.reshape(tm // T, T, D) * gt).reshape(tm, D)


def _shift_rows(cur3, prev3, k):
    row = lax.broadcasted_iota(jnp.int32, cur3.shape, 1)
    return jnp.where(row >= k, pltpu.roll(cur3, k, 1), pltpu.roll(prev3, k, 1))


def _mod_spec(T, tm, D):
    if T >= tm:
        per = T // tm
        return pl.BlockSpec((None, 1, D), lambda i, *_: (i // per, 0, 0))
    return pl.BlockSpec((tm // T, 1, D), lambda i, *_: (i, 0, 0))


def _tile_spec(T, tm, C):
    if T >= tm:
        per = T // tm
        return pl.BlockSpec((None, SUBLANES, C), lambda i, *_: (i // per, 0, 0))
    return pl.BlockSpec((tm // T, SUBLANES, C), lambda i, *_: (i, 0, 0))


def _const_spec(shape, n_grid=1):
    zeros = (0,) * len(shape)
    return pl.BlockSpec(shape, lambda *_: zeros)


def _ada_kernel(c_ref, w_ref, b_ref, o_ref):
    c = c_ref[...]
    a = (c * jax.nn.sigmoid(c)).astype(BF16)
    o_ref[...] = jnp.dot(a, w_ref[...].astype(BF16), preferred_element_type=F32) + b_ref[...]


def _ada_mods(c_all, ada_w, ada_b):
    R, D = c_all.shape
    L, _, N = ada_w.shape
    tn = 1536
    return pl.pallas_call(
        _ada_kernel,
        grid=(L, N // tn),
        in_specs=[pl.BlockSpec((R, D), lambda l, j: (0, 0)),
                  pl.BlockSpec((None, D, tn), lambda l, j: (l, 0, j)),
                  pl.BlockSpec((None, 1, tn), lambda l, j: (l, 0, j))],
        out_specs=pl.BlockSpec((None, R, tn), lambda l, j: (l, 0, j)),
        out_shape=jax.ShapeDtypeStruct((L, R, N), F32),
        compiler_params=_cparams(2),
        name="ada_mods",
    )(c_all, ada_w, ada_b.reshape(L, 1, N))


def _nmm_kernel(x_ref, g_ref, sc_ref, sh_ref, w_ref, o_ref, h_scr, *, T):
    @pl.when(pl.program_id(1) == 0)
    def _():
        h_scr[...] = _modnorm(x_ref[...], g_ref[...], sc_ref[...], sh_ref[...], T).astype(BF16)
    o_ref[...] = jnp.dot(h_scr[...], w_ref[...], preferred_element_type=F32).astype(o_ref.dtype)


def _norm_mod_matmul(x, g, sc, sh, w, *, T, tm, tn, out_dtype=F32):
    M, D = x.shape
    N = w.shape[1]
    return pl.pallas_call(
        functools.partial(_nmm_kernel, T=T),
        grid=(M // tm, N // tn),
        in_specs=[pl.BlockSpec((tm, D), lambda i, j: (i, 0)),
                  pl.BlockSpec((1, D), lambda i, j: (0, 0)),
                  _mod_spec(T, tm, D), _mod_spec(T, tm, D),
                  pl.BlockSpec((D, tn), lambda i, j: (0, j))],
        out_specs=pl.BlockSpec((tm, tn), lambda i, j: (i, j)),
        out_shape=jax.ShapeDtypeStruct((M, N), out_dtype),
        scratch_shapes=[pltpu.VMEM((tm, D), BF16)],
        compiler_params=_cparams(2),
        name="norm_mod_matmul",
    )(x, g.reshape(1, D), sc, sh, w)


def _nmm_multi_kernel(x_ref, g_ref, sc_ref, sh_ref, w_ref, *rest, T, n_out):
    outs, h_scr = rest[:n_out], rest[n_out]
    j = pl.program_id(1)

    @pl.when(j == 0)
    def _():
        h_scr[...] = _modnorm(x_ref[...], g_ref[...], sc_ref[...], sh_ref[...], T).astype(BF16)
    res = jnp.dot(h_scr[...], w_ref[...], preferred_element_type=F32)
    for o in range(n_out):
        @pl.when(j == o)
        def _(o=o):
            outs[o][...] = res


def _norm_mod_matmul_multi(x, g, sc, sh, w_stack, *, T, tm):
    M, D = x.shape
    n_out, _, N = w_stack.shape
    return pl.pallas_call(
        functools.partial(_nmm_multi_kernel, T=T, n_out=n_out),
        grid=(M // tm, n_out),
        in_specs=[pl.BlockSpec((tm, D), lambda i, j: (i, 0)),
                  pl.BlockSpec((1, D), lambda i, j: (0, 0)),
                  _mod_spec(T, tm, D), _mod_spec(T, tm, D),
                  pl.BlockSpec((None, D, N), lambda i, j: (j, 0, 0))],
        out_specs=[pl.BlockSpec((tm, N), lambda i, j: (i, 0)) for _ in range(n_out)],
        out_shape=[jax.ShapeDtypeStruct((M, N), F32) for _ in range(n_out)],
        scratch_shapes=[pltpu.VMEM((tm, D), BF16)],
        compiler_params=_cparams(2),
        name="norm_mod_matmul_multi",
    )(x, g.reshape(1, D), sc, sh, w_stack)


def _res_mm_kernel(a_ref, w_ref, x_ref, gt_ref, o_ref, *, T):
    y = jnp.dot(a_ref[...].astype(BF16), w_ref[...], preferred_element_type=F32)
    o_ref[...] = x_ref[...] + _gate_mul(y, gt_ref[...], T)


def _res_matmul(a, w, x, gt, *, T, tm):
    M, K = a.shape
    D = w.shape[1]
    return pl.pallas_call(
        functools.partial(_res_mm_kernel, T=T),
        grid=(M // tm,),
        in_specs=[pl.BlockSpec((tm, K), lambda i: (i, 0)),
                  pl.BlockSpec((K, D), lambda i: (0, 0)),
                  pl.BlockSpec((tm, D), lambda i: (i, 0)),
                  _mod_spec(T, tm, D)],
        out_specs=pl.BlockSpec((tm, D), lambda i: (i, 0)),
        out_shape=jax.ShapeDtypeStruct((M, D), F32),
        compiler_params=_cparams(1),
        name="res_matmul",
    )(a, w, x, gt)


def _ffn_kernel(x_ref, g_ref, sc_ref, sh_ref, gt_ref, st_ref, upw_ref, cw_ref, cb_ref, dnw_ref,
                o_ref, tail_ref, act_scr, carry_scr, *, T, F, fc):
    i = pl.program_id(0)
    tm, D = x_ref.shape
    nt = tm // SUBLANES
    long_seq = T >= tm
    x = x_ref[...]
    h = _modnorm(x, g_ref[...], sc_ref[...], sh_ref[...], T).astype(BF16)
    if long_seq:
        @pl.when(i % (T // tm) == 0)
        def _():
            carry_scr[...] = st_ref[...]
    for c in range(F // fc):
        lo, hi = c * fc, (c + 1) * fc
        gc = jnp.dot(h, upw_ref[:, lo:hi], preferred_element_type=F32)
        uc = jnp.dot(h, upw_ref[:, F + lo:F + hi], preferred_element_type=F32)
        g3 = gc.reshape(nt, SUBLANES, fc)
        if long_seq:
            prev3 = jnp.concatenate([carry_scr[:, lo:hi][None], g3[:nt - 1]], axis=0)
            carry_scr[:, lo:hi] = g3[nt - 1]
            tail_ref[:, lo:hi] = g3[nt - 1]
        else:
            prev3 = st_ref[:, :, lo:hi]
            tail_ref[:, :, lo:hi] = g3
        conv = (g3 * cw_ref[2:3, lo:hi] + _shift_rows(g3, prev3, 1) * cw_ref[1:2, lo:hi]
                + _shift_rows(g3, prev3, 2) * cw_ref[0:1, lo:hi] + cb_ref[:, lo:hi])
        act = conv * jax.nn.sigmoid(conv) * uc.reshape(nt, SUBLANES, fc)
        act_scr[:, lo:hi] = act.reshape(tm, fc).astype(BF16)
    y = jnp.dot(act_scr[...], dnw_ref[...], preferred_element_type=F32)
    o_ref[...] = x + _gate_mul(y, gt_ref[...], T)


def _conv_ffn(x, g, sc, sh, gt, st_tiles, up_w, conv_w, conv_b, down_w, *, T, tm):
    M, D = x.shape
    F = down_w.shape[0]
    B = st_tiles.shape[0]
    return pl.pallas_call(
        functools.partial(_ffn_kernel, T=T, F=F, fc=256),
        grid=(M // tm,),
        in_specs=[pl.BlockSpec((tm, D), lambda i: (i, 0)),
                  pl.BlockSpec((1, D), lambda i: (0, 0)),
                  _mod_spec(T, tm, D), _mod_spec(T, tm, D), _mod_spec(T, tm, D),
                  _tile_spec(T, tm, F),
                  pl.BlockSpec((D, 2 * F), lambda i: (0, 0)),
                  pl.BlockSpec((FFN_CONV, F), lambda i: (0, 0)),
                  pl.BlockSpec((1, F), lambda i: (0, 0)),
                  pl.BlockSpec((F, D), lambda i: (0, 0))],
        out_specs=[pl.BlockSpec((tm, D), lambda i: (i, 0)), _tile_spec(T, tm, F)],
        out_shape=[jax.ShapeDtypeStruct((M, D), F32), jax.ShapeDtypeStruct((B, SUBLANES, F), F32)],
        scratch_shapes=[pltpu.VMEM((tm, F), BF16), pltpu.VMEM((SUBLANES, F), F32)],
        compiler_params=_cparams(1),
        name="conv_ffn",
    )(x, g.reshape(1, D), sc, sh, gt, st_tiles, up_w, conv_w, conv_b.reshape(1, F), down_w)


def _rms_kernel(x_ref, g_ref, o_ref):
    x = x_ref[...]
    o_ref[...] = x * lax.rsqrt(jnp.mean(x * x, axis=-1, keepdims=True) + RMS_EPS) * g_ref[...]


def _rmsnorm(x, g, *, tm):
    M, D = x.shape
    return pl.pallas_call(
        _rms_kernel, grid=(M // tm,),
        in_specs=[pl.BlockSpec((tm, D), lambda i: (i, 0)), pl.BlockSpec((1, D), lambda i: (0, 0))],
        out_specs=pl.BlockSpec((tm, D), lambda i: (i, 0)),
        out_shape=jax.ShapeDtypeStruct((M, D), F32),
        compiler_params=_cparams(1), name="final_rmsnorm",
    )(x, g.reshape(1, D))


def _segsum(a):
    T = a.shape[-1]
    rep = jnp.broadcast_to(a[..., :, None], a.shape + (T,))
    strict = jnp.tril(jnp.ones((T, T), dtype=bool), -1)
    cs = jnp.cumsum(jnp.where(strict, rep, 0.0), axis=-2)
    return jnp.where(jnp.tril(jnp.ones((T, T), dtype=bool)), cs, -jnp.inf)


def _ssd_chunked_jax(xh, dt, A, Bm, Cm, h0):
    b, T = xh.shape[:2]
    Lc = min(SSD_CHUNK, T)
    nc = T // Lc
    X = (xh * dt[..., None]).reshape(b, nc, Lc, SSD_GROUPS, SSD_HPG, SSD_HEAD_DIM)
    a = jnp.moveaxis((dt * A).reshape(b, nc, Lc, SSD_GROUPS, SSD_HPG), 2, -1)
    a_cs = jnp.cumsum(a, axis=-1)
    Bc = Bm.reshape(b, nc, Lc, SSD_GROUPS, SSD_STATE)
    Cc = Cm.reshape(b, nc, Lc, SSD_GROUPS, SSD_STATE)
    decay_in = jnp.exp(_segsum(a))
    CB = jnp.einsum('bclgn,bcsgn->bcgls', Cc, Bc)
    y_diag = jnp.einsum('bcgjls,bcsgjp->bclgjp', CB[:, :, :, None] * decay_in, X)
    decay_to_end = jnp.moveaxis(jnp.exp(a_cs[..., -1:] - a_cs), -1, 2)
    chunk_states = jnp.einsum('bclgn,bclgjp->bcgjpn', Bc, X * decay_to_end[..., None])
    chunk_decay = jnp.exp(a_cs[..., -1])

    def carry_step(h, inp):
        dec, st = inp
        return h * dec[..., None, None] + st, h

    h_last, h_in = lax.scan(carry_step, h0, (jnp.moveaxis(chunk_decay, 1, 0), jnp.moveaxis(chunk_states, 1, 0)))
    h_in = jnp.moveaxis(h_in, 0, 1)
    decay_from_start = jnp.moveaxis(jnp.exp(a_cs), -1, 2)
    y_off = jnp.einsum('bclgn,bcgjpn->bclgjp', Cc, h_in) * decay_from_start[..., None]
    return (y_diag + y_off).reshape(b, T, SSD_GROUPS, SSD_HPG, SSD_HEAD_DIM), h_last


def _ssd_core_jax(z, xbc, dt, conv_state, ssm_state, conv_w, conv_b, dt_bias, a_log, d_skip, norm_w):
    b, T, _ = z.shape
    xbc_ext = jnp.concatenate([conv_state, xbc], axis=1)
    new_conv = xbc_ext[:, -(SSD_CONV - 1):]
    out = xbc_ext[:, 3:3 + T] * conv_w[3]
    for k in range(3):
        out = out + xbc_ext[:, k:k + T] * conv_w[k]
    xbc = jax.nn.silu(out + conv_b)
    xs, Bm, Cm = jnp.split(xbc, [SSD_D_INNER, SSD_D_INNER + SSD_GROUPS * SSD_STATE], axis=-1)
    xs = xs.reshape(b, T, SSD_GROUPS, SSD_HPG, SSD_HEAD_DIM)
    Bm = Bm.reshape(b, T, SSD_GROUPS, SSD_STATE)
    Cm = Cm.reshape(b, T, SSD_GROUPS, SSD_STATE)
    dt = jax.nn.softplus(dt + dt_bias).reshape(b, T, SSD_GROUPS, SSD_HPG)
    A = -jnp.exp(a_log).reshape(SSD_GROUPS, SSD_HPG)
    h0 = ssm_state.reshape(b, SSD_GROUPS, SSD_HPG, SSD_HEAD_DIM, SSD_STATE)
    y, h_last = _ssd_chunked_jax(xs, dt, A, Bm, Cm, h0)
    y = y + d_skip.reshape(SSD_GROUPS, SSD_HPG, 1) * xs
    yg = (y.reshape(b, T, SSD_D_INNER) * jax.nn.silu(z)).reshape(b, T, SSD_GROUPS, -1)
    yg = yg * lax.rsqrt(jnp.mean(yg * yg, axis=-1, keepdims=True) + RMS_EPS)
    y = yg.reshape(b, T, SSD_D_INNER) * norm_w
    return y, new_conv, h_last.reshape(b, SSD_HEADS, SSD_HEAD_DIM, SSD_STATE)


def _fox_prompt_core_jax(q, k, v, logf):
    b, T = q.shape[:2]
    cum = jnp.moveaxis(jnp.cumsum(logf, axis=1), 1, 2)
    kpos = jnp.arange(T)
    qoff = jnp.arange(FOX_BLOCK)
    scale = FOX_HEAD_DIM ** -0.5

    def one_block(start):
        qb = lax.dynamic_slice_in_dim(q, start, FOX_BLOCK, axis=1)
        cq = lax.dynamic_slice_in_dim(cum, start, FOX_BLOCK, axis=2)
        s = jnp.einsum('bqhd,bkhd->bhqk', qb, k).astype(F32) * scale
        s = s + cq[..., :, None] - cum[..., None, :]
        s = jnp.where(kpos[None, :] <= (start + qoff)[:, None], s, -jnp.inf)
        p = jax.nn.softmax(s, axis=-1)
        return jnp.einsum('bhqk,bkhd->bqhd', p, v)

    o = lax.map(one_block, jnp.arange(T // FOX_BLOCK) * FOX_BLOCK)
    return jnp.moveaxis(o, 0, 1).reshape(b, T, FOX_INNER)


def _fox_sample_core_jax(q, k, v, logf, cache_k, cache_v, cache_logf, page_table):
    b, T = q.shape[:2]
    past = page_table.shape[1] * cache_k.shape[1]
    k_past = cache_k[page_table].reshape(b, past, FOX_HEADS, FOX_HEAD_DIM)
    v_past = cache_v[page_table].reshape(b, past, FOX_HEADS, FOX_HEAD_DIM)
    cum_past = jnp.cumsum(cache_logf[page_table].reshape(b, past, FOX_HEADS), axis=1)
    cum_new = jnp.moveaxis(jnp.cumsum(logf, axis=1), 1, 2)
    bias_past = jnp.moveaxis(cum_past[:, -1:] - cum_past, 1, 2)
    scale = FOX_HEAD_DIM ** -0.5
    s_past = (jnp.einsum('bqhd,bkhd->bhqk', q, k_past) * scale + cum_new[..., :, None] + bias_past[..., None, :])
    s_new = (jnp.einsum('bqhd,bkhd->bhqk', q, k) * scale + cum_new[..., :, None] - cum_new[..., None, :])
    s_new = jnp.where(jnp.tril(jnp.ones((T, T), dtype=bool)), s_new, -jnp.inf)
    p = jax.nn.softmax(jnp.concatenate([s_past, s_new], axis=-1), axis=-1)
    o = (jnp.einsum('bhqk,bkhd->bqhd', p[..., :past], v_past) + jnp.einsum('bhqk,bkhd->bqhd', p[..., past:], v))
    return o.reshape(b, T, FOX_INNER)


def _wkv_scan_jax(r, w, k, v, a, bvec, S0):
    def step(S, inp):
        r_t, w_t, k_t, v_t, a_t, b_t = inp
        sa = jnp.einsum('bhij,bhj->bhi', S, a_t)
        S = S * w_t[:, :, None, :] + sa[..., None] * b_t[:, :, None, :] + v_t[..., None] * k_t[:, :, None, :]
        return S, jnp.einsum('bhij,bhj->bhi', S, r_t)

    xs = tuple(jnp.moveaxis(z, 1, 0) for z in (r, w, k, v, a, bvec))
    S, ys = lax.scan(step, S0, xs)
    return S, jnp.moveaxis(ys, 0, 1)


def _rwkv_core_jax(h, shift_state, wkv_state, mu, w_rkv, w0, w1, w2, a0, a1, a2, g1, g2, k_k, k_a, r_k, ln_w, ln_b):
    b, T, D = h.shape
    heads = lambda z: z.reshape(b, T, RWKV_HEADS, RWKV_HEAD)
    prev = jnp.concatenate([shift_state[:, None], h[:, :-1]], axis=1)
    xmix = h[None] + (prev - h)[None] * mu[:, None, None, :]
    r, k, v = jnp.einsum('sbtd,sde->sbte', xmix[:3], w_rkv)
    w_log = -jax.nn.softplus(-(w0 + jnp.tanh(xmix[3] @ w1) @ w2)) - 0.5
    decay = jnp.exp(-jnp.exp(w_log))
    a = jax.nn.sigmoid(a0 + (xmix[4] @ a1) @ a2)
    g = jax.nn.sigmoid(xmix[5] @ g1) @ g2
    kk = heads(k * k_k)
    kk = kk / jnp.maximum(jnp.sqrt(jnp.sum(kk * kk, axis=-1, keepdims=True)), 1e-12)
    k = k * (1 + (a - 1) * k_a)
    rh, kh, vh, wh, ah = heads(r), heads(k), heads(v), heads(decay), heads(a)
    S, y = _wkv_scan_jax(rh, wh, kh, vh, -kk, kk * ah, wkv_state)
    mean = jnp.mean(y, axis=-1, keepdims=True)
    var = jnp.mean(jnp.square(y - mean), axis=-1, keepdims=True)
    yn = ((y - mean) * lax.rsqrt(var + RWKV_GN_EPS)).reshape(b, T, D) * ln_w + ln_b
    bonus = jnp.sum(rh * kh * r_k, axis=-1, keepdims=True) * vh
    out = (yn + bonus.reshape(b, T, D)) * g
    return out, h[:, -1], S


def _pad_cols(w, n):
    return jnp.pad(w, ((0, 0), (0, n - w.shape[1])))


def _state_tiles(st):
    return jnp.pad(st, ((0, 0), (SUBLANES - st.shape[1], 0), (0, 0)))


def _run_group(x, mods, B, T, tm, layer_fn):
    return layer_fn(x, mods, B, T, tm)


def kernel(x_prompt, x_sample, c_prompt, c_sample, state_ssm, state_ssd_conv, cache_k, cache_v, cache_logf, page_table, state_wkv, state_shift, state_ffn_conv, ada_w, ada_b, norm_mix, norm_ffn, norm_final, ssd_in_w, ssd_conv_w, ssd_conv_b, ssd_dt_bias, ssd_a_log, ssd_d, ssd_norm_w, ssd_out_w, fox_in_w, fox_f_b, fox_out_w, rwkv_mu, rwkv_w_rkv, rwkv_w0, rwkv_w1, rwkv_w2, rwkv_a0, rwkv_a1, rwkv_a2, rwkv_g1, rwkv_g2, rwkv_k_k, rwkv_k_a, rwkv_r_k, rwkv_ln_w, rwkv_ln_b, rwkv_out_w, ffn_up_w, ffn_conv_w, ffn_conv_b, ffn_down_w):
    BP, TP, D = x_prompt.shape
    BS, TS, _ = x_sample.shape
    groups = [dict(B=BP, T=TP, tm=1024, tm_ffn=512), dict(B=BS, T=TS, tm=BS * TS, tm_ffn=512)]
    xs = [x_prompt.reshape(BP * TP, D), x_sample.reshape(BS * TS, D)]

    n_c = BP + BS
    n_c_pad = -(-n_c // SUBLANES) * SUBLANES
    c_all = jnp.pad(jnp.concatenate([c_prompt, c_sample], axis=0), ((0, n_c_pad - n_c), (0, 0)))
    mods_all = _ada_mods(c_all, ada_w, ada_b).reshape(DEPTH, n_c_pad, 6, 1, D)
    row0 = [0, BP]

    def mods_of(i, gi):
        gp = groups[gi]
        m = mods_all[i, row0[gi]:row0[gi] + gp["B"]]
        return [m[:, k] for k in range(6)]

    outs = {k: ([], []) for k in ("ssm", "sconv", "k", "v", "lf", "wkv", "shift", "fconv")}

    for i in range(DEPTH):
        kind, j = i % N_MIXERS, i // N_MIXERS
        for gi, gp in enumerate(groups):
            B, T, tm = gp["B"], gp["T"], gp["tm"]
            x = xs[gi]
            M = B * T
            sh_m, sc_m, gt_m, sh_f, sc_f, gt_f = mods_of(i, gi)
            if kind == 0:
                w_main = ssd_in_w[j][:, :SSD_D_INNER + SSD_CONV_CH].astype(BF16)
                w_dt = _pad_cols(ssd_in_w[j][:, SSD_D_INNER + SSD_CONV_CH:], LANES).astype(BF16)
                zx = _norm_mod_matmul(x, norm_mix[i], sc_m, sh_m, w_main, T=T, tm=tm, tn=1024)
                dt = _norm_mod_matmul(x, norm_mix[i], sc_m, sh_m, w_dt, T=T, tm=tm, tn=LANES)
                z = zx[:, :SSD_D_INNER].reshape(B, T, -1)
                xbc = zx[:, SSD_D_INNER:].reshape(B, T, -1)
                dt = dt[:, :SSD_HEADS].reshape(B, T, -1)
                if gi == 0:
                    conv0 = jnp.zeros((B, SSD_CONV - 1, SSD_CONV_CH), F32)
                    ssm0 = jnp.zeros((B, SSD_HEADS, SSD_HEAD_DIM, SSD_STATE), F32)
                else:
                    conv0, ssm0 = state_ssd_conv[j], state_ssm[j]
                y, new_conv, new_ssm = _ssd_core_jax(z, xbc, dt, conv0, ssm0, ssd_conv_w[j], ssd_conv_b[j],
                                                     ssd_dt_bias[j], ssd_a_log[j], ssd_d[j], ssd_norm_w[j])
                outs["ssm"][gi].append(new_ssm)
                outs["sconv"][gi].append(new_conv)
                x = _res_matmul(y.reshape(M, -1), ssd_out_w[j].astype(BF16), x, gt_m, T=T, tm=tm)
            elif kind == 1:
                w_qkv = fox_in_w[j][:, :3 * FOX_INNER].reshape(D, 3, FOX_INNER).transpose(1, 0, 2).astype(BF16)
                w_f = _pad_cols(fox_in_w[j][:, 3 * FOX_INNER:], LANES).astype(BF16)
                q, k, v = _norm_mod_matmul_multi(x, norm_mix[i], sc_m, sh_m, w_qkv, T=T, tm=min(tm, 512))
                f = _norm_mod_matmul(x, norm_mix[i], sc_m, sh_m, w_f, T=T, tm=tm, tn=LANES)
                logf = jax.nn.log_sigmoid(f[:, :FOX_HEADS] + fox_f_b[j]).reshape(B, T, FOX_HEADS)
                hd = lambda t: t.reshape(B, T, FOX_HEADS, FOX_HEAD_DIM)
                if gi == 0:
                    o = _fox_prompt_core_jax(hd(q), hd(k), hd(v), logf)
                else:
                    o = _fox_sample_core_jax(hd(q), hd(k), hd(v), logf, cache_k[j], cache_v[j], cache_logf[j], page_table)
                outs["k"][gi].append(hd(k))
                outs["v"][gi].append(hd(v))
                outs["lf"][gi].append(logf)
                x = _res_matmul(o.reshape(M, -1), fox_out_w[j].astype(BF16), x, gt_m, T=T, tm=tm)
            else:
                eye = jnp.eye(D, dtype=BF16)
                h = _norm_mod_matmul(x, norm_mix[i], sc_m, sh_m, eye, T=T, tm=tm, tn=1024).reshape(B, T, D)
                if gi == 0:
                    sh0 = jnp.zeros((B, D), F32)
                    wkv0 = jnp.zeros((B, RWKV_HEADS, RWKV_HEAD, RWKV_HEAD), F32)
                else:
                    sh0, wkv0 = state_shift[j], state_wkv[j]
                o, new_shift, new_wkv = _rwkv_core_jax(
                    h, sh0, wkv0, rwkv_mu[j], rwkv_w_rkv[j], rwkv_w0[j], rwkv_w1[j], rwkv_w2[j], rwkv_a0[j],
                    rwkv_a1[j], rwkv_a2[j], rwkv_g1[j], rwkv_g2[j], rwkv_k_k[j], rwkv_k_a[j], rwkv_r_k[j],
                    rwkv_ln_w[j], rwkv_ln_b[j])
                outs["wkv"][gi].append(new_wkv)
                outs["shift"][gi].append(new_shift)
                x = _res_matmul(o.reshape(M, -1), rwkv_out_w[j].astype(BF16), x, gt_m, T=T, tm=tm)
            if gi == 0:
                st_tiles = jnp.zeros((B, SUBLANES, D_FF), F32)
            else:
                st_tiles = _state_tiles(state_ffn_conv[i])
            x, tails = _conv_ffn(x, norm_ffn[i], sc_f, sh_f, gt_f, st_tiles, ffn_up_w[i].astype(BF16),
                                 ffn_conv_w[i], ffn_conv_b[i], ffn_down_w[i].astype(BF16), T=T, tm=gp["tm_ffn"])
            outs["fconv"][gi].append(tails[:, SUBLANES - (FFN_CONV - 1):])
            xs[gi] = x

    y_prompt = _rmsnorm(xs[0], norm_final, tm=1024).reshape(BP, TP, D)
    y_sample = _rmsnorm(xs[1], norm_final, tm=1024).reshape(BS, TS, D)
    st = lambda name, gi: jnp.stack(outs[name][gi])
    return (y_prompt, y_sample, st("ssm", 0), st("ssm", 1), st("sconv", 0), st("sconv", 1),
            st("k", 0), st("k", 1), st("v", 0), st("v", 1), st("lf", 0), st("lf", 1),
            st("wkv", 0), st("wkv", 1), st("shift", 0), st("shift", 1), st("fconv", 0), st("fconv", 1))
```

```python
import functools

import jax
import jax.numpy as jnp
from jax import lax
from jax.experimental import pallas as pl
from jax.experimental.pallas import tpu as pltpu

F32 = jnp.float32
BF16 = jnp.bfloat16
HI = lax.Precision.HIGHEST

D_MODEL = 1024
DEPTH = 4
N_MIXERS = 3
RMS_EPS = 1e-6
SSD_D_INNER = 2048
SSD_HEAD_DIM = 64
SSD_HEADS = 32
SSD_GROUPS = 4
SSD_HPG = 8
SSD_STATE = 128
SSD_CONV = 4
SSD_CONV_CH = 3072
SSD_CHUNK = 128
FOX_HEAD_DIM = 64
FOX_HEADS = 16
FOX_INNER = 1024
RWKV_HEAD = 64
RWKV_HEADS = 16
RWKV_GN_EPS = 64e-5
RWKV_CHUNK = 32
WKV_ROWS = 128
D_FF = 2816
FFN_CONV = 3

SUBLANES = 8
LANES = 128
VMEM_LIMIT = 56 * 1024 * 1024


def _cparams(n_grid):
    return pltpu.CompilerParams(dimension_semantics=("arbitrary",) * n_grid,
                                vmem_limit_bytes=VMEM_LIMIT)


def _nt(a, b):
    return lax.dot_general(a, b, (((1,), (1,)), ((), ())), preferred_element_type=F32)


def _tn(a, b):
    return lax.dot_general(a, b, (((0,), (0,)), ((), ())), preferred_element_type=F32)


def _modnorm(x, g, sc, sh, T):
    xn = x * lax.rsqrt(jnp.mean(x * x, axis=-1, keepdims=True) + RMS_EPS) * g
    if sc.ndim == 2:
        return xn * (1.0 + sc) + sh
    tm, D = x.shape
    x3 = xn.reshape(tm // T, T, D)
    return (x3 * (1.0 + sc) + sh).reshape(tm, D)


def _gate_mul(y, gt, T):
    if gt.ndim == 2:
        return y * gt
    tm, D = y.shape
    return (y.reshape(tm // T, T, D) * gt).reshape(tm, D)


def _shift_rows(cur3, prev3, k):
    row = lax.broadcasted_iota(jnp.int32, cur3.shape, 1)
    return jnp.where(row >= k, pltpu.roll(cur3, k, 1), pltpu.roll(prev3, k, 1))


def _mod_spec(T, tm, D):
    if T >= tm:
        per = T // tm
        return pl.BlockSpec((None, 1, D), lambda i, *_: (i // per, 0, 0))
    return pl.BlockSpec((tm // T, 1, D), lambda i, *_: (i, 0, 0))


def _tile_spec(T, tm, C):
    if T >= tm:
        per = T // tm
        return pl.BlockSpec((None, SUBLANES, C), lambda i, *_: (i // per, 0, 0))
    return pl.BlockSpec((tm // T, SUBLANES, C), lambda i, *_: (i, 0, 0))


def _ada_kernel(c_ref, w_ref, b_ref, o_ref):
    c = c_ref[...]
    a = (c * jax.nn.sigmoid(c)).astype(BF16)
    o_ref[...] = jnp.dot(a, w_ref[...].astype(BF16), preferred_element_type=F32) + b_ref[...]


def _ada_mods(c_all, ada_w, ada_b):
    R, D = c_all.shape
    L, _, N = ada_w.shape
    tn = 1536
    return pl.pallas_call(
        _ada_kernel,
        grid=(L, N // tn),
        in_specs=[pl.BlockSpec((R, D), lambda l, j: (0, 0)),
                  pl.BlockSpec((None, D, tn), lambda l, j: (l, 0, j)),
                  pl.BlockSpec((None, 1, tn), lambda l, j: (l, 0, j))],
        out_specs=pl.BlockSpec((None, R, tn), lambda l, j: (l, 0, j)),
        out_shape=jax.ShapeDtypeStruct((L, R, N), F32),
        compiler_params=_cparams(2),
        name="ada_mods",
    )(c_all, ada_w, ada_b.reshape(L, 1, N))


def _nmm_kernel(x_ref, g_ref, sc_ref, sh_ref, w_ref, o_ref, h_scr, *, T):
    @pl.when(pl.program_id(1) == 0)
    def _():
        h_scr[...] = _modnorm(x_ref[...], g_ref[...], sc_ref[...], sh_ref[...], T).astype(BF16)
    o_ref[...] = jnp.dot(h_scr[...], w_ref[...], preferred_element_type=F32).astype(o_ref.dtype)


def _norm_mod_matmul(x, g, sc, sh, w, *, T, tm, tn):
    M, D = x.shape
    N = w.shape[1]
    return pl.pallas_call(
        functools.partial(_nmm_kernel, T=T),
        grid=(M // tm, N // tn),
        in_specs=[pl.BlockSpec((tm, D), lambda i, j: (i, 0)),
                  pl.BlockSpec((1, D), lambda i, j: (0, 0)),
                  _mod_spec(T, tm, D), _mod_spec(T, tm, D),
                  pl.BlockSpec((D, tn), lambda i, j: (0, j))],
        out_specs=pl.BlockSpec((tm, tn), lambda i, j: (i, j)),
        out_shape=jax.ShapeDtypeStruct((M, N), F32),
        scratch_shapes=[pltpu.VMEM((tm, D), BF16)],
        compiler_params=_cparams(2),
        name="norm_mod_matmul",
    )(x, g.reshape(1, D), sc, sh, w)


def _nmm_multi_kernel(x_ref, g_ref, sc_ref, sh_ref, w_ref, *rest, T, n_out):
    outs, h_scr = rest[:n_out], rest[n_out]
    j = pl.program_id(1)

    @pl.when(j == 0)
    def _():
        h_scr[...] = _modnorm(x_ref[...], g_ref[...], sc_ref[...], sh_ref[...], T).astype(BF16)
    res = jnp.dot(h_scr[...], w_ref[...], preferred_element_type=F32)
    for o in range(n_out):
        @pl.when(j == o)
        def _(o=o):
            outs[o][...] = res


def _norm_mod_matmul_multi(x, g, sc, sh, w_stack, *, T, tm):
    M, D = x.shape
    n_out, _, N = w_stack.shape
    return pl.pallas_call(
        functools.partial(_nmm_multi_kernel, T=T, n_out=n_out),
        grid=(M // tm, n_out),
        in_specs=[pl.BlockSpec((tm, D), lambda i, j: (i, 0)),
                  pl.BlockSpec((1, D), lambda i, j: (0, 0)),
                  _mod_spec(T, tm, D), _mod_spec(T, tm, D),
                  pl.BlockSpec((None, D, N), lambda i, j: (j, 0, 0))],
        out_specs=[pl.BlockSpec((tm, N), lambda i, j: (i, 0)) for _ in range(n_out)],
        out_shape=[jax.ShapeDtypeStruct((M, N), F32) for _ in range(n_out)],
        scratch_shapes=[pltpu.VMEM((tm, D), BF16)],
        compiler_params=_cparams(2),
        name="norm_mod_matmul_multi",
    )(x, g.reshape(1, D), sc, sh, w_stack)


def _res_mm_kernel(a_ref, w_ref, x_ref, gt_ref, o_ref, *, T):
    y = jnp.dot(a_ref[...].astype(BF16), w_ref[...], preferred_element_type=F32)
    o_ref[...] = x_ref[...] + _gate_mul(y, gt_ref[...], T)


def _res_matmul(a, w, x, gt, *, T, tm):
    M, K = a.shape
    D = w.shape[1]
    return pl.pallas_call(
        functools.partial(_res_mm_kernel, T=T),
        grid=(M // tm,),
        in_specs=[pl.BlockSpec((tm, K), lambda i: (i, 0)),
                  pl.BlockSpec((K, D), lambda i: (0, 0)),
                  pl.BlockSpec((tm, D), lambda i: (i, 0)),
                  _mod_spec(T, tm, D)],
        out_specs=pl.BlockSpec((tm, D), lambda i: (i, 0)),
        out_shape=jax.ShapeDtypeStruct((M, D), F32),
        compiler_params=_cparams(1),
        name="res_matmul",
    )(a, w, x, gt)


def _ffn_kernel(x_ref, g_ref, sc_ref, sh_ref, gt_ref, st_ref, upw_ref, cw_ref, cb_ref, dnw_ref,
                o_ref, tail_ref, act_scr, carry_scr, *, T, F, fc):
    i = pl.program_id(0)
    tm, D = x_ref.shape
    nt = tm // SUBLANES
    long_seq = T >= tm
    x = x_ref[...]
    h = _modnorm(x, g_ref[...], sc_ref[...], sh_ref[...], T).astype(BF16)
    if long_seq:
        @pl.when(i % (T // tm) == 0)
        def _():
            carry_scr[...] = st_ref[...]
    for c in range(F // fc):
        lo, hi = c * fc, (c + 1) * fc
        gc = jnp.dot(h, upw_ref[:, lo:hi], preferred_element_type=F32)
        uc = jnp.dot(h, upw_ref[:, F + lo:F + hi], preferred_element_type=F32)
        g3 = gc.reshape(nt, SUBLANES, fc)
        if long_seq:
            prev3 = jnp.concatenate([carry_scr[:, lo:hi][None], g3[:nt - 1]], axis=0)
            carry_scr[:, lo:hi] = g3[nt - 1]
            tail_ref[:, lo:hi] = g3[nt - 1]
        else:
            prev3 = st_ref[:, :, lo:hi]
            tail_ref[:, :, lo:hi] = g3
        conv = (g3 * cw_ref[2:3, lo:hi] + _shift_rows(g3, prev3, 1) * cw_ref[1:2, lo:hi]
                + _shift_rows(g3, prev3, 2) * cw_ref[0:1, lo:hi] + cb_ref[:, lo:hi])
        act = conv * jax.nn.sigmoid(conv) * uc.reshape(nt, SUBLANES, fc)
        act_scr[:, lo:hi] = act.reshape(tm, fc).astype(BF16)
    y = jnp.dot(act_scr[...], dnw_ref[...], preferred_element_type=F32)
    o_ref[...] = x + _gate_mul(y, gt_ref[...], T)


def _conv_ffn(x, g, sc, sh, gt, st_tiles, up_w, conv_w, conv_b, down_w, *, T, tm):
    M, D = x.shape
    F = down_w.shape[0]
    B = st_tiles.shape[0]
    return pl.pallas_call(
        functools.partial(_ffn_kernel, T=T, F=F, fc=256),
        grid=(M // tm,),
        in_specs=[pl.BlockSpec((tm, D), lambda i: (i, 0)),
                  pl.BlockSpec((1, D), lambda i: (0, 0)),
                  _mod_spec(T, tm, D), _mod_spec(T, tm, D), _mod_spec(T, tm, D),
                  _tile_spec(T, tm, F),
                  pl.BlockSpec((D, 2 * F), lambda i: (0, 0)),
                  pl.BlockSpec((FFN_CONV, F), lambda i: (0, 0)),
                  pl.BlockSpec((1, F), lambda i: (0, 0)),
                  pl.BlockSpec((F, D), lambda i: (0, 0))],
        out_specs=[pl.BlockSpec((tm, D), lambda i: (i, 0)), _tile_spec(T, tm, F)],
        out_shape=[jax.ShapeDtypeStruct((M, D), F32), jax.ShapeDtypeStruct((B, SUBLANES, F), F32)],
        scratch_shapes=[pltpu.VMEM((tm, F), BF16), pltpu.VMEM((SUBLANES, F), F32)],
        compiler_params=_cparams(1),
        name="conv_ffn",
    )(x, g.reshape(1, D), sc, sh, gt, st_tiles, up_w, conv_w, conv_b.reshape(1, F), down_w)


def _rms_kernel(x_ref, g_ref, o_ref):
    x = x_ref[...]
    o_ref[...] = x * lax.rsqrt(jnp.mean(x * x, axis=-1, keepdims=True) + RMS_EPS) * g_ref[...]


def _rmsnorm(x, g, *, tm):
    M, D = x.shape
    return pl.pallas_call(
        _rms_kernel, grid=(M // tm,),
        in_specs=[pl.BlockSpec((tm, D), lambda i: (i, 0)), pl.BlockSpec((1, D), lambda i: (0, 0))],
        out_specs=pl.BlockSpec((tm, D), lambda i: (i, 0)),
        out_shape=jax.ShapeDtypeStruct((M, D), F32),
        compiler_params=_cparams(1), name="final_rmsnorm",
    )(x, g.reshape(1, D))


def _conv_silu(cur, prev_tile, w_ref, b_ref, lo, hi):
    L, C = cur.shape
    nt = L // SUBLANES
    c3 = cur.reshape(nt, SUBLANES, C)
    p3 = prev_tile[None] if nt == 1 else jnp.concatenate([prev_tile[None], c3[:nt - 1]], axis=0)
    out = c3 * w_ref[3:4, lo:hi] + b_ref[:, lo:hi]
    for k in range(1, SSD_CONV):
        out = out + _shift_rows(c3, p3, k) * w_ref[3 - k:4 - k, lo:hi]
    out = out * jax.nn.sigmoid(out)
    return out.reshape(L, C)


def _ssd_kernel(z0_ref, z1_ref, x0_ref, x1_ref, bc_ref, dt_ref, cst_ref, h0_ref, cw_ref, cb_ref, dtb_ref, alog_ref,
                dw_ref, nw_ref, ex_ref,
                y_ref, ctail_ref, hout_ref, tail_scr, st_scr, *, L, n_chunks):
    c = pl.program_id(1)
    P, Nn, G, J = SSD_HEAD_DIM, SSD_STATE, SSD_GROUPS, SSD_HPG
    GW = J * P

    @pl.when(c == 0)
    def _():
        tail_scr[...] = cst_ref[...]
        st_scr[...] = h0_ref[...]

    conv = []
    for q, blk_ref in enumerate((x0_ref, x1_ref, bc_ref)):
        blk = blk_ref[...]
        lo, hi = q * 1024, (q + 1) * 1024
        conv.append(_conv_silu(blk, tail_scr[:, lo:hi], cw_ref, cb_ref, lo, hi))
        tail_scr[:, lo:hi] = blk[L - SUBLANES:, :]
        ctail_ref[:, lo:hi] = blk[L - SUBLANES:, :]
    xs = jnp.concatenate(conv[:2], axis=1)
    Bm = conv[2][:, :G * Nn]
    Cm = conv[2][:, G * Nn:]

    dt = jax.nn.softplus(dt_ref[...] + dtb_ref[...])
    a = dt * (-jnp.exp(alog_ref[...]))
    ri = lax.broadcasted_iota(jnp.int32, (L, L), 0)
    ci = lax.broadcasted_iota(jnp.int32, (L, L), 1)
    causal = ri >= ci
    acs = jnp.dot(causal.astype(F32), a, preferred_element_type=F32, precision=HI)
    ex = ex_ref[...]
    dt_w = jnp.dot(dt, ex, preferred_element_type=F32, precision=HI)
    acs_w = jnp.dot(acs, ex, preferred_element_type=F32, precision=HI)
    acs_t = acs.T
    last_w = acs_w[L - 1:L, :]
    X = xs * dt_w
    Xd = X * jnp.exp(last_w - acs_w)
    dstart = jnp.exp(acs_w)
    cdec = jnp.exp(last_w)
    zz = jnp.concatenate([z0_ref[...], z1_ref[...]], axis=1)
    zg = zz * jax.nn.sigmoid(zz)
    dwide = dw_ref[...]
    nw = nw_ref[...]
    Xb = X.astype(BF16)
    for g in range(G):
        gl = slice(g * GW, (g + 1) * GW)
        Bg = Bm[:, g * Nn:(g + 1) * Nn].astype(BF16)
        Cg = Cm[:, g * Nn:(g + 1) * Nn].astype(BF16)
        CB = _nt(Cg, Bg)
        st = st_scr[g]
        y_off = jnp.dot(Cg, st.astype(BF16), preferred_element_type=F32) * dstart[:, gl]
        st_scr[g] = st * cdec[:, gl] + _tn(Bg, Xd[:, gl].astype(BF16))
        ys = []
        for j in range(J):
            h = g * J + j
            diff = acs[:, h:h + 1] - acs_t[h:h + 1, :]
            dec = jnp.exp(jnp.where(causal, diff, -jnp.inf))
            Mh = (CB * dec).astype(BF16)
            ys.append(jnp.dot(Mh, Xb[:, h * P:(h + 1) * P], preferred_element_type=F32))
        yg = jnp.concatenate(ys, axis=1) + y_off + dwide[:, gl] * xs[:, gl]
        yg = yg * zg[:, gl]
        yg = yg * lax.rsqrt(jnp.mean(yg * yg, axis=-1, keepdims=True) + RMS_EPS) * nw[:, gl]
        y_ref[:, gl] = yg.astype(y_ref.dtype)

    @pl.when(c == n_chunks - 1)
    def _():
        hout_ref[...] = st_scr[...]


def _ssd_core(zx, dt, conv_tiles, h0_t, conv_w, conv_b, dt_bias, a_log, d_skip, norm_w, *, B, T, L):
    M = zx.shape[0]
    nc = T // L
    DI, C = SSD_D_INNER, SSD_CONV_CH
    col = lambda q: pl.BlockSpec((L, 1024), lambda b, c, q=q: (b * nc + c, q))
    par = lambda n: pl.BlockSpec((1, n), lambda b, c: (0, 0))
    pad = lambda v: jnp.pad(v.astype(F32), (0, LANES - v.shape[0])).reshape(1, LANES)
    expand = (jnp.arange(LANES)[:, None] == (jnp.arange(DI) // SSD_HEAD_DIM)[None, :]).astype(F32)
    d_wide = jnp.repeat(d_skip.astype(F32), SSD_HEAD_DIM).reshape(1, DI)
    st_spec = pl.BlockSpec((None, SSD_GROUPS, SSD_STATE, SSD_HPG * SSD_HEAD_DIM), lambda b, c: (b, 0, 0, 0))
    tile_spec = pl.BlockSpec((None, SUBLANES, C), lambda b, c: (b, 0, 0))
    return pl.pallas_call(
        functools.partial(_ssd_kernel, L=L, n_chunks=nc),
        grid=(B, nc),
        in_specs=[col(0), col(1), col(2), col(3), col(4),
                  pl.BlockSpec((L, LANES), lambda b, c: (b * nc + c, 0)),
                  tile_spec, st_spec,
                  pl.BlockSpec((SSD_CONV, C), lambda b, c: (0, 0)), par(C), par(LANES), par(LANES),
                  par(DI), par(DI), pl.BlockSpec((LANES, DI), lambda b, c: (0, 0))],
        out_specs=[pl.BlockSpec((L, DI), lambda b, c: (b * nc + c, 0)), tile_spec, st_spec],
        out_shape=[jax.ShapeDtypeStruct((M, DI), BF16 if L % 16 == 0 else F32),
                   jax.ShapeDtypeStruct((B, SUBLANES, C), F32), jax.ShapeDtypeStruct(h0_t.shape, F32)],
        scratch_shapes=[pltpu.VMEM((SUBLANES, C), F32),
                        pltpu.VMEM((SSD_GROUPS, SSD_STATE, SSD_HPG * SSD_HEAD_DIM), F32)],
        compiler_params=_cparams(2), name="ssd_core",
    )(zx, zx, zx, zx, zx, dt, conv_tiles, h0_t, conv_w, conv_b.reshape(1, C), pad(dt_bias), pad(a_log),
      d_wide, norm_w.reshape(1, DI), expand)


def _fox_gate_kernel(f_ref, fb_ref, lf_ref, cum_ref, carry_scr):
    c = pl.program_id(1)
    tc = f_ref.shape[0]

    @pl.when(c == 0)
    def _():
        carry_scr[...] = jnp.zeros_like(carry_scr)
    lf = jax.nn.log_sigmoid(f_ref[...] + fb_ref[...])
    ri = lax.broadcasted_iota(jnp.int32, (tc, tc), 0)
    ci = lax.broadcasted_iota(jnp.int32, (tc, tc), 1)
    cum = jnp.dot((ri >= ci).astype(F32), lf, preferred_element_type=F32, precision=HI) + carry_scr[...]
    lf_ref[...] = lf
    cum_ref[...] = cum
    carry_scr[...] = cum[tc - 1:tc, :]


def _fox_gate(f_raw, f_b_pad, *, B, T, tc):
    M, W = f_raw.shape
    nc = T // tc
    row = pl.BlockSpec((tc, W), lambda b, c: (b * nc + c, 0))
    return pl.pallas_call(
        _fox_gate_kernel, grid=(B, nc),
        in_specs=[row, pl.BlockSpec((1, W), lambda b, c: (0, 0))],
        out_specs=[row, row],
        out_shape=[jax.ShapeDtypeStruct((M, W), F32)] * 2,
        scratch_shapes=[pltpu.VMEM((1, W), F32)],
        compiler_params=_cparams(2), name="fox_gate",
    )(f_raw, f_b_pad)


def _fox_flash_kernel(q_ref, k_ref, v_ref, cq_ref, ck_ref, o_ref, m_scr, l_scr, acc_scr, *, tq, tk, nk):
    qi, ki = pl.program_id(1), pl.program_id(2)
    H, Dh = FOX_HEADS, FOX_HEAD_DIM
    scale = Dh ** -0.5

    @pl.when(ki == 0)
    def _():
        m_scr[...] = jnp.full_like(m_scr, -jnp.inf)
        l_scr[...] = jnp.zeros_like(l_scr)
        acc_scr[...] = jnp.zeros_like(acc_scr)

    def compute(masked):
        if masked:
            rowp = qi * tq + lax.broadcasted_iota(jnp.int32, (tq, tk), 0)
            colp = ki * tk + lax.broadcasted_iota(jnp.int32, (tq, tk), 1)
            keep = colp <= rowp
        for h in range(H):
            sl = slice(h * Dh, (h + 1) * Dh)
            qh = (q_ref[:, sl] * scale).astype(BF16)
            s = _nt(qh, k_ref[:, sl].astype(BF16))
            s = s + cq_ref[:, h:h + 1] - ck_ref[h:h + 1, :]
            if masked:
                s = jnp.where(keep, s, -jnp.inf)
            m_prev = m_scr[h]
            m_new = jnp.maximum(m_prev, jnp.max(s, axis=-1, keepdims=True))
            p = jnp.exp(s - m_new)
            alpha = jnp.exp(m_prev - m_new)
            l_scr[h] = alpha * l_scr[h] + jnp.sum(p, axis=-1, keepdims=True)
            acc_scr[:, sl] = alpha * acc_scr[:, sl] + jnp.dot(p.astype(BF16), v_ref[:, sl].astype(BF16),
                                                              preferred_element_type=F32)
            m_scr[h] = m_new

    needed = ki * tk <= qi * tq + (tq - 1)
    diag = ki * tk + (tk - 1) > qi * tq

    @pl.when(needed & diag)
    def _():
        compute(True)

    @pl.when(needed & jnp.logical_not(diag))
    def _():
        compute(False)

    @pl.when(ki == nk - 1)
    def _():
        for h in range(H):
            sl = slice(h * Dh, (h + 1) * Dh)
            o_ref[:, sl] = (acc_scr[:, sl] / l_scr[h]).astype(o_ref.dtype)


def _fox_flash(q, k, v, cum, cum_t, *, B, T, tq, tk):
    M, D = q.shape
    nq, nk = T // tq, T // tk
    last_k = lambda qi: (qi * tq + tq - 1) // tk
    qrow = lambda b, qi, ki: (b * nq + qi, 0)
    krow = lambda b, qi, ki: (b * nk + jnp.minimum(ki, last_k(qi)), 0)
    return pl.pallas_call(
        functools.partial(_fox_flash_kernel, tq=tq, tk=tk, nk=nk),
        grid=(B, nq, nk),
        in_specs=[pl.BlockSpec((tq, D), qrow), pl.BlockSpec((tk, D), krow), pl.BlockSpec((tk, D), krow),
                  pl.BlockSpec((tq, cum.shape[1]), qrow),
                  pl.BlockSpec((None, FOX_HEADS, tk), lambda b, qi, ki: (b, 0, jnp.minimum(ki, last_k(qi))))],
        out_specs=pl.BlockSpec((tq, D), qrow),
        out_shape=jax.ShapeDtypeStruct((M, D), BF16),
        scratch_shapes=[pltpu.VMEM((FOX_HEADS, tq, 1), F32), pltpu.VMEM((FOX_HEADS, tq, 1), F32),
                        pltpu.VMEM((tq, D), F32)],
        compiler_params=_cparams(3), name="fox_flash",
    )(q, k, v, cum, cum_t)


def _fox_sample_kernel(pt_ref, q_ref, kn_ref, vn_ref, cn_ref, cnt_ref, kp_ref, vp_ref, lft_ref, o_ref,
                       qbd_scr, cq_scr, later_scr, m_scr, l_scr, acc_scr, *, n_pages):
    p = pl.program_id(1)
    H, Dh = FOX_HEADS, FOX_HEAD_DIM
    TS = q_ref.shape[0]
    R = H * TS
    D = H * Dh
    PG = kp_ref.shape[0]

    @pl.when(p == 0)
    def _():
        row = lax.broadcasted_iota(jnp.int32, (R, D), 0)
        lane = lax.broadcasted_iota(jnp.int32, (R, D), 1)
        q_rep = jnp.broadcast_to(q_ref[...][None], (H, TS, D)).reshape(R, D) * (Dh ** -0.5)
        qbd_scr[...] = jnp.where(lane // Dh == row // TS, q_rep, 0.0).astype(BF16)
        cw = cn_ref.shape[1]
        c_rep = jnp.broadcast_to(cn_ref[...][None], (H, TS, cw)).reshape(R, cw)
        r2 = lax.broadcasted_iota(jnp.int32, (R, cw), 0)
        l2 = lax.broadcasted_iota(jnp.int32, (R, cw), 1)
        cq_scr[...] = jnp.sum(jnp.where(l2 == r2 // TS, c_rep, 0.0), axis=-1, keepdims=True)
        later_scr[...] = jnp.zeros_like(later_scr)
        m_scr[...] = jnp.full_like(m_scr, -jnp.inf)
        l_scr[...] = jnp.zeros_like(l_scr)
        acc_scr[...] = jnp.zeros_like(acc_scr)

    def online(s, v_bf16):
        m_prev = m_scr[...]
        m_new = jnp.maximum(m_prev, jnp.max(s, axis=-1, keepdims=True))
        pr = jnp.exp(s - m_new)
        alpha = jnp.exp(m_prev - m_new)
        l_scr[...] = alpha * l_scr[...] + jnp.sum(pr, axis=-1, keepdims=True)
        acc_scr[...] = alpha * acc_scr[...] + jnp.dot(pr.astype(BF16), v_bf16, preferred_element_type=F32)
        m_scr[...] = m_new

    @pl.when(p < n_pages)
    def _():
        lft = lft_ref[...]
        si = lax.broadcasted_iota(jnp.int32, (PG, PG), 0)
        ki = lax.broadcasted_iota(jnp.int32, (PG, PG), 1)
        suf = jnp.dot(lft, (si > ki).astype(F32), preferred_element_type=F32, precision=HI) + later_scr[...]
        later_scr[...] = later_scr[...] + jnp.sum(lft, axis=-1, keepdims=True)
        bias = jnp.broadcast_to(suf[:, None, :], (H, TS, PG)).reshape(R, PG)
        s = _nt(qbd_scr[...], kp_ref[...].astype(BF16)) + cq_scr[...] + bias
        online(s, vp_ref[...].astype(BF16))

    @pl.when(p == n_pages)
    def _():
        cnt = cnt_ref[...]
        bias = jnp.broadcast_to(cnt[:, None, :], (H, TS, TS)).reshape(R, TS)
        s = _nt(qbd_scr[...], kn_ref[...].astype(BF16)) + cq_scr[...] - bias
        tq = lax.broadcasted_iota(jnp.int32, (R, TS), 0) % TS
        tk = lax.broadcasted_iota(jnp.int32, (R, TS), 1)
        s = jnp.where(tk <= tq, s, -jnp.inf)
        online(s, vn_ref[...].astype(BF16))
        row = lax.broadcasted_iota(jnp.int32, (R, D), 0)
        lane = lax.broadcasted_iota(jnp.int32, (R, D), 1)
        own = jnp.where(lane // Dh == row // TS, acc_scr[...] / l_scr[...], 0.0)
        o_ref[...] = jnp.sum(own.reshape(H, TS, D), axis=0)


def _fox_sample(q, k_new, v_new, cum_new, cum_new_t, cache_k, cache_v, cache_logf_t, page_table, *, B, TS):
    M, D = q.shape
    n_pages = page_table.shape[1]
    PG = cache_k.shape[1]
    H = FOX_HEADS
    R = H * TS
    row = pl.BlockSpec((TS, D), lambda b, p, pt: (b, 0))
    page = lambda b, p, pt: (pt[b * n_pages + (n_pages - 1 - jnp.minimum(p, n_pages - 1))], 0, 0)
    grid_spec = pltpu.PrefetchScalarGridSpec(
        num_scalar_prefetch=1, grid=(B, n_pages + 1),
        in_specs=[row, row, row,
                  pl.BlockSpec((TS, cum_new.shape[1]), lambda b, p, pt: (b, 0)),
                  pl.BlockSpec((None, H, TS), lambda b, p, pt: (b, 0, 0)),
                  pl.BlockSpec((None, PG, D), page), pl.BlockSpec((None, PG, D), page),
                  pl.BlockSpec((None, H, PG), page)],
        out_specs=row,
        scratch_shapes=[pltpu.VMEM((R, D), BF16), pltpu.VMEM((R, 1), F32), pltpu.VMEM((H, 1), F32),
                        pltpu.VMEM((R, 1), F32), pltpu.VMEM((R, 1), F32), pltpu.VMEM((R, D), F32)])
    return pl.pallas_call(
        functools.partial(_fox_sample_kernel, n_pages=n_pages),
        grid_spec=grid_spec,
        out_shape=jax.ShapeDtypeStruct((M, D), F32),
        compiler_params=_cparams(2), name="fox_sample",
    )(page_table.reshape(-1), q, k_new, v_new, cum_new, cum_new_t, cache_k, cache_v, cache_logf_t)


def _rwkv_in_kernel(x_ref, g_ref, sc_ref, sh_ref, st_ref, mu_ref, wrkv_ref, w0_ref, w1_ref, w2_ref, a0_ref,
                    a1_ref, a2_ref, g1_ref, g2_ref,
                    r_ref, k_ref, v_ref, lw_ref, a_ref, gg_ref, tail_ref, h_scr, d_scr, carry_scr, *, T):
    i, j = pl.program_id(0), pl.program_id(1)
    tm, D = x_ref.shape
    nt = tm // SUBLANES
    long_seq = T >= tm

    @pl.when(j == 0)
    def _():
        h = _modnorm(x_ref[...], g_ref[...], sc_ref[...], sh_ref[...], T)
        h3 = h.reshape(nt, SUBLANES, D)
        if long_seq:
            @pl.when(i % (T // tm) == 0)
            def _():
                carry_scr[...] = st_ref[...]
            prev3 = jnp.concatenate([carry_scr[...][None], h3[:nt - 1]], axis=0)
            carry_scr[...] = h3[nt - 1]
            tail_ref[...] = h3[nt - 1]
        else:
            prev3 = st_ref[...]
            tail_ref[...] = h3
        prev = _shift_rows(h3, prev3, 1).reshape(tm, D)
        h_scr[...] = h
        d_scr[...] = prev - h

    def mix(s):
        return (h_scr[...] + d_scr[...] * mu_ref[s:s + 1, :]).astype(BF16)

    for s, out in enumerate((r_ref, k_ref, v_ref)):
        @pl.when(j == s)
        def _(s=s, out=out):
            out[...] = jnp.dot(mix(s), wrkv_ref[...], preferred_element_type=F32)

    @pl.when(j == 3)
    def _():
        t = jnp.tanh(jnp.dot(mix(3), w1_ref[...], preferred_element_type=F32))
        pre_w = w0_ref[...] + jnp.dot(t.astype(BF16), w2_ref[...], preferred_element_type=F32)
        w_log = -jax.nn.softplus(-pre_w) - 0.5
        lw_ref[...] = -jnp.exp(w_log)
        u = jnp.dot(mix(4), a1_ref[...], preferred_element_type=F32)
        pre_a = a0_ref[...] + jnp.dot(u.astype(BF16), a2_ref[...], preferred_element_type=F32)
        a_ref[...] = jax.nn.sigmoid(pre_a)
        sg = jax.nn.sigmoid(jnp.dot(mix(5), g1_ref[...], preferred_element_type=F32))
        gg_ref[...] = jnp.dot(sg.astype(BF16), g2_ref[...], preferred_element_type=F32)


def _rwkv_in(x, g, sc, sh, st_tiles, mu, w_rkv, w0, w1, w2, a0, a1, a2, g1, g2, *, T, tm):
    M, D = x.shape
    B = st_tiles.shape[0]
    P = LANES
    c2 = lambda shape: pl.BlockSpec(shape, lambda i, j: (0,) * len(shape))
    row = pl.BlockSpec((tm, D), lambda i, j: (i, 0))
    return pl.pallas_call(
        functools.partial(_rwkv_in_kernel, T=T),
        grid=(M // tm, 4),
        in_specs=[row, c2((1, D)), _mod_spec(T, tm, D), _mod_spec(T, tm, D), _tile_spec(T, tm, D),
                  c2((6, D)), pl.BlockSpec((None, D, D), lambda i, j: (jnp.minimum(j, 2), 0, 0)),
                  c2((1, D)), c2((D, P)), c2((P, D)), c2((1, D)), c2((D, P)), c2((P, D)), c2((D, P)), c2((P, D))],
        out_specs=[row] * 6 + [_tile_spec(T, tm, D)],
        out_shape=[jax.ShapeDtypeStruct((M, D), F32)] * 6 + [jax.ShapeDtypeStruct((B, SUBLANES, D), F32)],
        scratch_shapes=[pltpu.VMEM((tm, D), F32), pltpu.VMEM((tm, D), F32), pltpu.VMEM((SUBLANES, D), F32)],
        compiler_params=_cparams(2),
        name="rwkv_in",
    )(x, g.reshape(1, D), sc, sh, st_tiles, mu, w_rkv, w0.reshape(1, D), w1, w2, a0.reshape(1, D), a1, a2, g1, g2)


def _wkv_kernel(r_ref, k_ref, v_ref, lw_ref, a_ref, g_ref, kk_ref, ka_ref, rk_ref, lnw_ref, lnb_ref, s0_ref,
                o_ref, sout_ref, s_scr, *, L, n_chunks):
    c = pl.program_id(1)
    H, N = RWKV_HEADS, RWKV_HEAD
    G = WKV_ROWS // L
    R = G * L

    @pl.when(c == 0)
    def _():
        s_scr[...] = s0_ref[...]

    ri = lax.broadcasted_iota(jnp.int32, (L, L), 0)
    ci = lax.broadcasted_iota(jnp.int32, (L, L), 1)
    lw = lw_ref[...]
    cs = jnp.dot((ri >= ci).astype(F32), lw, preferred_element_type=F32, precision=HI)
    e_pos = jnp.exp(cs)
    e_neg = jnp.exp(-cs)
    e_prev = jnp.exp(cs - lw)

    row = lax.broadcasted_iota(jnp.int32, (R, R), 0)
    col = lax.broadcasted_iota(jnp.int32, (R, R), 1)
    same = (row // L) == (col // L)
    strict = same & (row > col)
    incl = same & (row >= col)

    for grp in range(H // G):
        heads = range(grp * G, (grp + 1) * G)
        A_l, R_l, B_l, K_l, V_l, kp_l = [], [], [], [], [], []
        for h in heads:
            sl = slice(h * N, (h + 1) * N)
            kh, ah = k_ref[:, sl], a_ref[:, sl]
            kkh = kh * kk_ref[:, sl]
            nrm = jnp.sqrt(jnp.sum(kkh * kkh, axis=-1, keepdims=True))
            kkh = kkh / jnp.maximum(nrm, 1e-12)
            kph = kh * (1.0 + (ah - 1.0) * ka_ref[:, sl])
            en = e_neg[:, sl]
            A_l.append(-kkh * e_prev[:, sl])
            R_l.append(r_ref[:, sl] * e_pos[:, sl])
            B_l.append(kkh * ah * en)
            K_l.append(kph * en)
            V_l.append(v_ref[:, sl])
            kp_l.append(kph)
        AR = jnp.concatenate(A_l + R_l, axis=0).astype(BF16)
        BK = jnp.concatenate(B_l + K_l, axis=0).astype(BF16)
        V4 = jnp.concatenate(V_l, axis=0)
        Q = _nt(AR, BK)
        ars_a, ars_r = [], []
        for q, h in enumerate(heads):
            Sb = s_scr[h].astype(BF16)
            ars_a.append(_nt(AR[q * L:(q + 1) * L], Sb))
            ars_r.append(_nt(AR[R + q * L:R + (q + 1) * L], Sb))
        nmat = jnp.where(strict, Q[:R, :R], 0.0)
        ak = jnp.where(strict, Q[:R, R:], 0.0)
        rhs = jnp.concatenate(ars_a, axis=0) + jnp.dot(ak.astype(BF16), V4.astype(BF16), preferred_element_type=F32)
        Y = jnp.where(((row // 2) == (col // 2)) & (row > col), nmat, 0.0)
        b = 2
        while b < L:
            Ck = jnp.where(((row // (2 * b)) == (col // (2 * b))) & ((row % (2 * b)) >= b) & ((col % (2 * b)) < b),
                           nmat, 0.0)
            Yb = Y.astype(BF16)
            Tm = Ck + jnp.dot(Yb, Ck.astype(BF16), preferred_element_type=F32)
            Y = Y + Tm + jnp.dot(Tm.astype(BF16), Yb, preferred_element_type=F32)
            b *= 2
        E = rhs + jnp.dot(Y.astype(BF16), rhs.astype(BF16), preferred_element_type=F32)
        EV = jnp.concatenate([E, V4], axis=0).astype(BF16)
        rbk = jnp.concatenate([jnp.where(incl, Q[R:, :R], 0.0), jnp.where(incl, Q[R:, R:], 0.0)], axis=1)
        y4 = jnp.concatenate(ars_r, axis=0) + jnp.dot(rbk.astype(BF16), EV, preferred_element_type=F32)
        for q, h in enumerate(heads):
            sl = slice(h * N, (h + 1) * N)
            rows = slice(q * L, (q + 1) * L)
            ev_h = jnp.concatenate([EV[rows], EV[R + q * L:R + (q + 1) * L]], axis=0)
            bk_h = jnp.concatenate([BK[rows], BK[R + q * L:R + (q + 1) * L]], axis=0)
            s_scr[h] = (s_scr[h] + _tn(ev_h, bk_h)) * e_pos[L - 1:L, sl]
            y = y4[rows]
            mean = jnp.mean(y, axis=-1, keepdims=True)
            var = jnp.mean(jnp.square(y - mean), axis=-1, keepdims=True)
            yn = (y - mean) * lax.rsqrt(var + RWKV_GN_EPS) * lnw_ref[:, sl] + lnb_ref[:, sl]
            bonus = jnp.sum(r_ref[:, sl] * kp_l[q] * rk_ref[:, sl], axis=-1, keepdims=True) * V_l[q]
            o_ref[:, sl] = (yn + bonus) * g_ref[:, sl]

    @pl.when(c == n_chunks - 1)
    def _():
        sout_ref[...] = s_scr[...]


def _wkv_core(r, k, v, lw, a, g, k_k, k_a, r_k, ln_w, ln_b, s0, *, T, L):
    M, D = r.shape
    B = s0.shape[0]
    nc = T // L
    row = pl.BlockSpec((L, D), lambda b, c: (b * nc + c, 0))
    par = pl.BlockSpec((1, D), lambda b, c: (0, 0))
    st = pl.BlockSpec((None, RWKV_HEADS, RWKV_HEAD, RWKV_HEAD), lambda b, c: (b, 0, 0, 0))
    v1 = lambda t: t.reshape(1, D)
    return pl.pallas_call(
        functools.partial(_wkv_kernel, L=L, n_chunks=nc),
        grid=(B, nc),
        in_specs=[row] * 6 + [par] * 5 + [st],
        out_specs=[row, st],
        out_shape=[jax.ShapeDtypeStruct((M, D), F32), jax.ShapeDtypeStruct(s0.shape, F32)],
        scratch_shapes=[pltpu.VMEM((RWKV_HEADS, RWKV_HEAD, RWKV_HEAD), F32)],
        compiler_params=_cparams(2),
        name="wkv_core",
    )(r, k, v, lw, a, g, v1(k_k), v1(k_a), v1(r_k), v1(ln_w), v1(ln_b), s0)


def _pad_cols(w, n):
    return jnp.pad(w, ((0, 0), (0, n - w.shape[1])))


def _pad_rows(w, n):
    return jnp.pad(w, ((0, n - w.shape[0]), (0, 0)))


def _state_tiles(st):
    return jnp.pad(st, ((0, 0), (SUBLANES - st.shape[1], 0), (0, 0)))


def _ssm_to_groups(s):
    B = s.shape[0]
    s = s.reshape(B, SSD_GROUPS, SSD_HPG, SSD_HEAD_DIM, SSD_STATE)
    return s.transpose(0, 1, 4, 2, 3).reshape(B, SSD_GROUPS, SSD_STATE, SSD_HPG * SSD_HEAD_DIM)


def _ssm_from_groups(s):
    B = s.shape[0]
    s = s.reshape(B, SSD_GROUPS, SSD_STATE, SSD_HPG, SSD_HEAD_DIM)
    return s.transpose(0, 1, 3, 4, 2).reshape(B, SSD_HEADS, SSD_HEAD_DIM, SSD_STATE)


def kernel(x_prompt, x_sample, c_prompt, c_sample, state_ssm, state_ssd_conv, cache_k, cache_v, cache_logf, page_table, state_wkv, state_shift, state_ffn_conv, ada_w, ada_b, norm_mix, norm_ffn, norm_final, ssd_in_w, ssd_conv_w, ssd_conv_b, ssd_dt_bias, ssd_a_log, ssd_d, ssd_norm_w, ssd_out_w, fox_in_w, fox_f_b, fox_out_w, rwkv_mu, rwkv_w_rkv, rwkv_w0, rwkv_w1, rwkv_w2, rwkv_a0, rwkv_a1, rwkv_a2, rwkv_g1, rwkv_g2, rwkv_k_k, rwkv_k_a, rwkv_r_k, rwkv_ln_w, rwkv_ln_b, rwkv_out_w, ffn_up_w, ffn_conv_w, ffn_conv_b, ffn_down_w):
    BP, TP, D = x_prompt.shape
    BS, TS, _ = x_sample.shape
    groups = [dict(B=BP, T=TP, tm=1024, tm_s=512, ssd_l=SSD_CHUNK, wkv_l=RWKV_CHUNK, gate_tc=512),
              dict(B=BS, T=TS, tm=BS * TS, tm_s=512, ssd_l=TS, wkv_l=TS, gate_tc=TS)]
    xs = [x_prompt.reshape(BP * TP, D), x_sample.reshape(BS * TS, D)]

    n_c = BP + BS
    n_c_pad = -(-n_c // SUBLANES) * SUBLANES
    c_all = jnp.pad(jnp.concatenate([c_prompt, c_sample], axis=0), ((0, n_c_pad - n_c), (0, 0)))
    mods_all = _ada_mods(c_all, ada_w, ada_b).reshape(DEPTH, n_c_pad, 6, 1, D)
    row0 = [0, BP]

    def mods_of(i, gi):
        m = mods_all[i, row0[gi]:row0[gi] + groups[gi]["B"]]
        return [m[:, k] for k in range(6)]

    outs = {k: ([], []) for k in ("ssm", "sconv", "k", "v", "lf", "wkv", "shift", "fconv")}
    n_pool = cache_k.shape[1]

    for i in range(DEPTH):
        kind, j = i % N_MIXERS, i // N_MIXERS
        for gi, gp in enumerate(groups):
            B, T, tm, tm_s = gp["B"], gp["T"], gp["tm"], gp["tm_s"]
            x = xs[gi]
            prompt = gi == 0
            sh_m, sc_m, gt_m, sh_f, sc_f, gt_f = mods_of(i, gi)
            if kind == 0:
                w_main = ssd_in_w[j][:, :SSD_D_INNER + SSD_CONV_CH].astype(BF16)
                w_dt = _pad_cols(ssd_in_w[j][:, SSD_D_INNER + SSD_CONV_CH:], LANES).astype(BF16)
                zx = _norm_mod_matmul(x, norm_mix[i], sc_m, sh_m, w_main, T=T, tm=tm, tn=1024)
                dt = _norm_mod_matmul(x, norm_mix[i], sc_m, sh_m, w_dt, T=T, tm=tm, tn=LANES)
                if prompt:
                    conv_tiles = jnp.zeros((B, SUBLANES, SSD_CONV_CH), F32)
                    h0 = jnp.zeros((B, SSD_GROUPS, SSD_STATE, SSD_HPG * SSD_HEAD_DIM), F32)
                else:
                    conv_tiles = _state_tiles(state_ssd_conv[j])
                    h0 = _ssm_to_groups(state_ssm[j])
                y, ctail, h_t = _ssd_core(zx, dt, conv_tiles, h0, ssd_conv_w[j], ssd_conv_b[j], ssd_dt_bias[j],
                                          ssd_a_log[j], ssd_d[j], ssd_norm_w[j], B=B, T=T, L=gp["ssd_l"])
                outs["ssm"][gi].append(_ssm_from_groups(h_t))
                outs["sconv"][gi].append(ctail[:, SUBLANES - (SSD_CONV - 1):])
                x = _res_matmul(y, ssd_out_w[j].astype(BF16), x, gt_m, T=T, tm=tm)
            elif kind == 1:
                w_qkv = fox_in_w[j][:, :3 * FOX_INNER].reshape(D, 3, FOX_INNER).transpose(1, 0, 2).astype(BF16)
                w_f = _pad_cols(fox_in_w[j][:, 3 * FOX_INNER:], LANES).astype(BF16)
                q, k, v = _norm_mod_matmul_multi(x, norm_mix[i], sc_m, sh_m, w_qkv, T=T, tm=tm_s)
                f = _norm_mod_matmul(x, norm_mix[i], sc_m, sh_m, w_f, T=T, tm=tm, tn=LANES)
                f_b = jnp.pad(fox_f_b[j], (0, LANES - FOX_HEADS)).reshape(1, LANES)
                lf, cum = _fox_gate(f, f_b, B=B, T=T, tc=gp["gate_tc"])
                cum_t = jnp.swapaxes(cum[:, :FOX_HEADS].reshape(B, T, FOX_HEADS), 1, 2)
                if prompt:
                    o = _fox_flash(q, k, v, cum, cum_t, B=B, T=T, tq=512, tk=512)
                else:
                    page = cache_k.shape[2]
                    o = _fox_sample(q, k, v, cum, cum_t, cache_k[j].reshape(n_pool, page, FOX_INNER),
                                    cache_v[j].reshape(n_pool, page, FOX_INNER),
                                    jnp.swapaxes(cache_logf[j], 1, 2), page_table, B=B, TS=T)
                hd = lambda t: t.reshape(B, T, FOX_HEADS, FOX_HEAD_DIM)
                outs["k"][gi].append(hd(k))
                outs["v"][gi].append(hd(v))
                outs["lf"][gi].append(lf[:, :FOX_HEADS].reshape(B, T, FOX_HEADS))
                x = _res_matmul(o, fox_out_w[j].astype(BF16), x, gt_m, T=T, tm=tm)
            else:
                if prompt:
                    st_tiles = jnp.zeros((B, SUBLANES, D), F32)
                    s0 = jnp.zeros((B, RWKV_HEADS, RWKV_HEAD, RWKV_HEAD), F32)
                else:
                    st_tiles = _state_tiles(state_shift[j][:, None, :])
                    s0 = state_wkv[j]
                r, k, v, lw, a, g, tails = _rwkv_in(
                    x, norm_mix[i], sc_m, sh_m, st_tiles, rwkv_mu[j], rwkv_w_rkv[j].astype(BF16), rwkv_w0[j],
                    _pad_cols(rwkv_w1[j], LANES).astype(BF16), _pad_rows(rwkv_w2[j], LANES).astype(BF16), rwkv_a0[j],
                    _pad_cols(rwkv_a1[j], LANES).astype(BF16), _pad_rows(rwkv_a2[j], LANES).astype(BF16),
                    rwkv_g1[j].astype(BF16), rwkv_g2[j].astype(BF16), T=T, tm=tm_s)
                o, s_new = _wkv_core(r, k, v, lw, a, g, rwkv_k_k[j], rwkv_k_a[j], rwkv_r_k[j].reshape(-1),
                                     rwkv_ln_w[j], rwkv_ln_b[j], s0, T=T, L=gp["wkv_l"])
                outs["wkv"][gi].append(s_new)
                outs["shift"][gi].append(tails[:, SUBLANES - 1])
                x = _res_matmul(o, rwkv_out_w[j].astype(BF16), x, gt_m, T=T, tm=tm)
            if prompt:
                ffn_tiles = jnp.zeros((B, SUBLANES, D_FF), F32)
            else:
                ffn_tiles = _state_tiles(state_ffn_conv[i])
            x, tails = _conv_ffn(x, norm_ffn[i], sc_f, sh_f, gt_f, ffn_tiles, ffn_up_w[i].astype(BF16),
                                 ffn_conv_w[i], ffn_conv_b[i], ffn_down_w[i].astype(BF16), T=T, tm=tm_s)
            outs["fconv"][gi].append(tails[:, SUBLANES - (FFN_CONV - 1):])
            xs[gi] = x

    y_prompt = _rmsnorm(xs[0], norm_final, tm=1024).reshape(BP, TP, D)
    y_sample = _rmsnorm(xs[1], norm_final, tm=1024).reshape(BS, TS, D)
    st = lambda name, gi: jnp.stack(outs[name][gi])
    return (y_prompt, y_sample, st("ssm", 0), st("ssm", 1), st("sconv", 0), st("sconv", 1),
            st("k", 0), st("k", 1), st("v", 0), st("v", 1), st("lf", 0), st("lf", 1),
            st("wkv", 0), st("wkv", 1), st("shift", 0), st("shift", 1), st("fconv", 0), st("fconv", 1))
```

```python
import functools

import jax
import jax.numpy as jnp
from jax import lax
from jax.experimental import pallas as pl
from jax.experimental.pallas import tpu as pltpu

F32 = jnp.float32
BF16 = jnp.bfloat16
HI = lax.Precision.HIGHEST

D_MODEL = 1024
DEPTH = 4
N_MIXERS = 3
RMS_EPS = 1e-6
SSD_D_INNER = 2048
SSD_HEAD_DIM = 64
SSD_HEADS = 32
SSD_GROUPS = 4
SSD_HPG = 8
SSD_STATE = 128
SSD_CONV = 4
SSD_CONV_CH = 3072
SSD_CHUNK = 128
FOX_HEAD_DIM = 64
FOX_HEADS = 16
FOX_INNER = 1024
RWKV_HEAD = 64
RWKV_HEADS = 16
RWKV_GN_EPS = 64e-5
RWKV_CHUNK = 32
WKV_ROWS = 128
WKV_GROUP_HEADS = 4
WKV_STEP_CHUNKS = 4
FOX_PAGES_PER_STEP = 4
D_FF = 2816
FFN_CONV = 3

SUBLANES = 8
LANES = 128
VMEM_LIMIT = 56 * 1024 * 1024


def _cparams(n_grid):
    return pltpu.CompilerParams(dimension_semantics=("arbitrary",) * n_grid,
                                vmem_limit_bytes=VMEM_LIMIT)


def _nt(a, b):
    return lax.dot_general(a, b, (((1,), (1,)), ((), ())), preferred_element_type=F32)


def _tn(a, b):
    return lax.dot_general(a, b, (((0,), (0,)), ((), ())), preferred_element_type=F32)


def _modnorm(x, g, sc, sh, T):
    xn = x * lax.rsqrt(jnp.mean(x * x, axis=-1, keepdims=True) + RMS_EPS) * g
    if sc.ndim == 2:
        return xn * (1.0 + sc) + sh
    tm, D = x.shape
    x3 = xn.reshape(tm // T, T, D)
    return (x3 * (1.0 + sc) + sh).reshape(tm, D)


def _gate_mul(y, gt, T):
    if gt.ndim == 2:
        return y * gt
    tm, D = y.shape
    return (y.reshape(tm // T, T, D) * gt).reshape(tm, D)


def _shift_rows(cur3, prev3, k):
    row = lax.broadcasted_iota(jnp.int32, cur3.shape, 1)
    return jnp.where(row >= k, pltpu.roll(cur3, k, 1), pltpu.roll(prev3, k, 1))


def _mod_spec(T, tm, D):
    if T >= tm:
        per = T // tm
        return pl.BlockSpec((None, 1, D), lambda i, *_: (i // per, 0, 0))
    return pl.BlockSpec((tm // T, 1, D), lambda i, *_: (i, 0, 0))


def _tile_spec(T, tm, C):
    if T >= tm:
        per = T // tm
        return pl.BlockSpec((None, SUBLANES, C), lambda i, *_: (i // per, 0, 0))
    return pl.BlockSpec((tm // T, SUBLANES, C), lambda i, *_: (i, 0, 0))


def _ada_kernel(c_ref, w_ref, b_ref, o_ref):
    c = c_ref[...]
    a = (c * jax.nn.sigmoid(c)).astype(BF16)
    o_ref[...] = jnp.dot(a, w_ref[...].astype(BF16), preferred_element_type=F32) + b_ref[...]


def _ada_mods(c_all, ada_w, ada_b):
    R, D = c_all.shape
    L, _, N = ada_w.shape
    tn = 1536
    return pl.pallas_call(
        _ada_kernel,
        grid=(L, N // tn),
        in_specs=[pl.BlockSpec((R, D), lambda l, j: (0, 0)),
                  pl.BlockSpec((None, D, tn), lambda l, j: (l, 0, j)),
                  pl.BlockSpec((None, 1, tn), lambda l, j: (l, 0, j))],
        out_specs=pl.BlockSpec((None, R, tn), lambda l, j: (l, 0, j)),
        out_shape=jax.ShapeDtypeStruct((L, R, N), F32),
        compiler_params=_cparams(2),
        name="ada_mods",
    )(c_all, ada_w, ada_b.reshape(L, 1, N))


def _nmm_kernel(x_ref, g_ref, sc_ref, sh_ref, w_ref, o_ref, h_scr, *, T):
    @pl.when(pl.program_id(1) == 0)
    def _():
        h_scr[...] = _modnorm(x_ref[...], g_ref[...], sc_ref[...], sh_ref[...], T).astype(BF16)
    o_ref[...] = jnp.dot(h_scr[...], w_ref[...], preferred_element_type=F32).astype(o_ref.dtype)


def _norm_mod_matmul(x, g, sc, sh, w, *, T, tm, tn):
    M, D = x.shape
    N = w.shape[1]
    return pl.pallas_call(
        functools.partial(_nmm_kernel, T=T),
        grid=(M // tm, N // tn),
        in_specs=[pl.BlockSpec((tm, D), lambda i, j: (i, 0)),
                  pl.BlockSpec((1, D), lambda i, j: (0, 0)),
                  _mod_spec(T, tm, D), _mod_spec(T, tm, D),
                  pl.BlockSpec((D, tn), lambda i, j: (0, j))],
        out_specs=pl.BlockSpec((tm, tn), lambda i, j: (i, j)),
        out_shape=jax.ShapeDtypeStruct((M, N), F32),
        scratch_shapes=[pltpu.VMEM((tm, D), BF16)],
        compiler_params=_cparams(2),
        name="norm_mod_matmul",
    )(x, g.reshape(1, D), sc, sh, w)


def _nmm_multi_kernel(x_ref, g_ref, sc_ref, sh_ref, w_ref, *rest, T, n_out):
    outs, h_scr = rest[:n_out], rest[n_out]
    j = pl.program_id(1)

    @pl.when(j == 0)
    def _():
        h_scr[...] = _modnorm(x_ref[...], g_ref[...], sc_ref[...], sh_ref[...], T).astype(BF16)
    res = jnp.dot(h_scr[...], w_ref[...], preferred_element_type=F32)
    for o in range(n_out):
        @pl.when(j == o)
        def _(o=o):
            outs[o][...] = res


def _norm_mod_matmul_multi(x, g, sc, sh, w_stack, *, T, tm):
    M, D = x.shape
    n_out, _, N = w_stack.shape
    return pl.pallas_call(
        functools.partial(_nmm_multi_kernel, T=T, n_out=n_out),
        grid=(M // tm, n_out),
        in_specs=[pl.BlockSpec((tm, D), lambda i, j: (i, 0)),
                  pl.BlockSpec((1, D), lambda i, j: (0, 0)),
                  _mod_spec(T, tm, D), _mod_spec(T, tm, D),
                  pl.BlockSpec((None, D, N), lambda i, j: (j, 0, 0))],
        out_specs=[pl.BlockSpec((tm, N), lambda i, j: (i, 0)) for _ in range(n_out)],
        out_shape=[jax.ShapeDtypeStruct((M, N), F32) for _ in range(n_out)],
        scratch_shapes=[pltpu.VMEM((tm, D), BF16)],
        compiler_params=_cparams(2),
        name="norm_mod_matmul_multi",
    )(x, g.reshape(1, D), sc, sh, w_stack)


def _res_mm_kernel(a_ref, w_ref, x_ref, gt_ref, o_ref, *, T):
    y = jnp.dot(a_ref[...].astype(BF16), w_ref[...], preferred_element_type=F32)
    o_ref[...] = x_ref[...] + _gate_mul(y, gt_ref[...], T)


def _res_matmul(a, w, x, gt, *, T, tm):
    M, K = a.shape
    D = w.shape[1]
    return pl.pallas_call(
        functools.partial(_res_mm_kernel, T=T),
        grid=(M // tm,),
        in_specs=[pl.BlockSpec((tm, K), lambda i: (i, 0)),
                  pl.BlockSpec((K, D), lambda i: (0, 0)),
                  pl.BlockSpec((tm, D), lambda i: (i, 0)),
                  _mod_spec(T, tm, D)],
        out_specs=pl.BlockSpec((tm, D), lambda i: (i, 0)),
        out_shape=jax.ShapeDtypeStruct((M, D), F32),
        compiler_params=_cparams(1),
        name="res_matmul",
    )(a, w, x, gt)


def _ffn_kernel(x_ref, g_ref, sc_ref, sh_ref, gt_ref, st_ref, upw_ref, cw_ref, cb_ref, dnw_ref,
                o_ref, tail_ref, act_scr, carry_scr, *, T, F, fc):
    i = pl.program_id(0)
    tm, D = x_ref.shape
    nt = tm // SUBLANES
    long_seq = T >= tm
    x = x_ref[...]
    h = _modnorm(x, g_ref[...], sc_ref[...], sh_ref[...], T).astype(BF16)
    if long_seq:
        @pl.when(i % (T // tm) == 0)
        def _():
            carry_scr[...] = st_ref[...]
    for c in range(F // fc):
        lo, hi = c * fc, (c + 1) * fc
        gc = jnp.dot(h, upw_ref[:, lo:hi], preferred_element_type=F32)
        uc = jnp.dot(h, upw_ref[:, F + lo:F + hi], preferred_element_type=F32)
        g3 = gc.reshape(nt, SUBLANES, fc)
        if long_seq:
            prev3 = jnp.concatenate([carry_scr[:, lo:hi][None], g3[:nt - 1]], axis=0)
            carry_scr[:, lo:hi] = g3[nt - 1]
            tail_ref[:, lo:hi] = g3[nt - 1]
        else:
            prev3 = st_ref[:, :, lo:hi]
            tail_ref[:, :, lo:hi] = g3
        conv = (g3 * cw_ref[2:3, lo:hi] + _shift_rows(g3, prev3, 1) * cw_ref[1:2, lo:hi]
                + _shift_rows(g3, prev3, 2) * cw_ref[0:1, lo:hi] + cb_ref[:, lo:hi])
        act = conv * jax.nn.sigmoid(conv) * uc.reshape(nt, SUBLANES, fc)
        act_scr[:, lo:hi] = act.reshape(tm, fc).astype(BF16)
    y = jnp.dot(act_scr[...], dnw_ref[...], preferred_element_type=F32)
    o_ref[...] = x + _gate_mul(y, gt_ref[...], T)


def _conv_ffn(x, g, sc, sh, gt, st_tiles, up_w, conv_w, conv_b, down_w, *, T, tm):
    M, D = x.shape
    F = down_w.shape[0]
    B = st_tiles.shape[0]
    return pl.pallas_call(
        functools.partial(_ffn_kernel, T=T, F=F, fc=256),
        grid=(M // tm,),
        in_specs=[pl.BlockSpec((tm, D), lambda i: (i, 0)),
                  pl.BlockSpec((1, D), lambda i: (0, 0)),
                  _mod_spec(T, tm, D), _mod_spec(T, tm, D), _mod_spec(T, tm, D),
                  _tile_spec(T, tm, F),
                  pl.BlockSpec((D, 2 * F), lambda i: (0, 0)),
                  pl.BlockSpec((FFN_CONV, F), lambda i: (0, 0)),
                  pl.BlockSpec((1, F), lambda i: (0, 0)),
                  pl.BlockSpec((F, D), lambda i: (0, 0))],
        out_specs=[pl.BlockSpec((tm, D), lambda i: (i, 0)), _tile_spec(T, tm, F)],
        out_shape=[jax.ShapeDtypeStruct((M, D), F32), jax.ShapeDtypeStruct((B, SUBLANES, F), F32)],
        scratch_shapes=[pltpu.VMEM((tm, F), BF16), pltpu.VMEM((SUBLANES, F), F32)],
        compiler_params=_cparams(1),
        name="conv_ffn",
    )(x, g.reshape(1, D), sc, sh, gt, st_tiles, up_w, conv_w, conv_b.reshape(1, F), down_w)


def _rms_kernel(x_ref, g_ref, o_ref):
    x = x_ref[...]
    o_ref[...] = x * lax.rsqrt(jnp.mean(x * x, axis=-1, keepdims=True) + RMS_EPS) * g_ref[...]


def _rmsnorm(x, g, *, tm):
    M, D = x.shape
    return pl.pallas_call(
        _rms_kernel, grid=(M // tm,),
        in_specs=[pl.BlockSpec((tm, D), lambda i: (i, 0)), pl.BlockSpec((1, D), lambda i: (0, 0))],
        out_specs=pl.BlockSpec((tm, D), lambda i: (i, 0)),
        out_shape=jax.ShapeDtypeStruct((M, D), F32),
        compiler_params=_cparams(1), name="final_rmsnorm",
    )(x, g.reshape(1, D))


def _conv_silu(cur, prev_tile, w_ref, b_ref, lo, hi):
    L, C = cur.shape
    nt = L // SUBLANES
    c3 = cur.reshape(nt, SUBLANES, C)
    p3 = prev_tile[None] if nt == 1 else jnp.concatenate([prev_tile[None], c3[:nt - 1]], axis=0)
    out = c3 * w_ref[3:4, lo:hi] + b_ref[:, lo:hi]
    for k in range(1, SSD_CONV):
        out = out + _shift_rows(c3, p3, k) * w_ref[3 - k:4 - k, lo:hi]
    out = out * jax.nn.sigmoid(out)
    return out.reshape(L, C)


def _ssd_kernel(z0_ref, z1_ref, x0_ref, x1_ref, bc_ref, dt_ref, cst_ref, h0_ref, cw_ref, cb_ref, dtb_ref, alog_ref,
                dw_ref, nw_ref, ex_ref,
                y_ref, ctail_ref, hout_ref, tail_scr, st_scr, *, L, n_chunks, natural):
    c = pl.program_id(1)
    P, Nn, G, J = SSD_HEAD_DIM, SSD_STATE, SSD_GROUPS, SSD_HPG
    GW = J * P

    @pl.when(c == 0)
    def _():
        tail_scr[...] = cst_ref[...]
        st_scr[...] = h0_ref[...]

    conv = []
    for q, blk_ref in enumerate((x0_ref, x1_ref, bc_ref)):
        blk = blk_ref[...]
        lo, hi = q * 1024, (q + 1) * 1024
        conv.append(_conv_silu(blk, tail_scr[:, lo:hi], cw_ref, cb_ref, lo, hi))
        tail_scr[:, lo:hi] = blk[L - SUBLANES:, :]
        ctail_ref[:, lo:hi] = blk[L - SUBLANES:, :]
    xs = jnp.concatenate(conv[:2], axis=1)
    Bm = conv[2][:, :G * Nn]
    Cm = conv[2][:, G * Nn:]

    dt = jax.nn.softplus(dt_ref[...] + dtb_ref[...])
    a = dt * (-jnp.exp(alog_ref[...]))
    ri = lax.broadcasted_iota(jnp.int32, (L, L), 0)
    ci = lax.broadcasted_iota(jnp.int32, (L, L), 1)
    causal = ri >= ci
    acs = jnp.dot(causal.astype(F32), a, preferred_element_type=F32, precision=HI)
    ex = ex_ref[...]
    dt_w = jnp.dot(dt, ex, preferred_element_type=F32, precision=HI)
    acs_w = jnp.dot(acs, ex, preferred_element_type=F32, precision=HI)
    acs_t = acs.T
    last_w = acs_w[L - 1:L, :]
    X = xs * dt_w
    Xd = X * jnp.exp(last_w - acs_w)
    dstart = jnp.exp(acs_w)
    cdec = jnp.exp(last_w)
    zz = jnp.concatenate([z0_ref[...], z1_ref[...]], axis=1)
    zg = zz * jax.nn.sigmoid(zz)
    dwide = dw_ref[...]
    nw = nw_ref[...]
    Xb = X.astype(BF16)
    for g in range(G):
        gl = slice(g * GW, (g + 1) * GW)
        Bg = Bm[:, g * Nn:(g + 1) * Nn].astype(BF16)
        Cg = Cm[:, g * Nn:(g + 1) * Nn].astype(BF16)
        CB = _nt(Cg, Bg)
        st = st_scr[g]
        if natural:
            y_off = _nt(Cg, st.astype(BF16)) * dstart[:, gl]
            dec_h = jnp.exp(acs_t[g * J:(g + 1) * J, L - 1:L])
            dcol = jnp.broadcast_to(dec_h[:, None, :], (J, P, 1)).reshape(GW, 1)
            st_scr[g] = st * dcol + _tn(Xd[:, gl].astype(BF16), Bg)
        else:
            y_off = jnp.dot(Cg, st.astype(BF16), preferred_element_type=F32) * dstart[:, gl]
            st_scr[g] = st * cdec[:, gl] + _tn(Bg, Xd[:, gl].astype(BF16))
        ys = []
        for j in range(J):
            h = g * J + j
            diff = acs[:, h:h + 1] - acs_t[h:h + 1, :]
            dec = jnp.exp(jnp.where(causal, diff, -jnp.inf))
            Mh = (CB * dec).astype(BF16)
            ys.append(jnp.dot(Mh, Xb[:, h * P:(h + 1) * P], preferred_element_type=F32))
        yg = jnp.concatenate(ys, axis=1) + y_off + dwide[:, gl] * xs[:, gl]
        yg = yg * zg[:, gl]
        yg = yg * lax.rsqrt(jnp.mean(yg * yg, axis=-1, keepdims=True) + RMS_EPS) * nw[:, gl]
        y_ref[:, gl] = yg.astype(y_ref.dtype)

    @pl.when(c == n_chunks - 1)
    def _():
        hout_ref[...] = st_scr[...]


def _ssd_core(zx, dt, conv_tiles, h0, conv_w, conv_b, dt_bias, a_log, d_skip, norm_w, *, B, T, L, natural):
    M = zx.shape[0]
    nc = T // L
    DI, C = SSD_D_INNER, SSD_CONV_CH
    col = lambda q: pl.BlockSpec((L, 1024), lambda b, c, q=q: (b * nc + c, q))
    par = lambda n: pl.BlockSpec((1, n), lambda b, c: (0, 0))
    pad = lambda v: jnp.pad(v.astype(F32), (0, LANES - v.shape[0])).reshape(1, LANES)
    expand = (jnp.arange(LANES)[:, None] == (jnp.arange(DI) // SSD_HEAD_DIM)[None, :]).astype(F32)
    d_wide = jnp.repeat(d_skip.astype(F32), SSD_HEAD_DIM).reshape(1, DI)
    st_shape = h0.shape[1:]
    st_spec = pl.BlockSpec((None,) + st_shape, lambda b, c: (b, 0, 0, 0))
    tile_spec = pl.BlockSpec((None, SUBLANES, C), lambda b, c: (b, 0, 0))
    return pl.pallas_call(
        functools.partial(_ssd_kernel, L=L, n_chunks=nc, natural=natural),
        grid=(B, nc),
        in_specs=[col(0), col(1), col(2), col(3), col(4),
                  pl.BlockSpec((L, LANES), lambda b, c: (b * nc + c, 0)),
                  tile_spec, st_spec,
                  pl.BlockSpec((SSD_CONV, C), lambda b, c: (0, 0)), par(C), par(LANES), par(LANES),
                  par(DI), par(DI), pl.BlockSpec((LANES, DI), lambda b, c: (0, 0))],
        out_specs=[pl.BlockSpec((L, DI), lambda b, c: (b * nc + c, 0)), tile_spec, st_spec],
        out_shape=[jax.ShapeDtypeStruct((M, DI), BF16 if L % 16 == 0 else F32),
                   jax.ShapeDtypeStruct((B, SUBLANES, C), F32), jax.ShapeDtypeStruct(h0.shape, F32)],
        scratch_shapes=[pltpu.VMEM((SUBLANES, C), F32), pltpu.VMEM(st_shape, F32)],
        compiler_params=_cparams(2), name="ssd_core",
    )(zx, zx, zx, zx, zx, dt, conv_tiles, h0, conv_w, conv_b.reshape(1, C), pad(dt_bias), pad(a_log),
      d_wide, norm_w.reshape(1, DI), expand)


def _fox_gate_kernel(f_ref, fb_ref, lf_ref, cum_ref, carry_scr):
    c = pl.program_id(1)
    tc = f_ref.shape[0]

    @pl.when(c == 0)
    def _():
        carry_scr[...] = jnp.zeros_like(carry_scr)
    lf = jax.nn.log_sigmoid(f_ref[...] + fb_ref[...])
    ri = lax.broadcasted_iota(jnp.int32, (tc, tc), 0)
    ci = lax.broadcasted_iota(jnp.int32, (tc, tc), 1)
    cum = jnp.dot((ri >= ci).astype(F32), lf, preferred_element_type=F32, precision=HI) + carry_scr[...]
    lf_ref[...] = lf
    cum_ref[...] = cum
    carry_scr[...] = cum[tc - 1:tc, :]


def _fox_gate(f_raw, f_b_pad, *, B, T, tc):
    M, W = f_raw.shape
    nc = T // tc
    row = pl.BlockSpec((tc, W), lambda b, c: (b * nc + c, 0))
    return pl.pallas_call(
        _fox_gate_kernel, grid=(B, nc),
        in_specs=[row, pl.BlockSpec((1, W), lambda b, c: (0, 0))],
        out_specs=[row, row],
        out_shape=[jax.ShapeDtypeStruct((M, W), F32)] * 2,
        scratch_shapes=[pltpu.VMEM((1, W), F32)],
        compiler_params=_cparams(2), name="fox_gate",
    )(f_raw, f_b_pad)


def _fox_flash_kernel(q_ref, k_ref, v_ref, cq_ref, ck_ref, o_ref, m_scr, l_scr, acc_scr, *, tq, tk, nk):
    qi, ki = pl.program_id(1), pl.program_id(2)
    H, Dh = FOX_HEADS, FOX_HEAD_DIM
    scale = Dh ** -0.5

    @pl.when(ki == 0)
    def _():
        m_scr[...] = jnp.full_like(m_scr, -jnp.inf)
        l_scr[...] = jnp.zeros_like(l_scr)
        acc_scr[...] = jnp.zeros_like(acc_scr)

    def compute(masked):
        if masked:
            rowp = qi * tq + lax.broadcasted_iota(jnp.int32, (tq, tk), 0)
            colp = ki * tk + lax.broadcasted_iota(jnp.int32, (tq, tk), 1)
            keep = colp <= rowp
        for h in range(H):
            sl = slice(h * Dh, (h + 1) * Dh)
            qh = (q_ref[:, sl] * scale).astype(BF16)
            s = _nt(qh, k_ref[:, sl].astype(BF16))
            s = s + cq_ref[:, h:h + 1] - ck_ref[h:h + 1, :]
            if masked:
                s = jnp.where(keep, s, -jnp.inf)
            m_prev = m_scr[h]
            m_new = jnp.maximum(m_prev, jnp.max(s, axis=-1, keepdims=True))
            p = jnp.exp(s - m_new)
            alpha = jnp.exp(m_prev - m_new)
            l_scr[h] = alpha * l_scr[h] + jnp.sum(p, axis=-1, keepdims=True)
            acc_scr[:, sl] = alpha * acc_scr[:, sl] + jnp.dot(p.astype(BF16), v_ref[:, sl].astype(BF16),
                                                              preferred_element_type=F32)
            m_scr[h] = m_new

    needed = ki * tk <= qi * tq + (tq - 1)
    diag = ki * tk + (tk - 1) > qi * tq

    @pl.when(needed & diag)
    def _():
        compute(True)

    @pl.when(needed & jnp.logical_not(diag))
    def _():
        compute(False)

    @pl.when(ki == nk - 1)
    def _():
        for h in range(H):
            sl = slice(h * Dh, (h + 1) * Dh)
            o_ref[:, sl] = (acc_scr[:, sl] / l_scr[h]).astype(o_ref.dtype)


def _fox_flash(q, k, v, cum, cum_t, *, B, T, tq, tk):
    M, D = q.shape
    nq, nk = T // tq, T // tk
    last_k = lambda qi: (qi * tq + tq - 1) // tk
    qrow = lambda b, qi, ki: (b * nq + qi, 0)
    krow = lambda b, qi, ki: (b * nk + jnp.minimum(ki, last_k(qi)), 0)
    return pl.pallas_call(
        functools.partial(_fox_flash_kernel, tq=tq, tk=tk, nk=nk),
        grid=(B, nq, nk),
        in_specs=[pl.BlockSpec((tq, D), qrow), pl.BlockSpec((tk, D), krow), pl.BlockSpec((tk, D), krow),
                  pl.BlockSpec((tq, cum.shape[1]), qrow),
                  pl.BlockSpec((None, FOX_HEADS, tk), lambda b, qi, ki: (b, 0, jnp.minimum(ki, last_k(qi))))],
        out_specs=pl.BlockSpec((tq, D), qrow),
        out_shape=jax.ShapeDtypeStruct((M, D), BF16),
        scratch_shapes=[pltpu.VMEM((FOX_HEADS, tq, 1), F32), pltpu.VMEM((FOX_HEADS, tq, 1), F32),
                        pltpu.VMEM((tq, D), F32)],
        compiler_params=_cparams(3), name="fox_flash",
    )(q, k, v, cum, cum_t)


def _page_sums_kernel(lf_ref, suf_ref, tot_ref):
    PB, PG, H = lf_ref.shape
    si = lax.broadcasted_iota(jnp.int32, (PG, PG), 0)
    ki = lax.broadcasted_iota(jnp.int32, (PG, PG), 1)
    later = (ki > si).astype(F32)
    for i in range(PB):
        lf = lf_ref[i]
        suf = jnp.dot(later, lf, preferred_element_type=F32, precision=HI)
        suf_ref[i] = suf
        tot_ref[i] = jnp.broadcast_to(suf[0:1, :] + lf[0:1, :], (PG, H))


def _fox_page_sums(cache_logf, *, pb):
    n_pool, PG, H = cache_logf.shape
    spec = pl.BlockSpec((pb, PG, H), lambda i: (i, 0, 0))
    return pl.pallas_call(
        _page_sums_kernel, grid=(n_pool // pb,), in_specs=[spec], out_specs=[spec, spec],
        out_shape=[jax.ShapeDtypeStruct(cache_logf.shape, F32)] * 2,
        compiler_params=_cparams(1), name="fox_page_sums",
    )(cache_logf)


def _heads_to_rows(x, H, Dh):
    return jnp.concatenate([x[:, h * Dh:(h + 1) * Dh] for h in range(H)], axis=0)


def _fox_sample_kernel(pt_ref, q_ref, kn_ref, vn_ref, cn_ref, cnf_ref, *rest, n_steps, pps):
    kp_refs, vp_refs = rest[:pps], rest[pps:2 * pps]
    suf_refs, tot_refs = rest[2 * pps:3 * pps], rest[3 * pps:4 * pps]
    o_ref, q_scr, cq_scr, later_scr, m_scr, l_scr, acc_scr = rest[4 * pps:]
    p = pl.program_id(1)
    H, Dh = FOX_HEADS, FOX_HEAD_DIM
    TS = q_ref.shape[0]
    R = H * TS
    PG = kp_refs[0].shape[0]
    KC = PG * H

    @pl.when(p == 0)
    def _():
        q_scr[...] = _heads_to_rows(q_ref[...] * (Dh ** -0.5), H, Dh).astype(BF16)
        cw = cn_ref.shape[1]
        c_rep = jnp.broadcast_to(cn_ref[...][None], (H, TS, cw)).reshape(R, cw)
        r2 = lax.broadcasted_iota(jnp.int32, (R, cw), 0)
        l2 = lax.broadcasted_iota(jnp.int32, (R, cw), 1)
        cq_scr[...] = jnp.sum(jnp.where(l2 == r2 // TS, c_rep, 0.0), axis=-1, keepdims=True)
        later_scr[...] = jnp.zeros_like(later_scr)
        m_scr[...] = jnp.full_like(m_scr, -jnp.inf)
        l_scr[...] = jnp.zeros_like(l_scr)
        acc_scr[...] = jnp.zeros_like(acc_scr)

    def online(s, v_bf16):
        m_prev = m_scr[...]
        m_new = jnp.maximum(m_prev, jnp.max(s, axis=-1, keepdims=True))
        pr = jnp.exp(s - m_new)
        alpha = jnp.exp(m_prev - m_new)
        l_scr[...] = alpha * l_scr[...] + jnp.sum(pr, axis=-1, keepdims=True)
        acc_scr[...] = alpha * acc_scr[...] + jnp.dot(pr.astype(BF16), v_bf16, preferred_element_type=F32)
        m_scr[...] = m_new

    @pl.when(p < n_steps)
    def _():
        row_h = lax.broadcasted_iota(jnp.int32, (R, KC), 0) // TS
        col_h = lax.broadcasted_iota(jnp.int32, (R, KC), 1) % H
        own = row_h == col_h
        for s_ in range(pps - 1, -1, -1):
            kb = kp_refs[s_][...].reshape(KC, Dh).astype(BF16)
            vb = vp_refs[s_][...].reshape(KC, Dh).astype(BF16)
            bias = suf_refs[s_][...] + later_scr[...]
            later_scr[...] = later_scr[...] + tot_refs[s_][...]
            s = _nt(q_scr[...], kb) + cq_scr[...] + bias
            online(jnp.where(own, s, -jnp.inf), vb)

    @pl.when(p == n_steps)
    def _():
        kb = _heads_to_rows(kn_ref[...], H, Dh).astype(BF16)
        vb = _heads_to_rows(vn_ref[...], H, Dh).astype(BF16)
        s = _nt(q_scr[...], kb) + cq_scr[...] - cnf_ref[...]
        ri = lax.broadcasted_iota(jnp.int32, (R, R), 0)
        ci = lax.broadcasted_iota(jnp.int32, (R, R), 1)
        keep = (ri // TS == ci // TS) & (ci % TS <= ri % TS)
        online(jnp.where(keep, s, -jnp.inf), vb)
        o = acc_scr[...] / l_scr[...]
        o_ref[...] = jnp.concatenate([o[h * TS:(h + 1) * TS] for h in range(H)], axis=1)


def _fox_sample(q, k_new, v_new, cum_new, cum_new_flat, cache_k, cache_v, suf_flat, tot_flat, page_table,
                *, B, TS, pps):
    M, D = q.shape
    n_pages = page_table.shape[1]
    n_steps = n_pages // pps
    _, PG, H, Dh = cache_k.shape
    R = H * TS
    row = pl.BlockSpec((TS, D), lambda b, p, pt: (b, 0))

    def page(s, nd):
        return lambda b, p, pt: (pt[b * n_pages + n_pages - (jnp.minimum(p, n_steps - 1) + 1) * pps + s],) + (0,) * nd

    slots = range(pps)
    grid_spec = pltpu.PrefetchScalarGridSpec(
        num_scalar_prefetch=1, grid=(B, n_steps + 1),
        in_specs=[row, row, row,
                  pl.BlockSpec((TS, cum_new.shape[1]), lambda b, p, pt: (b, 0)),
                  pl.BlockSpec((None, 1, R), lambda b, p, pt: (b, 0, 0))]
                 + [pl.BlockSpec((None, PG, H, Dh), page(s, 3)) for s in slots]
                 + [pl.BlockSpec((None, PG, H, Dh), page(s, 3)) for s in slots]
                 + [pl.BlockSpec((None, 1, PG * H), page(s, 2)) for s in slots]
                 + [pl.BlockSpec((None, 1, PG * H), page(s, 2)) for s in slots],
        out_specs=row,
        scratch_shapes=[pltpu.VMEM((R, Dh), BF16), pltpu.VMEM((R, 1), F32), pltpu.VMEM((1, PG * H), F32),
                        pltpu.VMEM((R, 1), F32), pltpu.VMEM((R, 1), F32), pltpu.VMEM((R, Dh), F32)])
    return pl.pallas_call(
        functools.partial(_fox_sample_kernel, n_steps=n_steps, pps=pps),
        grid_spec=grid_spec,
        out_shape=jax.ShapeDtypeStruct((M, D), F32),
        compiler_params=_cparams(2), name="fox_sample",
    )(page_table.reshape(-1), q, k_new, v_new, cum_new, cum_new_flat,
      *([cache_k] * pps), *([cache_v] * pps), *([suf_flat] * pps), *([tot_flat] * pps))


def _rwkv_in_kernel(x_ref, g_ref, sc_ref, sh_ref, st_ref, mu_ref, wrkv_ref, w0_ref, w1_ref, w2_ref, a0_ref,
                    a1_ref, a2_ref, g1_ref, g2_ref,
                    r_ref, k_ref, v_ref, lw_ref, a_ref, gg_ref, tail_ref, h_scr, d_scr, carry_scr, *, T):
    i, j = pl.program_id(0), pl.program_id(1)
    tm, D = x_ref.shape
    nt = tm // SUBLANES
    long_seq = T >= tm

    @pl.when(j == 0)
    def _():
        h = _modnorm(x_ref[...], g_ref[...], sc_ref[...], sh_ref[...], T)
        h3 = h.reshape(nt, SUBLANES, D)
        if long_seq:
            @pl.when(i % (T // tm) == 0)
            def _():
                carry_scr[...] = st_ref[...]
            prev3 = jnp.concatenate([carry_scr[...][None], h3[:nt - 1]], axis=0)
            carry_scr[...] = h3[nt - 1]
            tail_ref[...] = h3[nt - 1]
        else:
            prev3 = st_ref[...]
            tail_ref[...] = h3
        prev = _shift_rows(h3, prev3, 1).reshape(tm, D)
        h_scr[...] = h
        d_scr[...] = prev - h

    def mix(s):
        return (h_scr[...] + d_scr[...] * mu_ref[s:s + 1, :]).astype(BF16)

    for s, out in enumerate((r_ref, k_ref, v_ref)):
        @pl.when(j == s)
        def _(s=s, out=out):
            out[...] = jnp.dot(mix(s), wrkv_ref[...], preferred_element_type=F32)

    @pl.when(j == 3)
    def _():
        t = jnp.tanh(jnp.dot(mix(3), w1_ref[...], preferred_element_type=F32))
        pre_w = w0_ref[...] + jnp.dot(t.astype(BF16), w2_ref[...], preferred_element_type=F32)
        w_log = -jax.nn.softplus(-pre_w) - 0.5
        lw_ref[...] = -jnp.exp(w_log)
        u = jnp.dot(mix(4), a1_ref[...], preferred_element_type=F32)
        pre_a = a0_ref[...] + jnp.dot(u.astype(BF16), a2_ref[...], preferred_element_type=F32)
        a_ref[...] = jax.nn.sigmoid(pre_a)
        sg = jax.nn.sigmoid(jnp.dot(mix(5), g1_ref[...], preferred_element_type=F32))
        gg_ref[...] = jnp.dot(sg.astype(BF16), g2_ref[...], preferred_element_type=F32)


def _rwkv_in(x, g, sc, sh, st_tiles, mu, w_rkv, w0, w1, w2, a0, a1, a2, g1, g2, *, T, tm):
    M, D = x.shape
    B = st_tiles.shape[0]
    P = LANES
    c2 = lambda shape: pl.BlockSpec(shape, lambda i, j: (0,) * len(shape))
    row = pl.BlockSpec((tm, D), lambda i, j: (i, 0))
    return pl.pallas_call(
        functools.partial(_rwkv_in_kernel, T=T),
        grid=(M // tm, 4),
        in_specs=[row, c2((1, D)), _mod_spec(T, tm, D), _mod_spec(T, tm, D), _tile_spec(T, tm, D),
                  c2((6, D)), pl.BlockSpec((None, D, D), lambda i, j: (jnp.minimum(j, 2), 0, 0)),
                  c2((1, D)), c2((D, P)), c2((P, D)), c2((1, D)), c2((D, P)), c2((P, D)), c2((D, P)), c2((P, D))],
        out_specs=[row] * 6 + [_tile_spec(T, tm, D)],
        out_shape=[jax.ShapeDtypeStruct((M, D), F32)] * 6 + [jax.ShapeDtypeStruct((B, SUBLANES, D), F32)],
        scratch_shapes=[pltpu.VMEM((tm, D), F32), pltpu.VMEM((tm, D), F32), pltpu.VMEM((SUBLANES, D), F32)],
        compiler_params=_cparams(2),
        name="rwkv_in",
    )(x, g.reshape(1, D), sc, sh, st_tiles, mu, w_rkv, w0.reshape(1, D), w1, w2, a0.reshape(1, D), a1, a2, g1, g2)


def _wkv_kernel(r_ref, k_ref, v_ref, lw_ref, a_ref, g_ref, kk_ref, ka_ref, rk_ref, lnw_ref, lnb_ref, s0_ref,
                o_ref, sout_ref, s_scr, *, L, n_chunks):
    c = pl.program_id(1)
    H, N = RWKV_HEADS, RWKV_HEAD
    G = WKV_ROWS // L
    R = G * L

    @pl.when(c == 0)
    def _():
        s_scr[...] = s0_ref[...]

    ri = lax.broadcasted_iota(jnp.int32, (L, L), 0)
    ci = lax.broadcasted_iota(jnp.int32, (L, L), 1)
    lw = lw_ref[...]
    cs = jnp.dot((ri >= ci).astype(F32), lw, preferred_element_type=F32, precision=HI)
    e_pos = jnp.exp(cs)
    e_neg = jnp.exp(-cs)
    e_prev = jnp.exp(cs - lw)

    row = lax.broadcasted_iota(jnp.int32, (R, R), 0)
    col = lax.broadcasted_iota(jnp.int32, (R, R), 1)
    same = (row // L) == (col // L)
    strict = same & (row > col)
    incl = same & (row >= col)

    for grp in range(H // G):
        heads = range(grp * G, (grp + 1) * G)
        A_l, R_l, B_l, K_l, V_l, kp_l = [], [], [], [], [], []
        for h in heads:
            sl = slice(h * N, (h + 1) * N)
            kh, ah = k_ref[:, sl], a_ref[:, sl]
            kkh = kh * kk_ref[:, sl]
            nrm = jnp.sqrt(jnp.sum(kkh * kkh, axis=-1, keepdims=True))
            kkh = kkh / jnp.maximum(nrm, 1e-12)
            kph = kh * (1.0 + (ah - 1.0) * ka_ref[:, sl])
            en = e_neg[:, sl]
            A_l.append(-kkh * e_prev[:, sl])
            R_l.append(r_ref[:, sl] * e_pos[:, sl])
            B_l.append(kkh * ah * en)
            K_l.append(kph * en)
            V_l.append(v_ref[:, sl])
            kp_l.append(kph)
        AR = jnp.concatenate(A_l + R_l, axis=0).astype(BF16)
        BK = jnp.concatenate(B_l + K_l, axis=0).astype(BF16)
        V4 = jnp.concatenate(V_l, axis=0)
        Q = _nt(AR, BK)
        ars_a, ars_r = [], []
        for q, h in enumerate(heads):
            Sb = s_scr[h].astype(BF16)
            ars_a.append(_nt(AR[q * L:(q + 1) * L], Sb))
            ars_r.append(_nt(AR[R + q * L:R + (q + 1) * L], Sb))
        nmat = jnp.where(strict, Q[:R, :R], 0.0)
        ak = jnp.where(strict, Q[:R, R:], 0.0)
        rhs = jnp.concatenate(ars_a, axis=0) + jnp.dot(ak.astype(BF16), V4.astype(BF16), preferred_element_type=F32)
        Y = jnp.where(((row // 2) == (col // 2)) & (row > col), nmat, 0.0)
        b = 2
        while b < L:
            Ck = jnp.where(((row // (2 * b)) == (col // (2 * b))) & ((row % (2 * b)) >= b) & ((col % (2 * b)) < b),
                           nmat, 0.0)
            Yb = Y.astype(BF16)
            Tm = Ck + jnp.dot(Yb, Ck.astype(BF16), preferred_element_type=F32)
            Y = Y + Tm + jnp.dot(Tm.astype(BF16), Yb, preferred_element_type=F32)
            b *= 2
        E = rhs + jnp.dot(Y.astype(BF16), rhs.astype(BF16), preferred_element_type=F32)
        EV = jnp.concatenate([E, V4], axis=0).astype(BF16)
        rbk = jnp.concatenate([jnp.where(incl, Q[R:, :R], 0.0), jnp.where(incl, Q[R:, R:], 0.0)], axis=1)
        y4 = jnp.concatenate(ars_r, axis=0) + jnp.dot(rbk.astype(BF16), EV, preferred_element_type=F32)
        for q, h in enumerate(heads):
            sl = slice(h * N, (h + 1) * N)
            rows = slice(q * L, (q + 1) * L)
            ev_h = jnp.concatenate([EV[rows], EV[R + q * L:R + (q + 1) * L]], axis=0)
            bk_h = jnp.concatenate([BK[rows], BK[R + q * L:R + (q + 1) * L]], axis=0)
            s_scr[h] = (s_scr[h] + _tn(ev_h, bk_h)) * e_pos[L - 1:L, sl]
            y = y4[rows]
            mean = jnp.mean(y, axis=-1, keepdims=True)
            var = jnp.mean(jnp.square(y - mean), axis=-1, keepdims=True)
            yn = (y - mean) * lax.rsqrt(var + RWKV_GN_EPS) * lnw_ref[:, sl] + lnb_ref[:, sl]
            bonus = jnp.sum(r_ref[:, sl] * kp_l[q] * rk_ref[:, sl], axis=-1, keepdims=True) * V_l[q]
            o_ref[:, sl] = (yn + bonus) * g_ref[:, sl]

    @pl.when(c == n_chunks - 1)
    def _():
        sout_ref[...] = s_scr[...]


def _wkv_core(r, k, v, lw, a, g, k_k, k_a, r_k, ln_w, ln_b, s0, *, T, L):
    M, D = r.shape
    B = s0.shape[0]
    nc = T // L
    row = pl.BlockSpec((L, D), lambda b, c: (b * nc + c, 0))
    par = pl.BlockSpec((1, D), lambda b, c: (0, 0))
    st = pl.BlockSpec((None, RWKV_HEADS, RWKV_HEAD, RWKV_HEAD), lambda b, c: (b, 0, 0, 0))
    v1 = lambda t: t.reshape(1, D)
    return pl.pallas_call(
        functools.partial(_wkv_kernel, L=L, n_chunks=nc),
        grid=(B, nc),
        in_specs=[row] * 6 + [par] * 5 + [st],
        out_specs=[row, st],
        out_shape=[jax.ShapeDtypeStruct((M, D), F32), jax.ShapeDtypeStruct(s0.shape, F32)],
        scratch_shapes=[pltpu.VMEM((RWKV_HEADS, RWKV_HEAD, RWKV_HEAD), F32)],
        compiler_params=_cparams(2),
        name="wkv_core",
    )(r, k, v, lw, a, g, v1(k_k), v1(k_a), v1(r_k), v1(ln_w), v1(ln_b), s0)


def _wkv_long_kernel(r_ref, k_ref, v_ref, lw_ref, a_ref, g_ref, kk_ref, ka_ref, rk_ref, lnw_ref, lnb_ref, s0_ref,
                     o_ref, sout_ref, s_scr, *, L, NC, n_steps):
    step = pl.program_id(1)
    H, N, G = RWKV_HEADS, RWKV_HEAD, WKV_GROUP_HEADS
    GW = G * N
    NG = H // G
    R = G * L
    TB = NC * L

    @pl.when(step == 0)
    def _():
        s_scr[...] = jnp.zeros_like(s_scr)
        for h in range(H):
            q = h % G
            s_scr[h // G, q * N:(q + 1) * N, q * N:(q + 1) * N] = s0_ref[h]

    ti = lax.broadcasted_iota(jnp.int32, (TB, TB), 0)
    tj = lax.broadcasted_iota(jnp.int32, (TB, TB), 1)
    same_c = (ti // L) == (tj // L)
    lw = lw_ref[...]
    cs = jnp.dot((same_c & (ti >= tj)).astype(F32), lw, preferred_element_type=F32, precision=HI)
    ctot = jnp.dot(same_c.astype(F32), lw, preferred_element_type=F32, precision=HI)
    e_pos, e_neg, e_prev = jnp.exp(cs), jnp.exp(-cs), jnp.exp(cs - lw)
    e_hat, e_tot = jnp.exp(ctot - cs), jnp.exp(ctot)
    gi = lax.broadcasted_iota(jnp.int32, (GW, GW), 0) // N
    gj = lax.broadcasted_iota(jnp.int32, (GW, GW), 1) // N
    gones = (gi == gj).astype(F32)

    def head_sum(x):
        return jnp.concatenate([jnp.dot(x[:, g * GW:(g + 1) * GW], gones, preferred_element_type=F32, precision=HI)
                                for g in range(NG)], axis=1)

    r, k, v, a = r_ref[...], k_ref[...], v_ref[...], a_ref[...]
    kkv = k * kk_ref[...]
    kkn = kkv / jnp.maximum(jnp.sqrt(head_sum(kkv * kkv)), 1e-12)
    kp = k * (1.0 + (a - 1.0) * ka_ref[...])
    ba = kkn * a
    nat = dict(A=-kkn * e_prev, R=r * e_pos, Bt=ba * e_neg, Kt=kp * e_neg, Bh=ba * e_hat, Kh=kp * e_hat, V=v)
    bonus = head_sum(r * kp * rk_ref[...]) * v

    row = lax.broadcasted_iota(jnp.int32, (R, GW), 0)
    lane = lax.broadcasted_iota(jnp.int32, (R, GW), 1)
    hm = (row // L) == (lane // N)

    def bd(name, c, g):
        x = nat[name][c * L:(c + 1) * L, g * GW:(g + 1) * GW]
        return jnp.where(hm, jnp.concatenate([x] * G, axis=0), 0.0)

    mr = lax.broadcasted_iota(jnp.int32, (R, R), 0)
    mc = lax.broadcasted_iota(jnp.int32, (R, R), 1)
    same_h = (mr // L) == (mc // L)
    strict = same_h & (mr > mc)
    incl = same_h & (mr >= mc)

    units = [(c, g) for c in range(NC) for g in range(NG)]
    U = {}
    for u in units:
        ops = {n: bd(n, *u) for n in nat}
        AR = jnp.concatenate([ops["A"], ops["R"]], axis=0).astype(BF16)
        BK = jnp.concatenate([ops["Bt"], ops["Kt"]], axis=0).astype(BF16)
        U[u] = dict(ops=ops, Q=_nt(AR, BK))
    for u in units:
        d = U[u]
        Q = d.pop("Q")
        d["nmat"] = jnp.where(strict, Q[:R, :R], 0.0)
        d["ak"] = jnp.where(strict, Q[:R, R:], 0.0).astype(BF16)
        d["rb"] = jnp.where(incl, Q[R:, :R], 0.0).astype(BF16)
        d["rk"] = jnp.where(incl, Q[R:, R:], 0.0).astype(BF16)
        d["Y"] = jnp.where(((mr // 2) == (mc // 2)) & (mr > mc), d["nmat"], 0.0)
    b = 2
    while b < L:
        lvl = ((mr // (2 * b)) == (mc // (2 * b))) & ((mr % (2 * b)) >= b) & ((mc % (2 * b)) < b)
        for u in units:
            d = U[u]
            Ck = jnp.where(lvl, d["nmat"], 0.0)
            d["Yb"] = d["Y"].astype(BF16)
            d["Tm"] = Ck + jnp.dot(d["Yb"], Ck.astype(BF16), preferred_element_type=F32)
        for u in units:
            d = U[u]
            d["Y"] = d["Y"] + d["Tm"] + jnp.dot(d["Tm"].astype(BF16), d["Yb"], preferred_element_type=F32)
        b *= 2
    for u in units:
        d = U[u]
        vb = d["ops"]["V"].astype(BF16)
        d["akv"] = jnp.dot(d["ak"], vb, preferred_element_type=F32)
        d["rkv"] = jnp.dot(d["rk"], vb, preferred_element_type=F32)
    for u in units:
        d = U[u]
        x = jnp.concatenate([d["ops"]["A"], d["akv"]], axis=1)
        d["AE"] = x + jnp.dot(d["Y"].astype(BF16), x.astype(BF16), preferred_element_type=F32)
    for u in units:
        d = U[u]
        base = jnp.concatenate([d["ops"]["R"], d["rkv"]], axis=1)
        ry = base + jnp.dot(d["rb"], d["AE"].astype(BF16), preferred_element_type=F32)
        d["Rbar"] = ry[:, :GW].astype(BF16)
        d["y0"] = ry[:, GW:]
    for u in units:
        d = U[u]
        bh = d["ops"]["Bh"].astype(BF16)
        kh = d["ops"]["Kh"].astype(BF16)
        d["W"] = _tn(d["AE"][:, :GW].astype(BF16), bh).astype(BF16)
        ev = jnp.concatenate([d["AE"][:, GW:], d["ops"]["V"]], axis=0).astype(BF16)
        d["Z"] = _tn(ev, jnp.concatenate([bh, kh], axis=0))

    ys = []
    for c in range(NC):
        yc = []
        for g in range(NG):
            d = U[(c, g)]
            S = s_scr[g]
            Sb = S.astype(BF16)
            ybd = d["y0"] + _nt(d["Rbar"], Sb)
            yc.append(sum(ybd[q * L:(q + 1) * L] for q in range(G)))
            decay = e_tot[c * L:c * L + 1, g * GW:(g + 1) * GW]
            s_scr[g] = S * decay + jnp.dot(Sb, d["W"], preferred_element_type=F32) + d["Z"]
        ys.append(jnp.concatenate(yc, axis=1))
    y = jnp.concatenate(ys, axis=0)

    mean = head_sum(y) * (1.0 / N)
    yc = y - mean
    var = head_sum(yc * yc) * (1.0 / N)
    yn = yc * lax.rsqrt(var + RWKV_GN_EPS) * lnw_ref[...] + lnb_ref[...]
    o_ref[...] = ((yn + bonus) * g_ref[...]).astype(o_ref.dtype)

    @pl.when(step == n_steps - 1)
    def _():
        for h in range(H):
            q = h % G
            sout_ref[h] = s_scr[h // G, q * N:(q + 1) * N, q * N:(q + 1) * N]


def _wkv_long(r, k, v, lw, a, g, k_k, k_a, r_k, ln_w, ln_b, s0, *, T, L, NC):
    M, D = r.shape
    B = s0.shape[0]
    TB = NC * L
    ns = T // TB
    row = pl.BlockSpec((TB, D), lambda b, c: (b * ns + c, 0))
    par = pl.BlockSpec((1, D), lambda b, c: (0, 0))
    st = pl.BlockSpec((None, RWKV_HEADS, RWKV_HEAD, RWKV_HEAD), lambda b, c: (b, 0, 0, 0))
    v1 = lambda t: t.reshape(1, D)
    GW = WKV_GROUP_HEADS * RWKV_HEAD
    return pl.pallas_call(
        functools.partial(_wkv_long_kernel, L=L, NC=NC, n_steps=ns),
        grid=(B, ns),
        in_specs=[row] * 6 + [par] * 5 + [st],
        out_specs=[row, st],
        out_shape=[jax.ShapeDtypeStruct((M, D), BF16), jax.ShapeDtypeStruct(s0.shape, F32)],
        scratch_shapes=[pltpu.VMEM((RWKV_HEADS // WKV_GROUP_HEADS, GW, GW), F32)],
        compiler_params=_cparams(2),
        name="wkv_long",
    )(r, k, v, lw, a, g, v1(k_k), v1(k_a), v1(r_k), v1(ln_w), v1(ln_b), s0)


def _pad_cols(w, n):
    return jnp.pad(w, ((0, 0), (0, n - w.shape[1])))


def _pad_rows(w, n):
    return jnp.pad(w, ((0, n - w.shape[0]), (0, 0)))


def _state_tiles(st):
    return jnp.pad(st, ((0, 0), (SUBLANES - st.shape[1], 0), (0, 0)))


def _ssm_from_groups(s):
    B = s.shape[0]
    s = s.reshape(B, SSD_GROUPS, SSD_STATE, SSD_HPG, SSD_HEAD_DIM)
    return s.transpose(0, 1, 3, 4, 2).reshape(B, SSD_HEADS, SSD_HEAD_DIM, SSD_STATE)


def kernel(x_prompt, x_sample, c_prompt, c_sample, state_ssm, state_ssd_conv, cache_k, cache_v, cache_logf, page_table, state_wkv, state_shift, state_ffn_conv, ada_w, ada_b, norm_mix, norm_ffn, norm_final, ssd_in_w, ssd_conv_w, ssd_conv_b, ssd_dt_bias, ssd_a_log, ssd_d, ssd_norm_w, ssd_out_w, fox_in_w, fox_f_b, fox_out_w, rwkv_mu, rwkv_w_rkv, rwkv_w0, rwkv_w1, rwkv_w2, rwkv_a0, rwkv_a1, rwkv_a2, rwkv_g1, rwkv_g2, rwkv_k_k, rwkv_k_a, rwkv_r_k, rwkv_ln_w, rwkv_ln_b, rwkv_out_w, ffn_up_w, ffn_conv_w, ffn_conv_b, ffn_down_w):
    BP, TP, D = x_prompt.shape
    BS, TS, _ = x_sample.shape
    groups = [dict(B=BP, T=TP, tm=1024, tm_s=512, ssd_l=SSD_CHUNK, wkv_l=RWKV_CHUNK, gate_tc=512),
              dict(B=BS, T=TS, tm=BS * TS, tm_s=512, ssd_l=TS, wkv_l=TS, gate_tc=TS)]
    xs = [x_prompt.reshape(BP * TP, D), x_sample.reshape(BS * TS, D)]

    n_c = BP + BS
    n_c_pad = -(-n_c // SUBLANES) * SUBLANES
    c_all = jnp.pad(jnp.concatenate([c_prompt, c_sample], axis=0), ((0, n_c_pad - n_c), (0, 0)))
    mods_all = _ada_mods(c_all, ada_w, ada_b).reshape(DEPTH, n_c_pad, 6, 1, D)
    row0 = [0, BP]

    def mods_of(i, gi):
        m = mods_all[i, row0[gi]:row0[gi] + groups[gi]["B"]]
        return [m[:, k] for k in range(6)]

    outs = {k: ([], []) for k in ("ssm", "sconv", "k", "v", "lf", "wkv", "shift", "fconv")}
    n_pool = cache_k.shape[1]

    for i in range(DEPTH):
        kind, j = i % N_MIXERS, i // N_MIXERS
        for gi, gp in enumerate(groups):
            B, T, tm, tm_s = gp["B"], gp["T"], gp["tm"], gp["tm_s"]
            x = xs[gi]
            prompt = gi == 0
            sh_m, sc_m, gt_m, sh_f, sc_f, gt_f = mods_of(i, gi)
            if kind == 0:
                w_main = ssd_in_w[j][:, :SSD_D_INNER + SSD_CONV_CH].astype(BF16)
                w_dt = _pad_cols(ssd_in_w[j][:, SSD_D_INNER + SSD_CONV_CH:], LANES).astype(BF16)
                zx = _norm_mod_matmul(x, norm_mix[i], sc_m, sh_m, w_main, T=T, tm=tm, tn=1024)
                dt = _norm_mod_matmul(x, norm_mix[i], sc_m, sh_m, w_dt, T=T, tm=tm, tn=LANES)
                if prompt:
                    conv_tiles = jnp.zeros((B, SUBLANES, SSD_CONV_CH), F32)
                    h0 = jnp.zeros((B, SSD_GROUPS, SSD_STATE, SSD_HPG * SSD_HEAD_DIM), F32)
                else:
                    conv_tiles = _state_tiles(state_ssd_conv[j])
                    h0 = state_ssm[j].reshape(B, SSD_GROUPS, SSD_HPG * SSD_HEAD_DIM, SSD_STATE)
                y, ctail, h_new = _ssd_core(zx, dt, conv_tiles, h0, ssd_conv_w[j], ssd_conv_b[j], ssd_dt_bias[j],
                                            ssd_a_log[j], ssd_d[j], ssd_norm_w[j], B=B, T=T, L=gp["ssd_l"],
                                            natural=not prompt)
                outs["ssm"][gi].append(_ssm_from_groups(h_new) if prompt else
                                       h_new.reshape(B, SSD_HEADS, SSD_HEAD_DIM, SSD_STATE))
                outs["sconv"][gi].append(ctail[:, SUBLANES - (SSD_CONV - 1):])
                x = _res_matmul(y, ssd_out_w[j].astype(BF16), x, gt_m, T=T, tm=tm)
            elif kind == 1:
                w_qkv = fox_in_w[j][:, :3 * FOX_INNER].reshape(D, 3, FOX_INNER).transpose(1, 0, 2).astype(BF16)
                w_f = _pad_cols(fox_in_w[j][:, 3 * FOX_INNER:], LANES).astype(BF16)
                q, k, v = _norm_mod_matmul_multi(x, norm_mix[i], sc_m, sh_m, w_qkv, T=T, tm=tm_s)
                f = _norm_mod_matmul(x, norm_mix[i], sc_m, sh_m, w_f, T=T, tm=tm, tn=LANES)
                f_b = jnp.pad(fox_f_b[j], (0, LANES - FOX_HEADS)).reshape(1, LANES)
                lf, cum = _fox_gate(f, f_b, B=B, T=T, tc=gp["gate_tc"])
                cum_t = jnp.swapaxes(cum[:, :FOX_HEADS].reshape(B, T, FOX_HEADS), 1, 2)
                if prompt:
                    o = _fox_flash(q, k, v, cum, cum_t, B=B, T=T, tq=512, tk=512)
                else:
                    suf, tot = _fox_page_sums(cache_logf[j], pb=16)
                    flat = lambda t: t.reshape(n_pool, 1, -1)
                    o = _fox_sample(q, k, v, cum, cum_t.reshape(B, 1, FOX_HEADS * T), cache_k[j], cache_v[j],
                                    flat(suf), flat(tot), page_table, B=B, TS=T, pps=FOX_PAGES_PER_STEP)
                hd = lambda t: t.reshape(B, T, FOX_HEADS, FOX_HEAD_DIM)
                outs["k"][gi].append(hd(k))
                outs["v"][gi].append(hd(v))
                outs["lf"][gi].append(lf[:, :FOX_HEADS].reshape(B, T, FOX_HEADS))
                x = _res_matmul(o, fox_out_w[j].astype(BF16), x, gt_m, T=T, tm=tm)
            else:
                if prompt:
                    st_tiles = jnp.zeros((B, SUBLANES, D), F32)
                    s0 = jnp.zeros((B, RWKV_HEADS, RWKV_HEAD, RWKV_HEAD), F32)
                else:
                    st_tiles = _state_tiles(state_shift[j][:, None, :])
                    s0 = state_wkv[j]
                r, k, v, lw, a, g, tails = _rwkv_in(
                    x, norm_mix[i], sc_m, sh_m, st_tiles, rwkv_mu[j], rwkv_w_rkv[j].astype(BF16), rwkv_w0[j],
                    _pad_cols(rwkv_w1[j], LANES).astype(BF16), _pad_rows(rwkv_w2[j], LANES).astype(BF16), rwkv_a0[j],
                    _pad_cols(rwkv_a1[j], LANES).astype(BF16), _pad_rows(rwkv_a2[j], LANES).astype(BF16),
                    rwkv_g1[j].astype(BF16), rwkv_g2[j].astype(BF16), T=T, tm=tm_s)
                wkv_par = (rwkv_k_k[j], rwkv_k_a[j], rwkv_r_k[j].reshape(-1), rwkv_ln_w[j], rwkv_ln_b[j], s0)
                if prompt:
                    o, s_new = _wkv_long(r, k, v, lw, a, g, *wkv_par, T=T, L=gp["wkv_l"], NC=WKV_STEP_CHUNKS)
                else:
                    o, s_new = _wkv_core(r, k, v, lw, a, g, *wkv_par, T=T, L=gp["wkv_l"])
                outs["wkv"][gi].append(s_new)
                outs["shift"][gi].append(tails[:, SUBLANES - 1])
                x = _res_matmul(o, rwkv_out_w[j].astype(BF16), x, gt_m, T=T, tm=tm)
            if prompt:
                ffn_tiles = jnp.zeros((B, SUBLANES, D_FF), F32)
            else:
                ffn_tiles = _state_tiles(state_ffn_conv[i])
            x, tails = _conv_ffn(x, norm_ffn[i], sc_f, sh_f, gt_f, ffn_tiles, ffn_up_w[i].astype(BF16),
                                 ffn_conv_w[i], ffn_conv_b[i], ffn_down_w[i].astype(BF16), T=T, tm=tm_s)
            outs["fconv"][gi].append(tails[:, SUBLANES - (FFN_CONV - 1):])
            xs[gi] = x

    y_prompt = _rmsnorm(xs[0], norm_final, tm=1024).reshape(BP, TP, D)
    y_sample = _rmsnorm(xs[1], norm_final, tm=1024).reshape(BS, TS, D)
    st = lambda name, gi: jnp.stack(outs[name][gi])
    return (y_prompt, y_sample, st("ssm", 0), st("ssm", 1), st("sconv", 0), st("sconv", 1),
            st("k", 0), st("k", 1), st("v", 0), st("v", 1), st("lf", 0), st("lf", 1),
            st("wkv", 0), st("wkv", 1), st("shift", 0), st("shift", 1), st("fconv", 0), st("fconv", 1))
```

```python
import functools

import jax
import jax.numpy as jnp
from jax import lax
from jax.experimental import pallas as pl
from jax.experimental.pallas import tpu as pltpu

F32 = jnp.float32
BF16 = jnp.bfloat16
HI = lax.Precision.HIGHEST

D_MODEL = 1024
DEPTH = 4
N_MIXERS = 3
RMS_EPS = 1e-6
SSD_D_INNER = 2048
SSD_HEAD_DIM = 64
SSD_HEADS = 32
SSD_GROUPS = 4
SSD_HPG = 8
SSD_STATE = 128
SSD_CONV = 4
SSD_CONV_CH = 3072
SSD_CHUNK = 128
FOX_HEAD_DIM = 64
FOX_HEADS = 16
FOX_INNER = 1024
RWKV_HEAD = 64
RWKV_HEADS = 16
RWKV_GN_EPS = 64e-5
RWKV_CHUNK = 32
WKV_ROWS = 128
WKV_GROUP_HEADS = 4
WKV_STEP_CHUNKS = 4
FOX_PAGES_PER_STEP = 8
D_FF = 2816
FFN_CONV = 3

SUBLANES = 8
LANES = 128
VMEM_LIMIT = 56 * 1024 * 1024


def _cparams(n_grid):
    return pltpu.CompilerParams(dimension_semantics=("arbitrary",) * n_grid,
                                vmem_limit_bytes=VMEM_LIMIT)


def _nt(a, b):
    return lax.dot_general(a, b, (((1,), (1,)), ((), ())), preferred_element_type=F32)


def _tn(a, b):
    return lax.dot_general(a, b, (((0,), (0,)), ((), ())), preferred_element_type=F32)


def _modnorm(x, g, sc, sh, T):
    xn = x * lax.rsqrt(jnp.mean(x * x, axis=-1, keepdims=True) + RMS_EPS) * g
    if sc.ndim == 2:
        return xn * (1.0 + sc) + sh
    tm, D = x.shape
    x3 = xn.reshape(tm // T, T, D)
    return (x3 * (1.0 + sc) + sh).reshape(tm, D)


def _gate_mul(y, gt, T):
    if gt.ndim == 2:
        return y * gt
    tm, D = y.shape
    return (y.reshape(tm // T, T, D) * gt).reshape(tm, D)


def _shift_rows(cur3, prev3, k):
    row = lax.broadcasted_iota(jnp.int32, cur3.shape, 1)
    return jnp.where(row >= k, pltpu.roll(cur3, k, 1), pltpu.roll(prev3, k, 1))


def _mod_spec(T, tm, D):
    if T >= tm:
        per = T // tm
        return pl.BlockSpec((None, 1, D), lambda i, *_: (i // per, 0, 0))
    return pl.BlockSpec((tm // T, 1, D), lambda i, *_: (i, 0, 0))


def _tile_spec(T, tm, C):
    if T >= tm:
        per = T // tm
        return pl.BlockSpec((None, SUBLANES, C), lambda i, *_: (i // per, 0, 0))
    return pl.BlockSpec((tm // T, SUBLANES, C), lambda i, *_: (i, 0, 0))


def _ada_kernel(c_ref, w_ref, b_ref, o_ref):
    c = c_ref[...]
    a = (c * jax.nn.sigmoid(c)).astype(BF16)
    o_ref[...] = jnp.dot(a, w_ref[...].astype(BF16), preferred_element_type=F32) + b_ref[...]


def _ada_mods(c_all, ada_w, ada_b):
    R, D = c_all.shape
    L, _, N = ada_w.shape
    tn = 1536
    return pl.pallas_call(
        _ada_kernel,
        grid=(L, N // tn),
        in_specs=[pl.BlockSpec((R, D), lambda l, j: (0, 0)),
                  pl.BlockSpec((None, D, tn), lambda l, j: (l, 0, j)),
                  pl.BlockSpec((None, 1, tn), lambda l, j: (l, 0, j))],
        out_specs=pl.BlockSpec((None, R, tn), lambda l, j: (l, 0, j)),
        out_shape=jax.ShapeDtypeStruct((L, R, N), F32),
        compiler_params=_cparams(2),
        name="ada_mods",
    )(c_all, ada_w, ada_b.reshape(L, 1, N))


def _nmm_kernel(x_ref, g_ref, sc_ref, sh_ref, w_ref, o_ref, h_scr, *, T):
    @pl.when(pl.program_id(1) == 0)
    def _():
        h_scr[...] = _modnorm(x_ref[...], g_ref[...], sc_ref[...], sh_ref[...], T).astype(BF16)
    o_ref[...] = jnp.dot(h_scr[...], w_ref[...], preferred_element_type=F32).astype(o_ref.dtype)


def _norm_mod_matmul(x, g, sc, sh, w, *, T, tm, tn):
    M, D = x.shape
    N = w.shape[1]
    return pl.pallas_call(
        functools.partial(_nmm_kernel, T=T),
        grid=(M // tm, N // tn),
        in_specs=[pl.BlockSpec((tm, D), lambda i, j: (i, 0)),
                  pl.BlockSpec((1, D), lambda i, j: (0, 0)),
                  _mod_spec(T, tm, D), _mod_spec(T, tm, D),
                  pl.BlockSpec((D, tn), lambda i, j: (0, j))],
        out_specs=pl.BlockSpec((tm, tn), lambda i, j: (i, j)),
        out_shape=jax.ShapeDtypeStruct((M, N), F32),
        scratch_shapes=[pltpu.VMEM((tm, D), BF16)],
        compiler_params=_cparams(2),
        name="norm_mod_matmul",
    )(x, g.reshape(1, D), sc, sh, w)


def _nmm_multi_kernel(x_ref, g_ref, sc_ref, sh_ref, w_ref, *rest, T, n_out):
    outs, h_scr = rest[:n_out], rest[n_out]
    j = pl.program_id(1)

    @pl.when(j == 0)
    def _():
        h_scr[...] = _modnorm(x_ref[...], g_ref[...], sc_ref[...], sh_ref[...], T).astype(BF16)
    res = jnp.dot(h_scr[...], w_ref[...], preferred_element_type=F32)
    for o in range(n_out):
        @pl.when(j == o)
        def _(o=o):
            outs[o][...] = res


def _norm_mod_matmul_multi(x, g, sc, sh, w_stack, *, T, tm):
    M, D = x.shape
    n_out, _, N = w_stack.shape
    return pl.pallas_call(
        functools.partial(_nmm_multi_kernel, T=T, n_out=n_out),
        grid=(M // tm, n_out),
        in_specs=[pl.BlockSpec((tm, D), lambda i, j: (i, 0)),
                  pl.BlockSpec((1, D), lambda i, j: (0, 0)),
                  _mod_spec(T, tm, D), _mod_spec(T, tm, D),
                  pl.BlockSpec((None, D, N), lambda i, j: (j, 0, 0))],
        out_specs=[pl.BlockSpec((tm, N), lambda i, j: (i, 0)) for _ in range(n_out)],
        out_shape=[jax.ShapeDtypeStruct((M, N), F32) for _ in range(n_out)],
        scratch_shapes=[pltpu.VMEM((tm, D), BF16)],
        compiler_params=_cparams(2),
        name="norm_mod_matmul_multi",
    )(x, g.reshape(1, D), sc, sh, w_stack)


def _res_mm_kernel(a_ref, w_ref, x_ref, gt_ref, o_ref, *, T):
    y = jnp.dot(a_ref[...].astype(BF16), w_ref[...], preferred_element_type=F32)
    o_ref[...] = x_ref[...] + _gate_mul(y, gt_ref[...], T)


def _res_matmul(a, w, x, gt, *, T, tm):
    M, K = a.shape
    D = w.shape[1]
    return pl.pallas_call(
        functools.partial(_res_mm_kernel, T=T),
        grid=(M // tm,),
        in_specs=[pl.BlockSpec((tm, K), lambda i: (i, 0)),
                  pl.BlockSpec((K, D), lambda i: (0, 0)),
                  pl.BlockSpec((tm, D), lambda i: (i, 0)),
                  _mod_spec(T, tm, D)],
        out_specs=pl.BlockSpec((tm, D), lambda i: (i, 0)),
        out_shape=jax.ShapeDtypeStruct((M, D), F32),
        compiler_params=_cparams(1),
        name="res_matmul",
    )(a, w, x, gt)


def _ffn_kernel(x_ref, g_ref, sc_ref, sh_ref, gt_ref, st_ref, upw_ref, cw_ref, cb_ref, dnw_ref,
                o_ref, tail_ref, act_scr, carry_scr, *, T, F, fc):
    i = pl.program_id(0)
    tm, D = x_ref.shape
    nt = tm // SUBLANES
    long_seq = T >= tm
    x = x_ref[...]
    h = _modnorm(x, g_ref[...], sc_ref[...], sh_ref[...], T).astype(BF16)
    if long_seq:
        @pl.when(i % (T // tm) == 0)
        def _():
            carry_scr[...] = st_ref[...]
    for c in range(F // fc):
        lo, hi = c * fc, (c + 1) * fc
        gc = jnp.dot(h, upw_ref[:, lo:hi], preferred_element_type=F32)
        uc = jnp.dot(h, upw_ref[:, F + lo:F + hi], preferred_element_type=F32)
        g3 = gc.reshape(nt, SUBLANES, fc)
        if long_seq:
            prev3 = jnp.concatenate([carry_scr[:, lo:hi][None], g3[:nt - 1]], axis=0)
            carry_scr[:, lo:hi] = g3[nt - 1]
            tail_ref[:, lo:hi] = g3[nt - 1]
        else:
            prev3 = st_ref[:, :, lo:hi]
            tail_ref[:, :, lo:hi] = g3
        conv = (g3 * cw_ref[2:3, lo:hi] + _shift_rows(g3, prev3, 1) * cw_ref[1:2, lo:hi]
                + _shift_rows(g3, prev3, 2) * cw_ref[0:1, lo:hi] + cb_ref[:, lo:hi])
        act = conv * jax.nn.sigmoid(conv) * uc.reshape(nt, SUBLANES, fc)
        act_scr[:, lo:hi] = act.reshape(tm, fc).astype(BF16)
    y = jnp.dot(act_scr[...], dnw_ref[...], preferred_element_type=F32)
    o_ref[...] = x + _gate_mul(y, gt_ref[...], T)


def _conv_ffn(x, g, sc, sh, gt, st_tiles, up_w, conv_w, conv_b, down_w, *, T, tm):
    M, D = x.shape
    F = down_w.shape[0]
    B = st_tiles.shape[0]
    return pl.pallas_call(
        functools.partial(_ffn_kernel, T=T, F=F, fc=256),
        grid=(M // tm,),
        in_specs=[pl.BlockSpec((tm, D), lambda i: (i, 0)),
                  pl.BlockSpec((1, D), lambda i: (0, 0)),
                  _mod_spec(T, tm, D), _mod_spec(T, tm, D), _mod_spec(T, tm, D),
                  _tile_spec(T, tm, F),
                  pl.BlockSpec((D, 2 * F), lambda i: (0, 0)),
                  pl.BlockSpec((FFN_CONV, F), lambda i: (0, 0)),
                  pl.BlockSpec((1, F), lambda i: (0, 0)),
                  pl.BlockSpec((F, D), lambda i: (0, 0))],
        out_specs=[pl.BlockSpec((tm, D), lambda i: (i, 0)), _tile_spec(T, tm, F)],
        out_shape=[jax.ShapeDtypeStruct((M, D), F32), jax.ShapeDtypeStruct((B, SUBLANES, F), F32)],
        scratch_shapes=[pltpu.VMEM((tm, F), BF16), pltpu.VMEM((SUBLANES, F), F32)],
        compiler_params=_cparams(1),
        name="conv_ffn",
    )(x, g.reshape(1, D), sc, sh, gt, st_tiles, up_w, conv_w, conv_b.reshape(1, F), down_w)


def _rms_kernel(x_ref, g_ref, o_ref):
    x = x_ref[...]
    o_ref[...] = x * lax.rsqrt(jnp.mean(x * x, axis=-1, keepdims=True) + RMS_EPS) * g_ref[...]


def _rmsnorm(x, g, *, tm):
    M, D = x.shape
    return pl.pallas_call(
        _rms_kernel, grid=(M // tm,),
        in_specs=[pl.BlockSpec((tm, D), lambda i: (i, 0)), pl.BlockSpec((1, D), lambda i: (0, 0))],
        out_specs=pl.BlockSpec((tm, D), lambda i: (i, 0)),
        out_shape=jax.ShapeDtypeStruct((M, D), F32),
        compiler_params=_cparams(1), name="final_rmsnorm",
    )(x, g.reshape(1, D))


def _conv_silu(cur, prev_tile, w_ref, b_ref, lo, hi):
    L, C = cur.shape
    nt = L // SUBLANES
    c3 = cur.reshape(nt, SUBLANES, C)
    p3 = prev_tile[None] if nt == 1 else jnp.concatenate([prev_tile[None], c3[:nt - 1]], axis=0)
    out = c3 * w_ref[3:4, lo:hi] + b_ref[:, lo:hi]
    for k in range(1, SSD_CONV):
        out = out + _shift_rows(c3, p3, k) * w_ref[3 - k:4 - k, lo:hi]
    out = out * jax.nn.sigmoid(out)
    return out.reshape(L, C)


def _ssd_kernel(z0_ref, z1_ref, x0_ref, x1_ref, bc_ref, dt_ref, cst_ref, h0_ref, cw_ref, cb_ref, dtb_ref, alog_ref,
                dw_ref, nw_ref, ex_ref,
                y_ref, ctail_ref, hout_ref, tail_scr, st_scr, *, L, n_chunks, natural):
    c = pl.program_id(1)
    P, Nn, G, J = SSD_HEAD_DIM, SSD_STATE, SSD_GROUPS, SSD_HPG
    GW = J * P

    @pl.when(c == 0)
    def _():
        tail_scr[...] = cst_ref[...]
        st_scr[...] = h0_ref[...]

    conv = []
    for q, blk_ref in enumerate((x0_ref, x1_ref, bc_ref)):
        blk = blk_ref[...]
        lo, hi = q * 1024, (q + 1) * 1024
        conv.append(_conv_silu(blk, tail_scr[:, lo:hi], cw_ref, cb_ref, lo, hi))
        tail_scr[:, lo:hi] = blk[L - SUBLANES:, :]
        ctail_ref[:, lo:hi] = blk[L - SUBLANES:, :]
    xs = jnp.concatenate(conv[:2], axis=1)
    Bm = conv[2][:, :G * Nn]
    Cm = conv[2][:, G * Nn:]

    dt = jax.nn.softplus(dt_ref[...] + dtb_ref[...])
    a = dt * (-jnp.exp(alog_ref[...]))
    ri = lax.broadcasted_iota(jnp.int32, (L, L), 0)
    ci = lax.broadcasted_iota(jnp.int32, (L, L), 1)
    causal = ri >= ci
    acs = jnp.dot(causal.astype(F32), a, preferred_element_type=F32, precision=HI)
    ex = ex_ref[...]
    dt_w = jnp.dot(dt, ex, preferred_element_type=F32, precision=HI)
    acs_w = jnp.dot(acs, ex, preferred_element_type=F32, precision=HI)
    acs_t = acs.T
    last_w = acs_w[L - 1:L, :]
    X = xs * dt_w
    Xd = X * jnp.exp(last_w - acs_w)
    dstart = jnp.exp(acs_w)
    cdec = jnp.exp(last_w)
    zz = jnp.concatenate([z0_ref[...], z1_ref[...]], axis=1)
    zg = zz * jax.nn.sigmoid(zz)
    dwide = dw_ref[...]
    nw = nw_ref[...]
    Xb = X.astype(BF16)
    for g in range(G):
        gl = slice(g * GW, (g + 1) * GW)
        Bg = Bm[:, g * Nn:(g + 1) * Nn].astype(BF16)
        Cg = Cm[:, g * Nn:(g + 1) * Nn].astype(BF16)
        CB = _nt(Cg, Bg)
        st = st_scr[g]
        if natural:
            y_off = _nt(Cg, st.astype(BF16)) * dstart[:, gl]
            dec_h = jnp.exp(acs_t[g * J:(g + 1) * J, L - 1:L])
            dcol = jnp.broadcast_to(dec_h[:, None, :], (J, P, 1)).reshape(GW, 1)
            st_scr[g] = st * dcol + _tn(Xd[:, gl].astype(BF16), Bg)
        else:
            y_off = jnp.dot(Cg, st.astype(BF16), preferred_element_type=F32) * dstart[:, gl]
            st_scr[g] = st * cdec[:, gl] + _tn(Bg, Xd[:, gl].astype(BF16))
        ys = []
        for j in range(J):
            h = g * J + j
            diff = acs[:, h:h + 1] - acs_t[h:h + 1, :]
            dec = jnp.exp(jnp.where(causal, diff, -jnp.inf))
            Mh = (CB * dec).astype(BF16)
            ys.append(jnp.dot(Mh, Xb[:, h * P:(h + 1) * P], preferred_element_type=F32))
        yg = jnp.concatenate(ys, axis=1) + y_off + dwide[:, gl] * xs[:, gl]
        yg = yg * zg[:, gl]
        yg = yg * lax.rsqrt(jnp.mean(yg * yg, axis=-1, keepdims=True) + RMS_EPS) * nw[:, gl]
        y_ref[:, gl] = yg.astype(y_ref.dtype)

    @pl.when(c == n_chunks - 1)
    def _():
        hout_ref[...] = st_scr[...]


def _ssd_core(zx, dt, conv_tiles, h0, conv_w, conv_b, dt_bias, a_log, d_skip, norm_w, *, B, T, L, natural):
    M = zx.shape[0]
    nc = T // L
    DI, C = SSD_D_INNER, SSD_CONV_CH
    col = lambda q: pl.BlockSpec((L, 1024), lambda b, c, q=q: (b * nc + c, q))
    par = lambda n: pl.BlockSpec((1, n), lambda b, c: (0, 0))
    pad = lambda v: jnp.pad(v.astype(F32), (0, LANES - v.shape[0])).reshape(1, LANES)
    expand = (jnp.arange(LANES)[:, None] == (jnp.arange(DI) // SSD_HEAD_DIM)[None, :]).astype(F32)
    d_wide = jnp.repeat(d_skip.astype(F32), SSD_HEAD_DIM).reshape(1, DI)
    st_shape = h0.shape[1:]
    st_spec = pl.BlockSpec((None,) + st_shape, lambda b, c: (b, 0, 0, 0))
    tile_spec = pl.BlockSpec((None, SUBLANES, C), lambda b, c: (b, 0, 0))
    return pl.pallas_call(
        functools.partial(_ssd_kernel, L=L, n_chunks=nc, natural=natural),
        grid=(B, nc),
        in_specs=[col(0), col(1), col(2), col(3), col(4),
                  pl.BlockSpec((L, LANES), lambda b, c: (b * nc + c, 0)),
                  tile_spec, st_spec,
                  pl.BlockSpec((SSD_CONV, C), lambda b, c: (0, 0)), par(C), par(LANES), par(LANES),
                  par(DI), par(DI), pl.BlockSpec((LANES, DI), lambda b, c: (0, 0))],
        out_specs=[pl.BlockSpec((L, DI), lambda b, c: (b * nc + c, 0)), tile_spec, st_spec],
        out_shape=[jax.ShapeDtypeStruct((M, DI), BF16 if L % 16 == 0 else F32),
                   jax.ShapeDtypeStruct((B, SUBLANES, C), F32), jax.ShapeDtypeStruct(h0.shape, F32)],
        scratch_shapes=[pltpu.VMEM((SUBLANES, C), F32), pltpu.VMEM(st_shape, F32)],
        compiler_params=_cparams(2), name="ssd_core",
    )(zx, zx, zx, zx, zx, dt, conv_tiles, h0, conv_w, conv_b.reshape(1, C), pad(dt_bias), pad(a_log),
      d_wide, norm_w.reshape(1, DI), expand)


def _fox_gate_kernel(f_ref, fb_ref, lf_ref, cum_ref, carry_scr):
    c = pl.program_id(1)
    tc = f_ref.shape[0]

    @pl.when(c == 0)
    def _():
        carry_scr[...] = jnp.zeros_like(carry_scr)
    lf = jax.nn.log_sigmoid(f_ref[...] + fb_ref[...])
    ri = lax.broadcasted_iota(jnp.int32, (tc, tc), 0)
    ci = lax.broadcasted_iota(jnp.int32, (tc, tc), 1)
    cum = jnp.dot((ri >= ci).astype(F32), lf, preferred_element_type=F32, precision=HI) + carry_scr[...]
    lf_ref[...] = lf
    cum_ref[...] = cum
    carry_scr[...] = cum[tc - 1:tc, :]


def _fox_gate(f_raw, f_b_pad, *, B, T, tc):
    M, W = f_raw.shape
    nc = T // tc
    row = pl.BlockSpec((tc, W), lambda b, c: (b * nc + c, 0))
    return pl.pallas_call(
        _fox_gate_kernel, grid=(B, nc),
        in_specs=[row, pl.BlockSpec((1, W), lambda b, c: (0, 0))],
        out_specs=[row, row],
        out_shape=[jax.ShapeDtypeStruct((M, W), F32)] * 2,
        scratch_shapes=[pltpu.VMEM((1, W), F32)],
        compiler_params=_cparams(2), name="fox_gate",
    )(f_raw, f_b_pad)


def _fox_flash_kernel(q_ref, k_ref, v_ref, cq_ref, ck_ref, o_ref, m_scr, l_scr, acc_scr, *, tq, tk, nk):
    qi, ki = pl.program_id(1), pl.program_id(2)
    H, Dh = FOX_HEADS, FOX_HEAD_DIM
    scale = Dh ** -0.5

    @pl.when(ki == 0)
    def _():
        m_scr[...] = jnp.full_like(m_scr, -jnp.inf)
        l_scr[...] = jnp.zeros_like(l_scr)
        acc_scr[...] = jnp.zeros_like(acc_scr)

    def compute(masked):
        if masked:
            rowp = qi * tq + lax.broadcasted_iota(jnp.int32, (tq, tk), 0)
            colp = ki * tk + lax.broadcasted_iota(jnp.int32, (tq, tk), 1)
            keep = colp <= rowp
        for h in range(H):
            sl = slice(h * Dh, (h + 1) * Dh)
            qh = (q_ref[:, sl] * scale).astype(BF16)
            s = _nt(qh, k_ref[:, sl].astype(BF16))
            s = s + cq_ref[:, h:h + 1] - ck_ref[h:h + 1, :]
            if masked:
                s = jnp.where(keep, s, -jnp.inf)
            m_prev = m_scr[h]
            m_new = jnp.maximum(m_prev, jnp.max(s, axis=-1, keepdims=True))
            p = jnp.exp(s - m_new)
            alpha = jnp.exp(m_prev - m_new)
            l_scr[h] = alpha * l_scr[h] + jnp.sum(p, axis=-1, keepdims=True)
            acc_scr[:, sl] = alpha * acc_scr[:, sl] + jnp.dot(p.astype(BF16), v_ref[:, sl].astype(BF16),
                                                              preferred_element_type=F32)
            m_scr[h] = m_new

    needed = ki * tk <= qi * tq + (tq - 1)
    diag = ki * tk + (tk - 1) > qi * tq

    @pl.when(needed & diag)
    def _():
        compute(True)

    @pl.when(needed & jnp.logical_not(diag))
    def _():
        compute(False)

    @pl.when(ki == nk - 1)
    def _():
        for h in range(H):
            sl = slice(h * Dh, (h + 1) * Dh)
            o_ref[:, sl] = (acc_scr[:, sl] / l_scr[h]).astype(o_ref.dtype)


def _fox_flash(q, k, v, cum, cum_t, *, B, T, tq, tk):
    M, D = q.shape
    nq, nk = T // tq, T // tk
    last_k = lambda qi: (qi * tq + tq - 1) // tk
    qrow = lambda b, qi, ki: (b * nq + qi, 0)
    krow = lambda b, qi, ki: (b * nk + jnp.minimum(ki, last_k(qi)), 0)
    return pl.pallas_call(
        functools.partial(_fox_flash_kernel, tq=tq, tk=tk, nk=nk),
        grid=(B, nq, nk),
        in_specs=[pl.BlockSpec((tq, D), qrow), pl.BlockSpec((tk, D), krow), pl.BlockSpec((tk, D), krow),
                  pl.BlockSpec((tq, cum.shape[1]), qrow),
                  pl.BlockSpec((None, FOX_HEADS, tk), lambda b, qi, ki: (b, 0, jnp.minimum(ki, last_k(qi))))],
        out_specs=pl.BlockSpec((tq, D), qrow),
        out_shape=jax.ShapeDtypeStruct((M, D), BF16),
        scratch_shapes=[pltpu.VMEM((FOX_HEADS, tq, 1), F32), pltpu.VMEM((FOX_HEADS, tq, 1), F32),
                        pltpu.VMEM((tq, D), F32)],
        compiler_params=_cparams(3), name="fox_flash",
    )(q, k, v, cum, cum_t)


def _fox_sample_kernel(pt_ref, q_ref, kn_ref, vn_ref, cn_ref, cnt_ref, *rest, n_steps, pps):
    kt_refs, vt_refs, lft_refs = rest[:pps], rest[pps:2 * pps], rest[2 * pps:3 * pps]
    o_ref, qbd_scr, cq_scr, later_scr, m_scr, l_scr, acc_scr = rest[3 * pps:]
    p = pl.program_id(1)
    H, Dh = FOX_HEADS, FOX_HEAD_DIM
    TS = q_ref.shape[0]
    R = H * TS
    D = H * Dh
    KB = pps * kt_refs[0].shape[-1]

    @pl.when(p == 0)
    def _():
        row = lax.broadcasted_iota(jnp.int32, (R, D), 0)
        lane = lax.broadcasted_iota(jnp.int32, (R, D), 1)
        q_rep = jnp.broadcast_to(q_ref[...][None], (H, TS, D)).reshape(R, D) * (Dh ** -0.5)
        qbd_scr[...] = jnp.where(lane // Dh == row // TS, q_rep, 0.0).astype(BF16)
        cw = cn_ref.shape[1]
        c_rep = jnp.broadcast_to(cn_ref[...][None], (H, TS, cw)).reshape(R, cw)
        r2 = lax.broadcasted_iota(jnp.int32, (R, cw), 0)
        l2 = lax.broadcasted_iota(jnp.int32, (R, cw), 1)
        cq_scr[...] = jnp.sum(jnp.where(l2 == r2 // TS, c_rep, 0.0), axis=-1, keepdims=True)
        later_scr[...] = jnp.zeros_like(later_scr)
        m_scr[...] = jnp.full_like(m_scr, -jnp.inf)
        l_scr[...] = jnp.zeros_like(l_scr)
        acc_scr[...] = jnp.zeros_like(acc_scr)

    def online(s, pv):
        m_prev = m_scr[...]
        m_new = jnp.maximum(m_prev, jnp.max(s, axis=-1, keepdims=True))
        pr = jnp.exp(s - m_new)
        alpha = jnp.exp(m_prev - m_new)
        l_scr[...] = alpha * l_scr[...] + jnp.sum(pr, axis=-1, keepdims=True)
        acc_scr[...] = alpha * acc_scr[...] + pv(pr.astype(BF16))
        m_scr[...] = m_new

    @pl.when(p < n_steps)
    def _():
        lft = jnp.concatenate([r[...] for r in lft_refs], axis=1)
        kt = jnp.concatenate([r[...].reshape(D, -1).astype(BF16) for r in kt_refs], axis=1)
        vt = jnp.concatenate([r[...].reshape(D, -1).astype(BF16) for r in vt_refs], axis=1)
        si = lax.broadcasted_iota(jnp.int32, (KB, KB), 0)
        ki = lax.broadcasted_iota(jnp.int32, (KB, KB), 1)
        suf = jnp.dot(lft, (si > ki).astype(F32), preferred_element_type=F32, precision=HI) + later_scr[...]
        later_scr[...] = later_scr[...] + jnp.sum(lft, axis=-1, keepdims=True)
        bias = jnp.broadcast_to(suf[:, None, :], (H, TS, KB)).reshape(R, KB)
        s = jnp.dot(qbd_scr[...], kt, preferred_element_type=F32) + cq_scr[...] + bias
        online(s, lambda pr: _nt(pr, vt))

    @pl.when(p == n_steps)
    def _():
        cnt = cnt_ref[...]
        bias = jnp.broadcast_to(cnt[:, None, :], (H, TS, TS)).reshape(R, TS)
        s = _nt(qbd_scr[...], kn_ref[...].astype(BF16)) + cq_scr[...] - bias
        tq = lax.broadcasted_iota(jnp.int32, (R, TS), 0) % TS
        tk = lax.broadcasted_iota(jnp.int32, (R, TS), 1)
        s = jnp.where(tk <= tq, s, -jnp.inf)
        vn = vn_ref[...].astype(BF16)
        online(s, lambda pr: jnp.dot(pr, vn, preferred_element_type=F32))
        row = lax.broadcasted_iota(jnp.int32, (R, D), 0)
        lane = lax.broadcasted_iota(jnp.int32, (R, D), 1)
        own = jnp.where(lane // Dh == row // TS, acc_scr[...] / l_scr[...], 0.0)
        o_ref[...] = jnp.sum(own.reshape(H, TS, D), axis=0)


def _fox_sample(q, k_new, v_new, cum_new, cum_new_t, cache_kt, cache_vt, cache_logf_t, page_table, *, B, TS, pps):
    M, D = q.shape
    n_pages = page_table.shape[1]
    n_steps = n_pages // pps
    _, H, Dh, PG = cache_kt.shape
    R = H * TS
    row = pl.BlockSpec((TS, D), lambda b, p, pt: (b, 0))

    def page(s, nd):
        return lambda b, p, pt: (pt[b * n_pages + n_pages - (jnp.minimum(p, n_steps - 1) + 1) * pps + s],) + (0,) * nd

    slots = range(pps)
    grid_spec = pltpu.PrefetchScalarGridSpec(
        num_scalar_prefetch=1, grid=(B, n_steps + 1),
        in_specs=[row, row, row,
                  pl.BlockSpec((TS, cum_new.shape[1]), lambda b, p, pt: (b, 0)),
                  pl.BlockSpec((None, H, TS), lambda b, p, pt: (b, 0, 0))]
                 + [pl.BlockSpec((None, H, Dh, PG), page(s, 3)) for s in slots]
                 + [pl.BlockSpec((None, H, Dh, PG), page(s, 3)) for s in slots]
                 + [pl.BlockSpec((None, H, PG), page(s, 2)) for s in slots],
        out_specs=row,
        scratch_shapes=[pltpu.VMEM((R, D), BF16), pltpu.VMEM((R, 1), F32), pltpu.VMEM((H, 1), F32),
                        pltpu.VMEM((R, 1), F32), pltpu.VMEM((R, 1), F32), pltpu.VMEM((R, D), F32)])
    return pl.pallas_call(
        functools.partial(_fox_sample_kernel, n_steps=n_steps, pps=pps),
        grid_spec=grid_spec,
        out_shape=jax.ShapeDtypeStruct((M, D), F32),
        compiler_params=_cparams(2), name="fox_sample",
    )(page_table.reshape(-1), q, k_new, v_new, cum_new, cum_new_t,
      *([cache_kt] * pps), *([cache_vt] * pps), *([cache_logf_t] * pps))


def _rwkv_in_kernel(x_ref, g_ref, sc_ref, sh_ref, st_ref, mu_ref, wrkv_ref, w0_ref, w1_ref, w2_ref, a0_ref,
                    a1_ref, a2_ref, g1_ref, g2_ref,
                    r_ref, k_ref, v_ref, lw_ref, a_ref, gg_ref, tail_ref, h_scr, d_scr, carry_scr, *, T):
    i, j = pl.program_id(0), pl.program_id(1)
    tm, D = x_ref.shape
    nt = tm // SUBLANES
    long_seq = T >= tm

    @pl.when(j == 0)
    def _():
        h = _modnorm(x_ref[...], g_ref[...], sc_ref[...], sh_ref[...], T)
        h3 = h.reshape(nt, SUBLANES, D)
        if long_seq:
            @pl.when(i % (T // tm) == 0)
            def _():
                carry_scr[...] = st_ref[...]
            prev3 = jnp.concatenate([carry_scr[...][None], h3[:nt - 1]], axis=0)
            carry_scr[...] = h3[nt - 1]
            tail_ref[...] = h3[nt - 1]
        else:
            prev3 = st_ref[...]
            tail_ref[...] = h3
        prev = _shift_rows(h3, prev3, 1).reshape(tm, D)
        h_scr[...] = h
        d_scr[...] = prev - h

    def mix(s):
        return (h_scr[...] + d_scr[...] * mu_ref[s:s + 1, :]).astype(BF16)

    for s, out in enumerate((r_ref, k_ref, v_ref)):
        @pl.when(j == s)
        def _(s=s, out=out):
            out[...] = jnp.dot(mix(s), wrkv_ref[...], preferred_element_type=F32)

    @pl.when(j == 3)
    def _():
        t = jnp.tanh(jnp.dot(mix(3), w1_ref[...], preferred_element_type=F32))
        pre_w = w0_ref[...] + jnp.dot(t.astype(BF16), w2_ref[...], preferred_element_type=F32)
        w_log = -jax.nn.softplus(-pre_w) - 0.5
        lw_ref[...] = -jnp.exp(w_log)
        u = jnp.dot(mix(4), a1_ref[...], preferred_element_type=F32)
        pre_a = a0_ref[...] + jnp.dot(u.astype(BF16), a2_ref[...], preferred_element_type=F32)
        a_ref[...] = jax.nn.sigmoid(pre_a)
        sg = jax.nn.sigmoid(jnp.dot(mix(5), g1_ref[...], preferred_element_type=F32))
        gg_ref[...] = jnp.dot(sg.astype(BF16), g2_ref[...], preferred_element_type=F32)


def _rwkv_in(x, g, sc, sh, st_tiles, mu, w_rkv, w0, w1, w2, a0, a1, a2, g1, g2, *, T, tm):
    M, D = x.shape
    B = st_tiles.shape[0]
    P = LANES
    c2 = lambda shape: pl.BlockSpec(shape, lambda i, j: (0,) * len(shape))
    row = pl.BlockSpec((tm, D), lambda i, j: (i, 0))
    return pl.pallas_call(
        functools.partial(_rwkv_in_kernel, T=T),
        grid=(M // tm, 4),
        in_specs=[row, c2((1, D)), _mod_spec(T, tm, D), _mod_spec(T, tm, D), _tile_spec(T, tm, D),
                  c2((6, D)), pl.BlockSpec((None, D, D), lambda i, j: (jnp.minimum(j, 2), 0, 0)),
                  c2((1, D)), c2((D, P)), c2((P, D)), c2((1, D)), c2((D, P)), c2((P, D)), c2((D, P)), c2((P, D))],
        out_specs=[row] * 6 + [_tile_spec(T, tm, D)],
        out_shape=[jax.ShapeDtypeStruct((M, D), F32)] * 6 + [jax.ShapeDtypeStruct((B, SUBLANES, D), F32)],
        scratch_shapes=[pltpu.VMEM((tm, D), F32), pltpu.VMEM((tm, D), F32), pltpu.VMEM((SUBLANES, D), F32)],
        compiler_params=_cparams(2),
        name="rwkv_in",
    )(x, g.reshape(1, D), sc, sh, st_tiles, mu, w_rkv, w0.reshape(1, D), w1, w2, a0.reshape(1, D), a1, a2, g1, g2)


def _wkv_kernel(r_ref, k_ref, v_ref, lw_ref, a_ref, g_ref, kk_ref, ka_ref, rk_ref, lnw_ref, lnb_ref, s0_ref,
                o_ref, sout_ref, s_scr, *, L, n_chunks):
    c = pl.program_id(1)
    H, N = RWKV_HEADS, RWKV_HEAD
    G = WKV_ROWS // L
    R = G * L

    @pl.when(c == 0)
    def _():
        s_scr[...] = s0_ref[...]

    ri = lax.broadcasted_iota(jnp.int32, (L, L), 0)
    ci = lax.broadcasted_iota(jnp.int32, (L, L), 1)
    lw = lw_ref[...]
    cs = jnp.dot((ri >= ci).astype(F32), lw, preferred_element_type=F32, precision=HI)
    e_pos = jnp.exp(cs)
    e_neg = jnp.exp(-cs)
    e_prev = jnp.exp(cs - lw)

    row = lax.broadcasted_iota(jnp.int32, (R, R), 0)
    col = lax.broadcasted_iota(jnp.int32, (R, R), 1)
    same = (row // L) == (col // L)
    strict = same & (row > col)
    incl = same & (row >= col)

    for grp in range(H // G):
        heads = range(grp * G, (grp + 1) * G)
        A_l, R_l, B_l, K_l, V_l, kp_l = [], [], [], [], [], []
        for h in heads:
            sl = slice(h * N, (h + 1) * N)
            kh, ah = k_ref[:, sl], a_ref[:, sl]
            kkh = kh * kk_ref[:, sl]
            nrm = jnp.sqrt(jnp.sum(kkh * kkh, axis=-1, keepdims=True))
            kkh = kkh / jnp.maximum(nrm, 1e-12)
            kph = kh * (1.0 + (ah - 1.0) * ka_ref[:, sl])
            en = e_neg[:, sl]
            A_l.append(-kkh * e_prev[:, sl])
            R_l.append(r_ref[:, sl] * e_pos[:, sl])
            B_l.append(kkh * ah * en)
            K_l.append(kph * en)
            V_l.append(v_ref[:, sl])
            kp_l.append(kph)
        AR = jnp.concatenate(A_l + R_l, axis=0).astype(BF16)
        BK = jnp.concatenate(B_l + K_l, axis=0).astype(BF16)
        V4 = jnp.concatenate(V_l, axis=0)
        Q = _nt(AR, BK)
        ars_a, ars_r = [], []
        for q, h in enumerate(heads):
            Sb = s_scr[h].astype(BF16)
            ars_a.append(_nt(AR[q * L:(q + 1) * L], Sb))
            ars_r.append(_nt(AR[R + q * L:R + (q + 1) * L], Sb))
        nmat = jnp.where(strict, Q[:R, :R], 0.0)
        ak = jnp.where(strict, Q[:R, R:], 0.0)
        rhs = jnp.concatenate(ars_a, axis=0) + jnp.dot(ak.astype(BF16), V4.astype(BF16), preferred_element_type=F32)
        Y = jnp.where(((row // 2) == (col // 2)) & (row > col), nmat, 0.0)
        b = 2
        while b < L:
            Ck = jnp.where(((row // (2 * b)) == (col // (2 * b))) & ((row % (2 * b)) >= b) & ((col % (2 * b)) < b),
                           nmat, 0.0)
            Yb = Y.astype(BF16)
            Tm = Ck + jnp.dot(Yb, Ck.astype(BF16), preferred_element_type=F32)
            Y = Y + Tm + jnp.dot(Tm.astype(BF16), Yb, preferred_element_type=F32)
            b *= 2
        E = rhs + jnp.dot(Y.astype(BF16), rhs.astype(BF16), preferred_element_type=F32)
        EV = jnp.concatenate([E, V4], axis=0).astype(BF16)
        rbk = jnp.concatenate([jnp.where(incl, Q[R:, :R], 0.0), jnp.where(incl, Q[R:, R:], 0.0)], axis=1)
        y4 = jnp.concatenate(ars_r, axis=0) + jnp.dot(rbk.astype(BF16), EV, preferred_element_type=F32)
        for q, h in enumerate(heads):
            sl = slice(h * N, (h + 1) * N)
            rows = slice(q * L, (q + 1) * L)
            ev_h = jnp.concatenate([EV[rows], EV[R + q * L:R + (q + 1) * L]], axis=0)
            bk_h = jnp.concatenate([BK[rows], BK[R + q * L:R + (q + 1) * L]], axis=0)
            s_scr[h] = (s_scr[h] + _tn(ev_h, bk_h)) * e_pos[L - 1:L, sl]
            y = y4[rows]
            mean = jnp.mean(y, axis=-1, keepdims=True)
            var = jnp.mean(jnp.square(y - mean), axis=-1, keepdims=True)
            yn = (y - mean) * lax.rsqrt(var + RWKV_GN_EPS) * lnw_ref[:, sl] + lnb_ref[:, sl]
            bonus = jnp.sum(r_ref[:, sl] * kp_l[q] * rk_ref[:, sl], axis=-1, keepdims=True) * V_l[q]
            o_ref[:, sl] = (yn + bonus) * g_ref[:, sl]

    @pl.when(c == n_chunks - 1)
    def _():
        sout_ref[...] = s_scr[...]


def _wkv_core(r, k, v, lw, a, g, k_k, k_a, r_k, ln_w, ln_b, s0, *, T, L):
    M, D = r.shape
    B = s0.shape[0]
    nc = T // L
    row = pl.BlockSpec((L, D), lambda b, c: (b * nc + c, 0))
    par = pl.BlockSpec((1, D), lambda b, c: (0, 0))
    st = pl.BlockSpec((None, RWKV_HEADS, RWKV_HEAD, RWKV_HEAD), lambda b, c: (b, 0, 0, 0))
    v1 = lambda t: t.reshape(1, D)
    return pl.pallas_call(
        functools.partial(_wkv_kernel, L=L, n_chunks=nc),
        grid=(B, nc),
        in_specs=[row] * 6 + [par] * 5 + [st],
        out_specs=[row, st],
        out_shape=[jax.ShapeDtypeStruct((M, D), F32), jax.ShapeDtypeStruct(s0.shape, F32)],
        scratch_shapes=[pltpu.VMEM((RWKV_HEADS, RWKV_HEAD, RWKV_HEAD), F32)],
        compiler_params=_cparams(2),
        name="wkv_core",
    )(r, k, v, lw, a, g, v1(k_k), v1(k_a), v1(r_k), v1(ln_w), v1(ln_b), s0)


def _wkv_long_kernel(r_ref, k_ref, v_ref, lw_ref, a_ref, g_ref, kk_ref, ka_ref, rk_ref, lnw_ref, lnb_ref, s0_ref,
                     o_ref, sout_ref, s_scr, *, L, NC, n_steps):
    step = pl.program_id(1)
    H, N, G = RWKV_HEADS, RWKV_HEAD, WKV_GROUP_HEADS
    GW = G * N
    NG = H // G
    R = G * L
    TB = NC * L

    @pl.when(step == 0)
    def _():
        s_scr[...] = jnp.zeros_like(s_scr)
        for h in range(H):
            q = h % G
            s_scr[h // G, q * N:(q + 1) * N, q * N:(q + 1) * N] = s0_ref[h]

    ti = lax.broadcasted_iota(jnp.int32, (TB, TB), 0)
    tj = lax.broadcasted_iota(jnp.int32, (TB, TB), 1)
    same_c = (ti // L) == (tj // L)
    lw = lw_ref[...]
    cs = jnp.dot((same_c & (ti >= tj)).astype(F32), lw, preferred_element_type=F32, precision=HI)
    ctot = jnp.dot(same_c.astype(F32), lw, preferred_element_type=F32, precision=HI)
    e_pos, e_neg, e_prev = jnp.exp(cs), jnp.exp(-cs), jnp.exp(cs - lw)
    e_hat, e_tot = jnp.exp(ctot - cs), jnp.exp(ctot)
    gi = lax.broadcasted_iota(jnp.int32, (GW, GW), 0) // N
    gj = lax.broadcasted_iota(jnp.int32, (GW, GW), 1) // N
    gones = (gi == gj).astype(F32)

    def head_sum(x):
        return jnp.concatenate([jnp.dot(x[:, g * GW:(g + 1) * GW], gones, preferred_element_type=F32, precision=HI)
                                for g in range(NG)], axis=1)

    r, k, v, a = r_ref[...], k_ref[...], v_ref[...], a_ref[...]
    kkv = k * kk_ref[...]
    kkn = kkv / jnp.maximum(jnp.sqrt(head_sum(kkv * kkv)), 1e-12)
    kp = k * (1.0 + (a - 1.0) * ka_ref[...])
    ba = kkn * a
    nat = dict(A=-kkn * e_prev, R=r * e_pos, Bt=ba * e_neg, Kt=kp * e_neg, Bh=ba * e_hat, Kh=kp * e_hat, V=v)
    bonus = head_sum(r * kp * rk_ref[...]) * v

    row = lax.broadcasted_iota(jnp.int32, (R, GW), 0)
    lane = lax.broadcasted_iota(jnp.int32, (R, GW), 1)
    hm = (row // L) == (lane // N)

    def bd(name, c, g):
        x = nat[name][c * L:(c + 1) * L, g * GW:(g + 1) * GW]
        return jnp.where(hm, jnp.concatenate([x] * G, axis=0), 0.0)

    mr = lax.broadcasted_iota(jnp.int32, (R, R), 0)
    mc = lax.broadcasted_iota(jnp.int32, (R, R), 1)
    same_h = (mr // L) == (mc // L)
    strict = same_h & (mr > mc)
    incl = same_h & (mr >= mc)

    units = [(c, g) for c in range(NC) for g in range(NG)]
    U = {}
    for u in units:
        ops = {n: bd(n, *u) for n in nat}
        AR = jnp.concatenate([ops["A"], ops["R"]], axis=0).astype(BF16)
        BK = jnp.concatenate([ops["Bt"], ops["Kt"]], axis=0).astype(BF16)
        U[u] = dict(ops=ops, Q=_nt(AR, BK))
    for u in units:
        d = U[u]
        Q = d.pop("Q")
        d["nmat"] = jnp.where(strict, Q[:R, :R], 0.0)
        d["ak"] = jnp.where(strict, Q[:R, R:], 0.0).astype(BF16)
        d["rb"] = jnp.where(incl, Q[R:, :R], 0.0).astype(BF16)
        d["rk"] = jnp.where(incl, Q[R:, R:], 0.0).astype(BF16)
        d["Y"] = jnp.where(((mr // 2) == (mc // 2)) & (mr > mc), d["nmat"], 0.0)
    b = 2
    while b < L:
        lvl = ((mr // (2 * b)) == (mc // (2 * b))) & ((mr % (2 * b)) >= b) & ((mc % (2 * b)) < b)
        for u in units:
            d = U[u]
            Ck = jnp.where(lvl, d["nmat"], 0.0)
            d["Yb"] = d["Y"].astype(BF16)
            d["Tm"] = Ck + jnp.dot(d["Yb"], Ck.astype(BF16), preferred_element_type=F32)
        for u in units:
            d = U[u]
            d["Y"] = d["Y"] + d["Tm"] + jnp.dot(d["Tm"].astype(BF16), d["Yb"], preferred_element_type=F32)
        b *= 2
    for u in units:
        d = U[u]
        vb = d["ops"]["V"].astype(BF16)
        d["akv"] = jnp.dot(d["ak"], vb, preferred_element_type=F32)
        d["rkv"] = jnp.dot(d["rk"], vb, preferred_element_type=F32)
    for u in units:
        d = U[u]
        x = jnp.concatenate([d["ops"]["A"], d["akv"]], axis=1)
        d["AE"] = x + jnp.dot(d["Y"].astype(BF16), x.astype(BF16), preferred_element_type=F32)
    for u in units:
        d = U[u]
        base = jnp.concatenate([d["ops"]["R"], d["rkv"]], axis=1)
        ry = base + jnp.dot(d["rb"], d["AE"].astype(BF16), preferred_element_type=F32)
        d["Rbar"] = ry[:, :GW].astype(BF16)
        d["y0"] = ry[:, GW:]
    for u in units:
        d = U[u]
        bh = d["ops"]["Bh"].astype(BF16)
        kh = d["ops"]["Kh"].astype(BF16)
        d["W"] = _tn(d["AE"][:, :GW].astype(BF16), bh).astype(BF16)
        ev = jnp.concatenate([d["AE"][:, GW:], d["ops"]["V"]], axis=0).astype(BF16)
        d["Z"] = _tn(ev, jnp.concatenate([bh, kh], axis=0))

    ys = []
    for c in range(NC):
        yc = []
        for g in range(NG):
            d = U[(c, g)]
            S = s_scr[g]
            Sb = S.astype(BF16)
            ybd = d["y0"] + _nt(d["Rbar"], Sb)
            yc.append(sum(ybd[q * L:(q + 1) * L] for q in range(G)))
            decay = e_tot[c * L:c * L + 1, g * GW:(g + 1) * GW]
            s_scr[g] = S * decay + jnp.dot(Sb, d["W"], preferred_element_type=F32) + d["Z"]
        ys.append(jnp.concatenate(yc, axis=1))
    y = jnp.concatenate(ys, axis=0)

    mean = head_sum(y) * (1.0 / N)
    yc = y - mean
    var = head_sum(yc * yc) * (1.0 / N)
    yn = yc * lax.rsqrt(var + RWKV_GN_EPS) * lnw_ref[...] + lnb_ref[...]
    o_ref[...] = ((yn + bonus) * g_ref[...]).astype(o_ref.dtype)

    @pl.when(step == n_steps - 1)
    def _():
        for h in range(H):
            q = h % G
            sout_ref[h] = s_scr[h // G, q * N:(q + 1) * N, q * N:(q + 1) * N]


def _wkv_long(r, k, v, lw, a, g, k_k, k_a, r_k, ln_w, ln_b, s0, *, T, L, NC):
    M, D = r.shape
    B = s0.shape[0]
    TB = NC * L
    ns = T // TB
    row = pl.BlockSpec((TB, D), lambda b, c: (b * ns + c, 0))
    par = pl.BlockSpec((1, D), lambda b, c: (0, 0))
    st = pl.BlockSpec((None, RWKV_HEADS, RWKV_HEAD, RWKV_HEAD), lambda b, c: (b, 0, 0, 0))
    v1 = lambda t: t.reshape(1, D)
    GW = WKV_GROUP_HEADS * RWKV_HEAD
    return pl.pallas_call(
        functools.partial(_wkv_long_kernel, L=L, NC=NC, n_steps=ns),
        grid=(B, ns),
        in_specs=[row] * 6 + [par] * 5 + [st],
        out_specs=[row, st],
        out_shape=[jax.ShapeDtypeStruct((M, D), BF16), jax.ShapeDtypeStruct(s0.shape, F32)],
        scratch_shapes=[pltpu.VMEM((RWKV_HEADS // WKV_GROUP_HEADS, GW, GW), F32)],
        compiler_params=_cparams(2),
        name="wkv_long",
    )(r, k, v, lw, a, g, v1(k_k), v1(k_a), v1(r_k), v1(ln_w), v1(ln_b), s0)


def _pad_cols(w, n):
    return jnp.pad(w, ((0, 0), (0, n - w.shape[1])))


def _pad_rows(w, n):
    return jnp.pad(w, ((0, n - w.shape[0]), (0, 0)))


def _state_tiles(st):
    return jnp.pad(st, ((0, 0), (SUBLANES - st.shape[1], 0), (0, 0)))


def _ssm_from_groups(s):
    B = s.shape[0]
    s = s.reshape(B, SSD_GROUPS, SSD_STATE, SSD_HPG, SSD_HEAD_DIM)
    return s.transpose(0, 1, 3, 4, 2).reshape(B, SSD_HEADS, SSD_HEAD_DIM, SSD_STATE)


def kernel(x_prompt, x_sample, c_prompt, c_sample, state_ssm, state_ssd_conv, cache_k, cache_v, cache_logf, page_table, state_wkv, state_shift, state_ffn_conv, ada_w, ada_b, norm_mix, norm_ffn, norm_final, ssd_in_w, ssd_conv_w, ssd_conv_b, ssd_dt_bias, ssd_a_log, ssd_d, ssd_norm_w, ssd_out_w, fox_in_w, fox_f_b, fox_out_w, rwkv_mu, rwkv_w_rkv, rwkv_w0, rwkv_w1, rwkv_w2, rwkv_a0, rwkv_a1, rwkv_a2, rwkv_g1, rwkv_g2, rwkv_k_k, rwkv_k_a, rwkv_r_k, rwkv_ln_w, rwkv_ln_b, rwkv_out_w, ffn_up_w, ffn_conv_w, ffn_conv_b, ffn_down_w):
    BP, TP, D = x_prompt.shape
    BS, TS, _ = x_sample.shape
    groups = [dict(B=BP, T=TP, tm=1024, tm_s=512, ssd_l=SSD_CHUNK, wkv_l=RWKV_CHUNK, gate_tc=512),
              dict(B=BS, T=TS, tm=BS * TS, tm_s=512, ssd_l=TS, wkv_l=TS, gate_tc=TS)]
    xs = [x_prompt.reshape(BP * TP, D), x_sample.reshape(BS * TS, D)]

    n_c = BP + BS
    n_c_pad = -(-n_c // SUBLANES) * SUBLANES
    c_all = jnp.pad(jnp.concatenate([c_prompt, c_sample], axis=0), ((0, n_c_pad - n_c), (0, 0)))
    mods_all = _ada_mods(c_all, ada_w, ada_b).reshape(DEPTH, n_c_pad, 6, 1, D)
    row0 = [0, BP]

    def mods_of(i, gi):
        m = mods_all[i, row0[gi]:row0[gi] + groups[gi]["B"]]
        return [m[:, k] for k in range(6)]

    outs = {k: ([], []) for k in ("ssm", "sconv", "k", "v", "lf", "wkv", "shift", "fconv")}
    n_pool = cache_k.shape[1]

    for i in range(DEPTH):
        kind, j = i % N_MIXERS, i // N_MIXERS
        for gi, gp in enumerate(groups):
            B, T, tm, tm_s = gp["B"], gp["T"], gp["tm"], gp["tm_s"]
            x = xs[gi]
            prompt = gi == 0
            sh_m, sc_m, gt_m, sh_f, sc_f, gt_f = mods_of(i, gi)
            if kind == 0:
                w_main = ssd_in_w[j][:, :SSD_D_INNER + SSD_CONV_CH].astype(BF16)
                w_dt = _pad_cols(ssd_in_w[j][:, SSD_D_INNER + SSD_CONV_CH:], LANES).astype(BF16)
                zx = _norm_mod_matmul(x, norm_mix[i], sc_m, sh_m, w_main, T=T, tm=tm, tn=1024)
                dt = _norm_mod_matmul(x, norm_mix[i], sc_m, sh_m, w_dt, T=T, tm=tm, tn=LANES)
                if prompt:
                    conv_tiles = jnp.zeros((B, SUBLANES, SSD_CONV_CH), F32)
                    h0 = jnp.zeros((B, SSD_GROUPS, SSD_STATE, SSD_HPG * SSD_HEAD_DIM), F32)
                else:
                    conv_tiles = _state_tiles(state_ssd_conv[j])
                    h0 = state_ssm[j].reshape(B, SSD_GROUPS, SSD_HPG * SSD_HEAD_DIM, SSD_STATE)
                y, ctail, h_new = _ssd_core(zx, dt, conv_tiles, h0, ssd_conv_w[j], ssd_conv_b[j], ssd_dt_bias[j],
                                            ssd_a_log[j], ssd_d[j], ssd_norm_w[j], B=B, T=T, L=gp["ssd_l"],
                                            natural=not prompt)
                outs["ssm"][gi].append(_ssm_from_groups(h_new) if prompt else
                                       h_new.reshape(B, SSD_HEADS, SSD_HEAD_DIM, SSD_STATE))
                outs["sconv"][gi].append(ctail[:, SUBLANES - (SSD_CONV - 1):])
                x = _res_matmul(y, ssd_out_w[j].astype(BF16), x, gt_m, T=T, tm=tm)
            elif kind == 1:
                w_qkv = fox_in_w[j][:, :3 * FOX_INNER].reshape(D, 3, FOX_INNER).transpose(1, 0, 2).astype(BF16)
                w_f = _pad_cols(fox_in_w[j][:, 3 * FOX_INNER:], LANES).astype(BF16)
                q, k, v = _norm_mod_matmul_multi(x, norm_mix[i], sc_m, sh_m, w_qkv, T=T, tm=tm_s)
                f = _norm_mod_matmul(x, norm_mix[i], sc_m, sh_m, w_f, T=T, tm=tm, tn=LANES)
                f_b = jnp.pad(fox_f_b[j], (0, LANES - FOX_HEADS)).reshape(1, LANES)
                lf, cum = _fox_gate(f, f_b, B=B, T=T, tc=gp["gate_tc"])
                cum_t = jnp.swapaxes(cum[:, :FOX_HEADS].reshape(B, T, FOX_HEADS), 1, 2)
                if prompt:
                    o = _fox_flash(q, k, v, cum, cum_t, B=B, T=T, tq=512, tk=1024)
                else:
                    o = _fox_sample(q, k, v, cum, cum_t, jnp.transpose(cache_k[j], (0, 2, 3, 1)),
                                    jnp.transpose(cache_v[j], (0, 2, 3, 1)), jnp.swapaxes(cache_logf[j], 1, 2),
                                    page_table, B=B, TS=T, pps=FOX_PAGES_PER_STEP)
                hd = lambda t: t.reshape(B, T, FOX_HEADS, FOX_HEAD_DIM)
                outs["k"][gi].append(hd(k))
                outs["v"][gi].append(hd(v))
                outs["lf"][gi].append(lf[:, :FOX_HEADS].reshape(B, T, FOX_HEADS))
                x = _res_matmul(o, fox_out_w[j].astype(BF16), x, gt_m, T=T, tm=tm)
            else:
                if prompt:
                    st_tiles = jnp.zeros((B, SUBLANES, D), F32)
                    s0 = jnp.zeros((B, RWKV_HEADS, RWKV_HEAD, RWKV_HEAD), F32)
                else:
                    st_tiles = _state_tiles(state_shift[j][:, None, :])
                    s0 = state_wkv[j]
                r, k, v, lw, a, g, tails = _rwkv_in(
                    x, norm_mix[i], sc_m, sh_m, st_tiles, rwkv_mu[j], rwkv_w_rkv[j].astype(BF16), rwkv_w0[j],
                    _pad_cols(rwkv_w1[j], LANES).astype(BF16), _pad_rows(rwkv_w2[j], LANES).astype(BF16), rwkv_a0[j],
                    _pad_cols(rwkv_a1[j], LANES).astype(BF16), _pad_rows(rwkv_a2[j], LANES).astype(BF16),
                    rwkv_g1[j].astype(BF16), rwkv_g2[j].astype(BF16), T=T, tm=tm_s)
                wkv_par = (rwkv_k_k[j], rwkv_k_a[j], rwkv_r_k[j].reshape(-1), rwkv_ln_w[j], rwkv_ln_b[j], s0)
                if prompt:
                    o, s_new = _wkv_long(r, k, v, lw, a, g, *wkv_par, T=T, L=gp["wkv_l"], NC=WKV_STEP_CHUNKS)
                else:
                    o, s_new = _wkv_core(r, k, v, lw, a, g, *wkv_par, T=T, L=gp["wkv_l"])
                outs["wkv"][gi].append(s_new)
                outs["shift"][gi].append(tails[:, SUBLANES - 1])
                x = _res_matmul(o, rwkv_out_w[j].astype(BF16), x, gt_m, T=T, tm=tm)
            if prompt:
                ffn_tiles = jnp.zeros((B, SUBLANES, D_FF), F32)
            else:
                ffn_tiles = _state_tiles(state_ffn_conv[i])
            x, tails = _conv_ffn(x, norm_ffn[i], sc_f, sh_f, gt_f, ffn_tiles, ffn_up_w[i].astype(BF16),
                                 ffn_conv_w[i], ffn_conv_b[i], ffn_down_w[i].astype(BF16), T=T, tm=tm_s)
            outs["fconv"][gi].append(tails[:, SUBLANES - (FFN_CONV - 1):])
            xs[gi] = x

    y_prompt = _rmsnorm(xs[0], norm_final, tm=1024).reshape(BP, TP, D)
    y_sample = _rmsnorm(xs[1], norm_final, tm=1024).reshape(BS, TS, D)
    st = lambda name, gi: jnp.stack(outs[name][gi])
    return (y_prompt, y_sample, st("ssm", 0), st("ssm", 1), st("sconv", 0), st("sconv", 1),
            st("k", 0), st("k", 1), st("v", 0), st("v", 1), st("lf", 0), st("lf", 1),
            st("wkv", 0), st("wkv", 1), st("shift", 0), st("shift", 1), st("fconv", 0), st("fconv", 1))
```

```python
import functools

import jax
import jax.numpy as jnp
from jax import lax
from jax.experimental import pallas as pl
from jax.experimental.pallas import tpu as pltpu

F32 = jnp.float32
BF16 = jnp.bfloat16
HI = lax.Precision.HIGHEST

D_MODEL = 1024
DEPTH = 4
N_MIXERS = 3
RMS_EPS = 1e-6
SSD_D_INNER = 2048
SSD_HEAD_DIM = 64
SSD_HEADS = 32
SSD_GROUPS = 4
SSD_HPG = 8
SSD_STATE = 128
SSD_CONV = 4
SSD_CONV_CH = 3072
SSD_CHUNK = 128
FOX_HEAD_DIM = 64
FOX_HEADS = 16
FOX_INNER = 1024
RWKV_HEAD = 64
RWKV_HEADS = 16
RWKV_GN_EPS = 64e-5
RWKV_CHUNK = 32
WKV_ROWS = 128
WKV_GROUP_HEADS = 4
WKV_STEP_CHUNKS = 4
FOX_PAGES_PER_STEP = 8
D_FF = 2816
FFN_CONV = 3

SUBLANES = 8
LANES = 128
VMEM_LIMIT = 56 * 1024 * 1024


def _cparams(n_grid):
    return pltpu.CompilerParams(dimension_semantics=("arbitrary",) * n_grid,
                                vmem_limit_bytes=VMEM_LIMIT)


def _nt(a, b):
    return lax.dot_general(a, b, (((1,), (1,)), ((), ())), preferred_element_type=F32)


def _tn(a, b):
    return lax.dot_general(a, b, (((0,), (0,)), ((), ())), preferred_element_type=F32)


def _modnorm(x, g, sc, sh, T):
    xn = x * lax.rsqrt(jnp.mean(x * x, axis=-1, keepdims=True) + RMS_EPS) * g
    if sc.ndim == 2:
        return xn * (1.0 + sc) + sh
    tm, D = x.shape
    x3 = xn.reshape(tm // T, T, D)
    return (x3 * (1.0 + sc) + sh).reshape(tm, D)


def _gate_mul(y, gt, T):
    if gt.ndim == 2:
        return y * gt
    tm, D = y.shape
    return (y.reshape(tm // T, T, D) * gt).reshape(tm, D)


def _shift_rows(cur3, prev3, k):
    row = lax.broadcasted_iota(jnp.int32, cur3.shape, 1)
    return pltpu.roll(jnp.where(row < SUBLANES - k, cur3, prev3), k, 1)


def _mod_spec(T, tm, D):
    if T >= tm:
        per = T // tm
        return pl.BlockSpec((None, 1, D), lambda i, *_: (i // per, 0, 0))
    return pl.BlockSpec((tm // T, 1, D), lambda i, *_: (i, 0, 0))


def _tile_spec(T, tm, C):
    if T >= tm:
        per = T // tm
        return pl.BlockSpec((None, SUBLANES, C), lambda i, *_: (i // per, 0, 0))
    return pl.BlockSpec((tm // T, SUBLANES, C), lambda i, *_: (i, 0, 0))


def _ada_kernel(c_ref, w_ref, b_ref, o_ref):
    c = c_ref[...]
    a = (c * jax.nn.sigmoid(c)).astype(BF16)
    o_ref[...] = jnp.dot(a, w_ref[...].astype(BF16), preferred_element_type=F32) + b_ref[...]


def _ada_mods(c_all, ada_w, ada_b):
    R, D = c_all.shape
    L, _, N = ada_w.shape
    tn = 1536
    return pl.pallas_call(
        _ada_kernel,
        grid=(L, N // tn),
        in_specs=[pl.BlockSpec((R, D), lambda l, j: (0, 0)),
                  pl.BlockSpec((None, D, tn), lambda l, j: (l, 0, j)),
                  pl.BlockSpec((None, 1, tn), lambda l, j: (l, 0, j))],
        out_specs=pl.BlockSpec((None, R, tn), lambda l, j: (l, 0, j)),
        out_shape=jax.ShapeDtypeStruct((L, R, N), F32),
        compiler_params=_cparams(2),
        name="ada_mods",
    )(c_all, ada_w, ada_b.reshape(L, 1, N))


def _nmm_kernel(x_ref, g_ref, sc_ref, sh_ref, w_ref, o_ref, h_scr, *, T):
    @pl.when(pl.program_id(1) == 0)
    def _():
        h_scr[...] = _modnorm(x_ref[...], g_ref[...], sc_ref[...], sh_ref[...], T).astype(BF16)
    o_ref[...] = jnp.dot(h_scr[...], w_ref[...], preferred_element_type=F32).astype(o_ref.dtype)


def _norm_mod_matmul(x, g, sc, sh, w, *, T, tm, tn):
    M, D = x.shape
    N = w.shape[1]
    return pl.pallas_call(
        functools.partial(_nmm_kernel, T=T),
        grid=(M // tm, N // tn),
        in_specs=[pl.BlockSpec((tm, D), lambda i, j: (i, 0)),
                  pl.BlockSpec((1, D), lambda i, j: (0, 0)),
                  _mod_spec(T, tm, D), _mod_spec(T, tm, D),
                  pl.BlockSpec((D, tn), lambda i, j: (0, j))],
        out_specs=pl.BlockSpec((tm, tn), lambda i, j: (i, j)),
        out_shape=jax.ShapeDtypeStruct((M, N), F32),
        scratch_shapes=[pltpu.VMEM((tm, D), BF16)],
        compiler_params=_cparams(2),
        name="norm_mod_matmul",
    )(x, g.reshape(1, D), sc, sh, w)


def _nmm_multi_kernel(x_ref, g_ref, sc_ref, sh_ref, w_ref, *rest, T, n_out):
    outs, h_scr = rest[:n_out], rest[n_out]
    j = pl.program_id(1)

    @pl.when(j == 0)
    def _():
        h_scr[...] = _modnorm(x_ref[...], g_ref[...], sc_ref[...], sh_ref[...], T).astype(BF16)
    res = jnp.dot(h_scr[...], w_ref[...], preferred_element_type=F32)
    for o in range(n_out):
        @pl.when(j == o)
        def _(o=o):
            outs[o][...] = res


def _norm_mod_matmul_multi(x, g, sc, sh, w_stack, *, T, tm):
    M, D = x.shape
    n_out, _, N = w_stack.shape
    return pl.pallas_call(
        functools.partial(_nmm_multi_kernel, T=T, n_out=n_out),
        grid=(M // tm, n_out),
        in_specs=[pl.BlockSpec((tm, D), lambda i, j: (i, 0)),
                  pl.BlockSpec((1, D), lambda i, j: (0, 0)),
                  _mod_spec(T, tm, D), _mod_spec(T, tm, D),
                  pl.BlockSpec((None, D, N), lambda i, j: (j, 0, 0))],
        out_specs=[pl.BlockSpec((tm, N), lambda i, j: (i, 0)) for _ in range(n_out)],
        out_shape=[jax.ShapeDtypeStruct((M, N), F32) for _ in range(n_out)],
        scratch_shapes=[pltpu.VMEM((tm, D), BF16)],
        compiler_params=_cparams(2),
        name="norm_mod_matmul_multi",
    )(x, g.reshape(1, D), sc, sh, w_stack)


def _res_mm_kernel(a_ref, w_ref, x_ref, gt_ref, o_ref, *, T):
    y = jnp.dot(a_ref[...].astype(BF16), w_ref[...], preferred_element_type=F32)
    o_ref[...] = x_ref[...] + _gate_mul(y, gt_ref[...], T)


def _res_matmul(a, w, x, gt, *, T, tm):
    M, K = a.shape
    D = w.shape[1]
    return pl.pallas_call(
        functools.partial(_res_mm_kernel, T=T),
        grid=(M // tm,),
        in_specs=[pl.BlockSpec((tm, K), lambda i: (i, 0)),
                  pl.BlockSpec((K, D), lambda i: (0, 0)),
                  pl.BlockSpec((tm, D), lambda i: (i, 0)),
                  _mod_spec(T, tm, D)],
        out_specs=pl.BlockSpec((tm, D), lambda i: (i, 0)),
        out_shape=jax.ShapeDtypeStruct((M, D), F32),
        compiler_params=_cparams(1),
        name="res_matmul",
    )(a, w, x, gt)


def _ffn_kernel(x_ref, g_ref, sc_ref, sh_ref, gt_ref, st_ref, upw_ref, cw_ref, cb_ref, dnw_ref,
                o_ref, tail_ref, act_scr, carry_scr, *, T, F, fc):
    i = pl.program_id(0)
    tm, D = x_ref.shape
    nt = tm // SUBLANES
    long_seq = T >= tm
    x = x_ref[...]
    h = _modnorm(x, g_ref[...], sc_ref[...], sh_ref[...], T).astype(BF16)
    if long_seq:
        @pl.when(i % (T // tm) == 0)
        def _():
            carry_scr[...] = st_ref[...]
    for c in range(F // fc):
        lo, hi = c * fc, (c + 1) * fc
        gc = jnp.dot(h, upw_ref[:, lo:hi], preferred_element_type=F32)
        uc = jnp.dot(h, upw_ref[:, F + lo:F + hi], preferred_element_type=F32)
        g3 = gc.reshape(nt, SUBLANES, fc)
        if long_seq:
            prev3 = jnp.concatenate([carry_scr[:, lo:hi][None], g3[:nt - 1]], axis=0)
            carry_scr[:, lo:hi] = g3[nt - 1]
            tail_ref[:, lo:hi] = g3[nt - 1]
        else:
            prev3 = st_ref[:, :, lo:hi]
            tail_ref[:, :, lo:hi] = g3
        conv = (g3 * cw_ref[2:3, lo:hi] + _shift_rows(g3, prev3, 1) * cw_ref[1:2, lo:hi]
                + _shift_rows(g3, prev3, 2) * cw_ref[0:1, lo:hi] + cb_ref[:, lo:hi])
        act = conv * jax.nn.sigmoid(conv) * uc.reshape(nt, SUBLANES, fc)
        act_scr[:, lo:hi] = act.reshape(tm, fc).astype(BF16)
    y = jnp.dot(act_scr[...], dnw_ref[...], preferred_element_type=F32)
    o_ref[...] = x + _gate_mul(y, gt_ref[...], T)


def _conv_ffn(x, g, sc, sh, gt, st_tiles, up_w, conv_w, conv_b, down_w, *, T, tm):
    M, D = x.shape
    F = down_w.shape[0]
    B = st_tiles.shape[0]
    return pl.pallas_call(
        functools.partial(_ffn_kernel, T=T, F=F, fc=256),
        grid=(M // tm,),
        in_specs=[pl.BlockSpec((tm, D), lambda i: (i, 0)),
                  pl.BlockSpec((1, D), lambda i: (0, 0)),
                  _mod_spec(T, tm, D), _mod_spec(T, tm, D), _mod_spec(T, tm, D),
                  _tile_spec(T, tm, F),
                  pl.BlockSpec((D, 2 * F), lambda i: (0, 0)),
                  pl.BlockSpec((FFN_CONV, F), lambda i: (0, 0)),
                  pl.BlockSpec((1, F), lambda i: (0, 0)),
                  pl.BlockSpec((F, D), lambda i: (0, 0))],
        out_specs=[pl.BlockSpec((tm, D), lambda i: (i, 0)), _tile_spec(T, tm, F)],
        out_shape=[jax.ShapeDtypeStruct((M, D), F32), jax.ShapeDtypeStruct((B, SUBLANES, F), F32)],
        scratch_shapes=[pltpu.VMEM((tm, F), BF16), pltpu.VMEM((SUBLANES, F), F32)],
        compiler_params=_cparams(1),
        name="conv_ffn",
    )(x, g.reshape(1, D), sc, sh, gt, st_tiles, up_w, conv_w, conv_b.reshape(1, F), down_w)


def _rms_kernel(x_ref, g_ref, o_ref):
    x = x_ref[...]
    o_ref[...] = x * lax.rsqrt(jnp.mean(x * x, axis=-1, keepdims=True) + RMS_EPS) * g_ref[...]


def _rmsnorm(x, g, *, tm):
    M, D = x.shape
    return pl.pallas_call(
        _rms_kernel, grid=(M // tm,),
        in_specs=[pl.BlockSpec((tm, D), lambda i: (i, 0)), pl.BlockSpec((1, D), lambda i: (0, 0))],
        out_specs=pl.BlockSpec((tm, D), lambda i: (i, 0)),
        out_shape=jax.ShapeDtypeStruct((M, D), F32),
        compiler_params=_cparams(1), name="final_rmsnorm",
    )(x, g.reshape(1, D))


def _conv_silu(cur, prev_tile, w_ref, b_ref, lo, hi):
    L, C = cur.shape
    nt = L // SUBLANES
    c3 = cur.reshape(nt, SUBLANES, C)
    p3 = prev_tile[None] if nt == 1 else jnp.concatenate([prev_tile[None], c3[:nt - 1]], axis=0)
    out = c3 * w_ref[3:4, lo:hi] + b_ref[:, lo:hi]
    for k in range(1, SSD_CONV):
        out = out + _shift_rows(c3, p3, k) * w_ref[3 - k:4 - k, lo:hi]
    out = out * jax.nn.sigmoid(out)
    return out.reshape(L, C)


def _ssd_kernel(z0_ref, z1_ref, x0_ref, x1_ref, bc_ref, dt_ref, cst_ref, h0_ref, cw_ref, cb_ref, dtb_ref, alog_ref,
                dw_ref, nw_ref, ex_ref, *rest, L, n_chunks, natural):
    y_ref, ctail_ref, hout_ref, tail_scr, st_scr = rest[-5:]
    c = pl.program_id(1)
    P, Nn, G, J = SSD_HEAD_DIM, SSD_STATE, SSD_GROUPS, SSD_HPG
    GW = J * P

    @pl.when(c == 0)
    def _():
        tail_scr[...] = cst_ref[...]
        st_scr[...] = h0_ref[...]

    conv = []
    for q, blk_ref in enumerate((x0_ref, x1_ref, bc_ref)):
        blk = blk_ref[...]
        lo, hi = q * 1024, (q + 1) * 1024
        conv.append(_conv_silu(blk, tail_scr[:, lo:hi], cw_ref, cb_ref, lo, hi))
        tail_scr[:, lo:hi] = blk[L - SUBLANES:, :]
        ctail_ref[:, lo:hi] = blk[L - SUBLANES:, :]
    xs = jnp.concatenate(conv[:2], axis=1)
    Bm = conv[2][:, :G * Nn]
    Cm = conv[2][:, G * Nn:]

    dt = jax.nn.softplus(dt_ref[...] + dtb_ref[...])
    a = dt * (-jnp.exp(alog_ref[...]))
    ri = lax.broadcasted_iota(jnp.int32, (L, L), 0)
    ci = lax.broadcasted_iota(jnp.int32, (L, L), 1)
    causal = ri >= ci
    acs = jnp.dot(causal.astype(F32), a, preferred_element_type=F32, precision=HI)
    ex = ex_ref[...]
    dt_w = jnp.dot(dt, ex, preferred_element_type=F32, precision=HI)
    acs_w = jnp.dot(acs, ex, preferred_element_type=F32, precision=HI)
    acs_t = acs.T
    last_w = acs_w[L - 1:L, :]
    X = xs * dt_w
    Xd = X * jnp.exp(last_w - acs_w)
    dstart = jnp.exp(acs_w)
    cdec = jnp.exp(last_w)
    zz = jnp.concatenate([z0_ref[...], z1_ref[...]], axis=1)
    zg = zz * jax.nn.sigmoid(zz)
    dwide = dw_ref[...]
    nw = nw_ref[...]
    Xb = X.astype(BF16)
    for g in range(G):
        gl = slice(g * GW, (g + 1) * GW)
        Bg = Bm[:, g * Nn:(g + 1) * Nn].astype(BF16)
        Cg = Cm[:, g * Nn:(g + 1) * Nn].astype(BF16)
        CB = _nt(Cg, Bg)
        st = st_scr[g]
        if natural:
            y_off = _nt(Cg, st.astype(BF16)) * dstart[:, gl]
            dec_h = jnp.exp(acs_t[g * J:(g + 1) * J, L - 1:L])
            dcol = jnp.broadcast_to(dec_h[:, None, :], (J, P, 1)).reshape(GW, 1)
            st_scr[g] = st * dcol + _tn(Xd[:, gl].astype(BF16), Bg)
        else:
            y_off = jnp.dot(Cg, st.astype(BF16), preferred_element_type=F32) * dstart[:, gl]
            st_scr[g] = st * cdec[:, gl] + _tn(Bg, Xd[:, gl].astype(BF16))
        ys = []
        for j in range(J):
            h = g * J + j
            diff = acs[:, h:h + 1] - acs_t[h:h + 1, :]
            dec = jnp.exp(jnp.where(causal, diff, -jnp.inf))
            Mh = (CB * dec).astype(BF16)
            ys.append(jnp.dot(Mh, Xb[:, h * P:(h + 1) * P], preferred_element_type=F32))
        yg = jnp.concatenate(ys, axis=1) + y_off + dwide[:, gl] * xs[:, gl]
        yg = yg * zg[:, gl]
        yg = yg * lax.rsqrt(jnp.mean(yg * yg, axis=-1, keepdims=True) + RMS_EPS) * nw[:, gl]
        y_ref[:, gl] = yg.astype(y_ref.dtype)

    @pl.when(c == n_chunks - 1)
    def _():
        hout_ref[...] = st_scr[...]


def _ssd_core(zx, dt, conv_tiles, h0, conv_w, conv_b, dt_bias, a_log, d_skip, norm_w, *, B, T, L, natural,
              layer=0, out_buf=None):
    M = zx.shape[0]
    nc = T // L
    DI, C = SSD_D_INNER, SSD_CONV_CH
    col = lambda q: pl.BlockSpec((L, 1024), lambda b, c, q=q: (b * nc + c, q))
    par = lambda n: pl.BlockSpec((1, n), lambda b, c: (0, 0))
    pad = lambda v: jnp.pad(v.astype(F32), (0, LANES - v.shape[0])).reshape(1, LANES)
    expand = (jnp.arange(LANES)[:, None] == (jnp.arange(DI) // SSD_HEAD_DIM)[None, :]).astype(F32)
    d_wide = jnp.repeat(d_skip.astype(F32), SSD_HEAD_DIM).reshape(1, DI)
    st_shape = h0.shape[2:]
    st_spec = pl.BlockSpec((None, None) + st_shape, lambda b, c: (layer, b, 0, 0, 0))
    tile_spec = pl.BlockSpec((None, SUBLANES, C), lambda b, c: (b, 0, 0))
    in_specs = [col(0), col(1), col(2), col(3), col(4),
                pl.BlockSpec((L, LANES), lambda b, c: (b * nc + c, 0)),
                tile_spec, st_spec,
                pl.BlockSpec((SSD_CONV, C), lambda b, c: (0, 0)), par(C), par(LANES), par(LANES),
                par(DI), par(DI), pl.BlockSpec((LANES, DI), lambda b, c: (0, 0))]
    args = [zx, zx, zx, zx, zx, dt, conv_tiles, h0, conv_w, conv_b.reshape(1, C), pad(dt_bias), pad(a_log),
            d_wide, norm_w.reshape(1, DI), expand]
    aliases = {}
    if out_buf is not None:
        in_specs.append(pl.BlockSpec(memory_space=pl.ANY))
        args.append(out_buf)
        aliases = {len(args) - 1: 2}
    return pl.pallas_call(
        functools.partial(_ssd_kernel, L=L, n_chunks=nc, natural=natural),
        grid=(B, nc),
        in_specs=in_specs,
        out_specs=[pl.BlockSpec((L, DI), lambda b, c: (b * nc + c, 0)), tile_spec, st_spec],
        out_shape=[jax.ShapeDtypeStruct((M, DI), BF16 if L % 16 == 0 else F32),
                   jax.ShapeDtypeStruct((B, SUBLANES, C), F32), jax.ShapeDtypeStruct(h0.shape, F32)],
        scratch_shapes=[pltpu.VMEM((SUBLANES, C), F32), pltpu.VMEM(st_shape, F32)],
        input_output_aliases=aliases,
        compiler_params=_cparams(2), name="ssd_core",
    )(*args)


def _fox_gate_kernel(f_ref, fb_ref, lf_ref, cum_ref, carry_scr):
    c = pl.program_id(1)
    tc = f_ref.shape[0]

    @pl.when(c == 0)
    def _():
        carry_scr[...] = jnp.zeros_like(carry_scr)
    lf = jax.nn.log_sigmoid(f_ref[...] + fb_ref[...])
    ri = lax.broadcasted_iota(jnp.int32, (tc, tc), 0)
    ci = lax.broadcasted_iota(jnp.int32, (tc, tc), 1)
    cum = jnp.dot((ri >= ci).astype(F32), lf, preferred_element_type=F32, precision=HI) + carry_scr[...]
    lf_ref[...] = lf
    cum_ref[...] = cum
    carry_scr[...] = cum[tc - 1:tc, :]


def _fox_gate(f_raw, f_b_pad, *, B, T, tc):
    M, W = f_raw.shape
    nc = T // tc
    row = pl.BlockSpec((tc, W), lambda b, c: (b * nc + c, 0))
    return pl.pallas_call(
        _fox_gate_kernel, grid=(B, nc),
        in_specs=[row, pl.BlockSpec((1, W), lambda b, c: (0, 0))],
        out_specs=[row, row],
        out_shape=[jax.ShapeDtypeStruct((M, W), F32)] * 2,
        scratch_shapes=[pltpu.VMEM((1, W), F32)],
        compiler_params=_cparams(2), name="fox_gate",
    )(f_raw, f_b_pad)


def _fox_flash_kernel(q_ref, k_ref, v_ref, cq_ref, ck_ref, o_ref, m_scr, l_scr, acc_scr, *, tq, tk, nk):
    qi, ki = pl.program_id(1), pl.program_id(2)
    H, Dh = FOX_HEADS, FOX_HEAD_DIM
    scale = Dh ** -0.5

    @pl.when(ki == 0)
    def _():
        m_scr[...] = jnp.full_like(m_scr, -jnp.inf)
        l_scr[...] = jnp.zeros_like(l_scr)
        acc_scr[...] = jnp.zeros_like(acc_scr)

    def compute(masked):
        if masked:
            rowp = qi * tq + lax.broadcasted_iota(jnp.int32, (tq, tk), 0)
            colp = ki * tk + lax.broadcasted_iota(jnp.int32, (tq, tk), 1)
            keep = colp <= rowp
        for h in range(H):
            sl = slice(h * Dh, (h + 1) * Dh)
            qh = (q_ref[:, sl] * scale).astype(BF16)
            s = _nt(qh, k_ref[:, sl].astype(BF16))
            s = s + cq_ref[:, h:h + 1] - ck_ref[h:h + 1, :]
            if masked:
                s = jnp.where(keep, s, -jnp.inf)
            m_prev = m_scr[h]
            m_new = jnp.maximum(m_prev, jnp.max(s, axis=-1, keepdims=True))
            p = jnp.exp(s - m_new)
            alpha = jnp.exp(m_prev - m_new)
            l_scr[h] = alpha * l_scr[h] + jnp.sum(p, axis=-1, keepdims=True)
            acc_scr[:, sl] = alpha * acc_scr[:, sl] + jnp.dot(p.astype(BF16), v_ref[:, sl].astype(BF16),
                                                              preferred_element_type=F32)
            m_scr[h] = m_new

    needed = ki * tk <= qi * tq + (tq - 1)
    diag = ki * tk + (tk - 1) > qi * tq

    @pl.when(needed & diag)
    def _():
        compute(True)

    @pl.when(needed & jnp.logical_not(diag))
    def _():
        compute(False)

    @pl.when(ki == nk - 1)
    def _():
        for h in range(H):
            sl = slice(h * Dh, (h + 1) * Dh)
            o_ref[:, sl] = (acc_scr[:, sl] / l_scr[h]).astype(o_ref.dtype)


def _fox_flash(q, k, v, cum, cum_t, *, B, T, tq, tk):
    M, D = q.shape
    nq, nk = T // tq, T // tk
    last_k = lambda qi: (qi * tq + tq - 1) // tk
    qrow = lambda b, qi, ki: (b * nq + qi, 0)
    krow = lambda b, qi, ki: (b * nk + jnp.minimum(ki, last_k(qi)), 0)
    return pl.pallas_call(
        functools.partial(_fox_flash_kernel, tq=tq, tk=tk, nk=nk),
        grid=(B, nq, nk),
        in_specs=[pl.BlockSpec((tq, D), qrow), pl.BlockSpec((tk, D), krow), pl.BlockSpec((tk, D), krow),
                  pl.BlockSpec((tq, cum.shape[1]), qrow),
                  pl.BlockSpec((None, FOX_HEADS, tk), lambda b, qi, ki: (b, 0, jnp.minimum(ki, last_k(qi))))],
        out_specs=pl.BlockSpec((tq, D), qrow),
        out_shape=jax.ShapeDtypeStruct((M, D), BF16),
        scratch_shapes=[pltpu.VMEM((FOX_HEADS, tq, 1), F32), pltpu.VMEM((FOX_HEADS, tq, 1), F32),
                        pltpu.VMEM((tq, D), F32)],
        compiler_params=_cparams(3), name="fox_flash",
    )(q, k, v, cum, cum_t)


def _fox_sample_kernel(pt_ref, q_ref, kn_ref, vn_ref, cn_ref, cnt_ref, *rest, n_steps, pps):
    kt_refs, vt_refs, lft_refs = rest[:pps], rest[pps:2 * pps], rest[2 * pps:3 * pps]
    o_ref, qbd_scr, cq_scr, later_scr, m_scr, l_scr, acc_scr = rest[3 * pps:]
    p = pl.program_id(1)
    H, Dh = FOX_HEADS, FOX_HEAD_DIM
    TS = q_ref.shape[0]
    R = H * TS
    D = H * Dh
    KB = pps * kt_refs[0].shape[-1]

    @pl.when(p == 0)
    def _():
        row = lax.broadcasted_iota(jnp.int32, (R, D), 0)
        lane = lax.broadcasted_iota(jnp.int32, (R, D), 1)
        q_rep = jnp.broadcast_to(q_ref[...][None], (H, TS, D)).reshape(R, D) * (Dh ** -0.5)
        qbd_scr[...] = jnp.where(lane // Dh == row // TS, q_rep, 0.0).astype(BF16)
        cw = cn_ref.shape[1]
        c_rep = jnp.broadcast_to(cn_ref[...][None], (H, TS, cw)).reshape(R, cw)
        r2 = lax.broadcasted_iota(jnp.int32, (R, cw), 0)
        l2 = lax.broadcasted_iota(jnp.int32, (R, cw), 1)
        cq_scr[...] = jnp.sum(jnp.where(l2 == r2 // TS, c_rep, 0.0), axis=-1, keepdims=True)
        later_scr[...] = jnp.zeros_like(later_scr)
        m_scr[...] = jnp.full_like(m_scr, -jnp.inf)
        l_scr[...] = jnp.zeros_like(l_scr)
        acc_scr[...] = jnp.zeros_like(acc_scr)

    def online(s, pv):
        m_prev = m_scr[...]
        m_new = jnp.maximum(m_prev, jnp.max(s, axis=-1, keepdims=True))
        pr = jnp.exp(s - m_new)
        alpha = jnp.exp(m_prev - m_new)
        l_scr[...] = alpha * l_scr[...] + jnp.sum(pr, axis=-1, keepdims=True)
        acc_scr[...] = alpha * acc_scr[...] + pv(pr.astype(BF16))
        m_scr[...] = m_new

    def past_pages():
        lft = jnp.concatenate([r[...] for r in lft_refs], axis=1)
        kt = jnp.concatenate([r[...].reshape(D, -1).astype(BF16) for r in kt_refs], axis=1)
        vt = jnp.concatenate([r[...].reshape(D, -1).astype(BF16) for r in vt_refs], axis=1)
        si = lax.broadcasted_iota(jnp.int32, (KB, KB), 0)
        ki = lax.broadcasted_iota(jnp.int32, (KB, KB), 1)
        suf = jnp.dot(lft, (si > ki).astype(F32), preferred_element_type=F32, precision=HI) + later_scr[...]
        later_scr[...] = later_scr[...] + jnp.sum(lft, axis=-1, keepdims=True)
        bias = jnp.broadcast_to(suf[:, None, :], (H, TS, KB)).reshape(R, KB)
        s = jnp.dot(qbd_scr[...], kt, preferred_element_type=F32) + cq_scr[...] + bias
        online(s, lambda pr: _nt(pr, vt))

    past_pages()

    @pl.when(p == n_steps - 1)
    def _():
        cnt = cnt_ref[...]
        bias = jnp.broadcast_to(cnt[:, None, :], (H, TS, TS)).reshape(R, TS)
        s = _nt(qbd_scr[...], kn_ref[...].astype(BF16)) + cq_scr[...] - bias
        tq = lax.broadcasted_iota(jnp.int32, (R, TS), 0) % TS
        tk = lax.broadcasted_iota(jnp.int32, (R, TS), 1)
        s = jnp.where(tk <= tq, s, -jnp.inf)
        vn = vn_ref[...].astype(BF16)
        online(s, lambda pr: jnp.dot(pr, vn, preferred_element_type=F32))
        row = lax.broadcasted_iota(jnp.int32, (R, D), 0)
        lane = lax.broadcasted_iota(jnp.int32, (R, D), 1)
        own = jnp.where(lane // Dh == row // TS, acc_scr[...] / l_scr[...], 0.0)
        o_ref[...] = jnp.sum(own.reshape(H, TS, D), axis=0)


def _fox_sample(q, k_new, v_new, cum_new, cum_new_t, cache_kt, cache_vt, cache_logf_t, page_table, *, B, TS, pps):
    M, D = q.shape
    n_pages = page_table.shape[1]
    n_steps = n_pages // pps
    _, H, Dh, PG = cache_kt.shape
    R = H * TS
    row = pl.BlockSpec((TS, D), lambda b, p, pt: (b, 0))

    def page(s, nd):
        return lambda b, p, pt: (pt[b * n_pages + n_pages - (p + 1) * pps + s],) + (0,) * nd

    slots = range(pps)
    grid_spec = pltpu.PrefetchScalarGridSpec(
        num_scalar_prefetch=1, grid=(B, n_steps),
        in_specs=[row, row, row,
                  pl.BlockSpec((TS, cum_new.shape[1]), lambda b, p, pt: (b, 0)),
                  pl.BlockSpec((None, H, TS), lambda b, p, pt: (b, 0, 0))]
                 + [pl.BlockSpec((None, H, Dh, PG), page(s, 3)) for s in slots]
                 + [pl.BlockSpec((None, H, Dh, PG), page(s, 3)) for s in slots]
                 + [pl.BlockSpec((None, H, PG), page(s, 2)) for s in slots],
        out_specs=row,
        scratch_shapes=[pltpu.VMEM((R, D), BF16), pltpu.VMEM((R, 1), F32), pltpu.VMEM((H, 1), F32),
                        pltpu.VMEM((R, 1), F32), pltpu.VMEM((R, 1), F32), pltpu.VMEM((R, D), F32)])
    return pl.pallas_call(
        functools.partial(_fox_sample_kernel, n_steps=n_steps, pps=pps),
        grid_spec=grid_spec,
        out_shape=jax.ShapeDtypeStruct((M, D), F32),
        compiler_params=_cparams(2), name="fox_sample",
    )(page_table.reshape(-1), q, k_new, v_new, cum_new, cum_new_t,
      *([cache_kt] * pps), *([cache_vt] * pps), *([cache_logf_t] * pps))


def _rwkv_in_kernel(x_ref, g_ref, sc_ref, sh_ref, st_ref, mu_ref, wrkv_ref, w0_ref, w1_ref, w2_ref, a0_ref,
                    a1_ref, a2_ref, g1_ref, g2_ref,
                    r_ref, k_ref, v_ref, lw_ref, a_ref, gg_ref, tail_ref, h_scr, d_scr, carry_scr, *, T):
    i, j = pl.program_id(0), pl.program_id(1)
    tm, D = x_ref.shape
    nt = tm // SUBLANES
    long_seq = T >= tm

    @pl.when(j == 0)
    def _():
        h = _modnorm(x_ref[...], g_ref[...], sc_ref[...], sh_ref[...], T)
        h3 = h.reshape(nt, SUBLANES, D)
        if long_seq:
            @pl.when(i % (T // tm) == 0)
            def _():
                carry_scr[...] = st_ref[...]
            prev3 = jnp.concatenate([carry_scr[...][None], h3[:nt - 1]], axis=0)
            carry_scr[...] = h3[nt - 1]
            tail_ref[...] = h3[nt - 1]
        else:
            prev3 = st_ref[...]
            tail_ref[...] = h3
        prev = _shift_rows(h3, prev3, 1).reshape(tm, D)
        h_scr[...] = h
        d_scr[...] = prev - h

    def mix(s):
        return (h_scr[...] + d_scr[...] * mu_ref[s:s + 1, :]).astype(BF16)

    for s, out in enumerate((r_ref, k_ref, v_ref)):
        @pl.when(j == s)
        def _(s=s, out=out):
            out[...] = jnp.dot(mix(s), wrkv_ref[...], preferred_element_type=F32)

    @pl.when(j == 3)
    def _():
        t = jnp.tanh(jnp.dot(mix(3), w1_ref[...], preferred_element_type=F32))
        pre_w = w0_ref[...] + jnp.dot(t.astype(BF16), w2_ref[...], preferred_element_type=F32)
        w_log = -jax.nn.softplus(-pre_w) - 0.5
        lw_ref[...] = -jnp.exp(w_log)
        u = jnp.dot(mix(4), a1_ref[...], preferred_element_type=F32)
        pre_a = a0_ref[...] + jnp.dot(u.astype(BF16), a2_ref[...], preferred_element_type=F32)
        a_ref[...] = jax.nn.sigmoid(pre_a)
        sg = jax.nn.sigmoid(jnp.dot(mix(5), g1_ref[...], preferred_element_type=F32))
        gg_ref[...] = jnp.dot(sg.astype(BF16), g2_ref[...], preferred_element_type=F32)


def _rwkv_in(x, g, sc, sh, st_tiles, mu, w_rkv, w0, w1, w2, a0, a1, a2, g1, g2, *, T, tm):
    M, D = x.shape
    B = st_tiles.shape[0]
    P = LANES
    c2 = lambda shape: pl.BlockSpec(shape, lambda i, j: (0,) * len(shape))
    row = pl.BlockSpec((tm, D), lambda i, j: (i, 0))
    return pl.pallas_call(
        functools.partial(_rwkv_in_kernel, T=T),
        grid=(M // tm, 4),
        in_specs=[row, c2((1, D)), _mod_spec(T, tm, D), _mod_spec(T, tm, D), _tile_spec(T, tm, D),
                  c2((6, D)), pl.BlockSpec((None, D, D), lambda i, j: (jnp.minimum(j, 2), 0, 0)),
                  c2((1, D)), c2((D, P)), c2((P, D)), c2((1, D)), c2((D, P)), c2((P, D)), c2((D, P)), c2((P, D))],
        out_specs=[row] * 6 + [_tile_spec(T, tm, D)],
        out_shape=[jax.ShapeDtypeStruct((M, D), F32)] * 6 + [jax.ShapeDtypeStruct((B, SUBLANES, D), F32)],
        scratch_shapes=[pltpu.VMEM((tm, D), F32), pltpu.VMEM((tm, D), F32), pltpu.VMEM((SUBLANES, D), F32)],
        compiler_params=_cparams(2),
        name="rwkv_in",
    )(x, g.reshape(1, D), sc, sh, st_tiles, mu, w_rkv, w0.reshape(1, D), w1, w2, a0.reshape(1, D), a1, a2, g1, g2)


def _wkv_kernel(r_ref, k_ref, v_ref, lw_ref, a_ref, g_ref, kk_ref, ka_ref, rk_ref, lnw_ref, lnb_ref, s0_ref,
                o_ref, sout_ref, s_scr, *, L, n_chunks):
    c = pl.program_id(1)
    H, N = RWKV_HEADS, RWKV_HEAD
    G = WKV_ROWS // L
    R = G * L

    @pl.when(c == 0)
    def _():
        s_scr[...] = s0_ref[...]

    ri = lax.broadcasted_iota(jnp.int32, (L, L), 0)
    ci = lax.broadcasted_iota(jnp.int32, (L, L), 1)
    lw = lw_ref[...]
    cs = jnp.dot((ri >= ci).astype(F32), lw, preferred_element_type=F32, precision=HI)
    e_pos = jnp.exp(cs)
    e_neg = jnp.exp(-cs)
    e_prev = jnp.exp(cs - lw)

    row = lax.broadcasted_iota(jnp.int32, (R, R), 0)
    col = lax.broadcasted_iota(jnp.int32, (R, R), 1)
    same = (row // L) == (col // L)
    strict = same & (row > col)
    incl = same & (row >= col)

    for grp in range(H // G):
        heads = range(grp * G, (grp + 1) * G)
        A_l, R_l, B_l, K_l, V_l, kp_l = [], [], [], [], [], []
        for h in heads:
            sl = slice(h * N, (h + 1) * N)
            kh, ah = k_ref[:, sl], a_ref[:, sl]
            kkh = kh * kk_ref[:, sl]
            nrm = jnp.sqrt(jnp.sum(kkh * kkh, axis=-1, keepdims=True))
            kkh = kkh / jnp.maximum(nrm, 1e-12)
            kph = kh * (1.0 + (ah - 1.0) * ka_ref[:, sl])
            en = e_neg[:, sl]
            A_l.append(-kkh * e_prev[:, sl])
            R_l.append(r_ref[:, sl] * e_pos[:, sl])
            B_l.append(kkh * ah * en)
            K_l.append(kph * en)
            V_l.append(v_ref[:, sl])
            kp_l.append(kph)
        AR = jnp.concatenate(A_l + R_l, axis=0).astype(BF16)
        BK = jnp.concatenate(B_l + K_l, axis=0).astype(BF16)
        V4 = jnp.concatenate(V_l, axis=0)
        Q = _nt(AR, BK)
        ars_a, ars_r = [], []
        for q, h in enumerate(heads):
            Sb = s_scr[h].astype(BF16)
            ars_a.append(_nt(AR[q * L:(q + 1) * L], Sb))
            ars_r.append(_nt(AR[R + q * L:R + (q + 1) * L], Sb))
        nmat = jnp.where(strict, Q[:R, :R], 0.0)
        ak = jnp.where(strict, Q[:R, R:], 0.0)
        rhs = jnp.concatenate(ars_a, axis=0) + jnp.dot(ak.astype(BF16), V4.astype(BF16), preferred_element_type=F32)
        Y = jnp.where(((row // 2) == (col // 2)) & (row > col), nmat, 0.0)
        b = 2
        while b < L:
            Ck = jnp.where(((row // (2 * b)) == (col // (2 * b))) & ((row % (2 * b)) >= b) & ((col % (2 * b)) < b),
                           nmat, 0.0)
            Yb = Y.astype(BF16)
            Tm = Ck + jnp.dot(Yb, Ck.astype(BF16), preferred_element_type=F32)
            Y = Y + Tm + jnp.dot(Tm.astype(BF16), Yb, preferred_element_type=F32)
            b *= 2
        E = rhs + jnp.dot(Y.astype(BF16), rhs.astype(BF16), preferred_element_type=F32)
        EV = jnp.concatenate([E, V4], axis=0).astype(BF16)
        rbk = jnp.concatenate([jnp.where(incl, Q[R:, :R], 0.0), jnp.where(incl, Q[R:, R:], 0.0)], axis=1)
        y4 = jnp.concatenate(ars_r, axis=0) + jnp.dot(rbk.astype(BF16), EV, preferred_element_type=F32)
        for q, h in enumerate(heads):
            sl = slice(h * N, (h + 1) * N)
            rows = slice(q * L, (q + 1) * L)
            ev_h = jnp.concatenate([EV[rows], EV[R + q * L:R + (q + 1) * L]], axis=0)
            bk_h = jnp.concatenate([BK[rows], BK[R + q * L:R + (q + 1) * L]], axis=0)
            s_scr[h] = (s_scr[h] + _tn(ev_h, bk_h)) * e_pos[L - 1:L, sl]
            y = y4[rows]
            mean = jnp.mean(y, axis=-1, keepdims=True)
            var = jnp.mean(jnp.square(y - mean), axis=-1, keepdims=True)
            yn = (y - mean) * lax.rsqrt(var + RWKV_GN_EPS) * lnw_ref[:, sl] + lnb_ref[:, sl]
            bonus = jnp.sum(r_ref[:, sl] * kp_l[q] * rk_ref[:, sl], axis=-1, keepdims=True) * V_l[q]
            o_ref[:, sl] = (yn + bonus) * g_ref[:, sl]

    @pl.when(c == n_chunks - 1)
    def _():
        sout_ref[...] = s_scr[...]


def _wkv_core(r, k, v, lw, a, g, k_k, k_a, r_k, ln_w, ln_b, s0, *, T, L):
    M, D = r.shape
    B = s0.shape[0]
    nc = T // L
    row = pl.BlockSpec((L, D), lambda b, c: (b * nc + c, 0))
    par = pl.BlockSpec((1, D), lambda b, c: (0, 0))
    st = pl.BlockSpec((None, RWKV_HEADS, RWKV_HEAD, RWKV_HEAD), lambda b, c: (b, 0, 0, 0))
    v1 = lambda t: t.reshape(1, D)
    return pl.pallas_call(
        functools.partial(_wkv_kernel, L=L, n_chunks=nc),
        grid=(B, nc),
        in_specs=[row] * 6 + [par] * 5 + [st],
        out_specs=[row, st],
        out_shape=[jax.ShapeDtypeStruct((M, D), F32), jax.ShapeDtypeStruct(s0.shape, F32)],
        scratch_shapes=[pltpu.VMEM((RWKV_HEADS, RWKV_HEAD, RWKV_HEAD), F32)],
        compiler_params=_cparams(2),
        name="wkv_core",
    )(r, k, v, lw, a, g, v1(k_k), v1(k_a), v1(r_k), v1(ln_w), v1(ln_b), s0)


def _wkv_long_kernel(r_ref, k_ref, v_ref, lw_ref, a_ref, g_ref, kk_ref, ka_ref, rk_ref, lnw_ref, lnb_ref, s0_ref,
                     o_ref, sout_ref, s_scr, *, L, NC, n_steps):
    step = pl.program_id(1)
    H, N, G = RWKV_HEADS, RWKV_HEAD, WKV_GROUP_HEADS
    GW = G * N
    NG = H // G
    R = G * L
    TB = NC * L

    @pl.when(step == 0)
    def _():
        s_scr[...] = jnp.zeros_like(s_scr)
        for h in range(H):
            q = h % G
            s_scr[h // G, q * N:(q + 1) * N, q * N:(q + 1) * N] = s0_ref[h]

    ti = lax.broadcasted_iota(jnp.int32, (TB, TB), 0)
    tj = lax.broadcasted_iota(jnp.int32, (TB, TB), 1)
    same_c = (ti // L) == (tj // L)
    lw = lw_ref[...]
    cs = jnp.dot((same_c & (ti >= tj)).astype(F32), lw, preferred_element_type=F32, precision=HI)
    ctot = jnp.dot(same_c.astype(F32), lw, preferred_element_type=F32, precision=HI)
    e_pos, e_neg, e_prev = jnp.exp(cs), jnp.exp(-cs), jnp.exp(cs - lw)
    e_hat, e_tot = jnp.exp(ctot - cs), jnp.exp(ctot)
    gi = lax.broadcasted_iota(jnp.int32, (GW, GW), 0) // N
    gj = lax.broadcasted_iota(jnp.int32, (GW, GW), 1) // N
    gones = (gi == gj).astype(F32)

    def head_sum(x):
        return jnp.concatenate([jnp.dot(x[:, g * GW:(g + 1) * GW], gones, preferred_element_type=F32, precision=HI)
                                for g in range(NG)], axis=1)

    r, k, v, a = r_ref[...], k_ref[...], v_ref[...], a_ref[...]
    kkv = k * kk_ref[...]
    kkn = kkv / jnp.maximum(jnp.sqrt(head_sum(kkv * kkv)), 1e-12)
    kp = k * (1.0 + (a - 1.0) * ka_ref[...])
    ba = kkn * a
    nat = dict(A=-kkn * e_prev, R=r * e_pos, Bt=ba * e_neg, Kt=kp * e_neg, Bh=ba * e_hat, Kh=kp * e_hat, V=v)
    bonus = head_sum(r * kp * rk_ref[...]) * v

    row = lax.broadcasted_iota(jnp.int32, (R, GW), 0)
    lane = lax.broadcasted_iota(jnp.int32, (R, GW), 1)
    hm = (row // L) == (lane // N)

    def bd(name, c, g):
        x = nat[name][c * L:(c + 1) * L, g * GW:(g + 1) * GW]
        return jnp.where(hm, jnp.concatenate([x] * G, axis=0), 0.0)

    mr = lax.broadcasted_iota(jnp.int32, (R, R), 0)
    mc = lax.broadcasted_iota(jnp.int32, (R, R), 1)
    same_h = (mr // L) == (mc // L)
    strict = same_h & (mr > mc)
    incl = same_h & (mr >= mc)

    units = [(c, g) for c in range(NC) for g in range(NG)]
    U = {}
    for u in units:
        ops = {n: bd(n, *u) for n in nat}
        AR = jnp.concatenate([ops["A"], ops["R"]], axis=0).astype(BF16)
        BK = jnp.concatenate([ops["Bt"], ops["Kt"]], axis=0).astype(BF16)
        U[u] = dict(ops=ops, Q=_nt(AR, BK))
    for u in units:
        d = U[u]
        Q = d.pop("Q")
        d["nmat"] = jnp.where(strict, Q[:R, :R], 0.0)
        d["ak"] = jnp.where(strict, Q[:R, R:], 0.0).astype(BF16)
        d["rb"] = jnp.where(incl, Q[R:, :R], 0.0).astype(BF16)
        d["rk"] = jnp.where(incl, Q[R:, R:], 0.0).astype(BF16)
        d["Y"] = jnp.where(((mr // 2) == (mc // 2)) & (mr > mc), d["nmat"], 0.0)
    b = 2
    while b < L:
        lvl = ((mr // (2 * b)) == (mc // (2 * b))) & ((mr % (2 * b)) >= b) & ((mc % (2 * b)) < b)
        for u in units:
            d = U[u]
            Ck = jnp.where(lvl, d["nmat"], 0.0)
            d["Yb"] = d["Y"].astype(BF16)
            d["Tm"] = Ck + jnp.dot(d["Yb"], Ck.astype(BF16), preferred_element_type=F32)
        for u in units:
            d = U[u]
            d["Y"] = d["Y"] + d["Tm"] + jnp.dot(d["Tm"].astype(BF16), d["Yb"], preferred_element_type=F32)
        b *= 2
    for u in units:
        d = U[u]
        vb = d["ops"]["V"].astype(BF16)
        d["akv"] = jnp.dot(d["ak"], vb, preferred_element_type=F32)
        d["rkv"] = jnp.dot(d["rk"], vb, preferred_element_type=F32)
    for u in units:
        d = U[u]
        x = jnp.concatenate([d["ops"]["A"], d["akv"]], axis=1)
        d["AE"] = x + jnp.dot(d["Y"].astype(BF16), x.astype(BF16), preferred_element_type=F32)
    for u in units:
        d = U[u]
        base = jnp.concatenate([d["ops"]["R"], d["rkv"]], axis=1)
        ry = base + jnp.dot(d["rb"], d["AE"].astype(BF16), preferred_element_type=F32)
        d["Rbar"] = ry[:, :GW].astype(BF16)
        d["y0"] = ry[:, GW:]
    for u in units:
        d = U[u]
        bh = d["ops"]["Bh"].astype(BF16)
        kh = d["ops"]["Kh"].astype(BF16)
        d["W"] = _tn(d["AE"][:, :GW].astype(BF16), bh).astype(BF16)
        ev = jnp.concatenate([d["AE"][:, GW:], d["ops"]["V"]], axis=0).astype(BF16)
        d["Z"] = _tn(ev, jnp.concatenate([bh, kh], axis=0))

    ys = []
    for c in range(NC):
        yc = []
        for g in range(NG):
            d = U[(c, g)]
            S = s_scr[g]
            Sb = S.astype(BF16)
            ybd = d["y0"] + _nt(d["Rbar"], Sb)
            yc.append(sum(ybd[q * L:(q + 1) * L] for q in range(G)))
            decay = e_tot[c * L:c * L + 1, g * GW:(g + 1) * GW]
            s_scr[g] = S * decay + jnp.dot(Sb, d["W"], preferred_element_type=F32) + d["Z"]
        ys.append(jnp.concatenate(yc, axis=1))
    y = jnp.concatenate(ys, axis=0)

    mean = head_sum(y) * (1.0 / N)
    yc = y - mean
    var = head_sum(yc * yc) * (1.0 / N)
    yn = yc * lax.rsqrt(var + RWKV_GN_EPS) * lnw_ref[...] + lnb_ref[...]
    o_ref[...] = ((yn + bonus) * g_ref[...]).astype(o_ref.dtype)

    @pl.when(step == n_steps - 1)
    def _():
        for h in range(H):
            q = h % G
            sout_ref[h] = s_scr[h // G, q * N:(q + 1) * N, q * N:(q + 1) * N]


def _wkv_long(r, k, v, lw, a, g, k_k, k_a, r_k, ln_w, ln_b, s0, *, T, L, NC):
    M, D = r.shape
    B = s0.shape[0]
    TB = NC * L
    ns = T // TB
    row = pl.BlockSpec((TB, D), lambda b, c: (b * ns + c, 0))
    par = pl.BlockSpec((1, D), lambda b, c: (0, 0))
    st = pl.BlockSpec((None, RWKV_HEADS, RWKV_HEAD, RWKV_HEAD), lambda b, c: (b, 0, 0, 0))
    v1 = lambda t: t.reshape(1, D)
    GW = WKV_GROUP_HEADS * RWKV_HEAD
    return pl.pallas_call(
        functools.partial(_wkv_long_kernel, L=L, NC=NC, n_steps=ns),
        grid=(B, ns),
        in_specs=[row] * 6 + [par] * 5 + [st],
        out_specs=[row, st],
        out_shape=[jax.ShapeDtypeStruct((M, D), BF16), jax.ShapeDtypeStruct(s0.shape, F32)],
        scratch_shapes=[pltpu.VMEM((RWKV_HEADS // WKV_GROUP_HEADS, GW, GW), F32)],
        compiler_params=_cparams(2),
        name="wkv_long",
    )(r, k, v, lw, a, g, v1(k_k), v1(k_a), v1(r_k), v1(ln_w), v1(ln_b), s0)


def _pad_cols(w, n):
    return jnp.pad(w, ((0, 0), (0, n - w.shape[1])))


def _pad_rows(w, n):
    return jnp.pad(w, ((0, n - w.shape[0]), (0, 0)))


def _state_tiles(st):
    return jnp.pad(st, ((0, 0), (SUBLANES - st.shape[1], 0), (0, 0)))


def _ssm_from_groups(s):
    B = s.shape[0]
    s = s.reshape(B, SSD_GROUPS, SSD_STATE, SSD_HPG, SSD_HEAD_DIM)
    return s.transpose(0, 1, 3, 4, 2).reshape(B, SSD_HEADS, SSD_HEAD_DIM, SSD_STATE)


def kernel(x_prompt, x_sample, c_prompt, c_sample, state_ssm, state_ssd_conv, cache_k, cache_v, cache_logf, page_table, state_wkv, state_shift, state_ffn_conv, ada_w, ada_b, norm_mix, norm_ffn, norm_final, ssd_in_w, ssd_conv_w, ssd_conv_b, ssd_dt_bias, ssd_a_log, ssd_d, ssd_norm_w, ssd_out_w, fox_in_w, fox_f_b, fox_out_w, rwkv_mu, rwkv_w_rkv, rwkv_w0, rwkv_w1, rwkv_w2, rwkv_a0, rwkv_a1, rwkv_a2, rwkv_g1, rwkv_g2, rwkv_k_k, rwkv_k_a, rwkv_r_k, rwkv_ln_w, rwkv_ln_b, rwkv_out_w, ffn_up_w, ffn_conv_w, ffn_conv_b, ffn_down_w):
    BP, TP, D = x_prompt.shape
    BS, TS, _ = x_sample.shape
    groups = [dict(B=BP, T=TP, tm=1024, tm_s=512, ssd_l=SSD_CHUNK, wkv_l=RWKV_CHUNK, gate_tc=512),
              dict(B=BS, T=TS, tm=BS * TS, tm_s=512, ssd_l=TS, wkv_l=TS, gate_tc=TS)]
    xs = [x_prompt.reshape(BP * TP, D), x_sample.reshape(BS * TS, D)]

    n_c = BP + BS
    n_c_pad = -(-n_c // SUBLANES) * SUBLANES
    c_all = jnp.pad(jnp.concatenate([c_prompt, c_sample], axis=0), ((0, n_c_pad - n_c), (0, 0)))
    mods_all = _ada_mods(c_all, ada_w, ada_b).reshape(DEPTH, n_c_pad, 6, 1, D)
    row0 = [0, BP]

    def mods_of(i, gi):
        m = mods_all[i, row0[gi]:row0[gi] + groups[gi]["B"]]
        return [m[:, k] for k in range(6)]

    outs = {k: ([], []) for k in ("ssm", "sconv", "k", "v", "lf", "wkv", "shift", "fconv")}
    ssm_sample_all = None

    for i in range(DEPTH):
        kind, j = i % N_MIXERS, i // N_MIXERS
        for gi, gp in enumerate(groups):
            B, T, tm, tm_s = gp["B"], gp["T"], gp["tm"], gp["tm_s"]
            x = xs[gi]
            prompt = gi == 0
            sh_m, sc_m, gt_m, sh_f, sc_f, gt_f = mods_of(i, gi)
            if kind == 0:
                w_main = ssd_in_w[j][:, :SSD_D_INNER + SSD_CONV_CH].astype(BF16)
                w_dt = _pad_cols(ssd_in_w[j][:, SSD_D_INNER + SSD_CONV_CH:], LANES).astype(BF16)
                zx = _norm_mod_matmul(x, norm_mix[i], sc_m, sh_m, w_main, T=T, tm=tm, tn=1024)
                dt = _norm_mod_matmul(x, norm_mix[i], sc_m, sh_m, w_dt, T=T, tm=tm, tn=LANES)
                ssd_par = (ssd_conv_w[j], ssd_conv_b[j], ssd_dt_bias[j], ssd_a_log[j], ssd_d[j], ssd_norm_w[j])
                if prompt:
                    conv_tiles = jnp.zeros((B, SUBLANES, SSD_CONV_CH), F32)
                    h0 = jnp.zeros((1, B, SSD_GROUPS, SSD_STATE, SSD_HPG * SSD_HEAD_DIM), F32)
                    y, ctail, h_new = _ssd_core(zx, dt, conv_tiles, h0, *ssd_par, B=B, T=T, L=gp["ssd_l"],
                                                natural=False)
                    outs["ssm"][gi].append(_ssm_from_groups(h_new[0]))
                else:
                    conv_tiles = _state_tiles(state_ssd_conv[j])
                    h0 = state_ssm.reshape(-1, B, SSD_GROUPS, SSD_HPG * SSD_HEAD_DIM, SSD_STATE)
                    y, ctail, ssm_sample_all = _ssd_core(zx, dt, conv_tiles, h0, *ssd_par, B=B, T=T, L=gp["ssd_l"],
                                                         natural=True, layer=j, out_buf=ssm_sample_all)
                outs["sconv"][gi].append(ctail[:, SUBLANES - (SSD_CONV - 1):])
                x = _res_matmul(y, ssd_out_w[j].astype(BF16), x, gt_m, T=T, tm=tm)
            elif kind == 1:
                w_qkv = fox_in_w[j][:, :3 * FOX_INNER].reshape(D, 3, FOX_INNER).transpose(1, 0, 2).astype(BF16)
                w_f = _pad_cols(fox_in_w[j][:, 3 * FOX_INNER:], LANES).astype(BF16)
                q, k, v = _norm_mod_matmul_multi(x, norm_mix[i], sc_m, sh_m, w_qkv, T=T, tm=tm_s)
                f = _norm_mod_matmul(x, norm_mix[i], sc_m, sh_m, w_f, T=T, tm=tm, tn=LANES)
                f_b = jnp.pad(fox_f_b[j], (0, LANES - FOX_HEADS)).reshape(1, LANES)
                lf, cum = _fox_gate(f, f_b, B=B, T=T, tc=gp["gate_tc"])
                cum_t = jnp.swapaxes(cum[:, :FOX_HEADS].reshape(B, T, FOX_HEADS), 1, 2)
                if prompt:
                    o = _fox_flash(q, k, v, cum, cum_t, B=B, T=T, tq=512, tk=512)
                else:
                    o = _fox_sample(q, k, v, cum, cum_t, jnp.transpose(cache_k[j], (0, 2, 3, 1)),
                                    jnp.transpose(cache_v[j], (0, 2, 3, 1)), jnp.swapaxes(cache_logf[j], 1, 2),
                                    page_table, B=B, TS=T, pps=FOX_PAGES_PER_STEP)
                hd = lambda t: t.reshape(B, T, FOX_HEADS, FOX_HEAD_DIM)
                outs["k"][gi].append(hd(k))
                outs["v"][gi].append(hd(v))
                outs["lf"][gi].append(lf[:, :FOX_HEADS].reshape(B, T, FOX_HEADS))
                x = _res_matmul(o, fox_out_w[j].astype(BF16), x, gt_m, T=T, tm=tm)
            else:
                if prompt:
                    st_tiles = jnp.zeros((B, SUBLANES, D), F32)
                    s0 = jnp.zeros((B, RWKV_HEADS, RWKV_HEAD, RWKV_HEAD), F32)
                else:
                    st_tiles = _state_tiles(state_shift[j][:, None, :])
                    s0 = state_wkv[j]
                r, k, v, lw, a, g, tails = _rwkv_in(
                    x, norm_mix[i], sc_m, sh_m, st_tiles, rwkv_mu[j], rwkv_w_rkv[j].astype(BF16), rwkv_w0[j],
                    _pad_cols(rwkv_w1[j], LANES).astype(BF16), _pad_rows(rwkv_w2[j], LANES).astype(BF16), rwkv_a0[j],
                    _pad_cols(rwkv_a1[j], LANES).astype(BF16), _pad_rows(rwkv_a2[j], LANES).astype(BF16),
                    rwkv_g1[j].astype(BF16), rwkv_g2[j].astype(BF16), T=T, tm=tm_s)
                wkv_par = (rwkv_k_k[j], rwkv_k_a[j], rwkv_r_k[j].reshape(-1), rwkv_ln_w[j], rwkv_ln_b[j], s0)
                if prompt:
                    o, s_new = _wkv_long(r, k, v, lw, a, g, *wkv_par, T=T, L=gp["wkv_l"], NC=WKV_STEP_CHUNKS)
                else:
                    o, s_new = _wkv_core(r, k, v, lw, a, g, *wkv_par, T=T, L=gp["wkv_l"])
                outs["wkv"][gi].append(s_new)
                outs["shift"][gi].append(tails[:, SUBLANES - 1])
                x = _res_matmul(o, rwkv_out_w[j].astype(BF16), x, gt_m, T=T, tm=tm)
            if prompt:
                ffn_tiles = jnp.zeros((B, SUBLANES, D_FF), F32)
            else:
                ffn_tiles = _state_tiles(state_ffn_conv[i])
            x, tails = _conv_ffn(x, norm_ffn[i], sc_f, sh_f, gt_f, ffn_tiles, ffn_up_w[i].astype(BF16),
                                 ffn_conv_w[i], ffn_conv_b[i], ffn_down_w[i].astype(BF16), T=T, tm=tm_s)
            outs["fconv"][gi].append(tails[:, SUBLANES - (FFN_CONV - 1):])
            xs[gi] = x

    y_prompt = _rmsnorm(xs[0], norm_final, tm=1024).reshape(BP, TP, D)
    y_sample = _rmsnorm(xs[1], norm_final, tm=1024).reshape(BS, TS, D)
    st = lambda name, gi: jnp.stack(outs[name][gi])
    ssm_sample = ssm_sample_all.reshape(state_ssm.shape)
    return (y_prompt, y_sample, st("ssm", 0), ssm_sample, st("sconv", 0), st("sconv", 1),
            st("k", 0), st("k", 1), st("v", 0), st("v", 1), st("lf", 0), st("lf", 1),
            st("wkv", 0), st("wkv", 1), st("shift", 0), st("shift", 1), st("fconv", 0), st("fconv", 1))
```

```python
import functools

import jax
import jax.numpy as jnp
from jax import lax
from jax.experimental import pallas as pl
from jax.experimental.pallas import tpu as pltpu

F32 = jnp.float32
BF16 = jnp.bfloat16

D_MODEL = 1024
DEPTH = 4
N_MIXERS = 3
RMS_EPS = 1e-6
SSD_D_INNER = 2048
SSD_HEAD_DIM = 64
SSD_HEADS = 32
SSD_GROUPS = 4
SSD_HPG = 8
SSD_STATE = 128
SSD_CONV = 4
SSD_CONV_CH = 3072
SSD_CHUNK = 128
FOX_HEAD_DIM = 64
FOX_HEADS = 16
FOX_INNER = 1024
RWKV_HEAD = 64
RWKV_HEADS = 16
RWKV_GN_EPS = 64e-5
RWKV_CHUNK = 32
WKV_ROWS = 128
WKV_GROUP_HEADS = 4
WKV_STEP_CHUNKS = 4
FOX_PAGES_PER_STEP = 8
FOX_BIAS_TERMS = 3
D_FF = 2816
FFN_CONV = 3

SUBLANES = 8
LANES = 128
VMEM_LIMIT = 56 * 1024 * 1024


def _cparams(n_grid):
    return pltpu.CompilerParams(dimension_semantics=("arbitrary",) * n_grid,
                                vmem_limit_bytes=VMEM_LIMIT)


def _nt(a, b):
    return lax.dot_general(a, b, (((1,), (1,)), ((), ())), preferred_element_type=F32)


def _tn(a, b):
    return lax.dot_general(a, b, (((0,), (0,)), ((), ())), preferred_element_type=F32)


def _split3(c):
    hi = c.astype(BF16)
    r1 = c - hi.astype(F32)
    mid = r1.astype(BF16)
    lo = (r1 - mid.astype(F32)).astype(BF16)
    return hi, mid, lo


def _sel_rows(sel, x):
    n = x.shape[1]
    r = jnp.dot(sel.astype(BF16), jnp.concatenate(_split3(x), axis=1), preferred_element_type=F32)
    return r[:, :n] + r[:, n:2 * n] + r[:, 2 * n:]


def _sel_cols(x, sel):
    m = x.shape[0]
    r = jnp.dot(jnp.concatenate(_split3(x), axis=0), sel.astype(BF16), preferred_element_type=F32)
    return r[:m] + r[m:2 * m] + r[2 * m:]


def _modnorm(x, g, sc, sh, T):
    xn = x * lax.rsqrt(jnp.mean(x * x, axis=-1, keepdims=True) + RMS_EPS) * g
    if sc.ndim == 2:
        return xn * (1.0 + sc) + sh
    tm, D = x.shape
    x3 = xn.reshape(tm // T, T, D)
    return (x3 * (1.0 + sc) + sh).reshape(tm, D)


def _gate_mul(y, gt, T):
    if gt.ndim == 2:
        return y * gt
    tm, D = y.shape
    return (y.reshape(tm // T, T, D) * gt).reshape(tm, D)


def _shift_rows(cur3, prev3, k):
    row = lax.broadcasted_iota(jnp.int32, cur3.shape, 1)
    return pltpu.roll(jnp.where(row < SUBLANES - k, cur3, prev3), k, 1)


def _mod_spec(T, tm, D):
    if T >= tm:
        per = T // tm
        return pl.BlockSpec((None, 1, D), lambda i, *_: (i // per, 0, 0))
    return pl.BlockSpec((tm // T, 1, D), lambda i, *_: (i, 0, 0))


def _tile_spec(T, tm, C):
    if T >= tm:
        per = T // tm
        return pl.BlockSpec((None, SUBLANES, C), lambda i, *_: (i // per, 0, 0))
    return pl.BlockSpec((tm // T, SUBLANES, C), lambda i, *_: (i, 0, 0))


def _ada_kernel(c_ref, w_ref, b_ref, o_ref):
    c = c_ref[...]
    a = (c * jax.nn.sigmoid(c)).astype(BF16)
    o_ref[...] = jnp.dot(a, w_ref[...].astype(BF16), preferred_element_type=F32) + b_ref[...]


def _ada_mods(c_all, ada_w, ada_b):
    R, D = c_all.shape
    L, _, N = ada_w.shape
    tn = 1536
    return pl.pallas_call(
        _ada_kernel,
        grid=(L, N // tn),
        in_specs=[pl.BlockSpec((R, D), lambda l, j: (0, 0)),
                  pl.BlockSpec((None, D, tn), lambda l, j: (l, 0, j)),
                  pl.BlockSpec((None, 1, tn), lambda l, j: (l, 0, j))],
        out_specs=pl.BlockSpec((None, R, tn), lambda l, j: (l, 0, j)),
        out_shape=jax.ShapeDtypeStruct((L, R, N), F32),
        compiler_params=_cparams(2),
        name="ada_mods",
    )(c_all, ada_w, ada_b.reshape(L, 1, N))


def _nmm_kernel(x_ref, g_ref, sc_ref, sh_ref, w_ref, o_ref, h_scr, *, T):
    @pl.when(pl.program_id(1) == 0)
    def _():
        h_scr[...] = _modnorm(x_ref[...], g_ref[...], sc_ref[...], sh_ref[...], T).astype(BF16)
    o_ref[...] = jnp.dot(h_scr[...], w_ref[...], preferred_element_type=F32).astype(o_ref.dtype)


def _norm_mod_matmul(x, g, sc, sh, w, *, T, tm, tn):
    M, D = x.shape
    N = w.shape[1]
    return pl.pallas_call(
        functools.partial(_nmm_kernel, T=T),
        grid=(M // tm, N // tn),
        in_specs=[pl.BlockSpec((tm, D), lambda i, j: (i, 0)),
                  pl.BlockSpec((1, D), lambda i, j: (0, 0)),
                  _mod_spec(T, tm, D), _mod_spec(T, tm, D),
                  pl.BlockSpec((D, tn), lambda i, j: (0, j))],
        out_specs=pl.BlockSpec((tm, tn), lambda i, j: (i, j)),
        out_shape=jax.ShapeDtypeStruct((M, N), F32),
        scratch_shapes=[pltpu.VMEM((tm, D), BF16)],
        compiler_params=_cparams(2),
        name="norm_mod_matmul",
    )(x, g.reshape(1, D), sc, sh, w)


def _nmm_multi_kernel(x_ref, g_ref, sc_ref, sh_ref, w_ref, *rest, T, n_out):
    outs, h_scr = rest[:n_out], rest[n_out]
    j = pl.program_id(1)

    @pl.when(j == 0)
    def _():
        h_scr[...] = _modnorm(x_ref[...], g_ref[...], sc_ref[...], sh_ref[...], T).astype(BF16)
    res = jnp.dot(h_scr[...], w_ref[...], preferred_element_type=F32)
    for o in range(n_out):
        @pl.when(j == o)
        def _(o=o):
            outs[o][...] = res


def _norm_mod_matmul_multi(x, g, sc, sh, w_stack, *, T, tm):
    M, D = x.shape
    n_out, _, N = w_stack.shape
    return pl.pallas_call(
        functools.partial(_nmm_multi_kernel, T=T, n_out=n_out),
        grid=(M // tm, n_out),
        in_specs=[pl.BlockSpec((tm, D), lambda i, j: (i, 0)),
                  pl.BlockSpec((1, D), lambda i, j: (0, 0)),
                  _mod_spec(T, tm, D), _mod_spec(T, tm, D),
                  pl.BlockSpec((None, D, N), lambda i, j: (j, 0, 0))],
        out_specs=[pl.BlockSpec((tm, N), lambda i, j: (i, 0)) for _ in range(n_out)],
        out_shape=[jax.ShapeDtypeStruct((M, N), F32) for _ in range(n_out)],
        scratch_shapes=[pltpu.VMEM((tm, D), BF16)],
        compiler_params=_cparams(2),
        name="norm_mod_matmul_multi",
    )(x, g.reshape(1, D), sc, sh, w_stack)


def _res_mm_kernel(a_ref, w_ref, x_ref, gt_ref, o_ref, *, T):
    y = jnp.dot(a_ref[...].astype(BF16), w_ref[...], preferred_element_type=F32)
    o_ref[...] = x_ref[...] + _gate_mul(y, gt_ref[...], T)


def _res_matmul(a, w, x, gt, *, T, tm):
    M, K = a.shape
    D = w.shape[1]
    return pl.pallas_call(
        functools.partial(_res_mm_kernel, T=T),
        grid=(M // tm,),
        in_specs=[pl.BlockSpec((tm, K), lambda i: (i, 0)),
                  pl.BlockSpec((K, D), lambda i: (0, 0)),
                  pl.BlockSpec((tm, D), lambda i: (i, 0)),
                  _mod_spec(T, tm, D)],
        out_specs=pl.BlockSpec((tm, D), lambda i: (i, 0)),
        out_shape=jax.ShapeDtypeStruct((M, D), F32),
        compiler_params=_cparams(1),
        name="res_matmul",
    )(a, w, x, gt)


def _ffn_kernel(x_ref, g_ref, sc_ref, sh_ref, gt_ref, st_ref, upw_ref, cw_ref, cb_ref, dnw_ref,
                o_ref, tail_ref, act_scr, carry_scr, *, T, F, fc):
    i = pl.program_id(0)
    tm, D = x_ref.shape
    nt = tm // SUBLANES
    long_seq = T >= tm
    x = x_ref[...]
    h = _modnorm(x, g_ref[...], sc_ref[...], sh_ref[...], T).astype(BF16)
    if long_seq:
        @pl.when(i % (T // tm) == 0)
        def _():
            carry_scr[...] = st_ref[...]
    for c in range(F // fc):
        lo, hi = c * fc, (c + 1) * fc
        gc = jnp.dot(h, upw_ref[:, lo:hi], preferred_element_type=F32)
        uc = jnp.dot(h, upw_ref[:, F + lo:F + hi], preferred_element_type=F32)
        g3 = gc.reshape(nt, SUBLANES, fc)
        if long_seq:
            prev3 = jnp.concatenate([carry_scr[:, lo:hi][None], g3[:nt - 1]], axis=0)
            carry_scr[:, lo:hi] = g3[nt - 1]
            tail_ref[:, lo:hi] = g3[nt - 1]
        else:
            prev3 = st_ref[:, :, lo:hi]
            tail_ref[:, :, lo:hi] = g3
        conv = (g3 * cw_ref[2:3, lo:hi] + _shift_rows(g3, prev3, 1) * cw_ref[1:2, lo:hi]
                + _shift_rows(g3, prev3, 2) * cw_ref[0:1, lo:hi] + cb_ref[:, lo:hi])
        act = conv * jax.nn.sigmoid(conv) * uc.reshape(nt, SUBLANES, fc)
        act_scr[:, lo:hi] = act.reshape(tm, fc).astype(BF16)
    y = jnp.dot(act_scr[...], dnw_ref[...], preferred_element_type=F32)
    o_ref[...] = x + _gate_mul(y, gt_ref[...], T)


def _conv_ffn(x, g, sc, sh, gt, st_tiles, up_w, conv_w, conv_b, down_w, *, T, tm):
    M, D = x.shape
    F = down_w.shape[0]
    B = st_tiles.shape[0]
    return pl.pallas_call(
        functools.partial(_ffn_kernel, T=T, F=F, fc=256),
        grid=(M // tm,),
        in_specs=[pl.BlockSpec((tm, D), lambda i: (i, 0)),
                  pl.BlockSpec((1, D), lambda i: (0, 0)),
                  _mod_spec(T, tm, D), _mod_spec(T, tm, D), _mod_spec(T, tm, D),
                  _tile_spec(T, tm, F),
                  pl.BlockSpec((D, 2 * F), lambda i: (0, 0)),
                  pl.BlockSpec((FFN_CONV, F), lambda i: (0, 0)),
                  pl.BlockSpec((1, F), lambda i: (0, 0)),
                  pl.BlockSpec((F, D), lambda i: (0, 0))],
        out_specs=[pl.BlockSpec((tm, D), lambda i: (i, 0)), _tile_spec(T, tm, F)],
        out_shape=[jax.ShapeDtypeStruct((M, D), F32), jax.ShapeDtypeStruct((B, SUBLANES, F), F32)],
        scratch_shapes=[pltpu.VMEM((tm, F), BF16), pltpu.VMEM((SUBLANES, F), F32)],
        compiler_params=_cparams(1),
        name="conv_ffn",
    )(x, g.reshape(1, D), sc, sh, gt, st_tiles, up_w, conv_w, conv_b.reshape(1, F), down_w)


def _rms_kernel(x_ref, g_ref, o_ref):
    x = x_ref[...]
    o_ref[...] = x * lax.rsqrt(jnp.mean(x * x, axis=-1, keepdims=True) + RMS_EPS) * g_ref[...]


def _rmsnorm(x, g, *, tm):
    M, D = x.shape
    return pl.pallas_call(
        _rms_kernel, grid=(M // tm,),
        in_specs=[pl.BlockSpec((tm, D), lambda i: (i, 0)), pl.BlockSpec((1, D), lambda i: (0, 0))],
        out_specs=pl.BlockSpec((tm, D), lambda i: (i, 0)),
        out_shape=jax.ShapeDtypeStruct((M, D), F32),
        compiler_params=_cparams(1), name="final_rmsnorm",
    )(x, g.reshape(1, D))


def _conv_silu(cur, prev_tile, w_ref, b_ref, lo, hi):
    L, C = cur.shape
    nt = L // SUBLANES
    c3 = cur.reshape(nt, SUBLANES, C)
    p3 = prev_tile[None] if nt == 1 else jnp.concatenate([prev_tile[None], c3[:nt - 1]], axis=0)
    out = c3 * w_ref[3:4, lo:hi] + b_ref[:, lo:hi]
    for k in range(1, SSD_CONV):
        out = out + _shift_rows(c3, p3, k) * w_ref[3 - k:4 - k, lo:hi]
    out = out * jax.nn.sigmoid(out)
    return out.reshape(L, C)


def _ssd_kernel(z0_ref, z1_ref, x0_ref, x1_ref, bc_ref, dt_ref, cst_ref, h0_ref, cw_ref, cb_ref, dtb_ref, alog_ref,
                dw_ref, nw_ref, ex_ref, *rest, L, n_chunks, natural):
    y_ref, ctail_ref, hout_ref, tail_scr, st_scr = rest[-5:]
    c = pl.program_id(1)
    P, Nn, G, J = SSD_HEAD_DIM, SSD_STATE, SSD_GROUPS, SSD_HPG
    GW = J * P

    @pl.when(c == 0)
    def _():
        tail_scr[...] = cst_ref[...]
        st_scr[...] = h0_ref[...]

    conv = []
    for q, blk_ref in enumerate((x0_ref, x1_ref, bc_ref)):
        blk = blk_ref[...]
        lo, hi = q * 1024, (q + 1) * 1024
        conv.append(_conv_silu(blk, tail_scr[:, lo:hi], cw_ref, cb_ref, lo, hi))
        tail_scr[:, lo:hi] = blk[L - SUBLANES:, :]
        ctail_ref[:, lo:hi] = blk[L - SUBLANES:, :]
    xs = jnp.concatenate(conv[:2], axis=1)
    Bm = conv[2][:, :G * Nn]
    Cm = conv[2][:, G * Nn:]

    dt = jax.nn.softplus(dt_ref[...] + dtb_ref[...])
    a = dt * (-jnp.exp(alog_ref[...]))
    ri = lax.broadcasted_iota(jnp.int32, (L, L), 0)
    ci = lax.broadcasted_iota(jnp.int32, (L, L), 1)
    causal = ri >= ci
    acs = _sel_rows(causal, a)
    wide = _sel_cols(jnp.concatenate([dt, acs], axis=0), ex_ref[...])
    dt_w, acs_w = wide[:L], wide[L:]
    acs_t = acs.T
    last_w = acs_w[L - 1:L, :]
    X = xs * dt_w
    Xd = X * jnp.exp(last_w - acs_w)
    dstart = jnp.exp(acs_w)
    cdec = jnp.exp(last_w)
    zz = jnp.concatenate([z0_ref[...], z1_ref[...]], axis=1)
    zg = zz * jax.nn.sigmoid(zz)
    dwide = dw_ref[...]
    nw = nw_ref[...]
    Xb = X.astype(BF16)
    for g in range(G):
        gl = slice(g * GW, (g + 1) * GW)
        Bg = Bm[:, g * Nn:(g + 1) * Nn].astype(BF16)
        Cg = Cm[:, g * Nn:(g + 1) * Nn].astype(BF16)
        CB = _nt(Cg, Bg)
        st = st_scr[g]
        if natural:
            y_off = _nt(Cg, st.astype(BF16)) * dstart[:, gl]
            dec_h = jnp.exp(acs_t[g * J:(g + 1) * J, L - 1:L])
            dcol = jnp.broadcast_to(dec_h[:, None, :], (J, P, 1)).reshape(GW, 1)
            st_scr[g] = st * dcol + _tn(Xd[:, gl].astype(BF16), Bg)
        else:
            y_off = jnp.dot(Cg, st.astype(BF16), preferred_element_type=F32) * dstart[:, gl]
            st_scr[g] = st * cdec[:, gl] + _tn(Bg, Xd[:, gl].astype(BF16))
        ys = []
        for j in range(J):
            h = g * J + j
            diff = acs[:, h:h + 1] - acs_t[h:h + 1, :]
            dec = jnp.exp(jnp.where(causal, diff, -jnp.inf))
            Mh = (CB * dec).astype(BF16)
            ys.append(jnp.dot(Mh, Xb[:, h * P:(h + 1) * P], preferred_element_type=F32))
        yg = jnp.concatenate(ys, axis=1) + y_off + dwide[:, gl] * xs[:, gl]
        yg = yg * zg[:, gl]
        yg = yg * lax.rsqrt(jnp.mean(yg * yg, axis=-1, keepdims=True) + RMS_EPS) * nw[:, gl]
        y_ref[:, gl] = yg.astype(y_ref.dtype)

    @pl.when(c == n_chunks - 1)
    def _():
        hout_ref[...] = st_scr[...]


def _ssd_core(zx, dt, conv_tiles, h0, conv_w, conv_b, dt_bias, a_log, d_skip, norm_w, *, B, T, L, natural,
              layer=0, out_buf=None):
    M = zx.shape[0]
    nc = T // L
    DI, C = SSD_D_INNER, SSD_CONV_CH
    col = lambda q: pl.BlockSpec((L, 1024), lambda b, c, q=q: (b * nc + c, q))
    par = lambda n: pl.BlockSpec((1, n), lambda b, c: (0, 0))
    pad = lambda v: jnp.pad(v.astype(F32), (0, LANES - v.shape[0])).reshape(1, LANES)
    expand = (jnp.arange(LANES)[:, None] == (jnp.arange(DI) // SSD_HEAD_DIM)[None, :]).astype(F32)
    d_wide = jnp.repeat(d_skip.astype(F32), SSD_HEAD_DIM).reshape(1, DI)
    st_shape = h0.shape[2:]
    st_spec = pl.BlockSpec((None, None) + st_shape, lambda b, c: (layer, b, 0, 0, 0))
    tile_spec = pl.BlockSpec((None, SUBLANES, C), lambda b, c: (b, 0, 0))
    in_specs = [col(0), col(1), col(2), col(3), col(4),
                pl.BlockSpec((L, LANES), lambda b, c: (b * nc + c, 0)),
                tile_spec, st_spec,
                pl.BlockSpec((SSD_CONV, C), lambda b, c: (0, 0)), par(C), par(LANES), par(LANES),
                par(DI), par(DI), pl.BlockSpec((LANES, DI), lambda b, c: (0, 0))]
    args = [zx, zx, zx, zx, zx, dt, conv_tiles, h0, conv_w, conv_b.reshape(1, C), pad(dt_bias), pad(a_log),
            d_wide, norm_w.reshape(1, DI), expand]
    aliases = {}
    if out_buf is not None:
        in_specs.append(pl.BlockSpec(memory_space=pl.ANY))
        args.append(out_buf)
        aliases = {len(args) - 1: 2}
    return pl.pallas_call(
        functools.partial(_ssd_kernel, L=L, n_chunks=nc, natural=natural),
        grid=(B, nc),
        in_specs=in_specs,
        out_specs=[pl.BlockSpec((L, DI), lambda b, c: (b * nc + c, 0)), tile_spec, st_spec],
        out_shape=[jax.ShapeDtypeStruct((M, DI), BF16 if L % 16 == 0 else F32),
                   jax.ShapeDtypeStruct((B, SUBLANES, C), F32), jax.ShapeDtypeStruct(h0.shape, F32)],
        scratch_shapes=[pltpu.VMEM((SUBLANES, C), F32), pltpu.VMEM(st_shape, F32)],
        input_output_aliases=aliases,
        compiler_params=_cparams(2), name="ssd_core",
    )(*args)


def _fox_gate_kernel(f_ref, fb_ref, lf_ref, cum_ref, carry_scr):
    c = pl.program_id(1)
    tc = f_ref.shape[0]

    @pl.when(c == 0)
    def _():
        carry_scr[...] = jnp.zeros_like(carry_scr)
    lf = jax.nn.log_sigmoid(f_ref[...] + fb_ref[...])
    ri = lax.broadcasted_iota(jnp.int32, (tc, tc), 0)
    ci = lax.broadcasted_iota(jnp.int32, (tc, tc), 1)
    cum = _sel_rows(ri >= ci, lf) + carry_scr[...]
    lf_ref[...] = lf
    cum_ref[...] = cum
    carry_scr[...] = cum[tc - 1:tc, :]


def _fox_gate(f_raw, f_b_pad, *, B, T, tc):
    M, W = f_raw.shape
    nc = T // tc
    row = pl.BlockSpec((tc, W), lambda b, c: (b * nc + c, 0))
    return pl.pallas_call(
        _fox_gate_kernel, grid=(B, nc),
        in_specs=[row, pl.BlockSpec((1, W), lambda b, c: (0, 0))],
        out_specs=[row, row],
        out_shape=[jax.ShapeDtypeStruct((M, W), F32)] * 2,
        scratch_shapes=[pltpu.VMEM((1, W), F32)],
        compiler_params=_cparams(2), name="fox_gate",
    )(f_raw, f_b_pad)


def _aug_extras(c_ref, h, tm):
    nt = FOX_BIAS_TERMS
    lane = lax.broadcasted_iota(jnp.int32, (tm, LANES - FOX_HEAD_DIM), 1)
    qe = jnp.where((lane >= nt) & (lane < 2 * nt), 1.0, 0.0)
    ke = jnp.where(lane < nt, 1.0, 0.0)
    for t, pc in enumerate(_split3(c_ref[:, h:h + 1])):
        qe = jnp.where(lane == t, pc.astype(F32), qe)
        ke = jnp.where(lane == nt + t, -pc.astype(F32), ke)
    return qe, ke


def _fox_proj_aug_kernel(x_ref, g_ref, sc_ref, sh_ref, w_ref, c_ref, k_ref, v_ref, qa_ref, ka_ref, va_ref, h_scr, *, T):
    j = pl.program_id(1)
    H, Dh = FOX_HEADS, FOX_HEAD_DIM
    tm = x_ref.shape[0]

    @pl.when(j == 0)
    def _():
        h_scr[...] = _modnorm(x_ref[...], g_ref[...], sc_ref[...], sh_ref[...], T).astype(BF16)
    res = jnp.dot(h_scr[...], w_ref[...], preferred_element_type=F32)

    @pl.when(j == 0)
    def _():
        for h in range(H):
            qe, _ = _aug_extras(c_ref, h, tm)
            qa_ref[h] = jnp.concatenate([res[:, h * Dh:(h + 1) * Dh] * (Dh ** -0.5), qe], axis=1).astype(BF16)

    @pl.when(j == 1)
    def _():
        k_ref[...] = res
        for h in range(H):
            _, ke = _aug_extras(c_ref, h, tm)
            ka_ref[h] = jnp.concatenate([res[:, h * Dh:(h + 1) * Dh], ke], axis=1).astype(BF16)

    @pl.when(j == 2)
    def _():
        v_ref[...] = res
        ones = (lax.broadcasted_iota(jnp.int32, (tm, LANES - Dh), 1) == 0).astype(F32)
        for h in range(H):
            va_ref[h] = jnp.concatenate([res[:, h * Dh:(h + 1) * Dh], ones], axis=1).astype(BF16)


def _fox_proj_aug(x, g, sc, sh, w_stack, cum, *, B, T, tm):
    M, D = x.shape
    per = T // tm
    H = FOX_HEADS
    row = pl.BlockSpec((tm, D), lambda i, j: (i, 0))
    aug = pl.BlockSpec((None, H, tm, LANES), lambda i, j: (i // per, 0, i % per, 0))
    return pl.pallas_call(
        functools.partial(_fox_proj_aug_kernel, T=T),
        grid=(M // tm, 3),
        in_specs=[row, pl.BlockSpec((1, D), lambda i, j: (0, 0)), _mod_spec(T, tm, D), _mod_spec(T, tm, D),
                  pl.BlockSpec((None, D, D), lambda i, j: (j, 0, 0)),
                  pl.BlockSpec((tm, cum.shape[1]), lambda i, j: (i, 0))],
        out_specs=[row, row, aug, aug, aug],
        out_shape=[jax.ShapeDtypeStruct((M, D), F32)] * 2 + [jax.ShapeDtypeStruct((B, H, T, LANES), BF16)] * 3,
        scratch_shapes=[pltpu.VMEM((tm, D), BF16)],
        compiler_params=_cparams(2), name="fox_proj_aug",
    )(x, g.reshape(1, D), sc, sh, w_stack, cum)


def _fox_flash_kernel(qa_ref, ka_ref, va_ref, o_ref, m_scr, acc_scr, *, tq, tk, nk):
    qi, ki = pl.program_id(1), pl.program_id(2)
    H, Dh = FOX_HEADS, FOX_HEAD_DIM

    @pl.when(ki == 0)
    def _():
        m_scr[...] = jnp.full_like(m_scr, -jnp.inf)
        acc_scr[...] = jnp.zeros_like(acc_scr)

    def compute(masked):
        if masked:
            rowp = qi * tq + lax.broadcasted_iota(jnp.int32, (tq, tk), 0)
            colp = ki * tk + lax.broadcasted_iota(jnp.int32, (tq, tk), 1)
            keep = colp <= rowp
        for h in range(H):
            s = _nt(qa_ref[h], ka_ref[h])
            if masked:
                s = jnp.where(keep, s, -jnp.inf)
            m_prev = m_scr[h]
            m_new = jnp.maximum(m_prev, jnp.max(s, axis=-1, keepdims=True))
            p = jnp.exp(s - m_new).astype(BF16)
            acc_scr[h] = jnp.exp(m_prev - m_new) * acc_scr[h] + jnp.dot(p, va_ref[h], preferred_element_type=F32)
            m_scr[h] = m_new

    needed = ki * tk <= qi * tq + (tq - 1)
    diag = ki * tk + (tk - 1) > qi * tq

    @pl.when(needed & diag)
    def _():
        compute(True)

    @pl.when(needed & jnp.logical_not(diag))
    def _():
        compute(False)

    @pl.when(ki == nk - 1)
    def _():
        for h in range(H):
            a = acc_scr[h]
            o_ref[:, h * Dh:(h + 1) * Dh] = (a[:, :Dh] / a[:, Dh:Dh + 1]).astype(o_ref.dtype)


def _fox_flash(qa, ka, va, *, B, T, tq, tk):
    H, Dh = FOX_HEADS, FOX_HEAD_DIM
    D = H * Dh
    nq, nk = T // tq, T // tk
    last_k = lambda qi: (qi * tq + tq - 1) // tk
    kblk = lambda b, qi, ki: (b, 0, jnp.minimum(ki, last_k(qi)), 0)
    return pl.pallas_call(
        functools.partial(_fox_flash_kernel, tq=tq, tk=tk, nk=nk),
        grid=(B, nq, nk),
        in_specs=[pl.BlockSpec((None, H, tq, LANES), lambda b, qi, ki: (b, 0, qi, 0)),
                  pl.BlockSpec((None, H, tk, LANES), kblk), pl.BlockSpec((None, H, tk, LANES), kblk)],
        out_specs=pl.BlockSpec((tq, D), lambda b, qi, ki: (b * nq + qi, 0)),
        out_shape=jax.ShapeDtypeStruct((B * T, D), BF16),
        scratch_shapes=[pltpu.VMEM((H, tq, 1), F32), pltpu.VMEM((H, tq, LANES), F32)],
        compiler_params=_cparams(3), name="fox_flash",
    )(qa, ka, va)


def _fox_sample_kernel(pt_ref, q_ref, kn_ref, vn_ref, cn_ref, cnt_ref, *rest, n_steps, pps):
    kt_refs, vt_refs, lft_refs = rest[:pps], rest[pps:2 * pps], rest[2 * pps:3 * pps]
    o_ref, qbd_scr, cq_scr, later_scr, m_scr, l_scr, acc_scr = rest[3 * pps:]
    p = pl.program_id(1)
    H, Dh = FOX_HEADS, FOX_HEAD_DIM
    TS = q_ref.shape[0]
    R = H * TS
    D = H * Dh
    KB = pps * kt_refs[0].shape[-1]

    @pl.when(p == 0)
    def _():
        row = lax.broadcasted_iota(jnp.int32, (R, D), 0)
        lane = lax.broadcasted_iota(jnp.int32, (R, D), 1)
        q_rep = jnp.broadcast_to(q_ref[...][None], (H, TS, D)).reshape(R, D) * (Dh ** -0.5)
        qbd_scr[...] = jnp.where(lane // Dh == row // TS, q_rep, 0.0).astype(BF16)
        cw = cn_ref.shape[1]
        c_rep = jnp.broadcast_to(cn_ref[...][None], (H, TS, cw)).reshape(R, cw)
        r2 = lax.broadcasted_iota(jnp.int32, (R, cw), 0)
        l2 = lax.broadcasted_iota(jnp.int32, (R, cw), 1)
        cq_scr[...] = jnp.sum(jnp.where(l2 == r2 // TS, c_rep, 0.0), axis=-1, keepdims=True)
        later_scr[...] = jnp.zeros_like(later_scr)
        m_scr[...] = jnp.full_like(m_scr, -jnp.inf)
        l_scr[...] = jnp.zeros_like(l_scr)
        acc_scr[...] = jnp.zeros_like(acc_scr)

    def online(s, pv):
        m_prev = m_scr[...]
        m_new = jnp.maximum(m_prev, jnp.max(s, axis=-1, keepdims=True))
        pr = jnp.exp(s - m_new)
        alpha = jnp.exp(m_prev - m_new)
        l_scr[...] = alpha * l_scr[...] + jnp.sum(pr, axis=-1, keepdims=True)
        acc_scr[...] = alpha * acc_scr[...] + pv(pr.astype(BF16))
        m_scr[...] = m_new

    def past_pages():
        lft = jnp.concatenate([r[...] for r in lft_refs], axis=1)
        kt = jnp.concatenate([r[...].reshape(D, -1).astype(BF16) for r in kt_refs], axis=1)
        vt = jnp.concatenate([r[...].reshape(D, -1).astype(BF16) for r in vt_refs], axis=1)
        si = lax.broadcasted_iota(jnp.int32, (KB, KB), 0)
        ki = lax.broadcasted_iota(jnp.int32, (KB, KB), 1)
        suf = _sel_cols(lft, si > ki) + later_scr[...]
        later_scr[...] = later_scr[...] + jnp.sum(lft, axis=-1, keepdims=True)
        bias = jnp.broadcast_to(suf[:, None, :], (H, TS, KB)).reshape(R, KB)
        s = jnp.dot(qbd_scr[...], kt, preferred_element_type=F32) + cq_scr[...] + bias
        online(s, lambda pr: _nt(pr, vt))

    past_pages()

    @pl.when(p == n_steps - 1)
    def _():
        cnt = cnt_ref[...]
        bias = jnp.broadcast_to(cnt[:, None, :], (H, TS, TS)).reshape(R, TS)
        s = _nt(qbd_scr[...], kn_ref[...].astype(BF16)) + cq_scr[...] - bias
        tq = lax.broadcasted_iota(jnp.int32, (R, TS), 0) % TS
        tk = lax.broadcasted_iota(jnp.int32, (R, TS), 1)
        s = jnp.where(tk <= tq, s, -jnp.inf)
        vn = vn_ref[...].astype(BF16)
        online(s, lambda pr: jnp.dot(pr, vn, preferred_element_type=F32))
        row = lax.broadcasted_iota(jnp.int32, (R, D), 0)
        lane = lax.broadcasted_iota(jnp.int32, (R, D), 1)
        own = jnp.where(lane // Dh == row // TS, acc_scr[...] / l_scr[...], 0.0)
        o_ref[...] = jnp.sum(own.reshape(H, TS, D), axis=0)


def _fox_sample(q, k_new, v_new, cum_new, cum_new_t, cache_kt, cache_vt, cache_logf_t, page_table, *, B, TS, pps):
    M, D = q.shape
    n_pages = page_table.shape[1]
    n_steps = n_pages // pps
    _, H, Dh, PG = cache_kt.shape
    R = H * TS
    row = pl.BlockSpec((TS, D), lambda b, p, pt: (b, 0))

    def page(s, nd):
        return lambda b, p, pt: (pt[b * n_pages + n_pages - (p + 1) * pps + s],) + (0,) * nd

    slots = range(pps)
    grid_spec = pltpu.PrefetchScalarGridSpec(
        num_scalar_prefetch=1, grid=(B, n_steps),
        in_specs=[row, row, row,
                  pl.BlockSpec((TS, cum_new.shape[1]), lambda b, p, pt: (b, 0)),
                  pl.BlockSpec((None, H, TS), lambda b, p, pt: (b, 0, 0))]
                 + [pl.BlockSpec((None, H, Dh, PG), page(s, 3)) for s in slots]
                 + [pl.BlockSpec((None, H, Dh, PG), page(s, 3)) for s in slots]
                 + [pl.BlockSpec((None, H, PG), page(s, 2)) for s in slots],
        out_specs=row,
        scratch_shapes=[pltpu.VMEM((R, D), BF16), pltpu.VMEM((R, 1), F32), pltpu.VMEM((H, 1), F32),
                        pltpu.VMEM((R, 1), F32), pltpu.VMEM((R, 1), F32), pltpu.VMEM((R, D), F32)])
    return pl.pallas_call(
        functools.partial(_fox_sample_kernel, n_steps=n_steps, pps=pps),
        grid_spec=grid_spec,
        out_shape=jax.ShapeDtypeStruct((M, D), F32),
        compiler_params=_cparams(2), name="fox_sample",
    )(page_table.reshape(-1), q, k_new, v_new, cum_new, cum_new_t,
      *([cache_kt] * pps), *([cache_vt] * pps), *([cache_logf_t] * pps))


def _rwkv_in_kernel(x_ref, g_ref, sc_ref, sh_ref, st_ref, mu_ref, wrkv_ref, w0_ref, w1_ref, w2_ref, a0_ref,
                    a1_ref, a2_ref, g1_ref, g2_ref,
                    r_ref, k_ref, v_ref, lw_ref, a_ref, gg_ref, tail_ref, h_scr, d_scr, carry_scr, *, T):
    i, j = pl.program_id(0), pl.program_id(1)
    tm, D = x_ref.shape
    nt = tm // SUBLANES
    long_seq = T >= tm

    @pl.when(j == 0)
    def _():
        h = _modnorm(x_ref[...], g_ref[...], sc_ref[...], sh_ref[...], T)
        h3 = h.reshape(nt, SUBLANES, D)
        if long_seq:
            @pl.when(i % (T // tm) == 0)
            def _():
                carry_scr[...] = st_ref[...]
            prev3 = jnp.concatenate([carry_scr[...][None], h3[:nt - 1]], axis=0)
            carry_scr[...] = h3[nt - 1]
            tail_ref[...] = h3[nt - 1]
        else:
            prev3 = st_ref[...]
            tail_ref[...] = h3
        prev = _shift_rows(h3, prev3, 1).reshape(tm, D)
        h_scr[...] = h
        d_scr[...] = prev - h

    def mix(s):
        return (h_scr[...] + d_scr[...] * mu_ref[s:s + 1, :]).astype(BF16)

    for s, out in enumerate((r_ref, k_ref, v_ref)):
        @pl.when(j == s)
        def _(s=s, out=out):
            out[...] = jnp.dot(mix(s), wrkv_ref[...], preferred_element_type=F32)

    @pl.when(j == 3)
    def _():
        t = jnp.tanh(jnp.dot(mix(3), w1_ref[...], preferred_element_type=F32))
        pre_w = w0_ref[...] + jnp.dot(t.astype(BF16), w2_ref[...], preferred_element_type=F32)
        w_log = -jax.nn.softplus(-pre_w) - 0.5
        lw_ref[...] = -jnp.exp(w_log)
        u = jnp.dot(mix(4), a1_ref[...], preferred_element_type=F32)
        pre_a = a0_ref[...] + jnp.dot(u.astype(BF16), a2_ref[...], preferred_element_type=F32)
        a_ref[...] = jax.nn.sigmoid(pre_a)
        sg = jax.nn.sigmoid(jnp.dot(mix(5), g1_ref[...], preferred_element_type=F32))
        gg_ref[...] = jnp.dot(sg.astype(BF16), g2_ref[...], preferred_element_type=F32)


def _rwkv_in(x, g, sc, sh, st_tiles, mu, w_rkv, w0, w1, w2, a0, a1, a2, g1, g2, *, T, tm):
    M, D = x.shape
    B = st_tiles.shape[0]
    P = LANES
    c2 = lambda shape: pl.BlockSpec(shape, lambda i, j: (0,) * len(shape))
    row = pl.BlockSpec((tm, D), lambda i, j: (i, 0))
    return pl.pallas_call(
        functools.partial(_rwkv_in_kernel, T=T),
        grid=(M // tm, 4),
        in_specs=[row, c2((1, D)), _mod_spec(T, tm, D), _mod_spec(T, tm, D), _tile_spec(T, tm, D),
                  c2((6, D)), pl.BlockSpec((None, D, D), lambda i, j: (jnp.minimum(j, 2), 0, 0)),
                  c2((1, D)), c2((D, P)), c2((P, D)), c2((1, D)), c2((D, P)), c2((P, D)), c2((D, P)), c2((P, D))],
        out_specs=[row] * 6 + [_tile_spec(T, tm, D)],
        out_shape=[jax.ShapeDtypeStruct((M, D), F32)] * 6 + [jax.ShapeDtypeStruct((B, SUBLANES, D), F32)],
        scratch_shapes=[pltpu.VMEM((tm, D), F32), pltpu.VMEM((tm, D), F32), pltpu.VMEM((SUBLANES, D), F32)],
        compiler_params=_cparams(2),
        name="rwkv_in",
    )(x, g.reshape(1, D), sc, sh, st_tiles, mu, w_rkv, w0.reshape(1, D), w1, w2, a0.reshape(1, D), a1, a2, g1, g2)


def _wkv_kernel(r_ref, k_ref, v_ref, lw_ref, a_ref, g_ref, kk_ref, ka_ref, rk_ref, lnw_ref, lnb_ref, s0_ref,
                o_ref, sout_ref, s_scr, *, L, n_chunks):
    c = pl.program_id(1)
    H, N = RWKV_HEADS, RWKV_HEAD
    G = WKV_ROWS // L
    R = G * L

    @pl.when(c == 0)
    def _():
        s_scr[...] = s0_ref[...]

    ri = lax.broadcasted_iota(jnp.int32, (L, L), 0)
    ci = lax.broadcasted_iota(jnp.int32, (L, L), 1)
    lw = lw_ref[...]
    cs = _sel_rows(ri >= ci, lw)
    e_pos = jnp.exp(cs)
    e_neg = jnp.exp(-cs)
    e_prev = jnp.exp(cs - lw)

    row = lax.broadcasted_iota(jnp.int32, (R, R), 0)
    col = lax.broadcasted_iota(jnp.int32, (R, R), 1)
    same = (row // L) == (col // L)
    strict = same & (row > col)
    incl = same & (row >= col)

    for grp in range(H // G):
        heads = range(grp * G, (grp + 1) * G)
        A_l, R_l, B_l, K_l, V_l, kp_l = [], [], [], [], [], []
        for h in heads:
            sl = slice(h * N, (h + 1) * N)
            kh, ah = k_ref[:, sl], a_ref[:, sl]
            kkh = kh * kk_ref[:, sl]
            nrm = jnp.sqrt(jnp.sum(kkh * kkh, axis=-1, keepdims=True))
            kkh = kkh / jnp.maximum(nrm, 1e-12)
            kph = kh * (1.0 + (ah - 1.0) * ka_ref[:, sl])
            en = e_neg[:, sl]
            A_l.append(-kkh * e_prev[:, sl])
            R_l.append(r_ref[:, sl] * e_pos[:, sl])
            B_l.append(kkh * ah * en)
            K_l.append(kph * en)
            V_l.append(v_ref[:, sl])
            kp_l.append(kph)
        AR = jnp.concatenate(A_l + R_l, axis=0).astype(BF16)
        BK = jnp.concatenate(B_l + K_l, axis=0).astype(BF16)
        V4 = jnp.concatenate(V_l, axis=0)
        Q = _nt(AR, BK)
        ars_a, ars_r = [], []
        for q, h in enumerate(heads):
            Sb = s_scr[h].astype(BF16)
            ars_a.append(_nt(AR[q * L:(q + 1) * L], Sb))
            ars_r.append(_nt(AR[R + q * L:R + (q + 1) * L], Sb))
        nmat = jnp.where(strict, Q[:R, :R], 0.0)
        ak = jnp.where(strict, Q[:R, R:], 0.0)
        rhs = jnp.concatenate(ars_a, axis=0) + jnp.dot(ak.astype(BF16), V4.astype(BF16), preferred_element_type=F32)
        Y = jnp.where(((row // 2) == (col // 2)) & (row > col), nmat, 0.0)
        b = 2
        while b < L:
            Ck = jnp.where(((row // (2 * b)) == (col // (2 * b))) & ((row % (2 * b)) >= b) & ((col % (2 * b)) < b),
                           nmat, 0.0)
            Yb = Y.astype(BF16)
            Tm = Ck + jnp.dot(Yb, Ck.astype(BF16), preferred_element_type=F32)
            Y = Y + Tm + jnp.dot(Tm.astype(BF16), Yb, preferred_element_type=F32)
            b *= 2
        E = rhs + jnp.dot(Y.astype(BF16), rhs.astype(BF16), preferred_element_type=F32)
        EV = jnp.concatenate([E, V4], axis=0).astype(BF16)
        rbk = jnp.concatenate([jnp.where(incl, Q[R:, :R], 0.0), jnp.where(incl, Q[R:, R:], 0.0)], axis=1)
        y4 = jnp.concatenate(ars_r, axis=0) + jnp.dot(rbk.astype(BF16), EV, preferred_element_type=F32)
        for q, h in enumerate(heads):
            sl = slice(h * N, (h + 1) * N)
            rows = slice(q * L, (q + 1) * L)
            ev_h = jnp.concatenate([EV[rows], EV[R + q * L:R + (q + 1) * L]], axis=0)
            bk_h = jnp.concatenate([BK[rows], BK[R + q * L:R + (q + 1) * L]], axis=0)
            s_scr[h] = (s_scr[h] + _tn(ev_h, bk_h)) * e_pos[L - 1:L, sl]
            y = y4[rows]
            mean = jnp.mean(y, axis=-1, keepdims=True)
            var = jnp.mean(jnp.square(y - mean), axis=-1, keepdims=True)
            yn = (y - mean) * lax.rsqrt(var + RWKV_GN_EPS) * lnw_ref[:, sl] + lnb_ref[:, sl]
            bonus = jnp.sum(r_ref[:, sl] * kp_l[q] * rk_ref[:, sl], axis=-1, keepdims=True) * V_l[q]
            o_ref[:, sl] = (yn + bonus) * g_ref[:, sl]

    @pl.when(c == n_chunks - 1)
    def _():
        sout_ref[...] = s_scr[...]


def _wkv_core(r, k, v, lw, a, g, k_k, k_a, r_k, ln_w, ln_b, s0, *, T, L):
    M, D = r.shape
    B = s0.shape[0]
    nc = T // L
    row = pl.BlockSpec((L, D), lambda b, c: (b * nc + c, 0))
    par = pl.BlockSpec((1, D), lambda b, c: (0, 0))
    st = pl.BlockSpec((None, RWKV_HEADS, RWKV_HEAD, RWKV_HEAD), lambda b, c: (b, 0, 0, 0))
    v1 = lambda t: t.reshape(1, D)
    return pl.pallas_call(
        functools.partial(_wkv_kernel, L=L, n_chunks=nc),
        grid=(B, nc),
        in_specs=[row] * 6 + [par] * 5 + [st],
        out_specs=[row, st],
        out_shape=[jax.ShapeDtypeStruct((M, D), F32), jax.ShapeDtypeStruct(s0.shape, F32)],
        scratch_shapes=[pltpu.VMEM((RWKV_HEADS, RWKV_HEAD, RWKV_HEAD), F32)],
        compiler_params=_cparams(2),
        name="wkv_core",
    )(r, k, v, lw, a, g, v1(k_k), v1(k_a), v1(r_k), v1(ln_w), v1(ln_b), s0)


def _wkv_long_kernel(r_ref, k_ref, v_ref, lw_ref, a_ref, g_ref, kk_ref, ka_ref, rk_ref, lnw_ref, lnb_ref, s0_ref,
                     o_ref, sout_ref, s_scr, *, L, NC, n_steps):
    step = pl.program_id(1)
    H, N, G = RWKV_HEADS, RWKV_HEAD, WKV_GROUP_HEADS
    GW = G * N
    NG = H // G
    R = G * L
    TB = NC * L

    @pl.when(step == 0)
    def _():
        s_scr[...] = jnp.zeros_like(s_scr)
        for h in range(H):
            q = h % G
            s_scr[h // G, q * N:(q + 1) * N, q * N:(q + 1) * N] = s0_ref[h]

    ti = lax.broadcasted_iota(jnp.int32, (TB, TB), 0)
    tj = lax.broadcasted_iota(jnp.int32, (TB, TB), 1)
    same_c = (ti // L) == (tj // L)
    lw = lw_ref[...]
    cs = _sel_rows(same_c & (ti >= tj), lw)
    ctot = jnp.concatenate([jnp.broadcast_to(cs[(c + 1) * L - 1:(c + 1) * L], (L, cs.shape[1])) for c in range(NC)],
                           axis=0)
    e_pos, e_neg, e_prev = jnp.exp(cs), jnp.exp(-cs), jnp.exp(cs - lw)
    e_hat, e_tot = jnp.exp(ctot - cs), jnp.exp(ctot)
    gi = lax.broadcasted_iota(jnp.int32, (GW, GW), 0) // N
    gj = lax.broadcasted_iota(jnp.int32, (GW, GW), 1) // N
    gones = (gi == gj).astype(F32)

    def head_sum(x):
        return jnp.concatenate([_sel_cols(x[:, g * GW:(g + 1) * GW], gones) for g in range(NG)], axis=1)

    r, k, v, a = r_ref[...], k_ref[...], v_ref[...], a_ref[...]
    kkv = k * kk_ref[...]
    kkn = kkv / jnp.maximum(jnp.sqrt(head_sum(kkv * kkv)), 1e-12)
    kp = k * (1.0 + (a - 1.0) * ka_ref[...])
    ba = kkn * a
    nat = dict(A=-kkn * e_prev, R=r * e_pos, Bt=ba * e_neg, Kt=kp * e_neg, Bh=ba * e_hat, Kh=kp * e_hat, V=v)
    bonus = head_sum(r * kp * rk_ref[...]) * v

    row = lax.broadcasted_iota(jnp.int32, (R, GW), 0)
    lane = lax.broadcasted_iota(jnp.int32, (R, GW), 1)
    hm = (row // L) == (lane // N)

    def bd(name, c, g):
        x = nat[name][c * L:(c + 1) * L, g * GW:(g + 1) * GW]
        return jnp.where(hm, jnp.concatenate([x] * G, axis=0), 0.0)

    mr = lax.broadcasted_iota(jnp.int32, (R, R), 0)
    mc = lax.broadcasted_iota(jnp.int32, (R, R), 1)
    same_h = (mr // L) == (mc // L)
    strict = same_h & (mr > mc)
    incl = same_h & (mr >= mc)

    units = [(c, g) for c in range(NC) for g in range(NG)]
    U = {}
    for u in units:
        ops = {n: bd(n, *u) for n in nat}
        AR = jnp.concatenate([ops["A"], ops["R"]], axis=0).astype(BF16)
        BK = jnp.concatenate([ops["Bt"], ops["Kt"]], axis=0).astype(BF16)
        U[u] = dict(ops=ops, Q=_nt(AR, BK))
    for u in units:
        d = U[u]
        Q = d.pop("Q")
        d["nmat"] = jnp.where(strict, Q[:R, :R], 0.0)
        d["ak"] = jnp.where(strict, Q[:R, R:], 0.0).astype(BF16)
        d["rb"] = jnp.where(incl, Q[R:, :R], 0.0).astype(BF16)
        d["rk"] = jnp.where(incl, Q[R:, R:], 0.0).astype(BF16)
        d["Y"] = jnp.where(((mr // 2) == (mc // 2)) & (mr > mc), d["nmat"], 0.0)
    b = 2
    while b < L:
        lvl = ((mr // (2 * b)) == (mc // (2 * b))) & ((mr % (2 * b)) >= b) & ((mc % (2 * b)) < b)
        for u in units:
            d = U[u]
            Ck = jnp.where(lvl, d["nmat"], 0.0)
            d["Yb"] = d["Y"].astype(BF16)
            d["Tm"] = Ck + jnp.dot(d["Yb"], Ck.astype(BF16), preferred_element_type=F32)
        for u in units:
            d = U[u]
            d["Y"] = d["Y"] + d["Tm"] + jnp.dot(d["Tm"].astype(BF16), d["Yb"], preferred_element_type=F32)
        b *= 2
    for u in units:
        d = U[u]
        vb = d["ops"]["V"].astype(BF16)
        d["akv"] = jnp.dot(d["ak"], vb, preferred_element_type=F32)
        d["rkv"] = jnp.dot(d["rk"], vb, preferred_element_type=F32)
    for u in units:
        d = U[u]
        x = jnp.concatenate([d["ops"]["A"], d["akv"]], axis=1)
        d["AE"] = x + jnp.dot(d["Y"].astype(BF16), x.astype(BF16), preferred_element_type=F32)
    for u in units:
        d = U[u]
        base = jnp.concatenate([d["ops"]["R"], d["rkv"]], axis=1)
        ry = base + jnp.dot(d["rb"], d["AE"].astype(BF16), preferred_element_type=F32)
        d["Rbar"] = ry[:, :GW].astype(BF16)
        d["y0"] = ry[:, GW:]
    for u in units:
        d = U[u]
        bh = d["ops"]["Bh"].astype(BF16)
        kh = d["ops"]["Kh"].astype(BF16)
        d["W"] = _tn(d["AE"][:, :GW].astype(BF16), bh).astype(BF16)
        ev = jnp.concatenate([d["AE"][:, GW:], d["ops"]["V"]], axis=0).astype(BF16)
        d["Z"] = _tn(ev, jnp.concatenate([bh, kh], axis=0))

    ys = []
    for c in range(NC):
        yc = []
        for g in range(NG):
            d = U[(c, g)]
            S = s_scr[g]
            Sb = S.astype(BF16)
            ybd = d["y0"] + _nt(d["Rbar"], Sb)
            yc.append(sum(ybd[q * L:(q + 1) * L] for q in range(G)))
            decay = e_tot[c * L:c * L + 1, g * GW:(g + 1) * GW]
            s_scr[g] = S * decay + jnp.dot(Sb, d["W"], preferred_element_type=F32) + d["Z"]
        ys.append(jnp.concatenate(yc, axis=1))
    y = jnp.concatenate(ys, axis=0)

    mean = head_sum(y) * (1.0 / N)
    yc = y - mean
    var = head_sum(yc * yc) * (1.0 / N)
    yn = yc * lax.rsqrt(var + RWKV_GN_EPS) * lnw_ref[...] + lnb_ref[...]
    o_ref[...] = ((yn + bonus) * g_ref[...]).astype(o_ref.dtype)

    @pl.when(step == n_steps - 1)
    def _():
        for h in range(H):
            q = h % G
            sout_ref[h] = s_scr[h // G, q * N:(q + 1) * N, q * N:(q + 1) * N]


def _wkv_long(r, k, v, lw, a, g, k_k, k_a, r_k, ln_w, ln_b, s0, *, T, L, NC):
    M, D = r.shape
    B = s0.shape[0]
    TB = NC * L
    ns = T // TB
    row = pl.BlockSpec((TB, D), lambda b, c: (b * ns + c, 0))
    par = pl.BlockSpec((1, D), lambda b, c: (0, 0))
    st = pl.BlockSpec((None, RWKV_HEADS, RWKV_HEAD, RWKV_HEAD), lambda b, c: (b, 0, 0, 0))
    v1 = lambda t: t.reshape(1, D)
    GW = WKV_GROUP_HEADS * RWKV_HEAD
    return pl.pallas_call(
        functools.partial(_wkv_long_kernel, L=L, NC=NC, n_steps=ns),
        grid=(B, ns),
        in_specs=[row] * 6 + [par] * 5 + [st],
        out_specs=[row, st],
        out_shape=[jax.ShapeDtypeStruct((M, D), BF16), jax.ShapeDtypeStruct(s0.shape, F32)],
        scratch_shapes=[pltpu.VMEM((RWKV_HEADS // WKV_GROUP_HEADS, GW, GW), F32)],
        compiler_params=_cparams(2),
        name="wkv_long",
    )(r, k, v, lw, a, g, v1(k_k), v1(k_a), v1(r_k), v1(ln_w), v1(ln_b), s0)


def _pad_cols(w, n):
    return jnp.pad(w, ((0, 0), (0, n - w.shape[1])))


def _pad_rows(w, n):
    return jnp.pad(w, ((0, n - w.shape[0]), (0, 0)))


def _state_tiles(st):
    return jnp.pad(st, ((0, 0), (SUBLANES - st.shape[1], 0), (0, 0)))


def _ssm_from_groups(s):
    B = s.shape[0]
    s = s.reshape(B, SSD_GROUPS, SSD_STATE, SSD_HPG, SSD_HEAD_DIM)
    return s.transpose(0, 1, 3, 4, 2).reshape(B, SSD_HEADS, SSD_HEAD_DIM, SSD_STATE)


def kernel(x_prompt, x_sample, c_prompt, c_sample, state_ssm, state_ssd_conv, cache_k, cache_v, cache_logf, page_table, state_wkv, state_shift, state_ffn_conv, ada_w, ada_b, norm_mix, norm_ffn, norm_final, ssd_in_w, ssd_conv_w, ssd_conv_b, ssd_dt_bias, ssd_a_log, ssd_d, ssd_norm_w, ssd_out_w, fox_in_w, fox_f_b, fox_out_w, rwkv_mu, rwkv_w_rkv, rwkv_w0, rwkv_w1, rwkv_w2, rwkv_a0, rwkv_a1, rwkv_a2, rwkv_g1, rwkv_g2, rwkv_k_k, rwkv_k_a, rwkv_r_k, rwkv_ln_w, rwkv_ln_b, rwkv_out_w, ffn_up_w, ffn_conv_w, ffn_conv_b, ffn_down_w):
    BP, TP, D = x_prompt.shape
    BS, TS, _ = x_sample.shape
    groups = [dict(B=BP, T=TP, tm=1024, tm_s=512, ssd_l=SSD_CHUNK, wkv_l=RWKV_CHUNK, gate_tc=512),
              dict(B=BS, T=TS, tm=BS * TS, tm_s=512, ssd_l=TS, wkv_l=TS, gate_tc=TS)]
    xs = [x_prompt.reshape(BP * TP, D), x_sample.reshape(BS * TS, D)]

    n_c = BP + BS
    n_c_pad = -(-n_c // SUBLANES) * SUBLANES
    c_all = jnp.pad(jnp.concatenate([c_prompt, c_sample], axis=0), ((0, n_c_pad - n_c), (0, 0)))
    mods_all = _ada_mods(c_all, ada_w, ada_b).reshape(DEPTH, n_c_pad, 6, 1, D)
    row0 = [0, BP]

    def mods_of(i, gi):
        m = mods_all[i, row0[gi]:row0[gi] + groups[gi]["B"]]
        return [m[:, k] for k in range(6)]

    outs = {k: ([], []) for k in ("ssm", "sconv", "k", "v", "lf", "wkv", "shift", "fconv")}
    ssm_sample_all = None

    for i in range(DEPTH):
        kind, j = i % N_MIXERS, i // N_MIXERS
        for gi, gp in enumerate(groups):
            B, T, tm, tm_s = gp["B"], gp["T"], gp["tm"], gp["tm_s"]
            x = xs[gi]
            prompt = gi == 0
            sh_m, sc_m, gt_m, sh_f, sc_f, gt_f = mods_of(i, gi)
            if kind == 0:
                w_main = ssd_in_w[j][:, :SSD_D_INNER + SSD_CONV_CH].astype(BF16)
                w_dt = _pad_cols(ssd_in_w[j][:, SSD_D_INNER + SSD_CONV_CH:], LANES).astype(BF16)
                zx = _norm_mod_matmul(x, norm_mix[i], sc_m, sh_m, w_main, T=T, tm=tm, tn=1024)
                dt = _norm_mod_matmul(x, norm_mix[i], sc_m, sh_m, w_dt, T=T, tm=tm, tn=LANES)
                ssd_par = (ssd_conv_w[j], ssd_conv_b[j], ssd_dt_bias[j], ssd_a_log[j], ssd_d[j], ssd_norm_w[j])
                if prompt:
                    conv_tiles = jnp.zeros((B, SUBLANES, SSD_CONV_CH), F32)
                    h0 = jnp.zeros((1, B, SSD_GROUPS, SSD_STATE, SSD_HPG * SSD_HEAD_DIM), F32)
                    y, ctail, h_new = _ssd_core(zx, dt, conv_tiles, h0, *ssd_par, B=B, T=T, L=gp["ssd_l"],
                                                natural=False)
                    outs["ssm"][gi].append(_ssm_from_groups(h_new[0]))
                else:
                    conv_tiles = _state_tiles(state_ssd_conv[j])
                    h0 = state_ssm.reshape(-1, B, SSD_GROUPS, SSD_HPG * SSD_HEAD_DIM, SSD_STATE)
                    y, ctail, ssm_sample_all = _ssd_core(zx, dt, conv_tiles, h0, *ssd_par, B=B, T=T, L=gp["ssd_l"],
                                                         natural=True, layer=j, out_buf=ssm_sample_all)
                outs["sconv"][gi].append(ctail[:, SUBLANES - (SSD_CONV - 1):])
                x = _res_matmul(y, ssd_out_w[j].astype(BF16), x, gt_m, T=T, tm=tm)
            elif kind == 1:
                w_qkv = fox_in_w[j][:, :3 * FOX_INNER].reshape(D, 3, FOX_INNER).transpose(1, 0, 2).astype(BF16)
                w_f = _pad_cols(fox_in_w[j][:, 3 * FOX_INNER:], LANES).astype(BF16)
                f = _norm_mod_matmul(x, norm_mix[i], sc_m, sh_m, w_f, T=T, tm=tm, tn=LANES)
                f_b = jnp.pad(fox_f_b[j], (0, LANES - FOX_HEADS)).reshape(1, LANES)
                lf, cum = _fox_gate(f, f_b, B=B, T=T, tc=gp["gate_tc"])
                if prompt:
                    k, v, qa, ka, va = _fox_proj_aug(x, norm_mix[i], sc_m, sh_m, w_qkv, cum, B=B, T=T, tm=tm_s)
                    o = _fox_flash(qa, ka, va, B=B, T=T, tq=512, tk=512)
                else:
                    q, k, v = _norm_mod_matmul_multi(x, norm_mix[i], sc_m, sh_m, w_qkv, T=T, tm=tm_s)
                    cum_t = jnp.swapaxes(cum[:, :FOX_HEADS].reshape(B, T, FOX_HEADS), 1, 2)
                    o = _fox_sample(q, k, v, cum, cum_t, jnp.transpose(cache_k[j], (0, 2, 3, 1)),
                                    jnp.transpose(cache_v[j], (0, 2, 3, 1)), jnp.swapaxes(cache_logf[j], 1, 2),
                                    page_table, B=B, TS=T, pps=FOX_PAGES_PER_STEP)
                hd = lambda t: t.reshape(B, T, FOX_HEADS, FOX_HEAD_DIM)
                outs["k"][gi].append(hd(k))
                outs["v"][gi].append(hd(v))
                outs["lf"][gi].append(lf[:, :FOX_HEADS].reshape(B, T, FOX_HEADS))
                x = _res_matmul(o, fox_out_w[j].astype(BF16), x, gt_m, T=T, tm=tm)
            else:
                if prompt:
                    st_tiles = jnp.zeros((B, SUBLANES, D), F32)
                    s0 = jnp.zeros((B, RWKV_HEADS, RWKV_HEAD, RWKV_HEAD), F32)
                else:
                    st_tiles = _state_tiles(state_shift[j][:, None, :])
                    s0 = state_wkv[j]
                r, k, v, lw, a, g, tails = _rwkv_in(
                    x, norm_mix[i], sc_m, sh_m, st_tiles, rwkv_mu[j], rwkv_w_rkv[j].astype(BF16), rwkv_w0[j],
                    _pad_cols(rwkv_w1[j], LANES).astype(BF16), _pad_rows(rwkv_w2[j], LANES).astype(BF16), rwkv_a0[j],
                    _pad_cols(rwkv_a1[j], LANES).astype(BF16), _pad_rows(rwkv_a2[j], LANES).astype(BF16),
                    rwkv_g1[j].astype(BF16), rwkv_g2[j].astype(BF16), T=T, tm=tm_s)
                wkv_par = (rwkv_k_k[j], rwkv_k_a[j], rwkv_r_k[j].reshape(-1), rwkv_ln_w[j], rwkv_ln_b[j], s0)
                if prompt:
                    o, s_new = _wkv_long(r, k, v, lw, a, g, *wkv_par, T=T, L=gp["wkv_l"], NC=WKV_STEP_CHUNKS)
                else:
                    o, s_new = _wkv_core(r, k, v, lw, a, g, *wkv_par, T=T, L=gp["wkv_l"])
                outs["wkv"][gi].append(s_new)
                outs["shift"][gi].append(tails[:, SUBLANES - 1])
                x = _res_matmul(o, rwkv_out_w[j].astype(BF16), x, gt_m, T=T, tm=tm)
            if prompt:
                ffn_tiles = jnp.zeros((B, SUBLANES, D_FF), F32)
            else:
                ffn_tiles = _state_tiles(state_ffn_conv[i])
            x, tails = _conv_ffn(x, norm_ffn[i], sc_f, sh_f, gt_f, ffn_tiles, ffn_up_w[i].astype(BF16),
                                 ffn_conv_w[i], ffn_conv_b[i], ffn_down_w[i].astype(BF16), T=T, tm=tm_s)
            outs["fconv"][gi].append(tails[:, SUBLANES - (FFN_CONV - 1):])
            xs[gi] = x

    y_prompt = _rmsnorm(xs[0], norm_final, tm=1024).reshape(BP, TP, D)
    y_sample = _rmsnorm(xs[1], norm_final, tm=1024).reshape(BS, TS, D)
    st = lambda name, gi: jnp.stack(outs[name][gi])
    ssm_sample = ssm_sample_all.reshape(state_ssm.shape)
    return (y_prompt, y_sample, st("ssm", 0), ssm_sample, st("sconv", 0), st("sconv", 1),
            st("k", 0), st("k", 1), st("v", 0), st("v", 1), st("lf", 0), st("lf", 1),
            st("wkv", 0), st("wkv", 1), st("shift", 0), st("shift", 1), st("fconv", 0), st("fconv", 1))
```

```python
import functools

import jax
import jax.numpy as jnp
from jax import lax
from jax.experimental import pallas as pl
from jax.experimental.pallas import tpu as pltpu

F32 = jnp.float32
BF16 = jnp.bfloat16

D_MODEL = 1024
DEPTH = 4
N_MIXERS = 3
RMS_EPS = 1e-6
SSD_D_INNER = 2048
SSD_HEAD_DIM = 64
SSD_HEADS = 32
SSD_GROUPS = 4
SSD_HPG = 8
SSD_STATE = 128
SSD_CONV = 4
SSD_CONV_CH = 3072
SSD_CHUNK = 128
FOX_HEAD_DIM = 64
FOX_HEADS = 16
FOX_INNER = 1024
RWKV_HEAD = 64
RWKV_HEADS = 16
RWKV_GN_EPS = 64e-5
RWKV_CHUNK = 32
WKV_ROWS = 128
WKV_GROUP_HEADS = 4
WKV_STEP_CHUNKS = 4
FOX_PAGES_PER_STEP = 8
FOX_BIAS_TERMS = 3
D_FF = 2816
FFN_CONV = 3

SUBLANES = 8
LANES = 128
VMEM_LIMIT = 56 * 1024 * 1024


def _cparams(n_grid):
    return pltpu.CompilerParams(dimension_semantics=("arbitrary",) * n_grid,
                                vmem_limit_bytes=VMEM_LIMIT)


def _nt(a, b):
    return lax.dot_general(a, b, (((1,), (1,)), ((), ())), preferred_element_type=F32)


def _tn(a, b):
    return lax.dot_general(a, b, (((0,), (0,)), ((), ())), preferred_element_type=F32)


def _split3(c):
    hi = c.astype(BF16)
    r1 = c - hi.astype(F32)
    mid = r1.astype(BF16)
    lo = (r1 - mid.astype(F32)).astype(BF16)
    return hi, mid, lo


def _sel_rows(sel, x):
    n = x.shape[1]
    r = jnp.dot(sel.astype(BF16), jnp.concatenate(_split3(x), axis=1), preferred_element_type=F32)
    return r[:, :n] + r[:, n:2 * n] + r[:, 2 * n:]


def _sel_cols(x, sel):
    m = x.shape[0]
    r = jnp.dot(jnp.concatenate(_split3(x), axis=0), sel.astype(BF16), preferred_element_type=F32)
    return r[:m] + r[m:2 * m] + r[2 * m:]


def _modnorm(x, g, sc, sh, T):
    xn = x * lax.rsqrt(jnp.mean(x * x, axis=-1, keepdims=True) + RMS_EPS) * g
    if sc.ndim == 2:
        return xn * (1.0 + sc) + sh
    tm, D = x.shape
    x3 = xn.reshape(tm // T, T, D)
    return (x3 * (1.0 + sc) + sh).reshape(tm, D)


def _gate_mul(y, gt, T):
    if gt.ndim == 2:
        return y * gt
    tm, D = y.shape
    return (y.reshape(tm // T, T, D) * gt).reshape(tm, D)


def _shift_rows(cur3, prev3, k):
    row = lax.broadcasted_iota(jnp.int32, cur3.shape, 1)
    return pltpu.roll(jnp.where(row < SUBLANES - k, cur3, prev3), k, 1)


def _mod_spec(T, tm, D):
    if T >= tm:
        per = T // tm
        return pl.BlockSpec((None, 1, D), lambda i, *_: (i // per, 0, 0))
    return pl.BlockSpec((tm // T, 1, D), lambda i, *_: (i, 0, 0))


def _tile_spec(T, tm, C):
    if T >= tm:
        per = T // tm
        return pl.BlockSpec((None, SUBLANES, C), lambda i, *_: (i // per, 0, 0))
    return pl.BlockSpec((tm // T, SUBLANES, C), lambda i, *_: (i, 0, 0))


def _ada_kernel(c_ref, w_ref, b_ref, o_ref):
    c = c_ref[...]
    a = (c * jax.nn.sigmoid(c)).astype(BF16)
    o_ref[...] = jnp.dot(a, w_ref[...].astype(BF16), preferred_element_type=F32) + b_ref[...]


def _ada_mods(c_all, ada_w, ada_b):
    R, D = c_all.shape
    L, _, N = ada_w.shape
    tn = 1536
    return pl.pallas_call(
        _ada_kernel,
        grid=(L, N // tn),
        in_specs=[pl.BlockSpec((R, D), lambda l, j: (0, 0)),
                  pl.BlockSpec((None, D, tn), lambda l, j: (l, 0, j)),
                  pl.BlockSpec((None, 1, tn), lambda l, j: (l, 0, j))],
        out_specs=pl.BlockSpec((None, R, tn), lambda l, j: (l, 0, j)),
        out_shape=jax.ShapeDtypeStruct((L, R, N), F32),
        compiler_params=_cparams(2),
        name="ada_mods",
    )(c_all, ada_w, ada_b.reshape(L, 1, N))


def _nmm_kernel(x_ref, g_ref, sc_ref, sh_ref, w_ref, o_ref, h_scr, *, T):
    @pl.when(pl.program_id(1) == 0)
    def _():
        h_scr[...] = _modnorm(x_ref[...], g_ref[...], sc_ref[...], sh_ref[...], T).astype(BF16)
    o_ref[...] = jnp.dot(h_scr[...], w_ref[...], preferred_element_type=F32).astype(o_ref.dtype)


def _norm_mod_matmul(x, g, sc, sh, w, *, T, tm, tn):
    M, D = x.shape
    N = w.shape[1]
    return pl.pallas_call(
        functools.partial(_nmm_kernel, T=T),
        grid=(M // tm, N // tn),
        in_specs=[pl.BlockSpec((tm, D), lambda i, j: (i, 0)),
                  pl.BlockSpec((1, D), lambda i, j: (0, 0)),
                  _mod_spec(T, tm, D), _mod_spec(T, tm, D),
                  pl.BlockSpec((D, tn), lambda i, j: (0, j))],
        out_specs=pl.BlockSpec((tm, tn), lambda i, j: (i, j)),
        out_shape=jax.ShapeDtypeStruct((M, N), F32),
        scratch_shapes=[pltpu.VMEM((tm, D), BF16)],
        compiler_params=_cparams(2),
        name="norm_mod_matmul",
    )(x, g.reshape(1, D), sc, sh, w)


def _nmm_multi_kernel(x_ref, g_ref, sc_ref, sh_ref, w_ref, *rest, T, n_out):
    outs, h_scr = rest[:n_out], rest[n_out]
    j = pl.program_id(1)

    @pl.when(j == 0)
    def _():
        h_scr[...] = _modnorm(x_ref[...], g_ref[...], sc_ref[...], sh_ref[...], T).astype(BF16)
    res = jnp.dot(h_scr[...], w_ref[...], preferred_element_type=F32)
    for o in range(n_out):
        @pl.when(j == o)
        def _(o=o):
            outs[o][...] = res


def _norm_mod_matmul_multi(x, g, sc, sh, w_stack, *, T, tm):
    M, D = x.shape
    n_out, _, N = w_stack.shape
    return pl.pallas_call(
        functools.partial(_nmm_multi_kernel, T=T, n_out=n_out),
        grid=(M // tm, n_out),
        in_specs=[pl.BlockSpec((tm, D), lambda i, j: (i, 0)),
                  pl.BlockSpec((1, D), lambda i, j: (0, 0)),
                  _mod_spec(T, tm, D), _mod_spec(T, tm, D),
                  pl.BlockSpec((None, D, N), lambda i, j: (j, 0, 0))],
        out_specs=[pl.BlockSpec((tm, N), lambda i, j: (i, 0)) for _ in range(n_out)],
        out_shape=[jax.ShapeDtypeStruct((M, N), F32) for _ in range(n_out)],
        scratch_shapes=[pltpu.VMEM((tm, D), BF16)],
        compiler_params=_cparams(2),
        name="norm_mod_matmul_multi",
    )(x, g.reshape(1, D), sc, sh, w_stack)


def _res_mm_kernel(a_ref, w_ref, x_ref, gt_ref, o_ref, *, T):
    y = jnp.dot(a_ref[...].astype(BF16), w_ref[...], preferred_element_type=F32)
    o_ref[...] = x_ref[...] + _gate_mul(y, gt_ref[...], T)


def _res_matmul(a, w, x, gt, *, T, tm):
    M, K = a.shape
    D = w.shape[1]
    return pl.pallas_call(
        functools.partial(_res_mm_kernel, T=T),
        grid=(M // tm,),
        in_specs=[pl.BlockSpec((tm, K), lambda i: (i, 0)),
                  pl.BlockSpec((K, D), lambda i: (0, 0)),
                  pl.BlockSpec((tm, D), lambda i: (i, 0)),
                  _mod_spec(T, tm, D)],
        out_specs=pl.BlockSpec((tm, D), lambda i: (i, 0)),
        out_shape=jax.ShapeDtypeStruct((M, D), F32),
        compiler_params=_cparams(1),
        name="res_matmul",
    )(a, w, x, gt)


def _ffn_kernel(x_ref, g_ref, sc_ref, sh_ref, gt_ref, st_ref, upw_ref, cw_ref, cb_ref, dnw_ref,
                o_ref, tail_ref, act_scr, carry_scr, *, T, F, fc):
    i = pl.program_id(0)
    tm, D = x_ref.shape
    nt = tm // SUBLANES
    long_seq = T >= tm
    x = x_ref[...]
    h = _modnorm(x, g_ref[...], sc_ref[...], sh_ref[...], T).astype(BF16)
    if long_seq:
        @pl.when(i % (T // tm) == 0)
        def _():
            carry_scr[...] = st_ref[...]
    for c in range(F // fc):
        lo, hi = c * fc, (c + 1) * fc
        gc = jnp.dot(h, upw_ref[:, lo:hi], preferred_element_type=F32)
        uc = jnp.dot(h, upw_ref[:, F + lo:F + hi], preferred_element_type=F32)
        g3 = gc.reshape(nt, SUBLANES, fc)
        if long_seq:
            prev3 = jnp.concatenate([carry_scr[:, lo:hi][None], g3[:nt - 1]], axis=0)
            carry_scr[:, lo:hi] = g3[nt - 1]
            tail_ref[:, lo:hi] = g3[nt - 1]
        else:
            prev3 = st_ref[:, :, lo:hi]
            tail_ref[:, :, lo:hi] = g3
        conv = (g3 * cw_ref[2:3, lo:hi] + _shift_rows(g3, prev3, 1) * cw_ref[1:2, lo:hi]
                + _shift_rows(g3, prev3, 2) * cw_ref[0:1, lo:hi] + cb_ref[:, lo:hi])
        act = conv * jax.nn.sigmoid(conv) * uc.reshape(nt, SUBLANES, fc)
        act_scr[:, lo:hi] = act.reshape(tm, fc).astype(BF16)
    y = jnp.dot(act_scr[...], dnw_ref[...], preferred_element_type=F32)
    o_ref[...] = x + _gate_mul(y, gt_ref[...], T)


def _conv_ffn(x, g, sc, sh, gt, st_tiles, up_w, conv_w, conv_b, down_w, *, T, tm):
    M, D = x.shape
    F = down_w.shape[0]
    B = st_tiles.shape[0]
    return pl.pallas_call(
        functools.partial(_ffn_kernel, T=T, F=F, fc=256),
        grid=(M // tm,),
        in_specs=[pl.BlockSpec((tm, D), lambda i: (i, 0)),
                  pl.BlockSpec((1, D), lambda i: (0, 0)),
                  _mod_spec(T, tm, D), _mod_spec(T, tm, D), _mod_spec(T, tm, D),
                  _tile_spec(T, tm, F),
                  pl.BlockSpec((D, 2 * F), lambda i: (0, 0)),
                  pl.BlockSpec((FFN_CONV, F), lambda i: (0, 0)),
                  pl.BlockSpec((1, F), lambda i: (0, 0)),
                  pl.BlockSpec((F, D), lambda i: (0, 0))],
        out_specs=[pl.BlockSpec((tm, D), lambda i: (i, 0)), _tile_spec(T, tm, F)],
        out_shape=[jax.ShapeDtypeStruct((M, D), F32), jax.ShapeDtypeStruct((B, SUBLANES, F), F32)],
        scratch_shapes=[pltpu.VMEM((tm, F), BF16), pltpu.VMEM((SUBLANES, F), F32)],
        compiler_params=_cparams(1),
        name="conv_ffn",
    )(x, g.reshape(1, D), sc, sh, gt, st_tiles, up_w, conv_w, conv_b.reshape(1, F), down_w)


def _rms_kernel(x_ref, g_ref, o_ref):
    x = x_ref[...]
    o_ref[...] = x * lax.rsqrt(jnp.mean(x * x, axis=-1, keepdims=True) + RMS_EPS) * g_ref[...]


def _rmsnorm(x, g, *, tm):
    M, D = x.shape
    return pl.pallas_call(
        _rms_kernel, grid=(M // tm,),
        in_specs=[pl.BlockSpec((tm, D), lambda i: (i, 0)), pl.BlockSpec((1, D), lambda i: (0, 0))],
        out_specs=pl.BlockSpec((tm, D), lambda i: (i, 0)),
        out_shape=jax.ShapeDtypeStruct((M, D), F32),
        compiler_params=_cparams(1), name="final_rmsnorm",
    )(x, g.reshape(1, D))


def _conv_silu(cur, prev_tile, w_ref, b_ref, lo, hi):
    L, C = cur.shape
    nt = L // SUBLANES
    c3 = cur.reshape(nt, SUBLANES, C)
    p3 = prev_tile[None] if nt == 1 else jnp.concatenate([prev_tile[None], c3[:nt - 1]], axis=0)
    out = c3 * w_ref[3:4, lo:hi] + b_ref[:, lo:hi]
    for k in range(1, SSD_CONV):
        out = out + _shift_rows(c3, p3, k) * w_ref[3 - k:4 - k, lo:hi]
    out = out * jax.nn.sigmoid(out)
    return out.reshape(L, C)


def _ssd_kernel(z0_ref, z1_ref, x0_ref, x1_ref, bc_ref, dt_ref, cst_ref, h0_ref, cw_ref, cb_ref, dtb_ref, alog_ref,
                dw_ref, nw_ref, ex_ref, *rest, L, n_chunks, natural):
    y_ref, ctail_ref, hout_ref, tail_scr, st_scr = rest[-5:]
    c = pl.program_id(1)
    P, Nn, G, J = SSD_HEAD_DIM, SSD_STATE, SSD_GROUPS, SSD_HPG
    GW = J * P

    @pl.when(c == 0)
    def _():
        tail_scr[...] = cst_ref[...]
        st_scr[...] = h0_ref[...]

    conv = []
    for q, blk_ref in enumerate((x0_ref, x1_ref, bc_ref)):
        blk = blk_ref[...]
        lo, hi = q * 1024, (q + 1) * 1024
        conv.append(_conv_silu(blk, tail_scr[:, lo:hi], cw_ref, cb_ref, lo, hi))
        tail_scr[:, lo:hi] = blk[L - SUBLANES:, :]
        ctail_ref[:, lo:hi] = blk[L - SUBLANES:, :]
    xs = jnp.concatenate(conv[:2], axis=1)
    Bm = conv[2][:, :G * Nn]
    Cm = conv[2][:, G * Nn:]

    dt = jax.nn.softplus(dt_ref[...] + dtb_ref[...])
    a = dt * (-jnp.exp(alog_ref[...]))
    ri = lax.broadcasted_iota(jnp.int32, (L, L), 0)
    ci = lax.broadcasted_iota(jnp.int32, (L, L), 1)
    causal = ri >= ci
    acs = _sel_rows(causal, a)
    wide = _sel_cols(jnp.concatenate([dt, acs], axis=0), ex_ref[...])
    dt_w, acs_w = wide[:L], wide[L:]
    acs_t = acs.T
    last_w = acs_w[L - 1:L, :]
    X = xs * dt_w
    Xd = X * jnp.exp(last_w - acs_w)
    dstart = jnp.exp(acs_w)
    cdec = jnp.exp(last_w)
    zz = jnp.concatenate([z0_ref[...], z1_ref[...]], axis=1)
    zg = zz * jax.nn.sigmoid(zz)
    dwide = dw_ref[...]
    nw = nw_ref[...]
    Xb = X.astype(BF16)
    for g in range(G):
        gl = slice(g * GW, (g + 1) * GW)
        Bg = Bm[:, g * Nn:(g + 1) * Nn].astype(BF16)
        Cg = Cm[:, g * Nn:(g + 1) * Nn].astype(BF16)
        CB = _nt(Cg, Bg)
        st = st_scr[g]
        if natural:
            y_off = _nt(Cg, st.astype(BF16)) * dstart[:, gl]
            dec_h = jnp.exp(acs_t[g * J:(g + 1) * J, L - 1:L])
            dcol = jnp.broadcast_to(dec_h[:, None, :], (J, P, 1)).reshape(GW, 1)
            st_scr[g] = st * dcol + _tn(Xd[:, gl].astype(BF16), Bg)
        else:
            y_off = jnp.dot(Cg, st.astype(BF16), preferred_element_type=F32) * dstart[:, gl]
            st_scr[g] = st * cdec[:, gl] + _tn(Bg, Xd[:, gl].astype(BF16))
        ys = []
        for j in range(J):
            h = g * J + j
            diff = acs[:, h:h + 1] - acs_t[h:h + 1, :]
            dec = jnp.exp(jnp.where(causal, diff, -jnp.inf))
            Mh = (CB * dec).astype(BF16)
            ys.append(jnp.dot(Mh, Xb[:, h * P:(h + 1) * P], preferred_element_type=F32))
        yg = jnp.concatenate(ys, axis=1) + y_off + dwide[:, gl] * xs[:, gl]
        yg = yg * zg[:, gl]
        yg = yg * lax.rsqrt(jnp.mean(yg * yg, axis=-1, keepdims=True) + RMS_EPS) * nw[:, gl]
        y_ref[:, gl] = yg.astype(y_ref.dtype)

    @pl.when(c == n_chunks - 1)
    def _():
        hout_ref[...] = st_scr[...]


def _ssd_core(zx, dt, conv_tiles, h0, conv_w, conv_b, dt_bias, a_log, d_skip, norm_w, *, B, T, L, natural,
              layer=0, out_buf=None):
    M = zx.shape[0]
    nc = T // L
    DI, C = SSD_D_INNER, SSD_CONV_CH
    col = lambda q: pl.BlockSpec((L, 1024), lambda b, c, q=q: (b * nc + c, q))
    par = lambda n: pl.BlockSpec((1, n), lambda b, c: (0, 0))
    pad = lambda v: jnp.pad(v.astype(F32), (0, LANES - v.shape[0])).reshape(1, LANES)
    expand = (jnp.arange(LANES)[:, None] == (jnp.arange(DI) // SSD_HEAD_DIM)[None, :]).astype(F32)
    d_wide = jnp.repeat(d_skip.astype(F32), SSD_HEAD_DIM).reshape(1, DI)
    st_shape = h0.shape[2:]
    st_spec = pl.BlockSpec((None, None) + st_shape, lambda b, c: (layer, b, 0, 0, 0))
    tile_spec = pl.BlockSpec((None, SUBLANES, C), lambda b, c: (b, 0, 0))
    in_specs = [col(0), col(1), col(2), col(3), col(4),
                pl.BlockSpec((L, LANES), lambda b, c: (b * nc + c, 0)),
                tile_spec, st_spec,
                pl.BlockSpec((SSD_CONV, C), lambda b, c: (0, 0)), par(C), par(LANES), par(LANES),
                par(DI), par(DI), pl.BlockSpec((LANES, DI), lambda b, c: (0, 0))]
    args = [zx, zx, zx, zx, zx, dt, conv_tiles, h0, conv_w, conv_b.reshape(1, C), pad(dt_bias), pad(a_log),
            d_wide, norm_w.reshape(1, DI), expand]
    aliases = {}
    if out_buf is not None:
        in_specs.append(pl.BlockSpec(memory_space=pl.ANY))
        args.append(out_buf)
        aliases = {len(args) - 1: 2}
    return pl.pallas_call(
        functools.partial(_ssd_kernel, L=L, n_chunks=nc, natural=natural),
        grid=(B, nc),
        in_specs=in_specs,
        out_specs=[pl.BlockSpec((L, DI), lambda b, c: (b * nc + c, 0)), tile_spec, st_spec],
        out_shape=[jax.ShapeDtypeStruct((M, DI), BF16 if L % 16 == 0 else F32),
                   jax.ShapeDtypeStruct((B, SUBLANES, C), F32), jax.ShapeDtypeStruct(h0.shape, F32)],
        scratch_shapes=[pltpu.VMEM((SUBLANES, C), F32), pltpu.VMEM(st_shape, F32)],
        input_output_aliases=aliases,
        compiler_params=_cparams(2), name="ssd_core",
    )(*args)


def _fox_gate_kernel(f_ref, fb_ref, lf_ref, cum_ref, carry_scr):
    c = pl.program_id(1)
    tc = f_ref.shape[0]

    @pl.when(c == 0)
    def _():
        carry_scr[...] = jnp.zeros_like(carry_scr)
    lf = jax.nn.log_sigmoid(f_ref[...] + fb_ref[...])
    ri = lax.broadcasted_iota(jnp.int32, (tc, tc), 0)
    ci = lax.broadcasted_iota(jnp.int32, (tc, tc), 1)
    cum = _sel_rows(ri >= ci, lf) + carry_scr[...]
    lf_ref[...] = lf
    cum_ref[...] = cum
    carry_scr[...] = cum[tc - 1:tc, :]


def _fox_gate(f_raw, f_b_pad, *, B, T, tc):
    M, W = f_raw.shape
    nc = T // tc
    row = pl.BlockSpec((tc, W), lambda b, c: (b * nc + c, 0))
    return pl.pallas_call(
        _fox_gate_kernel, grid=(B, nc),
        in_specs=[row, pl.BlockSpec((1, W), lambda b, c: (0, 0))],
        out_specs=[row, row],
        out_shape=[jax.ShapeDtypeStruct((M, W), F32)] * 2,
        scratch_shapes=[pltpu.VMEM((1, W), F32)],
        compiler_params=_cparams(2), name="fox_gate",
    )(f_raw, f_b_pad)


def _aug_extras(c_ref, h, tm):
    nt = FOX_BIAS_TERMS
    lane = lax.broadcasted_iota(jnp.int32, (tm, LANES - FOX_HEAD_DIM), 1)
    qe = jnp.where((lane >= nt) & (lane < 2 * nt), 1.0, 0.0)
    ke = jnp.where(lane < nt, 1.0, 0.0)
    for t, pc in enumerate(_split3(c_ref[:, h:h + 1])):
        qe = jnp.where(lane == t, pc.astype(F32), qe)
        ke = jnp.where(lane == nt + t, -pc.astype(F32), ke)
    return qe, ke


def _fox_proj_aug_kernel(x_ref, g_ref, sc_ref, sh_ref, w_ref, wt_ref, c_ref, kt_ref, vt_ref, qa_ref, ka_ref, va_ref,
                         h_scr, *, T):
    j = pl.program_id(1)
    H, Dh = FOX_HEADS, FOX_HEAD_DIM
    tm = x_ref.shape[0]

    @pl.when(j == 0)
    def _():
        h_scr[...] = _modnorm(x_ref[...], g_ref[...], sc_ref[...], sh_ref[...], T).astype(BF16)
    res = jnp.dot(h_scr[...], w_ref[...], preferred_element_type=F32)

    @pl.when(j == 0)
    def _():
        for h in range(H):
            qe, _ = _aug_extras(c_ref, h, tm)
            qa_ref[h] = jnp.concatenate([res[:, h * Dh:(h + 1) * Dh] * (Dh ** -0.5), qe], axis=1).astype(BF16)

    @pl.when(j == 1)
    def _():
        kt_ref[...] = _nt(wt_ref[...], h_scr[...])
        for h in range(H):
            _, ke = _aug_extras(c_ref, h, tm)
            ka_ref[h] = jnp.concatenate([res[:, h * Dh:(h + 1) * Dh], ke], axis=1).astype(BF16)

    @pl.when(j == 2)
    def _():
        vt_ref[...] = _nt(wt_ref[...], h_scr[...])
        ones = (lax.broadcasted_iota(jnp.int32, (tm, LANES - Dh), 1) == 0).astype(F32)
        for h in range(H):
            va_ref[h] = jnp.concatenate([res[:, h * Dh:(h + 1) * Dh], ones], axis=1).astype(BF16)


def _fox_proj_aug(x, g, sc, sh, w_stack, cum, *, B, T, tm):
    M, D = x.shape
    per = T // tm
    H = FOX_HEADS
    row = pl.BlockSpec((tm, D), lambda i, j: (i, 0))
    wsp = pl.BlockSpec((None, D, D), lambda i, j: (j, 0, 0))
    tr = pl.BlockSpec((None, D, tm), lambda i, j: (i // per, 0, i % per))
    aug = pl.BlockSpec((None, H, tm, LANES), lambda i, j: (i // per, 0, i % per, 0))
    return pl.pallas_call(
        functools.partial(_fox_proj_aug_kernel, T=T),
        grid=(M // tm, 3),
        in_specs=[row, pl.BlockSpec((1, D), lambda i, j: (0, 0)), _mod_spec(T, tm, D), _mod_spec(T, tm, D),
                  wsp, wsp, pl.BlockSpec((tm, cum.shape[1]), lambda i, j: (i, 0))],
        out_specs=[tr, tr, aug, aug, aug],
        out_shape=[jax.ShapeDtypeStruct((B, D, T), F32)] * 2 + [jax.ShapeDtypeStruct((B, H, T, LANES), BF16)] * 3,
        scratch_shapes=[pltpu.VMEM((tm, D), BF16)],
        compiler_params=_cparams(2), name="fox_proj_aug",
    )(x, g.reshape(1, D), sc, sh, w_stack, jnp.swapaxes(w_stack, 1, 2), cum)


def _fox_flash_kernel(qa_ref, ka_ref, va_ref, o_ref, m_scr, acc_scr, *, tq, tk, nk):
    qi, ki = pl.program_id(1), pl.program_id(2)
    H, Dh = FOX_HEADS, FOX_HEAD_DIM

    @pl.when(ki == 0)
    def _():
        m_scr[...] = jnp.full_like(m_scr, -jnp.inf)
        acc_scr[...] = jnp.zeros_like(acc_scr)

    def compute(masked):
        if masked:
            rowp = qi * tq + lax.broadcasted_iota(jnp.int32, (tq, tk), 0)
            colp = ki * tk + lax.broadcasted_iota(jnp.int32, (tq, tk), 1)
            keep = colp <= rowp
        for h in range(H):
            s = _nt(qa_ref[h], ka_ref[h])
            if masked:
                s = jnp.where(keep, s, -jnp.inf)
            m_prev = m_scr[h]
            m_new = jnp.maximum(m_prev, jnp.max(s, axis=-1, keepdims=True))
            p = jnp.exp(s - m_new).astype(BF16)
            acc_scr[h] = jnp.exp(m_prev - m_new) * acc_scr[h] + jnp.dot(p, va_ref[h], preferred_element_type=F32)
            m_scr[h] = m_new

    needed = ki * tk <= qi * tq + (tq - 1)
    diag = ki * tk + (tk - 1) > qi * tq

    @pl.when(needed & diag)
    def _():
        compute(True)

    @pl.when(needed & jnp.logical_not(diag))
    def _():
        compute(False)

    @pl.when(ki == nk - 1)
    def _():
        for h in range(H):
            a = acc_scr[h]
            o_ref[:, h * Dh:(h + 1) * Dh] = (a[:, :Dh] / a[:, Dh:Dh + 1]).astype(o_ref.dtype)


def _fox_flash(qa, ka, va, *, B, T, tq, tk):
    H, Dh = FOX_HEADS, FOX_HEAD_DIM
    D = H * Dh
    nq, nk = T // tq, T // tk
    last_k = lambda qi: (qi * tq + tq - 1) // tk
    kblk = lambda b, qi, ki: (b, 0, jnp.minimum(ki, last_k(qi)), 0)
    return pl.pallas_call(
        functools.partial(_fox_flash_kernel, tq=tq, tk=tk, nk=nk),
        grid=(B, nq, nk),
        in_specs=[pl.BlockSpec((None, H, tq, LANES), lambda b, qi, ki: (b, 0, qi, 0)),
                  pl.BlockSpec((None, H, tk, LANES), kblk), pl.BlockSpec((None, H, tk, LANES), kblk)],
        out_specs=pl.BlockSpec((tq, D), lambda b, qi, ki: (b * nq + qi, 0)),
        out_shape=jax.ShapeDtypeStruct((B * T, D), BF16),
        scratch_shapes=[pltpu.VMEM((H, tq, 1), F32), pltpu.VMEM((H, tq, LANES), F32)],
        compiler_params=_cparams(3), name="fox_flash",
    )(qa, ka, va)


def _fox_sample_kernel(pt_ref, q_ref, kn_ref, vn_ref, cn_ref, cnt_ref, *rest, n_steps, pps):
    kt_refs, vt_refs, lft_refs = rest[:pps], rest[pps:2 * pps], rest[2 * pps:3 * pps]
    o_ref, qbd_scr, cq_scr, later_scr, m_scr, l_scr, acc_scr = rest[3 * pps:]
    p = pl.program_id(1)
    H, Dh = FOX_HEADS, FOX_HEAD_DIM
    TS = q_ref.shape[0]
    R = H * TS
    D = H * Dh
    KB = pps * kt_refs[0].shape[-1]

    @pl.when(p == 0)
    def _():
        row = lax.broadcasted_iota(jnp.int32, (R, D), 0)
        lane = lax.broadcasted_iota(jnp.int32, (R, D), 1)
        q_rep = jnp.broadcast_to(q_ref[...][None], (H, TS, D)).reshape(R, D) * (Dh ** -0.5)
        qbd_scr[...] = jnp.where(lane // Dh == row // TS, q_rep, 0.0).astype(BF16)
        cw = cn_ref.shape[1]
        c_rep = jnp.broadcast_to(cn_ref[...][None], (H, TS, cw)).reshape(R, cw)
        r2 = lax.broadcasted_iota(jnp.int32, (R, cw), 0)
        l2 = lax.broadcasted_iota(jnp.int32, (R, cw), 1)
        cq_scr[...] = jnp.sum(jnp.where(l2 == r2 // TS, c_rep, 0.0), axis=-1, keepdims=True)
        later_scr[...] = jnp.zeros_like(later_scr)
        m_scr[...] = jnp.full_like(m_scr, -jnp.inf)
        l_scr[...] = jnp.zeros_like(l_scr)
        acc_scr[...] = jnp.zeros_like(acc_scr)

    def online(s, pv):
        m_prev = m_scr[...]
        m_new = jnp.maximum(m_prev, jnp.max(s, axis=-1, keepdims=True))
        pr = jnp.exp(s - m_new)
        alpha = jnp.exp(m_prev - m_new)
        l_scr[...] = alpha * l_scr[...] + jnp.sum(pr, axis=-1, keepdims=True)
        acc_scr[...] = alpha * acc_scr[...] + pv(pr.astype(BF16))
        m_scr[...] = m_new

    def past_pages():
        lft = jnp.concatenate([r[...] for r in lft_refs], axis=1)
        kt = jnp.concatenate([r[...].reshape(D, -1).astype(BF16) for r in kt_refs], axis=1)
        vt = jnp.concatenate([r[...].reshape(D, -1).astype(BF16) for r in vt_refs], axis=1)
        si = lax.broadcasted_iota(jnp.int32, (KB, KB), 0)
        ki = lax.broadcasted_iota(jnp.int32, (KB, KB), 1)
        suf = _sel_cols(lft, si > ki) + later_scr[...]
        later_scr[...] = later_scr[...] + jnp.sum(lft, axis=-1, keepdims=True)
        bias = jnp.broadcast_to(suf[:, None, :], (H, TS, KB)).reshape(R, KB)
        s = jnp.dot(qbd_scr[...], kt, preferred_element_type=F32) + cq_scr[...] + bias
        online(s, lambda pr: _nt(pr, vt))

    past_pages()

    @pl.when(p == n_steps - 1)
    def _():
        cnt = cnt_ref[...]
        bias = jnp.broadcast_to(cnt[:, None, :], (H, TS, TS)).reshape(R, TS)
        s = _nt(qbd_scr[...], kn_ref[...].astype(BF16)) + cq_scr[...] - bias
        tq = lax.broadcasted_iota(jnp.int32, (R, TS), 0) % TS
        tk = lax.broadcasted_iota(jnp.int32, (R, TS), 1)
        s = jnp.where(tk <= tq, s, -jnp.inf)
        vn = vn_ref[...].astype(BF16)
        online(s, lambda pr: jnp.dot(pr, vn, preferred_element_type=F32))
        row = lax.broadcasted_iota(jnp.int32, (R, D), 0)
        lane = lax.broadcasted_iota(jnp.int32, (R, D), 1)
        own = jnp.where(lane // Dh == row // TS, acc_scr[...] / l_scr[...], 0.0)
        o_ref[...] = jnp.sum(own.reshape(H, TS, D), axis=0)


def _fox_sample(q, k_new, v_new, cum_new, cum_new_t, cache_kt, cache_vt, cache_logf_t, page_table, *, B, TS, pps):
    M, D = q.shape
    n_pages = page_table.shape[1]
    n_steps = n_pages // pps
    _, H, Dh, PG = cache_kt.shape
    R = H * TS
    row = pl.BlockSpec((TS, D), lambda b, p, pt: (b, 0))

    def page(s, nd):
        return lambda b, p, pt: (pt[b * n_pages + n_pages - (p + 1) * pps + s],) + (0,) * nd

    slots = range(pps)
    grid_spec = pltpu.PrefetchScalarGridSpec(
        num_scalar_prefetch=1, grid=(B, n_steps),
        in_specs=[row, row, row,
                  pl.BlockSpec((TS, cum_new.shape[1]), lambda b, p, pt: (b, 0)),
                  pl.BlockSpec((None, H, TS), lambda b, p, pt: (b, 0, 0))]
                 + [pl.BlockSpec((None, H, Dh, PG), page(s, 3)) for s in slots]
                 + [pl.BlockSpec((None, H, Dh, PG), page(s, 3)) for s in slots]
                 + [pl.BlockSpec((None, H, PG), page(s, 2)) for s in slots],
        out_specs=row,
        scratch_shapes=[pltpu.VMEM((R, D), BF16), pltpu.VMEM((R, 1), F32), pltpu.VMEM((H, 1), F32),
                        pltpu.VMEM((R, 1), F32), pltpu.VMEM((R, 1), F32), pltpu.VMEM((R, D), F32)])
    return pl.pallas_call(
        functools.partial(_fox_sample_kernel, n_steps=n_steps, pps=pps),
        grid_spec=grid_spec,
        out_shape=jax.ShapeDtypeStruct((M, D), F32),
        compiler_params=_cparams(2), name="fox_sample",
    )(page_table.reshape(-1), q, k_new, v_new, cum_new, cum_new_t,
      *([cache_kt] * pps), *([cache_vt] * pps), *([cache_logf_t] * pps))


def _rwkv_in_kernel(x_ref, g_ref, sc_ref, sh_ref, st_ref, mu_ref, wrkv_ref, w0_ref, w1_ref, w2_ref, a0_ref,
                    a1_ref, a2_ref, g1_ref, g2_ref,
                    r_ref, k_ref, v_ref, lw_ref, a_ref, gg_ref, tail_ref, h_scr, d_scr, carry_scr, *, T):
    i, j = pl.program_id(0), pl.program_id(1)
    tm, D = x_ref.shape
    nt = tm // SUBLANES
    long_seq = T >= tm

    @pl.when(j == 0)
    def _():
        h = _modnorm(x_ref[...], g_ref[...], sc_ref[...], sh_ref[...], T)
        h3 = h.reshape(nt, SUBLANES, D)
        if long_seq:
            @pl.when(i % (T // tm) == 0)
            def _():
                carry_scr[...] = st_ref[...]
            prev3 = jnp.concatenate([carry_scr[...][None], h3[:nt - 1]], axis=0)
            carry_scr[...] = h3[nt - 1]
            tail_ref[...] = h3[nt - 1]
        else:
            prev3 = st_ref[...]
            tail_ref[...] = h3
        prev = _shift_rows(h3, prev3, 1).reshape(tm, D)
        h_scr[...] = h
        d_scr[...] = prev - h

    def mix(s):
        return (h_scr[...] + d_scr[...] * mu_ref[s:s + 1, :]).astype(BF16)

    for s, out in enumerate((r_ref, k_ref, v_ref)):
        @pl.when(j == s)
        def _(s=s, out=out):
            out[...] = jnp.dot(mix(s), wrkv_ref[...], preferred_element_type=F32)

    @pl.when(j == 3)
    def _():
        t = jnp.tanh(jnp.dot(mix(3), w1_ref[...], preferred_element_type=F32))
        pre_w = w0_ref[...] + jnp.dot(t.astype(BF16), w2_ref[...], preferred_element_type=F32)
        w_log = -jax.nn.softplus(-pre_w) - 0.5
        lw_ref[...] = -jnp.exp(w_log)
        u = jnp.dot(mix(4), a1_ref[...], preferred_element_type=F32)
        pre_a = a0_ref[...] + jnp.dot(u.astype(BF16), a2_ref[...], preferred_element_type=F32)
        a_ref[...] = jax.nn.sigmoid(pre_a)
        sg = jax.nn.sigmoid(jnp.dot(mix(5), g1_ref[...], preferred_element_type=F32))
        gg_ref[...] = jnp.dot(sg.astype(BF16), g2_ref[...], preferred_element_type=F32)


def _rwkv_in(x, g, sc, sh, st_tiles, mu, w_rkv, w0, w1, w2, a0, a1, a2, g1, g2, *, T, tm):
    M, D = x.shape
    B = st_tiles.shape[0]
    P = LANES
    c2 = lambda shape: pl.BlockSpec(shape, lambda i, j: (0,) * len(shape))
    row = pl.BlockSpec((tm, D), lambda i, j: (i, 0))
    return pl.pallas_call(
        functools.partial(_rwkv_in_kernel, T=T),
        grid=(M // tm, 4),
        in_specs=[row, c2((1, D)), _mod_spec(T, tm, D), _mod_spec(T, tm, D), _tile_spec(T, tm, D),
                  c2((6, D)), pl.BlockSpec((None, D, D), lambda i, j: (jnp.minimum(j, 2), 0, 0)),
                  c2((1, D)), c2((D, P)), c2((P, D)), c2((1, D)), c2((D, P)), c2((P, D)), c2((D, P)), c2((P, D))],
        out_specs=[row] * 6 + [_tile_spec(T, tm, D)],
        out_shape=[jax.ShapeDtypeStruct((M, D), F32)] * 6 + [jax.ShapeDtypeStruct((B, SUBLANES, D), F32)],
        scratch_shapes=[pltpu.VMEM((tm, D), F32), pltpu.VMEM((tm, D), F32), pltpu.VMEM((SUBLANES, D), F32)],
        compiler_params=_cparams(2),
        name="rwkv_in",
    )(x, g.reshape(1, D), sc, sh, st_tiles, mu, w_rkv, w0.reshape(1, D), w1, w2, a0.reshape(1, D), a1, a2, g1, g2)


def _wkv_kernel(r_ref, k_ref, v_ref, lw_ref, a_ref, g_ref, kk_ref, ka_ref, rk_ref, lnw_ref, lnb_ref, s0_ref,
                o_ref, sout_ref, s_scr, *, L, n_chunks):
    c = pl.program_id(1)
    H, N = RWKV_HEADS, RWKV_HEAD
    G = WKV_ROWS // L
    R = G * L

    @pl.when(c == 0)
    def _():
        s_scr[...] = s0_ref[...]

    ri = lax.broadcasted_iota(jnp.int32, (L, L), 0)
    ci = lax.broadcasted_iota(jnp.int32, (L, L), 1)
    lw = lw_ref[...]
    cs = _sel_rows(ri >= ci, lw)
    e_pos = jnp.exp(cs)
    e_neg = jnp.exp(-cs)
    e_prev = jnp.exp(cs - lw)

    row = lax.broadcasted_iota(jnp.int32, (R, R), 0)
    col = lax.broadcasted_iota(jnp.int32, (R, R), 1)
    same = (row // L) == (col // L)
    strict = same & (row > col)
    incl = same & (row >= col)

    for grp in range(H // G):
        heads = range(grp * G, (grp + 1) * G)
        A_l, R_l, B_l, K_l, V_l, kp_l = [], [], [], [], [], []
        for h in heads:
            sl = slice(h * N, (h + 1) * N)
            kh, ah = k_ref[:, sl], a_ref[:, sl]
            kkh = kh * kk_ref[:, sl]
            nrm = jnp.sqrt(jnp.sum(kkh * kkh, axis=-1, keepdims=True))
            kkh = kkh / jnp.maximum(nrm, 1e-12)
            kph = kh * (1.0 + (ah - 1.0) * ka_ref[:, sl])
            en = e_neg[:, sl]
            A_l.append(-kkh * e_prev[:, sl])
            R_l.append(r_ref[:, sl] * e_pos[:, sl])
            B_l.append(kkh * ah * en)
            K_l.append(kph * en)
            V_l.append(v_ref[:, sl])
            kp_l.append(kph)
        AR = jnp.concatenate(A_l + R_l, axis=0).astype(BF16)
        BK = jnp.concatenate(B_l + K_l, axis=0).astype(BF16)
        V4 = jnp.concatenate(V_l, axis=0)
        Q = _nt(AR, BK)
        ars_a, ars_r = [], []
        for q, h in enumerate(heads):
            Sb = s_scr[h].astype(BF16)
            ars_a.append(_nt(AR[q * L:(q + 1) * L], Sb))
            ars_r.append(_nt(AR[R + q * L:R + (q + 1) * L], Sb))
        nmat = jnp.where(strict, Q[:R, :R], 0.0)
        ak = jnp.where(strict, Q[:R, R:], 0.0)
        rhs = jnp.concatenate(ars_a, axis=0) + jnp.dot(ak.astype(BF16), V4.astype(BF16), preferred_element_type=F32)
        Y = jnp.where(((row // 2) == (col // 2)) & (row > col), nmat, 0.0)
        b = 2
        while b < L:
            Ck = jnp.where(((row // (2 * b)) == (col // (2 * b))) & ((row % (2 * b)) >= b) & ((col % (2 * b)) < b),
                           nmat, 0.0)
            Yb = Y.astype(BF16)
            Tm = Ck + jnp.dot(Yb, Ck.astype(BF16), preferred_element_type=F32)
            Y = Y + Tm + jnp.dot(Tm.astype(BF16), Yb, preferred_element_type=F32)
            b *= 2
        E = rhs + jnp.dot(Y.astype(BF16), rhs.astype(BF16), preferred_element_type=F32)
        EV = jnp.concatenate([E, V4], axis=0).astype(BF16)
        rbk = jnp.concatenate([jnp.where(incl, Q[R:, :R], 0.0), jnp.where(incl, Q[R:, R:], 0.0)], axis=1)
        y4 = jnp.concatenate(ars_r, axis=0) + jnp.dot(rbk.astype(BF16), EV, preferred_element_type=F32)
        for q, h in enumerate(heads):
            sl = slice(h * N, (h + 1) * N)
            rows = slice(q * L, (q + 1) * L)
            ev_h = jnp.concatenate([EV[rows], EV[R + q * L:R + (q + 1) * L]], axis=0)
            bk_h = jnp.concatenate([BK[rows], BK[R + q * L:R + (q + 1) * L]], axis=0)
            s_scr[h] = (s_scr[h] + _tn(ev_h, bk_h)) * e_pos[L - 1:L, sl]
            y = y4[rows]
            mean = jnp.mean(y, axis=-1, keepdims=True)
            var = jnp.mean(jnp.square(y - mean), axis=-1, keepdims=True)
            yn = (y - mean) * lax.rsqrt(var + RWKV_GN_EPS) * lnw_ref[:, sl] + lnb_ref[:, sl]
            bonus = jnp.sum(r_ref[:, sl] * kp_l[q] * rk_ref[:, sl], axis=-1, keepdims=True) * V_l[q]
            o_ref[:, sl] = (yn + bonus) * g_ref[:, sl]

    @pl.when(c == n_chunks - 1)
    def _():
        sout_ref[...] = s_scr[...]


def _wkv_core(r, k, v, lw, a, g, k_k, k_a, r_k, ln_w, ln_b, s0, *, T, L):
    M, D = r.shape
    B = s0.shape[0]
    nc = T // L
    row = pl.BlockSpec((L, D), lambda b, c: (b * nc + c, 0))
    par = pl.BlockSpec((1, D), lambda b, c: (0, 0))
    st = pl.BlockSpec((None, RWKV_HEADS, RWKV_HEAD, RWKV_HEAD), lambda b, c: (b, 0, 0, 0))
    v1 = lambda t: t.reshape(1, D)
    return pl.pallas_call(
        functools.partial(_wkv_kernel, L=L, n_chunks=nc),
        grid=(B, nc),
        in_specs=[row] * 6 + [par] * 5 + [st],
        out_specs=[row, st],
        out_shape=[jax.ShapeDtypeStruct((M, D), F32), jax.ShapeDtypeStruct(s0.shape, F32)],
        scratch_shapes=[pltpu.VMEM((RWKV_HEADS, RWKV_HEAD, RWKV_HEAD), F32)],
        compiler_params=_cparams(2),
        name="wkv_core",
    )(r, k, v, lw, a, g, v1(k_k), v1(k_a), v1(r_k), v1(ln_w), v1(ln_b), s0)


def _wkv_long_kernel(r_ref, k_ref, v_ref, lw_ref, a_ref, g_ref, kk_ref, ka_ref, rk_ref, lnw_ref, lnb_ref, s0_ref,
                     o_ref, sout_ref, s_scr, *, L, NC, n_steps):
    step = pl.program_id(1)
    H, N, G = RWKV_HEADS, RWKV_HEAD, WKV_GROUP_HEADS
    GW = G * N
    NG = H // G
    R = G * L
    TB = NC * L

    @pl.when(step == 0)
    def _():
        s_scr[...] = jnp.zeros_like(s_scr)
        for h in range(H):
            q = h % G
            s_scr[h // G, q * N:(q + 1) * N, q * N:(q + 1) * N] = s0_ref[h]

    ti = lax.broadcasted_iota(jnp.int32, (TB, TB), 0)
    tj = lax.broadcasted_iota(jnp.int32, (TB, TB), 1)
    same_c = (ti // L) == (tj // L)
    lw = lw_ref[...]
    cs = _sel_rows(same_c & (ti >= tj), lw)
    ctot = jnp.concatenate([jnp.broadcast_to(cs[(c + 1) * L - 1:(c + 1) * L], (L, cs.shape[1])) for c in range(NC)],
                           axis=0)
    e_pos, e_neg, e_prev = jnp.exp(cs), jnp.exp(-cs), jnp.exp(cs - lw)
    e_hat, e_tot = jnp.exp(ctot - cs), jnp.exp(ctot)
    gi = lax.broadcasted_iota(jnp.int32, (GW, GW), 0) // N
    gj = lax.broadcasted_iota(jnp.int32, (GW, GW), 1) // N
    gones = (gi == gj).astype(F32)

    def head_sum(x):
        return jnp.concatenate([_sel_cols(x[:, g * GW:(g + 1) * GW], gones) for g in range(NG)], axis=1)

    r, k, v, a = r_ref[...], k_ref[...], v_ref[...], a_ref[...]
    kkv = k * kk_ref[...]
    kkn = kkv / jnp.maximum(jnp.sqrt(head_sum(kkv * kkv)), 1e-12)
    kp = k * (1.0 + (a - 1.0) * ka_ref[...])
    ba = kkn * a
    nat = dict(A=-kkn * e_prev, R=r * e_pos, Bt=ba * e_neg, Kt=kp * e_neg, Bh=ba * e_hat, Kh=kp * e_hat, V=v)
    bonus = head_sum(r * kp * rk_ref[...]) * v

    row = lax.broadcasted_iota(jnp.int32, (R, GW), 0)
    lane = lax.broadcasted_iota(jnp.int32, (R, GW), 1)
    hm = (row // L) == (lane // N)

    def bd(name, c, g):
        x = nat[name][c * L:(c + 1) * L, g * GW:(g + 1) * GW]
        return jnp.where(hm, jnp.concatenate([x] * G, axis=0), 0.0)

    mr = lax.broadcasted_iota(jnp.int32, (R, R), 0)
    mc = lax.broadcasted_iota(jnp.int32, (R, R), 1)
    same_h = (mr // L) == (mc // L)
    strict = same_h & (mr > mc)
    incl = same_h & (mr >= mc)

    units = [(c, g) for c in range(NC) for g in range(NG)]
    U = {}
    for u in units:
        ops = {n: bd(n, *u) for n in nat}
        AR = jnp.concatenate([ops["A"], ops["R"]], axis=0).astype(BF16)
        BK = jnp.concatenate([ops["Bt"], ops["Kt"]], axis=0).astype(BF16)
        U[u] = dict(ops=ops, Q=_nt(AR, BK))
    for u in units:
        d = U[u]
        Q = d.pop("Q")
        d["nmat"] = jnp.where(strict, Q[:R, :R], 0.0)
        d["ak"] = jnp.where(strict, Q[:R, R:], 0.0).astype(BF16)
        d["rb"] = jnp.where(incl, Q[R:, :R], 0.0).astype(BF16)
        d["rk"] = jnp.where(incl, Q[R:, R:], 0.0).astype(BF16)
        d["Y"] = jnp.where(((mr // 2) == (mc // 2)) & (mr > mc), d["nmat"], 0.0)
    b = 2
    while b < L:
        lvl = ((mr // (2 * b)) == (mc // (2 * b))) & ((mr % (2 * b)) >= b) & ((mc % (2 * b)) < b)
        for u in units:
            d = U[u]
            Ck = jnp.where(lvl, d["nmat"], 0.0)
            d["Yb"] = d["Y"].astype(BF16)
            d["Tm"] = Ck + jnp.dot(d["Yb"], Ck.astype(BF16), preferred_element_type=F32)
        for u in units:
            d = U[u]
            d["Y"] = d["Y"] + d["Tm"] + jnp.dot(d["Tm"].astype(BF16), d["Yb"], preferred_element_type=F32)
        b *= 2
    for u in units:
        d = U[u]
        vb = d["ops"]["V"].astype(BF16)
        d["akv"] = jnp.dot(d["ak"], vb, preferred_element_type=F32)
        d["rkv"] = jnp.dot(d["rk"], vb, preferred_element_type=F32)
    for u in units:
        d = U[u]
        x = jnp.concatenate([d["ops"]["A"], d["akv"]], axis=1)
        d["AE"] = x + jnp.dot(d["Y"].astype(BF16), x.astype(BF16), preferred_element_type=F32)
    for u in units:
        d = U[u]
        base = jnp.concatenate([d["ops"]["R"], d["rkv"]], axis=1)
        ry = base + jnp.dot(d["rb"], d["AE"].astype(BF16), preferred_element_type=F32)
        d["Rbar"] = ry[:, :GW].astype(BF16)
        d["y0"] = ry[:, GW:]
    for u in units:
        d = U[u]
        bh = d["ops"]["Bh"].astype(BF16)
        kh = d["ops"]["Kh"].astype(BF16)
        d["W"] = _tn(d["AE"][:, :GW].astype(BF16), bh).astype(BF16)
        ev = jnp.concatenate([d["AE"][:, GW:], d["ops"]["V"]], axis=0).astype(BF16)
        d["Z"] = _tn(ev, jnp.concatenate([bh, kh], axis=0))

    ys = []
    for c in range(NC):
        yc = []
        for g in range(NG):
            d = U[(c, g)]
            S = s_scr[g]
            Sb = S.astype(BF16)
            ybd = d["y0"] + _nt(d["Rbar"], Sb)
            yc.append(sum(ybd[q * L:(q + 1) * L] for q in range(G)))
            decay = e_tot[c * L:c * L + 1, g * GW:(g + 1) * GW]
            s_scr[g] = S * decay + jnp.dot(Sb, d["W"], preferred_element_type=F32) + d["Z"]
        ys.append(jnp.concatenate(yc, axis=1))
    y = jnp.concatenate(ys, axis=0)

    mean = head_sum(y) * (1.0 / N)
    yc = y - mean
    var = head_sum(yc * yc) * (1.0 / N)
    yn = yc * lax.rsqrt(var + RWKV_GN_EPS) * lnw_ref[...] + lnb_ref[...]
    o_ref[...] = ((yn + bonus) * g_ref[...]).astype(o_ref.dtype)

    @pl.when(step == n_steps - 1)
    def _():
        for h in range(H):
            q = h % G
            sout_ref[h] = s_scr[h // G, q * N:(q + 1) * N, q * N:(q + 1) * N]


def _wkv_long(r, k, v, lw, a, g, k_k, k_a, r_k, ln_w, ln_b, s0, *, T, L, NC):
    M, D = r.shape
    B = s0.shape[0]
    TB = NC * L
    ns = T // TB
    row = pl.BlockSpec((TB, D), lambda b, c: (b * ns + c, 0))
    par = pl.BlockSpec((1, D), lambda b, c: (0, 0))
    st = pl.BlockSpec((None, RWKV_HEADS, RWKV_HEAD, RWKV_HEAD), lambda b, c: (b, 0, 0, 0))
    v1 = lambda t: t.reshape(1, D)
    GW = WKV_GROUP_HEADS * RWKV_HEAD
    return pl.pallas_call(
        functools.partial(_wkv_long_kernel, L=L, NC=NC, n_steps=ns),
        grid=(B, ns),
        in_specs=[row] * 6 + [par] * 5 + [st],
        out_specs=[row, st],
        out_shape=[jax.ShapeDtypeStruct((M, D), BF16), jax.ShapeDtypeStruct(s0.shape, F32)],
        scratch_shapes=[pltpu.VMEM((RWKV_HEADS // WKV_GROUP_HEADS, GW, GW), F32)],
        compiler_params=_cparams(2),
        name="wkv_long",
    )(r, k, v, lw, a, g, v1(k_k), v1(k_a), v1(r_k), v1(ln_w), v1(ln_b), s0)


def _pad_cols(w, n):
    return jnp.pad(w, ((0, 0), (0, n - w.shape[1])))


def _pad_rows(w, n):
    return jnp.pad(w, ((0, n - w.shape[0]), (0, 0)))


def _state_tiles(st):
    return jnp.pad(st, ((0, 0), (SUBLANES - st.shape[1], 0), (0, 0)))


def _ssm_from_groups(s):
    B = s.shape[0]
    s = s.reshape(B, SSD_GROUPS, SSD_STATE, SSD_HPG, SSD_HEAD_DIM)
    return s.transpose(0, 1, 3, 4, 2).reshape(B, SSD_HEADS, SSD_HEAD_DIM, SSD_STATE)


def kernel(x_prompt, x_sample, c_prompt, c_sample, state_ssm, state_ssd_conv, cache_k, cache_v, cache_logf, page_table, state_wkv, state_shift, state_ffn_conv, ada_w, ada_b, norm_mix, norm_ffn, norm_final, ssd_in_w, ssd_conv_w, ssd_conv_b, ssd_dt_bias, ssd_a_log, ssd_d, ssd_norm_w, ssd_out_w, fox_in_w, fox_f_b, fox_out_w, rwkv_mu, rwkv_w_rkv, rwkv_w0, rwkv_w1, rwkv_w2, rwkv_a0, rwkv_a1, rwkv_a2, rwkv_g1, rwkv_g2, rwkv_k_k, rwkv_k_a, rwkv_r_k, rwkv_ln_w, rwkv_ln_b, rwkv_out_w, ffn_up_w, ffn_conv_w, ffn_conv_b, ffn_down_w):
    BP, TP, D = x_prompt.shape
    BS, TS, _ = x_sample.shape
    groups = [dict(B=BP, T=TP, tm=1024, tm_s=512, ssd_l=SSD_CHUNK, wkv_l=RWKV_CHUNK, gate_tc=512),
              dict(B=BS, T=TS, tm=BS * TS, tm_s=512, ssd_l=TS, wkv_l=TS, gate_tc=TS)]
    xs = [x_prompt.reshape(BP * TP, D), x_sample.reshape(BS * TS, D)]

    n_c = BP + BS
    n_c_pad = -(-n_c // SUBLANES) * SUBLANES
    c_all = jnp.pad(jnp.concatenate([c_prompt, c_sample], axis=0), ((0, n_c_pad - n_c), (0, 0)))
    mods_all = _ada_mods(c_all, ada_w, ada_b).reshape(DEPTH, n_c_pad, 6, 1, D)
    row0 = [0, BP]

    def mods_of(i, gi):
        m = mods_all[i, row0[gi]:row0[gi] + groups[gi]["B"]]
        return [m[:, k] for k in range(6)]

    outs = {k: ([], []) for k in ("ssm", "sconv", "k", "v", "lf", "wkv", "shift", "fconv")}
    ssm_sample_all = None

    for i in range(DEPTH):
        kind, j = i % N_MIXERS, i // N_MIXERS
        for gi, gp in enumerate(groups):
            B, T, tm, tm_s = gp["B"], gp["T"], gp["tm"], gp["tm_s"]
            x = xs[gi]
            prompt = gi == 0
            sh_m, sc_m, gt_m, sh_f, sc_f, gt_f = mods_of(i, gi)
            if kind == 0:
                w_main = ssd_in_w[j][:, :SSD_D_INNER + SSD_CONV_CH].astype(BF16)
                w_dt = _pad_cols(ssd_in_w[j][:, SSD_D_INNER + SSD_CONV_CH:], LANES).astype(BF16)
                zx = _norm_mod_matmul(x, norm_mix[i], sc_m, sh_m, w_main, T=T, tm=tm, tn=1024)
                dt = _norm_mod_matmul(x, norm_mix[i], sc_m, sh_m, w_dt, T=T, tm=tm, tn=LANES)
                ssd_par = (ssd_conv_w[j], ssd_conv_b[j], ssd_dt_bias[j], ssd_a_log[j], ssd_d[j], ssd_norm_w[j])
                if prompt:
                    conv_tiles = jnp.zeros((B, SUBLANES, SSD_CONV_CH), F32)
                    h0 = jnp.zeros((1, B, SSD_GROUPS, SSD_STATE, SSD_HPG * SSD_HEAD_DIM), F32)
                    y, ctail, h_new = _ssd_core(zx, dt, conv_tiles, h0, *ssd_par, B=B, T=T, L=gp["ssd_l"],
                                                natural=False)
                    outs["ssm"][gi].append(_ssm_from_groups(h_new[0]))
                else:
                    conv_tiles = _state_tiles(state_ssd_conv[j])
                    h0 = state_ssm.reshape(-1, B, SSD_GROUPS, SSD_HPG * SSD_HEAD_DIM, SSD_STATE)
                    y, ctail, ssm_sample_all = _ssd_core(zx, dt, conv_tiles, h0, *ssd_par, B=B, T=T, L=gp["ssd_l"],
                                                         natural=True, layer=j, out_buf=ssm_sample_all)
                outs["sconv"][gi].append(ctail[:, SUBLANES - (SSD_CONV - 1):])
                x = _res_matmul(y, ssd_out_w[j].astype(BF16), x, gt_m, T=T, tm=tm)
            elif kind == 1:
                w_qkv = fox_in_w[j][:, :3 * FOX_INNER].reshape(D, 3, FOX_INNER).transpose(1, 0, 2).astype(BF16)
                w_f = _pad_cols(fox_in_w[j][:, 3 * FOX_INNER:], LANES).astype(BF16)
                f = _norm_mod_matmul(x, norm_mix[i], sc_m, sh_m, w_f, T=T, tm=tm, tn=LANES)
                f_b = jnp.pad(fox_f_b[j], (0, LANES - FOX_HEADS)).reshape(1, LANES)
                lf, cum = _fox_gate(f, f_b, B=B, T=T, tc=gp["gate_tc"])
                if prompt:
                    kt, vt, qa, ka, va = _fox_proj_aug(x, norm_mix[i], sc_m, sh_m, w_qkv, cum, B=B, T=T, tm=tm_s)
                    o = _fox_flash(qa, ka, va, B=B, T=T, tq=512, tk=512)
                    rows = lambda t: t.reshape(B, FOX_HEADS, FOX_HEAD_DIM, T).transpose(0, 3, 1, 2)
                    k_out, v_out = rows(kt), rows(vt)
                else:
                    q, k, v = _norm_mod_matmul_multi(x, norm_mix[i], sc_m, sh_m, w_qkv, T=T, tm=tm_s)
                    cum_t = jnp.swapaxes(cum[:, :FOX_HEADS].reshape(B, T, FOX_HEADS), 1, 2)
                    o = _fox_sample(q, k, v, cum, cum_t, jnp.transpose(cache_k[j], (0, 2, 3, 1)),
                                    jnp.transpose(cache_v[j], (0, 2, 3, 1)), jnp.swapaxes(cache_logf[j], 1, 2),
                                    page_table, B=B, TS=T, pps=FOX_PAGES_PER_STEP)
                    hd = lambda t: t.reshape(B, T, FOX_HEADS, FOX_HEAD_DIM)
                    k_out, v_out = hd(k), hd(v)
                outs["k"][gi].append(k_out)
                outs["v"][gi].append(v_out)
                outs["lf"][gi].append(lf[:, :FOX_HEADS].reshape(B, T, FOX_HEADS))
                x = _res_matmul(o, fox_out_w[j].astype(BF16), x, gt_m, T=T, tm=tm)
            else:
                if prompt:
                    st_tiles = jnp.zeros((B, SUBLANES, D), F32)
                    s0 = jnp.zeros((B, RWKV_HEADS, RWKV_HEAD, RWKV_HEAD), F32)
                else:
                    st_tiles = _state_tiles(state_shift[j][:, None, :])
                    s0 = state_wkv[j]
                r, k, v, lw, a, g, tails = _rwkv_in(
                    x, norm_mix[i], sc_m, sh_m, st_tiles, rwkv_mu[j], rwkv_w_rkv[j].astype(BF16), rwkv_w0[j],
                    _pad_cols(rwkv_w1[j], LANES).astype(BF16), _pad_rows(rwkv_w2[j], LANES).astype(BF16), rwkv_a0[j],
                    _pad_cols(rwkv_a1[j], LANES).astype(BF16), _pad_rows(rwkv_a2[j], LANES).astype(BF16),
                    rwkv_g1[j].astype(BF16), rwkv_g2[j].astype(BF16), T=T, tm=tm_s)
                wkv_par = (rwkv_k_k[j], rwkv_k_a[j], rwkv_r_k[j].reshape(-1), rwkv_ln_w[j], rwkv_ln_b[j], s0)
                if prompt:
                    o, s_new = _wkv_long(r, k, v, lw, a, g, *wkv_par, T=T, L=gp["wkv_l"], NC=WKV_STEP_CHUNKS)
                else:
                    o, s_new = _wkv_core(r, k, v, lw, a, g, *wkv_par, T=T, L=gp["wkv_l"])
                outs["wkv"][gi].append(s_new)
                outs["shift"][gi].append(tails[:, SUBLANES - 1])
                x = _res_matmul(o, rwkv_out_w[j].astype(BF16), x, gt_m, T=T, tm=tm)
            if prompt:
                ffn_tiles = jnp.zeros((B, SUBLANES, D_FF), F32)
            else:
                ffn_tiles = _state_tiles(state_ffn_conv[i])
            x, tails = _conv_ffn(x, norm_ffn[i], sc_f, sh_f, gt_f, ffn_tiles, ffn_up_w[i].astype(BF16),
                                 ffn_conv_w[i], ffn_conv_b[i], ffn_down_w[i].astype(BF16), T=T, tm=tm_s)
            outs["fconv"][gi].append(tails[:, SUBLANES - (FFN_CONV - 1):])
            xs[gi] = x

    y_prompt = _rmsnorm(xs[0], norm_final, tm=1024).reshape(BP, TP, D)
    y_sample = _rmsnorm(xs[1], norm_final, tm=1024).reshape(BS, TS, D)
    st = lambda name, gi: jnp.stack(outs[name][gi])
    ssm_sample = ssm_sample_all.reshape(state_ssm.shape)
    return (y_prompt, y_sample, st("ssm", 0), ssm_sample, st("sconv", 0), st("sconv", 1),
            st("k", 0), st("k", 1), st("v", 0), st("v", 1), st("lf", 0), st("lf", 1),
            st("wkv", 0), st("wkv", 1), st("shift", 0), st("shift", 1), st("fconv", 0), st("fconv", 1))
```

```python
import functools

import jax
import jax.numpy as jnp
from jax import lax
from jax.experimental import pallas as pl
from jax.experimental.pallas import tpu as pltpu

F32 = jnp.float32
BF16 = jnp.bfloat16

D_MODEL = 1024
DEPTH = 4
N_MIXERS = 3
RMS_EPS = 1e-6
SSD_D_INNER = 2048
SSD_HEAD_DIM = 64
SSD_HEADS = 32
SSD_GROUPS = 4
SSD_HPG = 8
SSD_STATE = 128
SSD_CONV = 4
SSD_CONV_CH = 3072
SSD_CHUNK = 128
FOX_HEAD_DIM = 64
FOX_HEADS = 16
FOX_INNER = 1024
RWKV_HEAD = 64
RWKV_HEADS = 16
RWKV_GN_EPS = 64e-5
RWKV_CHUNK = 32
WKV_ROWS = 128
WKV_GROUP_HEADS = 4
WKV_STEP_CHUNKS = 4
FOX_PAGES_PER_STEP = 8
FOX_BIAS_TERMS = 3
D_FF = 2816
FFN_CONV = 3

SUBLANES = 8
LANES = 128
VMEM_LIMIT = 56 * 1024 * 1024


def _cparams(n_grid):
    return pltpu.CompilerParams(dimension_semantics=("arbitrary",) * n_grid,
                                vmem_limit_bytes=VMEM_LIMIT)


def _nt(a, b):
    return lax.dot_general(a, b, (((1,), (1,)), ((), ())), preferred_element_type=F32)


def _tn(a, b):
    return lax.dot_general(a, b, (((0,), (0,)), ((), ())), preferred_element_type=F32)


def _split3(c):
    hi = c.astype(BF16)
    r1 = c - hi.astype(F32)
    mid = r1.astype(BF16)
    lo = (r1 - mid.astype(F32)).astype(BF16)
    return hi, mid, lo


def _sel_rows(sel, x):
    n = x.shape[1]
    r = jnp.dot(sel.astype(BF16), jnp.concatenate(_split3(x), axis=1), preferred_element_type=F32)
    return r[:, :n] + r[:, n:2 * n] + r[:, 2 * n:]


def _sel_cols(x, sel):
    m = x.shape[0]
    r = jnp.dot(jnp.concatenate(_split3(x), axis=0), sel.astype(BF16), preferred_element_type=F32)
    return r[:m] + r[m:2 * m] + r[2 * m:]


def _modnorm(x, g, sc, sh, T):
    xn = x * lax.rsqrt(jnp.mean(x * x, axis=-1, keepdims=True) + RMS_EPS) * g
    if sc.ndim == 2:
        return xn * (1.0 + sc) + sh
    tm, D = x.shape
    x3 = xn.reshape(tm // T, T, D)
    return (x3 * (1.0 + sc) + sh).reshape(tm, D)


def _gate_mul(y, gt, T):
    if gt.ndim == 2:
        return y * gt
    tm, D = y.shape
    return (y.reshape(tm // T, T, D) * gt).reshape(tm, D)


def _shift_rows(cur3, prev3, k):
    row = lax.broadcasted_iota(jnp.int32, cur3.shape, 1)
    return pltpu.roll(jnp.where(row < SUBLANES - k, cur3, prev3), k, 1)


def _mod_spec(T, tm, D):
    if T >= tm:
        per = T // tm
        return pl.BlockSpec((None, 1, D), lambda i, *_: (i // per, 0, 0))
    return pl.BlockSpec((tm // T, 1, D), lambda i, *_: (i, 0, 0))


def _tile_spec(T, tm, C):
    if T >= tm:
        per = T // tm
        return pl.BlockSpec((None, SUBLANES, C), lambda i, *_: (i // per, 0, 0))
    return pl.BlockSpec((tm // T, SUBLANES, C), lambda i, *_: (i, 0, 0))


def _ada_kernel(c_ref, w_ref, b_ref, o_ref):
    c = c_ref[...]
    a = (c * jax.nn.sigmoid(c)).astype(BF16)
    o_ref[...] = jnp.dot(a, w_ref[...].astype(BF16), preferred_element_type=F32) + b_ref[...]


def _ada_mods(c_all, ada_w, ada_b):
    R, D = c_all.shape
    L, _, N = ada_w.shape
    tn = 1536
    return pl.pallas_call(
        _ada_kernel,
        grid=(L, N // tn),
        in_specs=[pl.BlockSpec((R, D), lambda l, j: (0, 0)),
                  pl.BlockSpec((None, D, tn), lambda l, j: (l, 0, j)),
                  pl.BlockSpec((None, 1, tn), lambda l, j: (l, 0, j))],
        out_specs=pl.BlockSpec((None, R, tn), lambda l, j: (l, 0, j)),
        out_shape=jax.ShapeDtypeStruct((L, R, N), F32),
        compiler_params=_cparams(2),
        name="ada_mods",
    )(c_all, ada_w, ada_b.reshape(L, 1, N))


def _nmm_kernel(x_ref, g_ref, sc_ref, sh_ref, w_ref, o_ref, h_scr, *, T):
    @pl.when(pl.program_id(1) == 0)
    def _():
        h_scr[...] = _modnorm(x_ref[...], g_ref[...], sc_ref[...], sh_ref[...], T).astype(BF16)
    o_ref[...] = jnp.dot(h_scr[...], w_ref[...], preferred_element_type=F32).astype(o_ref.dtype)


def _norm_mod_matmul(x, g, sc, sh, w, *, T, tm, tn):
    M, D = x.shape
    N = w.shape[1]
    return pl.pallas_call(
        functools.partial(_nmm_kernel, T=T),
        grid=(M // tm, N // tn),
        in_specs=[pl.BlockSpec((tm, D), lambda i, j: (i, 0)),
                  pl.BlockSpec((1, D), lambda i, j: (0, 0)),
                  _mod_spec(T, tm, D), _mod_spec(T, tm, D),
                  pl.BlockSpec((D, tn), lambda i, j: (0, j))],
        out_specs=pl.BlockSpec((tm, tn), lambda i, j: (i, j)),
        out_shape=jax.ShapeDtypeStruct((M, N), F32),
        scratch_shapes=[pltpu.VMEM((tm, D), BF16)],
        compiler_params=_cparams(2),
        name="norm_mod_matmul",
    )(x, g.reshape(1, D), sc, sh, w)


def _nmm_multi_kernel(x_ref, g_ref, sc_ref, sh_ref, w_ref, *rest, T, n_out):
    outs, h_scr = rest[:n_out], rest[n_out]
    j = pl.program_id(1)

    @pl.when(j == 0)
    def _():
        h_scr[...] = _modnorm(x_ref[...], g_ref[...], sc_ref[...], sh_ref[...], T).astype(BF16)
    res = jnp.dot(h_scr[...], w_ref[...], preferred_element_type=F32)
    for o in range(n_out):
        @pl.when(j == o)
        def _(o=o):
            outs[o][...] = res


def _norm_mod_matmul_multi(x, g, sc, sh, w_stack, *, T, tm):
    M, D = x.shape
    n_out, _, N = w_stack.shape
    return pl.pallas_call(
        functools.partial(_nmm_multi_kernel, T=T, n_out=n_out),
        grid=(M // tm, n_out),
        in_specs=[pl.BlockSpec((tm, D), lambda i, j: (i, 0)),
                  pl.BlockSpec((1, D), lambda i, j: (0, 0)),
                  _mod_spec(T, tm, D), _mod_spec(T, tm, D),
                  pl.BlockSpec((None, D, N), lambda i, j: (j, 0, 0))],
        out_specs=[pl.BlockSpec((tm, N), lambda i, j: (i, 0)) for _ in range(n_out)],
        out_shape=[jax.ShapeDtypeStruct((M, N), F32) for _ in range(n_out)],
        scratch_shapes=[pltpu.VMEM((tm, D), BF16)],
        compiler_params=_cparams(2),
        name="norm_mod_matmul_multi",
    )(x, g.reshape(1, D), sc, sh, w_stack)


def _res_mm_kernel(a_ref, w_ref, x_ref, gt_ref, o_ref, *, T):
    y = jnp.dot(a_ref[...].astype(BF16), w_ref[...], preferred_element_type=F32)
    o_ref[...] = x_ref[...] + _gate_mul(y, gt_ref[...], T)


def _res_matmul(a, w, x, gt, *, T, tm):
    M, K = a.shape
    D = w.shape[1]
    return pl.pallas_call(
        functools.partial(_res_mm_kernel, T=T),
        grid=(M // tm,),
        in_specs=[pl.BlockSpec((tm, K), lambda i: (i, 0)),
                  pl.BlockSpec((K, D), lambda i: (0, 0)),
                  pl.BlockSpec((tm, D), lambda i: (i, 0)),
                  _mod_spec(T, tm, D)],
        out_specs=pl.BlockSpec((tm, D), lambda i: (i, 0)),
        out_shape=jax.ShapeDtypeStruct((M, D), F32),
        compiler_params=_cparams(1),
        name="res_matmul",
    )(a, w, x, gt)


def _ffn_kernel(x_ref, g_ref, sc_ref, sh_ref, gt_ref, st_ref, upw_ref, cw_ref, cb_ref, dnw_ref,
                o_ref, tail_ref, act_scr, carry_scr, *, T, F, fc):
    i = pl.program_id(0)
    tm, D = x_ref.shape
    nt = tm // SUBLANES
    long_seq = T >= tm
    x = x_ref[...]
    h = _modnorm(x, g_ref[...], sc_ref[...], sh_ref[...], T).astype(BF16)
    if long_seq:
        @pl.when(i % (T // tm) == 0)
        def _():
            carry_scr[...] = st_ref[...]
    for c in range(F // fc):
        lo, hi = c * fc, (c + 1) * fc
        gc = jnp.dot(h, upw_ref[:, lo:hi], preferred_element_type=F32)
        uc = jnp.dot(h, upw_ref[:, F + lo:F + hi], preferred_element_type=F32)
        g3 = gc.reshape(nt, SUBLANES, fc)
        if long_seq:
            prev3 = jnp.concatenate([carry_scr[:, lo:hi][None], g3[:nt - 1]], axis=0)
            carry_scr[:, lo:hi] = g3[nt - 1]
            tail_ref[:, lo:hi] = g3[nt - 1]
        else:
            prev3 = st_ref[:, :, lo:hi]
            tail_ref[:, :, lo:hi] = g3
        conv = (g3 * cw_ref[2:3, lo:hi] + _shift_rows(g3, prev3, 1) * cw_ref[1:2, lo:hi]
                + _shift_rows(g3, prev3, 2) * cw_ref[0:1, lo:hi] + cb_ref[:, lo:hi])
        act = conv * jax.nn.sigmoid(conv) * uc.reshape(nt, SUBLANES, fc)
        act_scr[:, lo:hi] = act.reshape(tm, fc).astype(BF16)
    y = jnp.dot(act_scr[...], dnw_ref[...], preferred_element_type=F32)
    o_ref[...] = x + _gate_mul(y, gt_ref[...], T)


def _conv_ffn(x, g, sc, sh, gt, st_tiles, up_w, conv_w, conv_b, down_w, *, T, tm):
    M, D = x.shape
    F = down_w.shape[0]
    B = st_tiles.shape[0]
    return pl.pallas_call(
        functools.partial(_ffn_kernel, T=T, F=F, fc=256),
        grid=(M // tm,),
        in_specs=[pl.BlockSpec((tm, D), lambda i: (i, 0)),
                  pl.BlockSpec((1, D), lambda i: (0, 0)),
                  _mod_spec(T, tm, D), _mod_spec(T, tm, D), _mod_spec(T, tm, D),
                  _tile_spec(T, tm, F),
                  pl.BlockSpec((D, 2 * F), lambda i: (0, 0)),
                  pl.BlockSpec((FFN_CONV, F), lambda i: (0, 0)),
                  pl.BlockSpec((1, F), lambda i: (0, 0)),
                  pl.BlockSpec((F, D), lambda i: (0, 0))],
        out_specs=[pl.BlockSpec((tm, D), lambda i: (i, 0)), _tile_spec(T, tm, F)],
        out_shape=[jax.ShapeDtypeStruct((M, D), F32), jax.ShapeDtypeStruct((B, SUBLANES, F), F32)],
        scratch_shapes=[pltpu.VMEM((tm, F), BF16), pltpu.VMEM((SUBLANES, F), F32)],
        compiler_params=_cparams(1),
        name="conv_ffn",
    )(x, g.reshape(1, D), sc, sh, gt, st_tiles, up_w, conv_w, conv_b.reshape(1, F), down_w)


def _rms_kernel(x_ref, g_ref, o_ref):
    x = x_ref[...]
    o_ref[...] = x * lax.rsqrt(jnp.mean(x * x, axis=-1, keepdims=True) + RMS_EPS) * g_ref[...]


def _rmsnorm(x, g, *, tm):
    M, D = x.shape
    return pl.pallas_call(
        _rms_kernel, grid=(M // tm,),
        in_specs=[pl.BlockSpec((tm, D), lambda i: (i, 0)), pl.BlockSpec((1, D), lambda i: (0, 0))],
        out_specs=pl.BlockSpec((tm, D), lambda i: (i, 0)),
        out_shape=jax.ShapeDtypeStruct((M, D), F32),
        compiler_params=_cparams(1), name="final_rmsnorm",
    )(x, g.reshape(1, D))


def _conv_silu(cur, prev_tile, w_ref, b_ref, lo, hi):
    L, C = cur.shape
    nt = L // SUBLANES
    c3 = cur.reshape(nt, SUBLANES, C)
    p3 = prev_tile[None] if nt == 1 else jnp.concatenate([prev_tile[None], c3[:nt - 1]], axis=0)
    out = c3 * w_ref[3:4, lo:hi] + b_ref[:, lo:hi]
    for k in range(1, SSD_CONV):
        out = out + _shift_rows(c3, p3, k) * w_ref[3 - k:4 - k, lo:hi]
    out = out * jax.nn.sigmoid(out)
    return out.reshape(L, C)


def _ssd_kernel(z0_ref, z1_ref, x0_ref, x1_ref, bc_ref, dt_ref, cst_ref, h0_ref, cw_ref, cb_ref, dtb_ref, alog_ref,
                dw_ref, nw_ref, ex_ref, *rest, L, n_chunks, natural):
    y_ref, ctail_ref, hout_ref, tail_scr, st_scr = rest[-5:]
    c = pl.program_id(1)
    P, Nn, G, J = SSD_HEAD_DIM, SSD_STATE, SSD_GROUPS, SSD_HPG
    GW = J * P

    @pl.when(c == 0)
    def _():
        tail_scr[...] = cst_ref[...]
        st_scr[...] = h0_ref[...]

    conv = []
    for q, blk_ref in enumerate((x0_ref, x1_ref, bc_ref)):
        blk = blk_ref[...]
        lo, hi = q * 1024, (q + 1) * 1024
        conv.append(_conv_silu(blk, tail_scr[:, lo:hi], cw_ref, cb_ref, lo, hi))
        tail_scr[:, lo:hi] = blk[L - SUBLANES:, :]
        ctail_ref[:, lo:hi] = blk[L - SUBLANES:, :]
    xs = jnp.concatenate(conv[:2], axis=1)
    Bm = conv[2][:, :G * Nn]
    Cm = conv[2][:, G * Nn:]

    dt = jax.nn.softplus(dt_ref[...] + dtb_ref[...])
    a = dt * (-jnp.exp(alog_ref[...]))
    ri = lax.broadcasted_iota(jnp.int32, (L, L), 0)
    ci = lax.broadcasted_iota(jnp.int32, (L, L), 1)
    causal = ri >= ci
    acs = _sel_rows(causal, a)
    wide = _sel_cols(jnp.concatenate([dt, acs], axis=0), ex_ref[...])
    dt_w, acs_w = wide[:L], wide[L:]
    acs_t = acs.T
    last_w = acs_w[L - 1:L, :]
    X = xs * dt_w
    Xd = X * jnp.exp(last_w - acs_w)
    dstart = jnp.exp(acs_w)
    cdec = jnp.exp(last_w)
    zz = jnp.concatenate([z0_ref[...], z1_ref[...]], axis=1)
    zg = zz * jax.nn.sigmoid(zz)
    dwide = dw_ref[...]
    nw = nw_ref[...]
    Xb = X.astype(BF16)
    for g in range(G):
        gl = slice(g * GW, (g + 1) * GW)
        Bg = Bm[:, g * Nn:(g + 1) * Nn].astype(BF16)
        Cg = Cm[:, g * Nn:(g + 1) * Nn].astype(BF16)
        CB = _nt(Cg, Bg)
        st = st_scr[g]
        if natural:
            y_off = _nt(Cg, st.astype(BF16)) * dstart[:, gl]
            dec_h = jnp.exp(acs_t[g * J:(g + 1) * J, L - 1:L])
            dcol = jnp.broadcast_to(dec_h[:, None, :], (J, P, 1)).reshape(GW, 1)
            st_scr[g] = st * dcol + _tn(Xd[:, gl].astype(BF16), Bg)
        else:
            y_off = jnp.dot(Cg, st.astype(BF16), preferred_element_type=F32) * dstart[:, gl]
            st_scr[g] = st * cdec[:, gl] + _tn(Bg, Xd[:, gl].astype(BF16))
        ys = []
        for j in range(J):
            h = g * J + j
            diff = acs[:, h:h + 1] - acs_t[h:h + 1, :]
            dec = jnp.exp(jnp.where(causal, diff, -jnp.inf))
            Mh = (CB * dec).astype(BF16)
            ys.append(jnp.dot(Mh, Xb[:, h * P:(h + 1) * P], preferred_element_type=F32))
        yg = jnp.concatenate(ys, axis=1) + y_off + dwide[:, gl] * xs[:, gl]
        yg = yg * zg[:, gl]
        yg = yg * lax.rsqrt(jnp.mean(yg * yg, axis=-1, keepdims=True) + RMS_EPS) * nw[:, gl]
        y_ref[:, gl] = yg.astype(y_ref.dtype)

    @pl.when(c == n_chunks - 1)
    def _():
        hout_ref[...] = st_scr[...]


def _ssd_core(zx, dt, conv_tiles, h0, conv_w, conv_b, dt_bias, a_log, d_skip, norm_w, *, B, T, L, natural,
              layer=0, out_buf=None):
    M = zx.shape[0]
    nc = T // L
    DI, C = SSD_D_INNER, SSD_CONV_CH
    col = lambda q: pl.BlockSpec((L, 1024), lambda b, c, q=q: (b * nc + c, q))
    par = lambda n: pl.BlockSpec((1, n), lambda b, c: (0, 0))
    pad = lambda v: jnp.pad(v.astype(F32), (0, LANES - v.shape[0])).reshape(1, LANES)
    expand = (jnp.arange(LANES)[:, None] == (jnp.arange(DI) // SSD_HEAD_DIM)[None, :]).astype(F32)
    d_wide = jnp.repeat(d_skip.astype(F32), SSD_HEAD_DIM).reshape(1, DI)
    st_shape = h0.shape[2:]
    st_spec = pl.BlockSpec((None, None) + st_shape, lambda b, c: (layer, b, 0, 0, 0))
    tile_spec = pl.BlockSpec((None, SUBLANES, C), lambda b, c: (b, 0, 0))
    in_specs = [col(0), col(1), col(2), col(3), col(4),
                pl.BlockSpec((L, LANES), lambda b, c: (b * nc + c, 0)),
                tile_spec, st_spec,
                pl.BlockSpec((SSD_CONV, C), lambda b, c: (0, 0)), par(C), par(LANES), par(LANES),
                par(DI), par(DI), pl.BlockSpec((LANES, DI), lambda b, c: (0, 0))]
    args = [zx, zx, zx, zx, zx, dt, conv_tiles, h0, conv_w, conv_b.reshape(1, C), pad(dt_bias), pad(a_log),
            d_wide, norm_w.reshape(1, DI), expand]
    aliases = {}
    if out_buf is not None:
        in_specs.append(pl.BlockSpec(memory_space=pl.ANY))
        args.append(out_buf)
        aliases = {len(args) - 1: 2}
    return pl.pallas_call(
        functools.partial(_ssd_kernel, L=L, n_chunks=nc, natural=natural),
        grid=(B, nc),
        in_specs=in_specs,
        out_specs=[pl.BlockSpec((L, DI), lambda b, c: (b * nc + c, 0)), tile_spec, st_spec],
        out_shape=[jax.ShapeDtypeStruct((M, DI), BF16 if L % 16 == 0 else F32),
                   jax.ShapeDtypeStruct((B, SUBLANES, C), F32), jax.ShapeDtypeStruct(h0.shape, F32)],
        scratch_shapes=[pltpu.VMEM((SUBLANES, C), F32), pltpu.VMEM(st_shape, F32)],
        input_output_aliases=aliases,
        compiler_params=_cparams(2), name="ssd_core",
    )(*args)


def _fox_gate_kernel(f_ref, fb_ref, lf_ref, cum_ref, carry_scr):
    c = pl.program_id(1)
    tc = f_ref.shape[0]

    @pl.when(c == 0)
    def _():
        carry_scr[...] = jnp.zeros_like(carry_scr)
    lf = jax.nn.log_sigmoid(f_ref[...] + fb_ref[...])
    ri = lax.broadcasted_iota(jnp.int32, (tc, tc), 0)
    ci = lax.broadcasted_iota(jnp.int32, (tc, tc), 1)
    cum = _sel_rows(ri >= ci, lf) + carry_scr[...]
    lf_ref[...] = lf
    cum_ref[...] = cum
    carry_scr[...] = cum[tc - 1:tc, :]


def _fox_gate(f_raw, f_b_pad, *, B, T, tc):
    M, W = f_raw.shape
    nc = T // tc
    row = pl.BlockSpec((tc, W), lambda b, c: (b * nc + c, 0))
    return pl.pallas_call(
        _fox_gate_kernel, grid=(B, nc),
        in_specs=[row, pl.BlockSpec((1, W), lambda b, c: (0, 0))],
        out_specs=[row, row],
        out_shape=[jax.ShapeDtypeStruct((M, W), F32)] * 2,
        scratch_shapes=[pltpu.VMEM((1, W), F32)],
        compiler_params=_cparams(2), name="fox_gate",
    )(f_raw, f_b_pad)


def _aug_tails(c_ref, place_ref, const_ref, which):
    pieces = _split3(c_ref[...])
    out = const_ref[which:which + 1, :]
    for t, pc in enumerate(pieces):
        out = out + jnp.dot(pc, place_ref[which, t], preferred_element_type=F32)
    return out


def _aug_placement():
    nt, Dh = FOX_BIAS_TERMS, FOX_HEAD_DIM
    src = jnp.arange(LANES)[:, None]
    lane = jnp.arange(FOX_HEADS * Dh)[None, :]
    own = (lane // Dh == src) & (src < FOX_HEADS)
    col = lane % Dh
    place_q = jnp.stack([(own & (col == t)).astype(F32) for t in range(nt)])
    place_k = jnp.stack([-(own & (col == nt + t)).astype(F32) for t in range(nt)])
    const = jnp.concatenate([((col >= nt) & (col < 2 * nt)).astype(F32), (col < nt).astype(F32)], axis=0)
    return jnp.stack([place_q, place_k]).astype(BF16), const


def _fox_proj_aug_kernel(x_ref, g_ref, sc_ref, sh_ref, w_ref, wt_ref, c_ref, place_ref, const_ref,
                         kt_ref, vt_ref, qa_ref, ka_ref, va_ref, h_scr, *, T):
    j = pl.program_id(1)
    H, Dh = FOX_HEADS, FOX_HEAD_DIM
    tm = x_ref.shape[0]

    @pl.when(j == 0)
    def _():
        h_scr[...] = _modnorm(x_ref[...], g_ref[...], sc_ref[...], sh_ref[...], T).astype(BF16)
    res = jnp.dot(h_scr[...], w_ref[...], preferred_element_type=F32)

    @pl.when(j == 0)
    def _():
        tails = _aug_tails(c_ref, place_ref, const_ref, 0)
        for h in range(H):
            sl = slice(h * Dh, (h + 1) * Dh)
            qa_ref[h] = jnp.concatenate([res[:, sl] * (Dh ** -0.5), tails[:, sl]], axis=1).astype(BF16)

    @pl.when(j == 1)
    def _():
        kt_ref[...] = _nt(wt_ref[...], h_scr[...])
        tails = _aug_tails(c_ref, place_ref, const_ref, 1)
        for h in range(H):
            sl = slice(h * Dh, (h + 1) * Dh)
            ka_ref[h] = jnp.concatenate([res[:, sl], tails[:, sl]], axis=1).astype(BF16)

    @pl.when(j == 2)
    def _():
        vt_ref[...] = _nt(wt_ref[...], h_scr[...])
        ones = (lax.broadcasted_iota(jnp.int32, (tm, LANES - Dh), 1) == 0).astype(F32)
        for h in range(H):
            va_ref[h] = jnp.concatenate([res[:, h * Dh:(h + 1) * Dh], ones], axis=1).astype(BF16)


def _fox_proj_aug(x, g, sc, sh, w_stack, cum, *, B, T, tm):
    M, D = x.shape
    per = T // tm
    H = FOX_HEADS
    row = pl.BlockSpec((tm, D), lambda i, j: (i, 0))
    wsp = pl.BlockSpec((None, D, D), lambda i, j: (j, 0, 0))
    tr = pl.BlockSpec((None, D, tm), lambda i, j: (i // per, 0, i % per))
    aug = pl.BlockSpec((None, H, tm, LANES), lambda i, j: (i // per, 0, i % per, 0))
    place, const = _aug_placement()
    return pl.pallas_call(
        functools.partial(_fox_proj_aug_kernel, T=T),
        grid=(M // tm, 3),
        in_specs=[row, pl.BlockSpec((1, D), lambda i, j: (0, 0)), _mod_spec(T, tm, D), _mod_spec(T, tm, D),
                  wsp, wsp, pl.BlockSpec((tm, cum.shape[1]), lambda i, j: (i, 0)),
                  pl.BlockSpec(place.shape, lambda i, j: (0, 0, 0, 0)), pl.BlockSpec(const.shape, lambda i, j: (0, 0))],
        out_specs=[tr, tr, aug, aug, aug],
        out_shape=[jax.ShapeDtypeStruct((B, D, T), F32)] * 2 + [jax.ShapeDtypeStruct((B, H, T, LANES), BF16)] * 3,
        scratch_shapes=[pltpu.VMEM((tm, D), BF16)],
        compiler_params=_cparams(2), name="fox_proj_aug",
    )(x, g.reshape(1, D), sc, sh, w_stack, jnp.swapaxes(w_stack, 1, 2), cum, place, const)


def _fox_flash_kernel(qa_ref, ka_ref, va_ref, o_ref, m_scr, acc_scr, *, tq, tk, nk):
    qi, ki = pl.program_id(1), pl.program_id(2)
    H, Dh = FOX_HEADS, FOX_HEAD_DIM

    @pl.when(ki == 0)
    def _():
        m_scr[...] = jnp.full_like(m_scr, -jnp.inf)
        acc_scr[...] = jnp.zeros_like(acc_scr)

    def compute(masked):
        if masked:
            rowp = qi * tq + lax.broadcasted_iota(jnp.int32, (tq, tk), 0)
            colp = ki * tk + lax.broadcasted_iota(jnp.int32, (tq, tk), 1)
            keep = colp <= rowp
        for h in range(H):
            s = _nt(qa_ref[h], ka_ref[h])
            if masked:
                s = jnp.where(keep, s, -jnp.inf)
            m_prev = m_scr[h]
            m_new = jnp.maximum(m_prev, jnp.max(s, axis=-1, keepdims=True))
            p = jnp.exp(s - m_new).astype(BF16)
            acc_scr[h] = jnp.exp(m_prev - m_new) * acc_scr[h] + jnp.dot(p, va_ref[h], preferred_element_type=F32)
            m_scr[h] = m_new

    needed = ki * tk <= qi * tq + (tq - 1)
    diag = ki * tk + (tk - 1) > qi * tq

    @pl.when(needed & diag)
    def _():
        compute(True)

    @pl.when(needed & jnp.logical_not(diag))
    def _():
        compute(False)

    @pl.when(ki == nk - 1)
    def _():
        for h in range(H):
            a = acc_scr[h]
            o_ref[:, h * Dh:(h + 1) * Dh] = (a[:, :Dh] / a[:, Dh:Dh + 1]).astype(o_ref.dtype)


def _fox_flash(qa, ka, va, *, B, T, tq, tk):
    H, Dh = FOX_HEADS, FOX_HEAD_DIM
    D = H * Dh
    nq, nk = T // tq, T // tk
    last_k = lambda qi: (qi * tq + tq - 1) // tk
    kblk = lambda b, qi, ki: (b, 0, jnp.minimum(ki, last_k(qi)), 0)
    return pl.pallas_call(
        functools.partial(_fox_flash_kernel, tq=tq, tk=tk, nk=nk),
        grid=(B, nq, nk),
        in_specs=[pl.BlockSpec((None, H, tq, LANES), lambda b, qi, ki: (b, 0, qi, 0)),
                  pl.BlockSpec((None, H, tk, LANES), kblk), pl.BlockSpec((None, H, tk, LANES), kblk)],
        out_specs=pl.BlockSpec((tq, D), lambda b, qi, ki: (b * nq + qi, 0)),
        out_shape=jax.ShapeDtypeStruct((B * T, D), BF16),
        scratch_shapes=[pltpu.VMEM((H, tq, 1), F32), pltpu.VMEM((H, tq, LANES), F32)],
        compiler_params=_cparams(3), name="fox_flash",
    )(qa, ka, va)


def _fox_sample_kernel(pt_ref, q_ref, kn_ref, vn_ref, cn_ref, cnt_ref, *rest, n_steps, pps):
    kt_refs, vt_refs, lft_refs = rest[:pps], rest[pps:2 * pps], rest[2 * pps:3 * pps]
    o_ref, qbd_scr, cq_scr, later_scr, m_scr, l_scr, acc_scr = rest[3 * pps:]
    p = pl.program_id(1)
    H, Dh = FOX_HEADS, FOX_HEAD_DIM
    TS = q_ref.shape[0]
    R = H * TS
    D = H * Dh
    KB = pps * kt_refs[0].shape[-1]

    @pl.when(p == 0)
    def _():
        row = lax.broadcasted_iota(jnp.int32, (R, D), 0)
        lane = lax.broadcasted_iota(jnp.int32, (R, D), 1)
        q_rep = jnp.broadcast_to(q_ref[...][None], (H, TS, D)).reshape(R, D) * (Dh ** -0.5)
        qbd_scr[...] = jnp.where(lane // Dh == row // TS, q_rep, 0.0).astype(BF16)
        cw = cn_ref.shape[1]
        c_rep = jnp.broadcast_to(cn_ref[...][None], (H, TS, cw)).reshape(R, cw)
        r2 = lax.broadcasted_iota(jnp.int32, (R, cw), 0)
        l2 = lax.broadcasted_iota(jnp.int32, (R, cw), 1)
        cq_scr[...] = jnp.sum(jnp.where(l2 == r2 // TS, c_rep, 0.0), axis=-1, keepdims=True)
        later_scr[...] = jnp.zeros_like(later_scr)
        m_scr[...] = jnp.full_like(m_scr, -jnp.inf)
        l_scr[...] = jnp.zeros_like(l_scr)
        acc_scr[...] = jnp.zeros_like(acc_scr)

    def online(s, pv):
        m_prev = m_scr[...]
        m_new = jnp.maximum(m_prev, jnp.max(s, axis=-1, keepdims=True))
        pr = jnp.exp(s - m_new)
        alpha = jnp.exp(m_prev - m_new)
        l_scr[...] = alpha * l_scr[...] + jnp.sum(pr, axis=-1, keepdims=True)
        acc_scr[...] = alpha * acc_scr[...] + pv(pr.astype(BF16))
        m_scr[...] = m_new

    def past_pages():
        lft = jnp.concatenate([r[...] for r in lft_refs], axis=1)
        kt = jnp.concatenate([r[...].reshape(D, -1).astype(BF16) for r in kt_refs], axis=1)
        vt = jnp.concatenate([r[...].reshape(D, -1).astype(BF16) for r in vt_refs], axis=1)
        si = lax.broadcasted_iota(jnp.int32, (KB, KB), 0)
        ki = lax.broadcasted_iota(jnp.int32, (KB, KB), 1)
        suf = _sel_cols(lft, si > ki) + later_scr[...]
        later_scr[...] = later_scr[...] + jnp.sum(lft, axis=-1, keepdims=True)
        bias = jnp.broadcast_to(suf[:, None, :], (H, TS, KB)).reshape(R, KB)
        s = jnp.dot(qbd_scr[...], kt, preferred_element_type=F32) + cq_scr[...] + bias
        online(s, lambda pr: _nt(pr, vt))

    past_pages()

    @pl.when(p == n_steps - 1)
    def _():
        cnt = cnt_ref[...]
        bias = jnp.broadcast_to(cnt[:, None, :], (H, TS, TS)).reshape(R, TS)
        s = _nt(qbd_scr[...], kn_ref[...].astype(BF16)) + cq_scr[...] - bias
        tq = lax.broadcasted_iota(jnp.int32, (R, TS), 0) % TS
        tk = lax.broadcasted_iota(jnp.int32, (R, TS), 1)
        s = jnp.where(tk <= tq, s, -jnp.inf)
        vn = vn_ref[...].astype(BF16)
        online(s, lambda pr: jnp.dot(pr, vn, preferred_element_type=F32))
        row = lax.broadcasted_iota(jnp.int32, (R, D), 0)
        lane = lax.broadcasted_iota(jnp.int32, (R, D), 1)
        own = jnp.where(lane // Dh == row // TS, acc_scr[...] / l_scr[...], 0.0)
        o_ref[...] = jnp.sum(own.reshape(H, TS, D), axis=0)


def _fox_sample(q, k_new, v_new, cum_new, cum_new_t, cache_kt, cache_vt, cache_logf_t, page_table, *, B, TS, pps):
    M, D = q.shape
    n_pages = page_table.shape[1]
    n_steps = n_pages // pps
    _, H, Dh, PG = cache_kt.shape
    R = H * TS
    row = pl.BlockSpec((TS, D), lambda b, p, pt: (b, 0))

    def page(s, nd):
        return lambda b, p, pt: (pt[b * n_pages + n_pages - (p + 1) * pps + s],) + (0,) * nd

    slots = range(pps)
    grid_spec = pltpu.PrefetchScalarGridSpec(
        num_scalar_prefetch=1, grid=(B, n_steps),
        in_specs=[row, row, row,
                  pl.BlockSpec((TS, cum_new.shape[1]), lambda b, p, pt: (b, 0)),
                  pl.BlockSpec((None, H, TS), lambda b, p, pt: (b, 0, 0))]
                 + [pl.BlockSpec((None, H, Dh, PG), page(s, 3)) for s in slots]
                 + [pl.BlockSpec((None, H, Dh, PG), page(s, 3)) for s in slots]
                 + [pl.BlockSpec((None, H, PG), page(s, 2)) for s in slots],
        out_specs=row,
        scratch_shapes=[pltpu.VMEM((R, D), BF16), pltpu.VMEM((R, 1), F32), pltpu.VMEM((H, 1), F32),
                        pltpu.VMEM((R, 1), F32), pltpu.VMEM((R, 1), F32), pltpu.VMEM((R, D), F32)])
    return pl.pallas_call(
        functools.partial(_fox_sample_kernel, n_steps=n_steps, pps=pps),
        grid_spec=grid_spec,
        out_shape=jax.ShapeDtypeStruct((M, D), F32),
        compiler_params=_cparams(2), name="fox_sample",
    )(page_table.reshape(-1), q, k_new, v_new, cum_new, cum_new_t,
      *([cache_kt] * pps), *([cache_vt] * pps), *([cache_logf_t] * pps))


def _rwkv_in_kernel(x_ref, g_ref, sc_ref, sh_ref, st_ref, mu_ref, wrkv_ref, w0_ref, w1_ref, w2_ref, a0_ref,
                    a1_ref, a2_ref, g1_ref, g2_ref,
                    r_ref, k_ref, v_ref, lw_ref, a_ref, gg_ref, tail_ref, h_scr, d_scr, carry_scr, *, T):
    i, j = pl.program_id(0), pl.program_id(1)
    tm, D = x_ref.shape
    nt = tm // SUBLANES
    long_seq = T >= tm

    @pl.when(j == 0)
    def _():
        h = _modnorm(x_ref[...], g_ref[...], sc_ref[...], sh_ref[...], T)
        h3 = h.reshape(nt, SUBLANES, D)
        if long_seq:
            @pl.when(i % (T // tm) == 0)
            def _():
                carry_scr[...] = st_ref[...]
            prev3 = jnp.concatenate([carry_scr[...][None], h3[:nt - 1]], axis=0)
            carry_scr[...] = h3[nt - 1]
            tail_ref[...] = h3[nt - 1]
        else:
            prev3 = st_ref[...]
            tail_ref[...] = h3
        prev = _shift_rows(h3, prev3, 1).reshape(tm, D)
        h_scr[...] = h
        d_scr[...] = prev - h

    def mix(s):
        return (h_scr[...] + d_scr[...] * mu_ref[s:s + 1, :]).astype(BF16)

    for s, out in enumerate((r_ref, k_ref, v_ref)):
        @pl.when(j == s)
        def _(s=s, out=out):
            out[...] = jnp.dot(mix(s), wrkv_ref[...], preferred_element_type=F32)

    @pl.when(j == 3)
    def _():
        t = jnp.tanh(jnp.dot(mix(3), w1_ref[...], preferred_element_type=F32))
        pre_w = w0_ref[...] + jnp.dot(t.astype(BF16), w2_ref[...], preferred_element_type=F32)
        w_log = -jax.nn.softplus(-pre_w) - 0.5
        lw_ref[...] = -jnp.exp(w_log)
        u = jnp.dot(mix(4), a1_ref[...], preferred_element_type=F32)
        pre_a = a0_ref[...] + jnp.dot(u.astype(BF16), a2_ref[...], preferred_element_type=F32)
        a_ref[...] = jax.nn.sigmoid(pre_a)
        sg = jax.nn.sigmoid(jnp.dot(mix(5), g1_ref[...], preferred_element_type=F32))
        gg_ref[...] = jnp.dot(sg.astype(BF16), g2_ref[...], preferred_element_type=F32)


def _rwkv_in(x, g, sc, sh, st_tiles, mu, w_rkv, w0, w1, w2, a0, a1, a2, g1, g2, *, T, tm):
    M, D = x.shape
    B = st_tiles.shape[0]
    P = LANES
    c2 = lambda shape: pl.BlockSpec(shape, lambda i, j: (0,) * len(shape))
    row = pl.BlockSpec((tm, D), lambda i, j: (i, 0))
    return pl.pallas_call(
        functools.partial(_rwkv_in_kernel, T=T),
        grid=(M // tm, 4),
        in_specs=[row, c2((1, D)), _mod_spec(T, tm, D), _mod_spec(T, tm, D), _tile_spec(T, tm, D),
                  c2((6, D)), pl.BlockSpec((None, D, D), lambda i, j: (jnp.minimum(j, 2), 0, 0)),
                  c2((1, D)), c2((D, P)), c2((P, D)), c2((1, D)), c2((D, P)), c2((P, D)), c2((D, P)), c2((P, D))],
        out_specs=[row] * 6 + [_tile_spec(T, tm, D)],
        out_shape=[jax.ShapeDtypeStruct((M, D), F32)] * 6 + [jax.ShapeDtypeStruct((B, SUBLANES, D), F32)],
        scratch_shapes=[pltpu.VMEM((tm, D), F32), pltpu.VMEM((tm, D), F32), pltpu.VMEM((SUBLANES, D), F32)],
        compiler_params=_cparams(2),
        name="rwkv_in",
    )(x, g.reshape(1, D), sc, sh, st_tiles, mu, w_rkv, w0.reshape(1, D), w1, w2, a0.reshape(1, D), a1, a2, g1, g2)


def _wkv_kernel(r_ref, k_ref, v_ref, lw_ref, a_ref, g_ref, kk_ref, ka_ref, rk_ref, lnw_ref, lnb_ref, s0_ref,
                o_ref, sout_ref, s_scr, *, L, n_chunks):
    c = pl.program_id(1)
    H, N = RWKV_HEADS, RWKV_HEAD
    G = WKV_ROWS // L
    R = G * L

    @pl.when(c == 0)
    def _():
        s_scr[...] = s0_ref[...]

    ri = lax.broadcasted_iota(jnp.int32, (L, L), 0)
    ci = lax.broadcasted_iota(jnp.int32, (L, L), 1)
    lw = lw_ref[...]
    cs = _sel_rows(ri >= ci, lw)
    e_pos = jnp.exp(cs)
    e_neg = jnp.exp(-cs)
    e_prev = jnp.exp(cs - lw)

    row = lax.broadcasted_iota(jnp.int32, (R, R), 0)
    col = lax.broadcasted_iota(jnp.int32, (R, R), 1)
    same = (row // L) == (col // L)
    strict = same & (row > col)
    incl = same & (row >= col)

    for grp in range(H // G):
        heads = range(grp * G, (grp + 1) * G)
        A_l, R_l, B_l, K_l, V_l, kp_l = [], [], [], [], [], []
        for h in heads:
            sl = slice(h * N, (h + 1) * N)
            kh, ah = k_ref[:, sl], a_ref[:, sl]
            kkh = kh * kk_ref[:, sl]
            nrm = jnp.sqrt(jnp.sum(kkh * kkh, axis=-1, keepdims=True))
            kkh = kkh / jnp.maximum(nrm, 1e-12)
            kph = kh * (1.0 + (ah - 1.0) * ka_ref[:, sl])
            en = e_neg[:, sl]
            A_l.append(-kkh * e_prev[:, sl])
            R_l.append(r_ref[:, sl] * e_pos[:, sl])
            B_l.append(kkh * ah * en)
            K_l.append(kph * en)
            V_l.append(v_ref[:, sl])
            kp_l.append(kph)
        AR = jnp.concatenate(A_l + R_l, axis=0).astype(BF16)
        BK = jnp.concatenate(B_l + K_l, axis=0).astype(BF16)
        V4 = jnp.concatenate(V_l, axis=0)
        Q = _nt(AR, BK)
        ars_a, ars_r = [], []
        for q, h in enumerate(heads):
            Sb = s_scr[h].astype(BF16)
            ars_a.append(_nt(AR[q * L:(q + 1) * L], Sb))
            ars_r.append(_nt(AR[R + q * L:R + (q + 1) * L], Sb))
        nmat = jnp.where(strict, Q[:R, :R], 0.0)
        ak = jnp.where(strict, Q[:R, R:], 0.0)
        rhs = jnp.concatenate(ars_a, axis=0) + jnp.dot(ak.astype(BF16), V4.astype(BF16), preferred_element_type=F32)
        Y = jnp.where(((row // 2) == (col // 2)) & (row > col), nmat, 0.0)
        b = 2
        while b < L:
            Ck = jnp.where(((row // (2 * b)) == (col // (2 * b))) & ((row % (2 * b)) >= b) & ((col % (2 * b)) < b),
                           nmat, 0.0)
            Yb = Y.astype(BF16)
            Tm = Ck + jnp.dot(Yb, Ck.astype(BF16), preferred_element_type=F32)
            Y = Y + Tm + jnp.dot(Tm.astype(BF16), Yb, preferred_element_type=F32)
            b *= 2
        E = rhs + jnp.dot(Y.astype(BF16), rhs.astype(BF16), preferred_element_type=F32)
        EV = jnp.concatenate([E, V4], axis=0).astype(BF16)
        rbk = jnp.concatenate([jnp.where(incl, Q[R:, :R], 0.0), jnp.where(incl, Q[R:, R:], 0.0)], axis=1)
        y4 = jnp.concatenate(ars_r, axis=0) + jnp.dot(rbk.astype(BF16), EV, preferred_element_type=F32)
        for q, h in enumerate(heads):
            sl = slice(h * N, (h + 1) * N)
            rows = slice(q * L, (q + 1) * L)
            ev_h = jnp.concatenate([EV[rows], EV[R + q * L:R + (q + 1) * L]], axis=0)
            bk_h = jnp.concatenate([BK[rows], BK[R + q * L:R + (q + 1) * L]], axis=0)
            s_scr[h] = (s_scr[h] + _tn(ev_h, bk_h)) * e_pos[L - 1:L, sl]
            y = y4[rows]
            mean = jnp.mean(y, axis=-1, keepdims=True)
            var = jnp.mean(jnp.square(y - mean), axis=-1, keepdims=True)
            yn = (y - mean) * lax.rsqrt(var + RWKV_GN_EPS) * lnw_ref[:, sl] + lnb_ref[:, sl]
            bonus = jnp.sum(r_ref[:, sl] * kp_l[q] * rk_ref[:, sl], axis=-1, keepdims=True) * V_l[q]
            o_ref[:, sl] = (yn + bonus) * g_ref[:, sl]

    @pl.when(c == n_chunks - 1)
    def _():
        sout_ref[...] = s_scr[...]


def _wkv_core(r, k, v, lw, a, g, k_k, k_a, r_k, ln_w, ln_b, s0, *, T, L):
    M, D = r.shape
    B = s0.shape[0]
    nc = T // L
    row = pl.BlockSpec((L, D), lambda b, c: (b * nc + c, 0))
    par = pl.BlockSpec((1, D), lambda b, c: (0, 0))
    st = pl.BlockSpec((None, RWKV_HEADS, RWKV_HEAD, RWKV_HEAD), lambda b, c: (b, 0, 0, 0))
    v1 = lambda t: t.reshape(1, D)
    return pl.pallas_call(
        functools.partial(_wkv_kernel, L=L, n_chunks=nc),
        grid=(B, nc),
        in_specs=[row] * 6 + [par] * 5 + [st],
        out_specs=[row, st],
        out_shape=[jax.ShapeDtypeStruct((M, D), F32), jax.ShapeDtypeStruct(s0.shape, F32)],
        scratch_shapes=[pltpu.VMEM((RWKV_HEADS, RWKV_HEAD, RWKV_HEAD), F32)],
        compiler_params=_cparams(2),
        name="wkv_core",
    )(r, k, v, lw, a, g, v1(k_k), v1(k_a), v1(r_k), v1(ln_w), v1(ln_b), s0)


def _wkv_long_kernel(r_ref, k_ref, v_ref, lw_ref, a_ref, g_ref, kk_ref, ka_ref, rk_ref, lnw_ref, lnb_ref, s0_ref,
                     o_ref, sout_ref, s_scr, *, L, NC, n_steps):
    step = pl.program_id(1)
    H, N, G = RWKV_HEADS, RWKV_HEAD, WKV_GROUP_HEADS
    GW = G * N
    NG = H // G
    R = G * L
    TB = NC * L

    @pl.when(step == 0)
    def _():
        s_scr[...] = jnp.zeros_like(s_scr)
        for h in range(H):
            q = h % G
            s_scr[h // G, q * N:(q + 1) * N, q * N:(q + 1) * N] = s0_ref[h]

    ti = lax.broadcasted_iota(jnp.int32, (TB, TB), 0)
    tj = lax.broadcasted_iota(jnp.int32, (TB, TB), 1)
    same_c = (ti // L) == (tj // L)
    lw = lw_ref[...]
    cs = _sel_rows(same_c & (ti >= tj), lw)
    ctot = jnp.concatenate([jnp.broadcast_to(cs[(c + 1) * L - 1:(c + 1) * L], (L, cs.shape[1])) for c in range(NC)],
                           axis=0)
    e_pos, e_neg, e_prev = jnp.exp(cs), jnp.exp(-cs), jnp.exp(cs - lw)
    e_hat, e_tot = jnp.exp(ctot - cs), jnp.exp(ctot)
    gi = lax.broadcasted_iota(jnp.int32, (GW, GW), 0) // N
    gj = lax.broadcasted_iota(jnp.int32, (GW, GW), 1) // N
    gones = (gi == gj).astype(F32)

    def head_sum(x):
        return jnp.concatenate([_sel_cols(x[:, g * GW:(g + 1) * GW], gones) for g in range(NG)], axis=1)

    r, k, v, a = r_ref[...], k_ref[...], v_ref[...], a_ref[...]
    kkv = k * kk_ref[...]
    kkn = kkv / jnp.maximum(jnp.sqrt(head_sum(kkv * kkv)), 1e-12)
    kp = k * (1.0 + (a - 1.0) * ka_ref[...])
    ba = kkn * a
    nat = dict(A=-kkn * e_prev, R=r * e_pos, Bt=ba * e_neg, Kt=kp * e_neg, Bh=ba * e_hat, Kh=kp * e_hat, V=v)
    bonus = head_sum(r * kp * rk_ref[...]) * v

    row = lax.broadcasted_iota(jnp.int32, (R, GW), 0)
    lane = lax.broadcasted_iota(jnp.int32, (R, GW), 1)
    hm = (row // L) == (lane // N)

    def bd(name, c, g):
        x = nat[name][c * L:(c + 1) * L, g * GW:(g + 1) * GW]
        return jnp.where(hm, jnp.concatenate([x] * G, axis=0), 0.0)

    mr = lax.broadcasted_iota(jnp.int32, (R, R), 0)
    mc = lax.broadcasted_iota(jnp.int32, (R, R), 1)
    same_h = (mr // L) == (mc // L)
    strict = same_h & (mr > mc)
    incl = same_h & (mr >= mc)

    units = [(c, g) for c in range(NC) for g in range(NG)]
    U = {}
    for u in units:
        ops = {n: bd(n, *u) for n in nat}
        AR = jnp.concatenate([ops["A"], ops["R"]], axis=0).astype(BF16)
        BK = jnp.concatenate([ops["Bt"], ops["Kt"]], axis=0).astype(BF16)
        U[u] = dict(ops=ops, Q=_nt(AR, BK))
    for u in units:
        d = U[u]
        Q = d.pop("Q")
        d["nmat"] = jnp.where(strict, Q[:R, :R], 0.0)
        d["ak"] = jnp.where(strict, Q[:R, R:], 0.0).astype(BF16)
        d["rb"] = jnp.where(incl, Q[R:, :R], 0.0).astype(BF16)
        d["rk"] = jnp.where(incl, Q[R:, R:], 0.0).astype(BF16)
        d["Y"] = jnp.where(((mr // 2) == (mc // 2)) & (mr > mc), d["nmat"], 0.0)
    b = 2
    while b < L:
        lvl = ((mr // (2 * b)) == (mc // (2 * b))) & ((mr % (2 * b)) >= b) & ((mc % (2 * b)) < b)
        for u in units:
            d = U[u]
            Ck = jnp.where(lvl, d["nmat"], 0.0)
            d["Yb"] = d["Y"].astype(BF16)
            d["Tm"] = Ck + jnp.dot(d["Yb"], Ck.astype(BF16), preferred_element_type=F32)
        for u in units:
            d = U[u]
            d["Y"] = d["Y"] + d["Tm"] + jnp.dot(d["Tm"].astype(BF16), d["Yb"], preferred_element_type=F32)
        b *= 2
    for u in units:
        d = U[u]
        vb = d["ops"]["V"].astype(BF16)
        d["akv"] = jnp.dot(d["ak"], vb, preferred_element_type=F32)
        d["rkv"] = jnp.dot(d["rk"], vb, preferred_element_type=F32)
    for u in units:
        d = U[u]
        x = jnp.concatenate([d["ops"]["A"], d["akv"]], axis=1)
        d["AE"] = x + jnp.dot(d["Y"].astype(BF16), x.astype(BF16), preferred_element_type=F32)
    for u in units:
        d = U[u]
        base = jnp.concatenate([d["ops"]["R"], d["rkv"]], axis=1)
        ry = base + jnp.dot(d["rb"], d["AE"].astype(BF16), preferred_element_type=F32)
        d["Rbar"] = ry[:, :GW].astype(BF16)
        d["y0"] = ry[:, GW:]
    for u in units:
        d = U[u]
        bh = d["ops"]["Bh"].astype(BF16)
        kh = d["ops"]["Kh"].astype(BF16)
        d["W"] = _tn(d["AE"][:, :GW].astype(BF16), bh).astype(BF16)
        ev = jnp.concatenate([d["AE"][:, GW:], d["ops"]["V"]], axis=0).astype(BF16)
        d["Z"] = _tn(ev, jnp.concatenate([bh, kh], axis=0))

    ys = []
    for c in range(NC):
        yc = []
        for g in range(NG):
            d = U[(c, g)]
            S = s_scr[g]
            Sb = S.astype(BF16)
            ybd = d["y0"] + _nt(d["Rbar"], Sb)
            yc.append(sum(ybd[q * L:(q + 1) * L] for q in range(G)))
            decay = e_tot[c * L:c * L + 1, g * GW:(g + 1) * GW]
            s_scr[g] = S * decay + jnp.dot(Sb, d["W"], preferred_element_type=F32) + d["Z"]
        ys.append(jnp.concatenate(yc, axis=1))
    y = jnp.concatenate(ys, axis=0)

    mean = head_sum(y) * (1.0 / N)
    yc = y - mean
    var = head_sum(yc * yc) * (1.0 / N)
    yn = yc * lax.rsqrt(var + RWKV_GN_EPS) * lnw_ref[...] + lnb_ref[...]
    o_ref[...] = ((yn + bonus) * g_ref[...]).astype(o_ref.dtype)

    @pl.when(step == n_steps - 1)
    def _():
        for h in range(H):
            q = h % G
            sout_ref[h] = s_scr[h // G, q * N:(q + 1) * N, q * N:(q + 1) * N]


def _wkv_long(r, k, v, lw, a, g, k_k, k_a, r_k, ln_w, ln_b, s0, *, T, L, NC):
    M, D = r.shape
    B = s0.shape[0]
    TB = NC * L
    ns = T // TB
    row = pl.BlockSpec((TB, D), lambda b, c: (b * ns + c, 0))
    par = pl.BlockSpec((1, D), lambda b, c: (0, 0))
    st = pl.BlockSpec((None, RWKV_HEADS, RWKV_HEAD, RWKV_HEAD), lambda b, c: (b, 0, 0, 0))
    v1 = lambda t: t.reshape(1, D)
    GW = WKV_GROUP_HEADS * RWKV_HEAD
    return pl.pallas_call(
        functools.partial(_wkv_long_kernel, L=L, NC=NC, n_steps=ns),
        grid=(B, ns),
        in_specs=[row] * 6 + [par] * 5 + [st],
        out_specs=[row, st],
        out_shape=[jax.ShapeDtypeStruct((M, D), BF16), jax.ShapeDtypeStruct(s0.shape, F32)],
        scratch_shapes=[pltpu.VMEM((RWKV_HEADS // WKV_GROUP_HEADS, GW, GW), F32)],
        compiler_params=_cparams(2),
        name="wkv_long",
    )(r, k, v, lw, a, g, v1(k_k), v1(k_a), v1(r_k), v1(ln_w), v1(ln_b), s0)


def _pad_cols(w, n):
    return jnp.pad(w, ((0, 0), (0, n - w.shape[1])))


def _pad_rows(w, n):
    return jnp.pad(w, ((0, n - w.shape[0]), (0, 0)))


def _state_tiles(st):
    return jnp.pad(st, ((0, 0), (SUBLANES - st.shape[1], 0), (0, 0)))


def _ssm_from_groups(s):
    B = s.shape[0]
    s = s.reshape(B, SSD_GROUPS, SSD_STATE, SSD_HPG, SSD_HEAD_DIM)
    return s.transpose(0, 1, 3, 4, 2).reshape(B, SSD_HEADS, SSD_HEAD_DIM, SSD_STATE)


def kernel(x_prompt, x_sample, c_prompt, c_sample, state_ssm, state_ssd_conv, cache_k, cache_v, cache_logf, page_table, state_wkv, state_shift, state_ffn_conv, ada_w, ada_b, norm_mix, norm_ffn, norm_final, ssd_in_w, ssd_conv_w, ssd_conv_b, ssd_dt_bias, ssd_a_log, ssd_d, ssd_norm_w, ssd_out_w, fox_in_w, fox_f_b, fox_out_w, rwkv_mu, rwkv_w_rkv, rwkv_w0, rwkv_w1, rwkv_w2, rwkv_a0, rwkv_a1, rwkv_a2, rwkv_g1, rwkv_g2, rwkv_k_k, rwkv_k_a, rwkv_r_k, rwkv_ln_w, rwkv_ln_b, rwkv_out_w, ffn_up_w, ffn_conv_w, ffn_conv_b, ffn_down_w):
    BP, TP, D = x_prompt.shape
    BS, TS, _ = x_sample.shape
    groups = [dict(B=BP, T=TP, tm=1024, tm_s=512, ssd_l=SSD_CHUNK, wkv_l=RWKV_CHUNK, gate_tc=512),
              dict(B=BS, T=TS, tm=BS * TS, tm_s=512, ssd_l=TS, wkv_l=TS, gate_tc=TS)]
    xs = [x_prompt.reshape(BP * TP, D), x_sample.reshape(BS * TS, D)]

    n_c = BP + BS
    n_c_pad = -(-n_c // SUBLANES) * SUBLANES
    c_all = jnp.pad(jnp.concatenate([c_prompt, c_sample], axis=0), ((0, n_c_pad - n_c), (0, 0)))
    mods_all = _ada_mods(c_all, ada_w, ada_b).reshape(DEPTH, n_c_pad, 6, 1, D)
    row0 = [0, BP]

    def mods_of(i, gi):
        m = mods_all[i, row0[gi]:row0[gi] + groups[gi]["B"]]
        return [m[:, k] for k in range(6)]

    outs = {k: ([], []) for k in ("ssm", "sconv", "k", "v", "lf", "wkv", "shift", "fconv")}
    ssm_sample_all = None

    for i in range(DEPTH):
        kind, j = i % N_MIXERS, i // N_MIXERS
        for gi, gp in enumerate(groups):
            B, T, tm, tm_s = gp["B"], gp["T"], gp["tm"], gp["tm_s"]
            x = xs[gi]
            prompt = gi == 0
            sh_m, sc_m, gt_m, sh_f, sc_f, gt_f = mods_of(i, gi)
            if kind == 0:
                w_main = ssd_in_w[j][:, :SSD_D_INNER + SSD_CONV_CH].astype(BF16)
                w_dt = _pad_cols(ssd_in_w[j][:, SSD_D_INNER + SSD_CONV_CH:], LANES).astype(BF16)
                zx = _norm_mod_matmul(x, norm_mix[i], sc_m, sh_m, w_main, T=T, tm=tm, tn=1024)
                dt = _norm_mod_matmul(x, norm_mix[i], sc_m, sh_m, w_dt, T=T, tm=tm, tn=LANES)
                ssd_par = (ssd_conv_w[j], ssd_conv_b[j], ssd_dt_bias[j], ssd_a_log[j], ssd_d[j], ssd_norm_w[j])
                if prompt:
                    conv_tiles = jnp.zeros((B, SUBLANES, SSD_CONV_CH), F32)
                    h0 = jnp.zeros((1, B, SSD_GROUPS, SSD_STATE, SSD_HPG * SSD_HEAD_DIM), F32)
                    y, ctail, h_new = _ssd_core(zx, dt, conv_tiles, h0, *ssd_par, B=B, T=T, L=gp["ssd_l"],
                                                natural=False)
                    outs["ssm"][gi].append(_ssm_from_groups(h_new[0]))
                else:
                    conv_tiles = _state_tiles(state_ssd_conv[j])
                    h0 = state_ssm.reshape(-1, B, SSD_GROUPS, SSD_HPG * SSD_HEAD_DIM, SSD_STATE)
                    y, ctail, ssm_sample_all = _ssd_core(zx, dt, conv_tiles, h0, *ssd_par, B=B, T=T, L=gp["ssd_l"],
                                                         natural=True, layer=j, out_buf=ssm_sample_all)
                outs["sconv"][gi].append(ctail[:, SUBLANES - (SSD_CONV - 1):])
                x = _res_matmul(y, ssd_out_w[j].astype(BF16), x, gt_m, T=T, tm=tm)
            elif kind == 1:
                w_qkv = fox_in_w[j][:, :3 * FOX_INNER].reshape(D, 3, FOX_INNER).transpose(1, 0, 2).astype(BF16)
                w_f = _pad_cols(fox_in_w[j][:, 3 * FOX_INNER:], LANES).astype(BF16)
                f = _norm_mod_matmul(x, norm_mix[i], sc_m, sh_m, w_f, T=T, tm=tm, tn=LANES)
                f_b = jnp.pad(fox_f_b[j], (0, LANES - FOX_HEADS)).reshape(1, LANES)
                lf, cum = _fox_gate(f, f_b, B=B, T=T, tc=gp["gate_tc"])
                if prompt:
                    kt, vt, qa, ka, va = _fox_proj_aug(x, norm_mix[i], sc_m, sh_m, w_qkv, cum, B=B, T=T, tm=tm_s)
                    o = _fox_flash(qa, ka, va, B=B, T=T, tq=512, tk=512)
                    rows = lambda t: t.reshape(B, FOX_HEADS, FOX_HEAD_DIM, T).transpose(0, 3, 1, 2)
                    k_out, v_out = rows(kt), rows(vt)
                else:
                    q, k, v = _norm_mod_matmul_multi(x, norm_mix[i], sc_m, sh_m, w_qkv, T=T, tm=tm_s)
                    cum_t = jnp.swapaxes(cum[:, :FOX_HEADS].reshape(B, T, FOX_HEADS), 1, 2)
                    o = _fox_sample(q, k, v, cum, cum_t, jnp.transpose(cache_k[j], (0, 2, 3, 1)),
                                    jnp.transpose(cache_v[j], (0, 2, 3, 1)), jnp.swapaxes(cache_logf[j], 1, 2),
                                    page_table, B=B, TS=T, pps=FOX_PAGES_PER_STEP)
                    hd = lambda t: t.reshape(B, T, FOX_HEADS, FOX_HEAD_DIM)
                    k_out, v_out = hd(k), hd(v)
                outs["k"][gi].append(k_out)
                outs["v"][gi].append(v_out)
                outs["lf"][gi].append(lf[:, :FOX_HEADS].reshape(B, T, FOX_HEADS))
                x = _res_matmul(o, fox_out_w[j].astype(BF16), x, gt_m, T=T, tm=tm)
            else:
                if prompt:
                    st_tiles = jnp.zeros((B, SUBLANES, D), F32)
                    s0 = jnp.zeros((B, RWKV_HEADS, RWKV_HEAD, RWKV_HEAD), F32)
                else:
                    st_tiles = _state_tiles(state_shift[j][:, None, :])
                    s0 = state_wkv[j]
                r, k, v, lw, a, g, tails = _rwkv_in(
                    x, norm_mix[i], sc_m, sh_m, st_tiles, rwkv_mu[j], rwkv_w_rkv[j].astype(BF16), rwkv_w0[j],
                    _pad_cols(rwkv_w1[j], LANES).astype(BF16), _pad_rows(rwkv_w2[j], LANES).astype(BF16), rwkv_a0[j],
                    _pad_cols(rwkv_a1[j], LANES).astype(BF16), _pad_rows(rwkv_a2[j], LANES).astype(BF16),
                    rwkv_g1[j].astype(BF16), rwkv_g2[j].astype(BF16), T=T, tm=tm_s)
                wkv_par = (rwkv_k_k[j], rwkv_k_a[j], rwkv_r_k[j].reshape(-1), rwkv_ln_w[j], rwkv_ln_b[j], s0)
                if prompt:
                    o, s_new = _wkv_long(r, k, v, lw, a, g, *wkv_par, T=T, L=gp["wkv_l"], NC=WKV_STEP_CHUNKS)
                else:
                    o, s_new = _wkv_core(r, k, v, lw, a, g, *wkv_par, T=T, L=gp["wkv_l"])
                outs["wkv"][gi].append(s_new)
                outs["shift"][gi].append(tails[:, SUBLANES - 1])
                x = _res_matmul(o, rwkv_out_w[j].astype(BF16), x, gt_m, T=T, tm=tm)
            if prompt:
                ffn_tiles = jnp.zeros((B, SUBLANES, D_FF), F32)
            else:
                ffn_tiles = _state_tiles(state_ffn_conv[i])
            x, tails = _conv_ffn(x, norm_ffn[i], sc_f, sh_f, gt_f, ffn_tiles, ffn_up_w[i].astype(BF16),
                                 ffn_conv_w[i], ffn_conv_b[i], ffn_down_w[i].astype(BF16), T=T, tm=tm_s)
            outs["fconv"][gi].append(tails[:, SUBLANES - (FFN_CONV - 1):])
            xs[gi] = x

    y_prompt = _rmsnorm(xs[0], norm_final, tm=1024).reshape(BP, TP, D)
    y_sample = _rmsnorm(xs[1], norm_final, tm=1024).reshape(BS, TS, D)
    st = lambda name, gi: jnp.stack(outs[name][gi])
    ssm_sample = ssm_sample_all.reshape(state_ssm.shape)
    return (y_prompt, y_sample, st("ssm", 0), ssm_sample, st("sconv", 0), st("sconv", 1),
            st("k", 0), st("k", 1), st("v", 0), st("v", 1), st("lf", 0), st("lf", 1),
            st("wkv", 0), st("wkv", 1), st("shift", 0), st("shift", 1), st("fconv", 0), st("fconv", 1))
```

```python
import functools

import jax
import jax.numpy as jnp
from jax import lax
from jax.experimental import pallas as pl
from jax.experimental.pallas import tpu as pltpu

F32 = jnp.float32
BF16 = jnp.bfloat16

D_MODEL = 1024
DEPTH = 4
N_MIXERS = 3
RMS_EPS = 1e-6
SSD_D_INNER = 2048
SSD_HEAD_DIM = 64
SSD_HEADS = 32
SSD_GROUPS = 4
SSD_HPG = 8
SSD_STATE = 128
SSD_CONV = 4
SSD_CONV_CH = 3072
SSD_CHUNK = 128
FOX_HEAD_DIM = 64
FOX_HEADS = 16
FOX_INNER = 1024
RWKV_HEAD = 64
RWKV_HEADS = 16
RWKV_GN_EPS = 64e-5
RWKV_CHUNK = 32
WKV_ROWS = 128
WKV_GROUP_HEADS = 4
WKV_STEP_CHUNKS = 4
FOX_PAGES_PER_STEP = 8
FOX_BIAS_TERMS = 3
D_FF = 2816
FFN_CONV = 3

SUBLANES = 8
LANES = 128
VMEM_LIMIT = 56 * 1024 * 1024


def _cparams(n_grid):
    return pltpu.CompilerParams(dimension_semantics=("arbitrary",) * n_grid,
                                vmem_limit_bytes=VMEM_LIMIT)


def _nt(a, b):
    return lax.dot_general(a, b, (((1,), (1,)), ((), ())), preferred_element_type=F32)


def _tn(a, b):
    return lax.dot_general(a, b, (((0,), (0,)), ((), ())), preferred_element_type=F32)


def _split3(c):
    hi = c.astype(BF16)
    r1 = c - hi.astype(F32)
    mid = r1.astype(BF16)
    lo = (r1 - mid.astype(F32)).astype(BF16)
    return hi, mid, lo


def _sel_rows(sel, x):
    n = x.shape[1]
    r = jnp.dot(sel.astype(BF16), jnp.concatenate(_split3(x), axis=1), preferred_element_type=F32)
    return r[:, :n] + r[:, n:2 * n] + r[:, 2 * n:]


def _sel_cols(x, sel):
    m = x.shape[0]
    r = jnp.dot(jnp.concatenate(_split3(x), axis=0), sel.astype(BF16), preferred_element_type=F32)
    return r[:m] + r[m:2 * m] + r[2 * m:]


def _modnorm(x, g, sc, sh, T):
    xn = x * lax.rsqrt(jnp.mean(x * x, axis=-1, keepdims=True) + RMS_EPS) * g
    if sc.ndim == 2:
        return xn * (1.0 + sc) + sh
    tm, D = x.shape
    x3 = xn.reshape(tm // T, T, D)
    return (x3 * (1.0 + sc) + sh).reshape(tm, D)


def _gate_mul(y, gt, T):
    if gt.ndim == 2:
        return y * gt
    tm, D = y.shape
    return (y.reshape(tm // T, T, D) * gt).reshape(tm, D)


def _shift_rows(cur3, prev3, k):
    row = lax.broadcasted_iota(jnp.int32, cur3.shape, 1)
    return pltpu.roll(jnp.where(row < SUBLANES - k, cur3, prev3), k, 1)


def _mod_spec(T, tm, D):
    if T >= tm:
        per = T // tm
        return pl.BlockSpec((None, 1, D), lambda i, *_: (i // per, 0, 0))
    return pl.BlockSpec((tm // T, 1, D), lambda i, *_: (i, 0, 0))


def _tile_spec(T, tm, C):
    if T >= tm:
        per = T // tm
        return pl.BlockSpec((None, SUBLANES, C), lambda i, *_: (i // per, 0, 0))
    return pl.BlockSpec((tm // T, SUBLANES, C), lambda i, *_: (i, 0, 0))


def _ada_kernel(c_ref, w_ref, b_ref, o_ref):
    c = c_ref[...]
    a = (c * jax.nn.sigmoid(c)).astype(BF16)
    o_ref[...] = jnp.dot(a, w_ref[...].astype(BF16), preferred_element_type=F32) + b_ref[...]


def _ada_mods(c_all, ada_w, ada_b):
    R, D = c_all.shape
    L, _, N = ada_w.shape
    tn = 1536
    return pl.pallas_call(
        _ada_kernel,
        grid=(L, N // tn),
        in_specs=[pl.BlockSpec((R, D), lambda l, j: (0, 0)),
                  pl.BlockSpec((None, D, tn), lambda l, j: (l, 0, j)),
                  pl.BlockSpec((None, 1, tn), lambda l, j: (l, 0, j))],
        out_specs=pl.BlockSpec((None, R, tn), lambda l, j: (l, 0, j)),
        out_shape=jax.ShapeDtypeStruct((L, R, N), F32),
        compiler_params=_cparams(2),
        name="ada_mods",
    )(c_all, ada_w, ada_b.reshape(L, 1, N))


def _nmm_kernel(x_ref, g_ref, sc_ref, sh_ref, w_ref, o_ref, h_scr, *, T):
    @pl.when(pl.program_id(1) == 0)
    def _():
        h_scr[...] = _modnorm(x_ref[...], g_ref[...], sc_ref[...], sh_ref[...], T).astype(BF16)
    o_ref[...] = jnp.dot(h_scr[...], w_ref[...], preferred_element_type=F32).astype(o_ref.dtype)


def _norm_mod_matmul(x, g, sc, sh, w, *, T, tm, tn):
    M, D = x.shape
    N = w.shape[1]
    return pl.pallas_call(
        functools.partial(_nmm_kernel, T=T),
        grid=(M // tm, N // tn),
        in_specs=[pl.BlockSpec((tm, D), lambda i, j: (i, 0)),
                  pl.BlockSpec((1, D), lambda i, j: (0, 0)),
                  _mod_spec(T, tm, D), _mod_spec(T, tm, D),
                  pl.BlockSpec((D, tn), lambda i, j: (0, j))],
        out_specs=pl.BlockSpec((tm, tn), lambda i, j: (i, j)),
        out_shape=jax.ShapeDtypeStruct((M, N), F32),
        scratch_shapes=[pltpu.VMEM((tm, D), BF16)],
        compiler_params=_cparams(2),
        name="norm_mod_matmul",
    )(x, g.reshape(1, D), sc, sh, w)


def _nmm_multi_kernel(x_ref, g_ref, sc_ref, sh_ref, w_ref, *rest, T, n_out):
    outs, h_scr = rest[:n_out], rest[n_out]
    j = pl.program_id(1)

    @pl.when(j == 0)
    def _():
        h_scr[...] = _modnorm(x_ref[...], g_ref[...], sc_ref[...], sh_ref[...], T).astype(BF16)
    res = jnp.dot(h_scr[...], w_ref[...], preferred_element_type=F32)
    for o in range(n_out):
        @pl.when(j == o)
        def _(o=o):
            outs[o][...] = res


def _norm_mod_matmul_multi(x, g, sc, sh, w_stack, *, T, tm):
    M, D = x.shape
    n_out, _, N = w_stack.shape
    return pl.pallas_call(
        functools.partial(_nmm_multi_kernel, T=T, n_out=n_out),
        grid=(M // tm, n_out),
        in_specs=[pl.BlockSpec((tm, D), lambda i, j: (i, 0)),
                  pl.BlockSpec((1, D), lambda i, j: (0, 0)),
                  _mod_spec(T, tm, D), _mod_spec(T, tm, D),
                  pl.BlockSpec((None, D, N), lambda i, j: (j, 0, 0))],
        out_specs=[pl.BlockSpec((tm, N), lambda i, j: (i, 0)) for _ in range(n_out)],
        out_shape=[jax.ShapeDtypeStruct((M, N), F32) for _ in range(n_out)],
        scratch_shapes=[pltpu.VMEM((tm, D), BF16)],
        compiler_params=_cparams(2),
        name="norm_mod_matmul_multi",
    )(x, g.reshape(1, D), sc, sh, w_stack)


def _res_mm_kernel(a_ref, w_ref, x_ref, gt_ref, o_ref, *, T):
    y = jnp.dot(a_ref[...].astype(BF16), w_ref[...], preferred_element_type=F32)
    o_ref[...] = x_ref[...] + _gate_mul(y, gt_ref[...], T)


def _res_matmul(a, w, x, gt, *, T, tm):
    M, K = a.shape
    D = w.shape[1]
    return pl.pallas_call(
        functools.partial(_res_mm_kernel, T=T),
        grid=(M // tm,),
        in_specs=[pl.BlockSpec((tm, K), lambda i: (i, 0)),
                  pl.BlockSpec((K, D), lambda i: (0, 0)),
                  pl.BlockSpec((tm, D), lambda i: (i, 0)),
                  _mod_spec(T, tm, D)],
        out_specs=pl.BlockSpec((tm, D), lambda i: (i, 0)),
        out_shape=jax.ShapeDtypeStruct((M, D), F32),
        compiler_params=_cparams(1),
        name="res_matmul",
    )(a, w, x, gt)


def _ffn_kernel(x_ref, g_ref, sc_ref, sh_ref, gt_ref, st_ref, upw_ref, cw_ref, cb_ref, dnw_ref,
                o_ref, tail_ref, act_scr, carry_scr, *, T, F, fc):
    i = pl.program_id(0)
    tm, D = x_ref.shape
    nt = tm // SUBLANES
    long_seq = T >= tm
    x = x_ref[...]
    h = _modnorm(x, g_ref[...], sc_ref[...], sh_ref[...], T).astype(BF16)
    if long_seq:
        @pl.when(i % (T // tm) == 0)
        def _():
            carry_scr[...] = st_ref[...]
    for c in range(F // fc):
        lo, hi = c * fc, (c + 1) * fc
        gc = jnp.dot(h, upw_ref[:, lo:hi], preferred_element_type=F32)
        uc = jnp.dot(h, upw_ref[:, F + lo:F + hi], preferred_element_type=F32)
        g3 = gc.reshape(nt, SUBLANES, fc)
        if long_seq:
            prev3 = jnp.concatenate([carry_scr[:, lo:hi][None], g3[:nt - 1]], axis=0)
            carry_scr[:, lo:hi] = g3[nt - 1]
            tail_ref[:, lo:hi] = g3[nt - 1]
        else:
            prev3 = st_ref[:, :, lo:hi]
            tail_ref[:, :, lo:hi] = g3
        conv = (g3 * cw_ref[2:3, lo:hi] + _shift_rows(g3, prev3, 1) * cw_ref[1:2, lo:hi]
                + _shift_rows(g3, prev3, 2) * cw_ref[0:1, lo:hi] + cb_ref[:, lo:hi])
        act = conv * jax.nn.sigmoid(conv) * uc.reshape(nt, SUBLANES, fc)
        act_scr[:, lo:hi] = act.reshape(tm, fc).astype(BF16)
    y = jnp.dot(act_scr[...], dnw_ref[...], preferred_element_type=F32)
    o_ref[...] = x + _gate_mul(y, gt_ref[...], T)


def _conv_ffn(x, g, sc, sh, gt, st_tiles, up_w, conv_w, conv_b, down_w, *, T, tm):
    M, D = x.shape
    F = down_w.shape[0]
    B = st_tiles.shape[0]
    return pl.pallas_call(
        functools.partial(_ffn_kernel, T=T, F=F, fc=256),
        grid=(M // tm,),
        in_specs=[pl.BlockSpec((tm, D), lambda i: (i, 0)),
                  pl.BlockSpec((1, D), lambda i: (0, 0)),
                  _mod_spec(T, tm, D), _mod_spec(T, tm, D), _mod_spec(T, tm, D),
                  _tile_spec(T, tm, F),
                  pl.BlockSpec((D, 2 * F), lambda i: (0, 0)),
                  pl.BlockSpec((FFN_CONV, F), lambda i: (0, 0)),
                  pl.BlockSpec((1, F), lambda i: (0, 0)),
                  pl.BlockSpec((F, D), lambda i: (0, 0))],
        out_specs=[pl.BlockSpec((tm, D), lambda i: (i, 0)), _tile_spec(T, tm, F)],
        out_shape=[jax.ShapeDtypeStruct((M, D), F32), jax.ShapeDtypeStruct((B, SUBLANES, F), F32)],
        scratch_shapes=[pltpu.VMEM((tm, F), BF16), pltpu.VMEM((SUBLANES, F), F32)],
        compiler_params=_cparams(1),
        name="conv_ffn",
    )(x, g.reshape(1, D), sc, sh, gt, st_tiles, up_w, conv_w, conv_b.reshape(1, F), down_w)


def _rms_kernel(x_ref, g_ref, o_ref):
    x = x_ref[...]
    o_ref[...] = x * lax.rsqrt(jnp.mean(x * x, axis=-1, keepdims=True) + RMS_EPS) * g_ref[...]


def _rmsnorm(x, g, *, tm):
    M, D = x.shape
    return pl.pallas_call(
        _rms_kernel, grid=(M // tm,),
        in_specs=[pl.BlockSpec((tm, D), lambda i: (i, 0)), pl.BlockSpec((1, D), lambda i: (0, 0))],
        out_specs=pl.BlockSpec((tm, D), lambda i: (i, 0)),
        out_shape=jax.ShapeDtypeStruct((M, D), F32),
        compiler_params=_cparams(1), name="final_rmsnorm",
    )(x, g.reshape(1, D))


def _conv_silu(cur, prev_tile, w_ref, b_ref, lo, hi):
    L, C = cur.shape
    nt = L // SUBLANES
    c3 = cur.reshape(nt, SUBLANES, C)
    p3 = prev_tile[None] if nt == 1 else jnp.concatenate([prev_tile[None], c3[:nt - 1]], axis=0)
    out = c3 * w_ref[3:4, lo:hi] + b_ref[:, lo:hi]
    for k in range(1, SSD_CONV):
        out = out + _shift_rows(c3, p3, k) * w_ref[3 - k:4 - k, lo:hi]
    out = out * jax.nn.sigmoid(out)
    return out.reshape(L, C)


def _ssd_kernel(z0_ref, z1_ref, x0_ref, x1_ref, bc_ref, dt_ref, cst_ref, h0_ref, cw_ref, cb_ref, dtb_ref, alog_ref,
                dw_ref, nw_ref, ex_ref, *rest, L, n_chunks, natural):
    y_ref, ctail_ref, hout_ref, tail_scr, st_scr = rest[-5:]
    prev_ref = rest[0] if len(rest) == 6 else None
    c = pl.program_id(1)
    P, Nn, G, J = SSD_HEAD_DIM, SSD_STATE, SSD_GROUPS, SSD_HPG
    GW = J * P

    @pl.when(c == 0)
    def _():
        tail_scr[...] = cst_ref[...]
        st_scr[...] = h0_ref[...]

    conv = []
    for q, blk_ref in enumerate((x0_ref, x1_ref, bc_ref)):
        blk = blk_ref[...]
        lo, hi = q * 1024, (q + 1) * 1024
        conv.append(_conv_silu(blk, tail_scr[:, lo:hi], cw_ref, cb_ref, lo, hi))
        tail_scr[:, lo:hi] = blk[L - SUBLANES:, :]
        ctail_ref[:, lo:hi] = blk[L - SUBLANES:, :]
    xs = jnp.concatenate(conv[:2], axis=1)
    Bm = conv[2][:, :G * Nn]
    Cm = conv[2][:, G * Nn:]

    dt = jax.nn.softplus(dt_ref[...] + dtb_ref[...])
    a = dt * (-jnp.exp(alog_ref[...]))
    ri = lax.broadcasted_iota(jnp.int32, (L, L), 0)
    ci = lax.broadcasted_iota(jnp.int32, (L, L), 1)
    causal = ri >= ci
    acs = _sel_rows(causal, a)
    wide = _sel_cols(jnp.concatenate([dt, acs], axis=0), ex_ref[...])
    dt_w, acs_w = wide[:L], wide[L:]
    acs_t = acs.T
    last_w = acs_w[L - 1:L, :]
    X = xs * dt_w
    Xd = X * jnp.exp(last_w - acs_w)
    dstart = jnp.exp(acs_w)
    cdec = jnp.exp(last_w)
    zz = jnp.concatenate([z0_ref[...], z1_ref[...]], axis=1)
    zg = zz * jax.nn.sigmoid(zz)
    dwide = dw_ref[...]
    nw = nw_ref[...]
    Xb = X.astype(BF16)
    for g in range(G):
        gl = slice(g * GW, (g + 1) * GW)
        Bg = Bm[:, g * Nn:(g + 1) * Nn].astype(BF16)
        Cg = Cm[:, g * Nn:(g + 1) * Nn].astype(BF16)
        CB = _nt(Cg, Bg)
        st = st_scr[g]
        if natural:
            y_off = _nt(Cg, st.astype(BF16)) * dstart[:, gl]
            dec_h = jnp.exp(acs_t[g * J:(g + 1) * J, L - 1:L])
            dcol = jnp.broadcast_to(dec_h[:, None, :], (J, P, 1)).reshape(GW, 1)
            st_scr[g] = st * dcol + _tn(Xd[:, gl].astype(BF16), Bg)
        else:
            y_off = jnp.dot(Cg, st.astype(BF16), preferred_element_type=F32) * dstart[:, gl]
            st_scr[g] = st * cdec[:, gl] + _tn(Bg, Xd[:, gl].astype(BF16))
        ys = []
        for j in range(J):
            h = g * J + j
            diff = acs[:, h:h + 1] - acs_t[h:h + 1, :]
            dec = jnp.exp(jnp.where(causal, diff, -jnp.inf))
            Mh = (CB * dec).astype(BF16)
            ys.append(jnp.dot(Mh, Xb[:, h * P:(h + 1) * P], preferred_element_type=F32))
        yg = jnp.concatenate(ys, axis=1) + y_off + dwide[:, gl] * xs[:, gl]
        yg = yg * zg[:, gl]
        yg = yg * lax.rsqrt(jnp.mean(yg * yg, axis=-1, keepdims=True) + RMS_EPS) * nw[:, gl]
        y_ref[:, gl] = yg.astype(y_ref.dtype)

    @pl.when(c == n_chunks - 1)
    def _():
        n_prev = hout_ref.shape[0] - 1
        if prev_ref is not None:
            hout_ref[0:n_prev] = prev_ref[...]
        hout_ref[n_prev] = st_scr[...]


def _ssd_core(zx, dt, conv_tiles, h0, conv_w, conv_b, dt_bias, a_log, d_skip, norm_w, *, B, T, L, natural,
              layer=0, prev_out=None):
    M = zx.shape[0]
    nc = T // L
    DI, C = SSD_D_INNER, SSD_CONV_CH
    col = lambda q: pl.BlockSpec((L, 1024), lambda b, c, q=q: (b * nc + c, q))
    par = lambda n: pl.BlockSpec((1, n), lambda b, c: (0, 0))
    pad = lambda v: jnp.pad(v.astype(F32), (0, LANES - v.shape[0])).reshape(1, LANES)
    expand = (jnp.arange(LANES)[:, None] == (jnp.arange(DI) // SSD_HEAD_DIM)[None, :]).astype(F32)
    d_wide = jnp.repeat(d_skip.astype(F32), SSD_HEAD_DIM).reshape(1, DI)
    st_shape = h0.shape[2:]
    st_spec = pl.BlockSpec((None, None) + st_shape, lambda b, c: (layer, b, 0, 0, 0))
    tile_spec = pl.BlockSpec((None, SUBLANES, C), lambda b, c: (b, 0, 0))
    in_specs = [col(0), col(1), col(2), col(3), col(4),
                pl.BlockSpec((L, LANES), lambda b, c: (b * nc + c, 0)),
                tile_spec, st_spec,
                pl.BlockSpec((SSD_CONV, C), lambda b, c: (0, 0)), par(C), par(LANES), par(LANES),
                par(DI), par(DI), pl.BlockSpec((LANES, DI), lambda b, c: (0, 0))]
    args = [zx, zx, zx, zx, zx, dt, conv_tiles, h0, conv_w, conv_b.reshape(1, C), pad(dt_bias), pad(a_log),
            d_wide, norm_w.reshape(1, DI), expand]
    slots = lambda n: pl.BlockSpec((n, None) + st_shape, lambda b, c: (0, b, 0, 0, 0))
    if prev_out is not None:
        in_specs.append(slots(layer))
        args.append(prev_out)
    return pl.pallas_call(
        functools.partial(_ssd_kernel, L=L, n_chunks=nc, natural=natural),
        grid=(B, nc),
        in_specs=in_specs,
        out_specs=[pl.BlockSpec((L, DI), lambda b, c: (b * nc + c, 0)), tile_spec, slots(layer + 1)],
        out_shape=[jax.ShapeDtypeStruct((M, DI), BF16 if L % 16 == 0 else F32),
                   jax.ShapeDtypeStruct((B, SUBLANES, C), F32),
                   jax.ShapeDtypeStruct((layer + 1, B) + st_shape, F32)],
        scratch_shapes=[pltpu.VMEM((SUBLANES, C), F32), pltpu.VMEM(st_shape, F32)],
        compiler_params=_cparams(2), name="ssd_core",
    )(*args)


def _fox_gate_kernel(f_ref, fb_ref, lf_ref, cum_ref, carry_scr):
    c = pl.program_id(1)
    tc = f_ref.shape[0]

    @pl.when(c == 0)
    def _():
        carry_scr[...] = jnp.zeros_like(carry_scr)
    lf = jax.nn.log_sigmoid(f_ref[...] + fb_ref[...])
    ri = lax.broadcasted_iota(jnp.int32, (tc, tc), 0)
    ci = lax.broadcasted_iota(jnp.int32, (tc, tc), 1)
    cum = _sel_rows(ri >= ci, lf) + carry_scr[...]
    lf_ref[...] = lf
    cum_ref[...] = cum
    carry_scr[...] = cum[tc - 1:tc, :]


def _fox_gate(f_raw, f_b_pad, *, B, T, tc):
    M, W = f_raw.shape
    nc = T // tc
    row = pl.BlockSpec((tc, W), lambda b, c: (b * nc + c, 0))
    return pl.pallas_call(
        _fox_gate_kernel, grid=(B, nc),
        in_specs=[row, pl.BlockSpec((1, W), lambda b, c: (0, 0))],
        out_specs=[row, row],
        out_shape=[jax.ShapeDtypeStruct((M, W), F32)] * 2,
        scratch_shapes=[pltpu.VMEM((1, W), F32)],
        compiler_params=_cparams(2), name="fox_gate",
    )(f_raw, f_b_pad)


def _aug_tails(c_ref, place_ref, const_ref, which):
    pieces = _split3(c_ref[...])
    out = const_ref[which:which + 1, :]
    for t, pc in enumerate(pieces):
        out = out + jnp.dot(pc, place_ref[which, t], preferred_element_type=F32)
    return out


def _aug_placement():
    nt, Dh = FOX_BIAS_TERMS, FOX_HEAD_DIM
    src = jnp.arange(LANES)[:, None]
    lane = jnp.arange(FOX_HEADS * Dh)[None, :]
    own = (lane // Dh == src) & (src < FOX_HEADS)
    col = lane % Dh
    place_q = jnp.stack([(own & (col == t)).astype(F32) for t in range(nt)])
    place_k = jnp.stack([-(own & (col == nt + t)).astype(F32) for t in range(nt)])
    const = jnp.concatenate([((col >= nt) & (col < 2 * nt)).astype(F32), (col < nt).astype(F32)], axis=0)
    return jnp.stack([place_q, place_k]).astype(BF16), const


def _fox_proj_aug_kernel(x_ref, g_ref, sc_ref, sh_ref, w_ref, wt_ref, c_ref, place_ref, const_ref,
                         kt_ref, vt_ref, qa_ref, ka_ref, va_ref, h_scr, *, T):
    j = pl.program_id(1)
    H, Dh = FOX_HEADS, FOX_HEAD_DIM
    tm = x_ref.shape[0]

    @pl.when(j == 0)
    def _():
        h_scr[...] = _modnorm(x_ref[...], g_ref[...], sc_ref[...], sh_ref[...], T).astype(BF16)
    res = jnp.dot(h_scr[...], w_ref[...], preferred_element_type=F32)

    @pl.when(j == 0)
    def _():
        tails = _aug_tails(c_ref, place_ref, const_ref, 0)
        for h in range(H):
            sl = slice(h * Dh, (h + 1) * Dh)
            qa_ref[h] = jnp.concatenate([res[:, sl] * (Dh ** -0.5), tails[:, sl]], axis=1).astype(BF16)

    @pl.when(j == 1)
    def _():
        kt_ref[...] = _nt(wt_ref[...], h_scr[...])
        tails = _aug_tails(c_ref, place_ref, const_ref, 1)
        for h in range(H):
            sl = slice(h * Dh, (h + 1) * Dh)
            ka_ref[h] = jnp.concatenate([res[:, sl], tails[:, sl]], axis=1).astype(BF16)

    @pl.when(j == 2)
    def _():
        vt_ref[...] = _nt(wt_ref[...], h_scr[...])
        ones = (lax.broadcasted_iota(jnp.int32, (tm, LANES - Dh), 1) == 0).astype(F32)
        for h in range(H):
            va_ref[h] = jnp.concatenate([res[:, h * Dh:(h + 1) * Dh], ones], axis=1).astype(BF16)


def _fox_proj_aug(x, g, sc, sh, w_stack, cum, *, B, T, tm):
    M, D = x.shape
    per = T // tm
    H = FOX_HEADS
    row = pl.BlockSpec((tm, D), lambda i, j: (i, 0))
    wsp = pl.BlockSpec((None, D, D), lambda i, j: (j, 0, 0))
    tr = pl.BlockSpec((None, D, tm), lambda i, j: (i // per, 0, i % per))
    aug = pl.BlockSpec((None, H, tm, LANES), lambda i, j: (i // per, 0, i % per, 0))
    place, const = _aug_placement()
    return pl.pallas_call(
        functools.partial(_fox_proj_aug_kernel, T=T),
        grid=(M // tm, 3),
        in_specs=[row, pl.BlockSpec((1, D), lambda i, j: (0, 0)), _mod_spec(T, tm, D), _mod_spec(T, tm, D),
                  wsp, wsp, pl.BlockSpec((tm, cum.shape[1]), lambda i, j: (i, 0)),
                  pl.BlockSpec(place.shape, lambda i, j: (0, 0, 0, 0)), pl.BlockSpec(const.shape, lambda i, j: (0, 0))],
        out_specs=[tr, tr, aug, aug, aug],
        out_shape=[jax.ShapeDtypeStruct((B, D, T), F32)] * 2 + [jax.ShapeDtypeStruct((B, H, T, LANES), BF16)] * 3,
        scratch_shapes=[pltpu.VMEM((tm, D), BF16)],
        compiler_params=_cparams(2), name="fox_proj_aug",
    )(x, g.reshape(1, D), sc, sh, w_stack, jnp.swapaxes(w_stack, 1, 2), cum, place, const)


def _fox_flash_kernel(qa_ref, ka_ref, va_ref, o_ref, m_scr, acc_scr, *, tq, tk, nk):
    qi, ki = pl.program_id(1), pl.program_id(2)
    H, Dh = FOX_HEADS, FOX_HEAD_DIM

    @pl.when(ki == 0)
    def _():
        m_scr[...] = jnp.full_like(m_scr, -jnp.inf)
        acc_scr[...] = jnp.zeros_like(acc_scr)

    def compute(masked):
        if masked:
            rowp = qi * tq + lax.broadcasted_iota(jnp.int32, (tq, tk), 0)
            colp = ki * tk + lax.broadcasted_iota(jnp.int32, (tq, tk), 1)
            keep = colp <= rowp
        for h in range(H):
            s = _nt(qa_ref[h], ka_ref[h])
            if masked:
                s = jnp.where(keep, s, -jnp.inf)
            m_prev = m_scr[h]
            m_new = jnp.maximum(m_prev, jnp.max(s, axis=-1, keepdims=True))
            p = jnp.exp(s - m_new).astype(BF16)
            acc_scr[h] = jnp.exp(m_prev - m_new) * acc_scr[h] + jnp.dot(p, va_ref[h], preferred_element_type=F32)
            m_scr[h] = m_new

    needed = ki * tk <= qi * tq + (tq - 1)
    diag = ki * tk + (tk - 1) > qi * tq

    @pl.when(needed & diag)
    def _():
        compute(True)

    @pl.when(needed & jnp.logical_not(diag))
    def _():
        compute(False)

    @pl.when(ki == nk - 1)
    def _():
        for h in range(H):
            a = acc_scr[h]
            o_ref[:, h * Dh:(h + 1) * Dh] = (a[:, :Dh] / a[:, Dh:Dh + 1]).astype(o_ref.dtype)


def _fox_flash(qa, ka, va, *, B, T, tq, tk):
    H, Dh = FOX_HEADS, FOX_HEAD_DIM
    D = H * Dh
    nq, nk = T // tq, T // tk
    last_k = lambda qi: (qi * tq + tq - 1) // tk
    kblk = lambda b, qi, ki: (b, 0, jnp.minimum(ki, last_k(qi)), 0)
    return pl.pallas_call(
        functools.partial(_fox_flash_kernel, tq=tq, tk=tk, nk=nk),
        grid=(B, nq, nk),
        in_specs=[pl.BlockSpec((None, H, tq, LANES), lambda b, qi, ki: (b, 0, qi, 0)),
                  pl.BlockSpec((None, H, tk, LANES), kblk), pl.BlockSpec((None, H, tk, LANES), kblk)],
        out_specs=pl.BlockSpec((tq, D), lambda b, qi, ki: (b * nq + qi, 0)),
        out_shape=jax.ShapeDtypeStruct((B * T, D), BF16),
        scratch_shapes=[pltpu.VMEM((H, tq, 1), F32), pltpu.VMEM((H, tq, LANES), F32)],
        compiler_params=_cparams(3), name="fox_flash",
    )(qa, ka, va)


def _fox_sample_kernel(pt_ref, q_ref, kn_ref, vn_ref, cn_ref, cnt_ref, *rest, n_steps, pps):
    kt_refs, vt_refs, lft_refs = rest[:pps], rest[pps:2 * pps], rest[2 * pps:3 * pps]
    o_ref, qbd_scr, cq_scr, later_scr, m_scr, l_scr, acc_scr = rest[3 * pps:]
    p = pl.program_id(1)
    H, Dh = FOX_HEADS, FOX_HEAD_DIM
    TS = q_ref.shape[0]
    R = H * TS
    D = H * Dh
    KB = pps * kt_refs[0].shape[-1]

    @pl.when(p == 0)
    def _():
        row = lax.broadcasted_iota(jnp.int32, (R, D), 0)
        lane = lax.broadcasted_iota(jnp.int32, (R, D), 1)
        q_rep = jnp.broadcast_to(q_ref[...][None], (H, TS, D)).reshape(R, D) * (Dh ** -0.5)
        qbd_scr[...] = jnp.where(lane // Dh == row // TS, q_rep, 0.0).astype(BF16)
        cw = cn_ref.shape[1]
        c_rep = jnp.broadcast_to(cn_ref[...][None], (H, TS, cw)).reshape(R, cw)
        r2 = lax.broadcasted_iota(jnp.int32, (R, cw), 0)
        l2 = lax.broadcasted_iota(jnp.int32, (R, cw), 1)
        cq_scr[...] = jnp.sum(jnp.where(l2 == r2 // TS, c_rep, 0.0), axis=-1, keepdims=True)
        later_scr[...] = jnp.zeros_like(later_scr)
        m_scr[...] = jnp.full_like(m_scr, -jnp.inf)
        l_scr[...] = jnp.zeros_like(l_scr)
        acc_scr[...] = jnp.zeros_like(acc_scr)

    def online(s, pv):
        m_prev = m_scr[...]
        m_new = jnp.maximum(m_prev, jnp.max(s, axis=-1, keepdims=True))
        pr = jnp.exp(s - m_new)
        alpha = jnp.exp(m_prev - m_new)
        l_scr[...] = alpha * l_scr[...] + jnp.sum(pr, axis=-1, keepdims=True)
        acc_scr[...] = alpha * acc_scr[...] + pv(pr.astype(BF16))
        m_scr[...] = m_new

    def past_pages():
        lft = jnp.concatenate([r[...] for r in lft_refs], axis=1)
        kt = jnp.concatenate([r[...].reshape(D, -1).astype(BF16) for r in kt_refs], axis=1)
        vt = jnp.concatenate([r[...].reshape(D, -1).astype(BF16) for r in vt_refs], axis=1)
        si = lax.broadcasted_iota(jnp.int32, (KB, KB), 0)
        ki = lax.broadcasted_iota(jnp.int32, (KB, KB), 1)
        suf = _sel_cols(lft, si > ki) + later_scr[...]
        later_scr[...] = later_scr[...] + jnp.sum(lft, axis=-1, keepdims=True)
        bias = jnp.broadcast_to(suf[:, None, :], (H, TS, KB)).reshape(R, KB)
        s = jnp.dot(qbd_scr[...], kt, preferred_element_type=F32) + cq_scr[...] + bias
        online(s, lambda pr: _nt(pr, vt))

    past_pages()

    @pl.when(p == n_steps - 1)
    def _():
        cnt = cnt_ref[...]
        bias = jnp.broadcast_to(cnt[:, None, :], (H, TS, TS)).reshape(R, TS)
        s = _nt(qbd_scr[...], kn_ref[...].astype(BF16)) + cq_scr[...] - bias
        tq = lax.broadcasted_iota(jnp.int32, (R, TS), 0) % TS
        tk = lax.broadcasted_iota(jnp.int32, (R, TS), 1)
        s = jnp.where(tk <= tq, s, -jnp.inf)
        vn = vn_ref[...].astype(BF16)
        online(s, lambda pr: jnp.dot(pr, vn, preferred_element_type=F32))
        row = lax.broadcasted_iota(jnp.int32, (R, D), 0)
        lane = lax.broadcasted_iota(jnp.int32, (R, D), 1)
        own = jnp.where(lane // Dh == row // TS, acc_scr[...] / l_scr[...], 0.0)
        o_ref[...] = jnp.sum(own.reshape(H, TS, D), axis=0)


def _fox_sample(q, k_new, v_new, cum_new, cum_new_t, cache_kt, cache_vt, cache_logf_t, page_table, *, B, TS, pps):
    M, D = q.shape
    n_pages = page_table.shape[1]
    n_steps = n_pages // pps
    _, H, Dh, PG = cache_kt.shape
    R = H * TS
    row = pl.BlockSpec((TS, D), lambda b, p, pt: (b, 0))

    def page(s, nd):
        return lambda b, p, pt: (pt[b * n_pages + n_pages - (p + 1) * pps + s],) + (0,) * nd

    slots = range(pps)
    grid_spec = pltpu.PrefetchScalarGridSpec(
        num_scalar_prefetch=1, grid=(B, n_steps),
        in_specs=[row, row, row,
                  pl.BlockSpec((TS, cum_new.shape[1]), lambda b, p, pt: (b, 0)),
                  pl.BlockSpec((None, H, TS), lambda b, p, pt: (b, 0, 0))]
                 + [pl.BlockSpec((None, H, Dh, PG), page(s, 3)) for s in slots]
                 + [pl.BlockSpec((None, H, Dh, PG), page(s, 3)) for s in slots]
                 + [pl.BlockSpec((None, H, PG), page(s, 2)) for s in slots],
        out_specs=row,
        scratch_shapes=[pltpu.VMEM((R, D), BF16), pltpu.VMEM((R, 1), F32), pltpu.VMEM((H, 1), F32),
                        pltpu.VMEM((R, 1), F32), pltpu.VMEM((R, 1), F32), pltpu.VMEM((R, D), F32)])
    return pl.pallas_call(
        functools.partial(_fox_sample_kernel, n_steps=n_steps, pps=pps),
        grid_spec=grid_spec,
        out_shape=jax.ShapeDtypeStruct((M, D), F32),
        compiler_params=_cparams(2), name="fox_sample",
    )(page_table.reshape(-1), q, k_new, v_new, cum_new, cum_new_t,
      *([cache_kt] * pps), *([cache_vt] * pps), *([cache_logf_t] * pps))


def _rwkv_in_kernel(x_ref, g_ref, sc_ref, sh_ref, st_ref, mu_ref, wrkv_ref, w0_ref, w1_ref, w2_ref, a0_ref,
                    a1_ref, a2_ref, g1_ref, g2_ref,
                    r_ref, k_ref, v_ref, lw_ref, a_ref, gg_ref, tail_ref, h_scr, d_scr, carry_scr, *, T):
    i, j = pl.program_id(0), pl.program_id(1)
    tm, D = x_ref.shape
    nt = tm // SUBLANES
    long_seq = T >= tm

    @pl.when(j == 0)
    def _():
        h = _modnorm(x_ref[...], g_ref[...], sc_ref[...], sh_ref[...], T)
        h3 = h.reshape(nt, SUBLANES, D)
        if long_seq:
            @pl.when(i % (T // tm) == 0)
            def _():
                carry_scr[...] = st_ref[...]
            prev3 = jnp.concatenate([carry_scr[...][None], h3[:nt - 1]], axis=0)
            carry_scr[...] = h3[nt - 1]
            tail_ref[...] = h3[nt - 1]
        else:
            prev3 = st_ref[...]
            tail_ref[...] = h3
        prev = _shift_rows(h3, prev3, 1).reshape(tm, D)
        h_scr[...] = h
        d_scr[...] = prev - h

    def mix(s):
        return (h_scr[...] + d_scr[...] * mu_ref[s:s + 1, :]).astype(BF16)

    for s, out in enumerate((r_ref, k_ref, v_ref)):
        @pl.when(j == s)
        def _(s=s, out=out):
            out[...] = jnp.dot(mix(s), wrkv_ref[...], preferred_element_type=F32)

    @pl.when(j == 3)
    def _():
        t = jnp.tanh(jnp.dot(mix(3), w1_ref[...], preferred_element_type=F32))
        pre_w = w0_ref[...] + jnp.dot(t.astype(BF16), w2_ref[...], preferred_element_type=F32)
        w_log = -jax.nn.softplus(-pre_w) - 0.5
        lw_ref[...] = -jnp.exp(w_log)
        u = jnp.dot(mix(4), a1_ref[...], preferred_element_type=F32)
        pre_a = a0_ref[...] + jnp.dot(u.astype(BF16), a2_ref[...], preferred_element_type=F32)
        a_ref[...] = jax.nn.sigmoid(pre_a)
        sg = jax.nn.sigmoid(jnp.dot(mix(5), g1_ref[...], preferred_element_type=F32))
        gg_ref[...] = jnp.dot(sg.astype(BF16), g2_ref[...], preferred_element_type=F32)


def _rwkv_in(x, g, sc, sh, st_tiles, mu, w_rkv, w0, w1, w2, a0, a1, a2, g1, g2, *, T, tm):
    M, D = x.shape
    B = st_tiles.shape[0]
    P = LANES
    c2 = lambda shape: pl.BlockSpec(shape, lambda i, j: (0,) * len(shape))
    row = pl.BlockSpec((tm, D), lambda i, j: (i, 0))
    return pl.pallas_call(
        functools.partial(_rwkv_in_kernel, T=T),
        grid=(M // tm, 4),
        in_specs=[row, c2((1, D)), _mod_spec(T, tm, D), _mod_spec(T, tm, D), _tile_spec(T, tm, D),
                  c2((6, D)), pl.BlockSpec((None, D, D), lambda i, j: (jnp.minimum(j, 2), 0, 0)),
                  c2((1, D)), c2((D, P)), c2((P, D)), c2((1, D)), c2((D, P)), c2((P, D)), c2((D, P)), c2((P, D))],
        out_specs=[row] * 6 + [_tile_spec(T, tm, D)],
        out_shape=[jax.ShapeDtypeStruct((M, D), F32)] * 6 + [jax.ShapeDtypeStruct((B, SUBLANES, D), F32)],
        scratch_shapes=[pltpu.VMEM((tm, D), F32), pltpu.VMEM((tm, D), F32), pltpu.VMEM((SUBLANES, D), F32)],
        compiler_params=_cparams(2),
        name="rwkv_in",
    )(x, g.reshape(1, D), sc, sh, st_tiles, mu, w_rkv, w0.reshape(1, D), w1, w2, a0.reshape(1, D), a1, a2, g1, g2)


def _wkv_kernel(r_ref, k_ref, v_ref, lw_ref, a_ref, g_ref, kk_ref, ka_ref, rk_ref, lnw_ref, lnb_ref, s0_ref,
                o_ref, sout_ref, s_scr, *, L, n_chunks):
    c = pl.program_id(1)
    H, N = RWKV_HEADS, RWKV_HEAD
    G = WKV_ROWS // L
    R = G * L

    @pl.when(c == 0)
    def _():
        s_scr[...] = s0_ref[...]

    ri = lax.broadcasted_iota(jnp.int32, (L, L), 0)
    ci = lax.broadcasted_iota(jnp.int32, (L, L), 1)
    lw = lw_ref[...]
    cs = _sel_rows(ri >= ci, lw)
    e_pos = jnp.exp(cs)
    e_neg = jnp.exp(-cs)
    e_prev = jnp.exp(cs - lw)

    row = lax.broadcasted_iota(jnp.int32, (R, R), 0)
    col = lax.broadcasted_iota(jnp.int32, (R, R), 1)
    same = (row // L) == (col // L)
    strict = same & (row > col)
    incl = same & (row >= col)

    for grp in range(H // G):
        heads = range(grp * G, (grp + 1) * G)
        A_l, R_l, B_l, K_l, V_l, kp_l = [], [], [], [], [], []
        for h in heads:
            sl = slice(h * N, (h + 1) * N)
            kh, ah = k_ref[:, sl], a_ref[:, sl]
            kkh = kh * kk_ref[:, sl]
            nrm = jnp.sqrt(jnp.sum(kkh * kkh, axis=-1, keepdims=True))
            kkh = kkh / jnp.maximum(nrm, 1e-12)
            kph = kh * (1.0 + (ah - 1.0) * ka_ref[:, sl])
            en = e_neg[:, sl]
            A_l.append(-kkh * e_prev[:, sl])
            R_l.append(r_ref[:, sl] * e_pos[:, sl])
            B_l.append(kkh * ah * en)
            K_l.append(kph * en)
            V_l.append(v_ref[:, sl])
            kp_l.append(kph)
        AR = jnp.concatenate(A_l + R_l, axis=0).astype(BF16)
        BK = jnp.concatenate(B_l + K_l, axis=0).astype(BF16)
        V4 = jnp.concatenate(V_l, axis=0)
        Q = _nt(AR, BK)
        ars_a, ars_r = [], []
        for q, h in enumerate(heads):
            Sb = s_scr[h].astype(BF16)
            ars_a.append(_nt(AR[q * L:(q + 1) * L], Sb))
            ars_r.append(_nt(AR[R + q * L:R + (q + 1) * L], Sb))
        nmat = jnp.where(strict, Q[:R, :R], 0.0)
        ak = jnp.where(strict, Q[:R, R:], 0.0)
        rhs = jnp.concatenate(ars_a, axis=0) + jnp.dot(ak.astype(BF16), V4.astype(BF16), preferred_element_type=F32)
        Y = jnp.where(((row // 2) == (col // 2)) & (row > col), nmat, 0.0)
        b = 2
        while b < L:
            Ck = jnp.where(((row // (2 * b)) == (col // (2 * b))) & ((row % (2 * b)) >= b) & ((col % (2 * b)) < b),
                           nmat, 0.0)
            Yb = Y.astype(BF16)
            Tm = Ck + jnp.dot(Yb, Ck.astype(BF16), preferred_element_type=F32)
            Y = Y + Tm + jnp.dot(Tm.astype(BF16), Yb, preferred_element_type=F32)
            b *= 2
        E = rhs + jnp.dot(Y.astype(BF16), rhs.astype(BF16), preferred_element_type=F32)
        EV = jnp.concatenate([E, V4], axis=0).astype(BF16)
        rbk = jnp.concatenate([jnp.where(incl, Q[R:, :R], 0.0), jnp.where(incl, Q[R:, R:], 0.0)], axis=1)
        y4 = jnp.concatenate(ars_r, axis=0) + jnp.dot(rbk.astype(BF16), EV, preferred_element_type=F32)
        for q, h in enumerate(heads):
            sl = slice(h * N, (h + 1) * N)
            rows = slice(q * L, (q + 1) * L)
            ev_h = jnp.concatenate([EV[rows], EV[R + q * L:R + (q + 1) * L]], axis=0)
            bk_h = jnp.concatenate([BK[rows], BK[R + q * L:R + (q + 1) * L]], axis=0)
            s_scr[h] = (s_scr[h] + _tn(ev_h, bk_h)) * e_pos[L - 1:L, sl]
            y = y4[rows]
            mean = jnp.mean(y, axis=-1, keepdims=True)
            var = jnp.mean(jnp.square(y - mean), axis=-1, keepdims=True)
            yn = (y - mean) * lax.rsqrt(var + RWKV_GN_EPS) * lnw_ref[:, sl] + lnb_ref[:, sl]
            bonus = jnp.sum(r_ref[:, sl] * kp_l[q] * rk_ref[:, sl], axis=-1, keepdims=True) * V_l[q]
            o_ref[:, sl] = (yn + bonus) * g_ref[:, sl]

    @pl.when(c == n_chunks - 1)
    def _():
        sout_ref[...] = s_scr[...]


def _wkv_core(r, k, v, lw, a, g, k_k, k_a, r_k, ln_w, ln_b, s0, *, T, L):
    M, D = r.shape
    B = s0.shape[0]
    nc = T // L
    row = pl.BlockSpec((L, D), lambda b, c: (b * nc + c, 0))
    par = pl.BlockSpec((1, D), lambda b, c: (0, 0))
    st = pl.BlockSpec((None, RWKV_HEADS, RWKV_HEAD, RWKV_HEAD), lambda b, c: (b, 0, 0, 0))
    v1 = lambda t: t.reshape(1, D)
    return pl.pallas_call(
        functools.partial(_wkv_kernel, L=L, n_chunks=nc),
        grid=(B, nc),
        in_specs=[row] * 6 + [par] * 5 + [st],
        out_specs=[row, st],
        out_shape=[jax.ShapeDtypeStruct((M, D), F32), jax.ShapeDtypeStruct(s0.shape, F32)],
        scratch_shapes=[pltpu.VMEM((RWKV_HEADS, RWKV_HEAD, RWKV_HEAD), F32)],
        compiler_params=_cparams(2),
        name="wkv_core",
    )(r, k, v, lw, a, g, v1(k_k), v1(k_a), v1(r_k), v1(ln_w), v1(ln_b), s0)


def _wkv_long_kernel(r_ref, k_ref, v_ref, lw_ref, a_ref, g_ref, kk_ref, ka_ref, rk_ref, lnw_ref, lnb_ref, s0_ref,
                     o_ref, sout_ref, s_scr, *, L, NC, n_steps):
    step = pl.program_id(1)
    H, N, G = RWKV_HEADS, RWKV_HEAD, WKV_GROUP_HEADS
    GW = G * N
    NG = H // G
    R = G * L
    TB = NC * L

    @pl.when(step == 0)
    def _():
        s_scr[...] = jnp.zeros_like(s_scr)
        for h in range(H):
            q = h % G
            s_scr[h // G, q * N:(q + 1) * N, q * N:(q + 1) * N] = s0_ref[h]

    ti = lax.broadcasted_iota(jnp.int32, (TB, TB), 0)
    tj = lax.broadcasted_iota(jnp.int32, (TB, TB), 1)
    same_c = (ti // L) == (tj // L)
    lw = lw_ref[...]
    cs = _sel_rows(same_c & (ti >= tj), lw)
    ctot = jnp.concatenate([jnp.broadcast_to(cs[(c + 1) * L - 1:(c + 1) * L], (L, cs.shape[1])) for c in range(NC)],
                           axis=0)
    e_pos, e_neg, e_prev = jnp.exp(cs), jnp.exp(-cs), jnp.exp(cs - lw)
    e_hat, e_tot = jnp.exp(ctot - cs), jnp.exp(ctot)
    gi = lax.broadcasted_iota(jnp.int32, (GW, GW), 0) // N
    gj = lax.broadcasted_iota(jnp.int32, (GW, GW), 1) // N
    gones = (gi == gj).astype(F32)

    def head_sum(x):
        return jnp.concatenate([_sel_cols(x[:, g * GW:(g + 1) * GW], gones) for g in range(NG)], axis=1)

    r, k, v, a = r_ref[...], k_ref[...], v_ref[...], a_ref[...]
    kkv = k * kk_ref[...]
    kkn = kkv / jnp.maximum(jnp.sqrt(head_sum(kkv * kkv)), 1e-12)
    kp = k * (1.0 + (a - 1.0) * ka_ref[...])
    ba = kkn * a
    nat = dict(A=-kkn * e_prev, R=r * e_pos, Bt=ba * e_neg, Kt=kp * e_neg, Bh=ba * e_hat, Kh=kp * e_hat, V=v)
    bonus = head_sum(r * kp * rk_ref[...]) * v

    row = lax.broadcasted_iota(jnp.int32, (R, GW), 0)
    lane = lax.broadcasted_iota(jnp.int32, (R, GW), 1)
    hm = (row // L) == (lane // N)

    def bd(name, c, g):
        x = nat[name][c * L:(c + 1) * L, g * GW:(g + 1) * GW]
        return jnp.where(hm, jnp.concatenate([x] * G, axis=0), 0.0)

    mr = lax.broadcasted_iota(jnp.int32, (R, R), 0)
    mc = lax.broadcasted_iota(jnp.int32, (R, R), 1)
    same_h = (mr // L) == (mc // L)
    strict = same_h & (mr > mc)
    incl = same_h & (mr >= mc)

    units = [(c, g) for c in range(NC) for g in range(NG)]
    U = {}
    for u in units:
        ops = {n: bd(n, *u) for n in nat}
        AR = jnp.concatenate([ops["A"], ops["R"]], axis=0).astype(BF16)
        BK = jnp.concatenate([ops["Bt"], ops["Kt"]], axis=0).astype(BF16)
        U[u] = dict(ops=ops, Q=_nt(AR, BK))
    for u in units:
        d = U[u]
        Q = d.pop("Q")
        d["nmat"] = jnp.where(strict, Q[:R, :R], 0.0)
        d["ak"] = jnp.where(strict, Q[:R, R:], 0.0).astype(BF16)
        d["rb"] = jnp.where(incl, Q[R:, :R], 0.0).astype(BF16)
        d["rk"] = jnp.where(incl, Q[R:, R:], 0.0).astype(BF16)
        d["Y"] = jnp.where(((mr // 2) == (mc // 2)) & (mr > mc), d["nmat"], 0.0)
    b = 2
    while b < L:
        lvl = ((mr // (2 * b)) == (mc // (2 * b))) & ((mr % (2 * b)) >= b) & ((mc % (2 * b)) < b)
        for u in units:
            d = U[u]
            Ck = jnp.where(lvl, d["nmat"], 0.0)
            d["Yb"] = d["Y"].astype(BF16)
            d["Tm"] = Ck + jnp.dot(d["Yb"], Ck.astype(BF16), preferred_element_type=F32)
        for u in units:
            d = U[u]
            d["Y"] = d["Y"] + d["Tm"] + jnp.dot(d["Tm"].astype(BF16), d["Yb"], preferred_element_type=F32)
        b *= 2
    for u in units:
        d = U[u]
        vb = d["ops"]["V"].astype(BF16)
        d["akv"] = jnp.dot(d["ak"], vb, preferred_element_type=F32)
        d["rkv"] = jnp.dot(d["rk"], vb, preferred_element_type=F32)
    for u in units:
        d = U[u]
        x = jnp.concatenate([d["ops"]["A"], d["akv"]], axis=1)
        d["AE"] = x + jnp.dot(d["Y"].astype(BF16), x.astype(BF16), preferred_element_type=F32)
    for u in units:
        d = U[u]
        base = jnp.concatenate([d["ops"]["R"], d["rkv"]], axis=1)
        ry = base + jnp.dot(d["rb"], d["AE"].astype(BF16), preferred_element_type=F32)
        d["Rbar"] = ry[:, :GW].astype(BF16)
        d["y0"] = ry[:, GW:]
    for u in units:
        d = U[u]
        bh = d["ops"]["Bh"].astype(BF16)
        kh = d["ops"]["Kh"].astype(BF16)
        d["W"] = _tn(d["AE"][:, :GW].astype(BF16), bh).astype(BF16)
        ev = jnp.concatenate([d["AE"][:, GW:], d["ops"]["V"]], axis=0).astype(BF16)
        d["Z"] = _tn(ev, jnp.concatenate([bh, kh], axis=0))

    ys = []
    for c in range(NC):
        yc = []
        for g in range(NG):
            d = U[(c, g)]
            S = s_scr[g]
            Sb = S.astype(BF16)
            ybd = d["y0"] + _nt(d["Rbar"], Sb)
            yc.append(sum(ybd[q * L:(q + 1) * L] for q in range(G)))
            decay = e_tot[c * L:c * L + 1, g * GW:(g + 1) * GW]
            s_scr[g] = S * decay + jnp.dot(Sb, d["W"], preferred_element_type=F32) + d["Z"]
        ys.append(jnp.concatenate(yc, axis=1))
    y = jnp.concatenate(ys, axis=0)

    mean = head_sum(y) * (1.0 / N)
    yc = y - mean
    var = head_sum(yc * yc) * (1.0 / N)
    yn = yc * lax.rsqrt(var + RWKV_GN_EPS) * lnw_ref[...] + lnb_ref[...]
    o_ref[...] = ((yn + bonus) * g_ref[...]).astype(o_ref.dtype)

    @pl.when(step == n_steps - 1)
    def _():
        for h in range(H):
            q = h % G
            sout_ref[h] = s_scr[h // G, q * N:(q + 1) * N, q * N:(q + 1) * N]


def _wkv_long(r, k, v, lw, a, g, k_k, k_a, r_k, ln_w, ln_b, s0, *, T, L, NC):
    M, D = r.shape
    B = s0.shape[0]
    TB = NC * L
    ns = T // TB
    row = pl.BlockSpec((TB, D), lambda b, c: (b * ns + c, 0))
    par = pl.BlockSpec((1, D), lambda b, c: (0, 0))
    st = pl.BlockSpec((None, RWKV_HEADS, RWKV_HEAD, RWKV_HEAD), lambda b, c: (b, 0, 0, 0))
    v1 = lambda t: t.reshape(1, D)
    GW = WKV_GROUP_HEADS * RWKV_HEAD
    return pl.pallas_call(
        functools.partial(_wkv_long_kernel, L=L, NC=NC, n_steps=ns),
        grid=(B, ns),
        in_specs=[row] * 6 + [par] * 5 + [st],
        out_specs=[row, st],
        out_shape=[jax.ShapeDtypeStruct((M, D), BF16), jax.ShapeDtypeStruct(s0.shape, F32)],
        scratch_shapes=[pltpu.VMEM((RWKV_HEADS // WKV_GROUP_HEADS, GW, GW), F32)],
        compiler_params=_cparams(2),
        name="wkv_long",
    )(r, k, v, lw, a, g, v1(k_k), v1(k_a), v1(r_k), v1(ln_w), v1(ln_b), s0)


def _pad_cols(w, n):
    return jnp.pad(w, ((0, 0), (0, n - w.shape[1])))


def _pad_rows(w, n):
    return jnp.pad(w, ((0, n - w.shape[0]), (0, 0)))


def _state_tiles(st):
    return jnp.pad(st, ((0, 0), (SUBLANES - st.shape[1], 0), (0, 0)))


def _ssm_from_groups(s):
    B = s.shape[0]
    s = s.reshape(B, SSD_GROUPS, SSD_STATE, SSD_HPG, SSD_HEAD_DIM)
    return s.transpose(0, 1, 3, 4, 2).reshape(B, SSD_HEADS, SSD_HEAD_DIM, SSD_STATE)


def kernel(x_prompt, x_sample, c_prompt, c_sample, state_ssm, state_ssd_conv, cache_k, cache_v, cache_logf, page_table, state_wkv, state_shift, state_ffn_conv, ada_w, ada_b, norm_mix, norm_ffn, norm_final, ssd_in_w, ssd_conv_w, ssd_conv_b, ssd_dt_bias, ssd_a_log, ssd_d, ssd_norm_w, ssd_out_w, fox_in_w, fox_f_b, fox_out_w, rwkv_mu, rwkv_w_rkv, rwkv_w0, rwkv_w1, rwkv_w2, rwkv_a0, rwkv_a1, rwkv_a2, rwkv_g1, rwkv_g2, rwkv_k_k, rwkv_k_a, rwkv_r_k, rwkv_ln_w, rwkv_ln_b, rwkv_out_w, ffn_up_w, ffn_conv_w, ffn_conv_b, ffn_down_w):
    BP, TP, D = x_prompt.shape
    BS, TS, _ = x_sample.shape
    groups = [dict(B=BP, T=TP, tm=1024, tm_s=512, ssd_l=SSD_CHUNK, wkv_l=RWKV_CHUNK, gate_tc=512),
              dict(B=BS, T=TS, tm=BS * TS, tm_s=512, ssd_l=TS, wkv_l=TS, gate_tc=TS)]
    xs = [x_prompt.reshape(BP * TP, D), x_sample.reshape(BS * TS, D)]

    n_c = BP + BS
    n_c_pad = -(-n_c // SUBLANES) * SUBLANES
    c_all = jnp.pad(jnp.concatenate([c_prompt, c_sample], axis=0), ((0, n_c_pad - n_c), (0, 0)))
    mods_all = _ada_mods(c_all, ada_w, ada_b).reshape(DEPTH, n_c_pad, 6, 1, D)
    row0 = [0, BP]

    def mods_of(i, gi):
        m = mods_all[i, row0[gi]:row0[gi] + groups[gi]["B"]]
        return [m[:, k] for k in range(6)]

    outs = {k: ([], []) for k in ("ssm", "sconv", "k", "v", "lf", "wkv", "shift", "fconv")}
    ssm_sample_all = None

    for i in range(DEPTH):
        kind, j = i % N_MIXERS, i // N_MIXERS
        for gi, gp in enumerate(groups):
            B, T, tm, tm_s = gp["B"], gp["T"], gp["tm"], gp["tm_s"]
            x = xs[gi]
            prompt = gi == 0
            sh_m, sc_m, gt_m, sh_f, sc_f, gt_f = mods_of(i, gi)
            if kind == 0:
                w_main = ssd_in_w[j][:, :SSD_D_INNER + SSD_CONV_CH].astype(BF16)
                w_dt = _pad_cols(ssd_in_w[j][:, SSD_D_INNER + SSD_CONV_CH:], LANES).astype(BF16)
                zx = _norm_mod_matmul(x, norm_mix[i], sc_m, sh_m, w_main, T=T, tm=tm, tn=1024)
                dt = _norm_mod_matmul(x, norm_mix[i], sc_m, sh_m, w_dt, T=T, tm=tm, tn=LANES)
                ssd_par = (ssd_conv_w[j], ssd_conv_b[j], ssd_dt_bias[j], ssd_a_log[j], ssd_d[j], ssd_norm_w[j])
                if prompt:
                    conv_tiles = jnp.zeros((B, SUBLANES, SSD_CONV_CH), F32)
                    h0 = jnp.zeros((1, B, SSD_GROUPS, SSD_STATE, SSD_HPG * SSD_HEAD_DIM), F32)
                    y, ctail, h_new = _ssd_core(zx, dt, conv_tiles, h0, *ssd_par, B=B, T=T, L=gp["ssd_l"],
                                                natural=False)
                    outs["ssm"][gi].append(_ssm_from_groups(h_new[0]))
                else:
                    conv_tiles = _state_tiles(state_ssd_conv[j])
                    h0 = state_ssm.reshape(-1, B, SSD_GROUPS, SSD_HPG * SSD_HEAD_DIM, SSD_STATE)
                    y, ctail, ssm_sample_all = _ssd_core(zx, dt, conv_tiles, h0, *ssd_par, B=B, T=T, L=gp["ssd_l"],
                                                         natural=True, layer=j, prev_out=ssm_sample_all)
                outs["sconv"][gi].append(ctail[:, SUBLANES - (SSD_CONV - 1):])
                x = _res_matmul(y, ssd_out_w[j].astype(BF16), x, gt_m, T=T, tm=tm)
            elif kind == 1:
                w_qkv = fox_in_w[j][:, :3 * FOX_INNER].reshape(D, 3, FOX_INNER).transpose(1, 0, 2).astype(BF16)
                w_f = _pad_cols(fox_in_w[j][:, 3 * FOX_INNER:], LANES).astype(BF16)
                f = _norm_mod_matmul(x, norm_mix[i], sc_m, sh_m, w_f, T=T, tm=tm, tn=LANES)
                f_b = jnp.pad(fox_f_b[j], (0, LANES - FOX_HEADS)).reshape(1, LANES)
                lf, cum = _fox_gate(f, f_b, B=B, T=T, tc=gp["gate_tc"])
                if prompt:
                    kt, vt, qa, ka, va = _fox_proj_aug(x, norm_mix[i], sc_m, sh_m, w_qkv, cum, B=B, T=T, tm=tm_s)
                    o = _fox_flash(qa, ka, va, B=B, T=T, tq=512, tk=512)
                    rows = lambda t: t.reshape(B, FOX_HEADS, FOX_HEAD_DIM, T).transpose(0, 3, 1, 2)
                    k_out, v_out = rows(kt), rows(vt)
                else:
                    q, k, v = _norm_mod_matmul_multi(x, norm_mix[i], sc_m, sh_m, w_qkv, T=T, tm=tm_s)
                    cum_t = jnp.swapaxes(cum[:, :FOX_HEADS].reshape(B, T, FOX_HEADS), 1, 2)
                    o = _fox_sample(q, k, v, cum, cum_t, jnp.transpose(cache_k[j], (0, 2, 3, 1)),
                                    jnp.transpose(cache_v[j], (0, 2, 3, 1)), jnp.swapaxes(cache_logf[j], 1, 2),
                                    page_table, B=B, TS=T, pps=FOX_PAGES_PER_STEP)
                    hd = lambda t: t.reshape(B, T, FOX_HEADS, FOX_HEAD_DIM)
                    k_out, v_out = hd(k), hd(v)
                outs["k"][gi].append(k_out)
                outs["v"][gi].append(v_out)
                outs["lf"][gi].append(lf[:, :FOX_HEADS].reshape(B, T, FOX_HEADS))
                x = _res_matmul(o, fox_out_w[j].astype(BF16), x, gt_m, T=T, tm=tm)
            else:
                if prompt:
                    st_tiles = jnp.zeros((B, SUBLANES, D), F32)
                    s0 = jnp.zeros((B, RWKV_HEADS, RWKV_HEAD, RWKV_HEAD), F32)
                else:
                    st_tiles = _state_tiles(state_shift[j][:, None, :])
                    s0 = state_wkv[j]
                r, k, v, lw, a, g, tails = _rwkv_in(
                    x, norm_mix[i], sc_m, sh_m, st_tiles, rwkv_mu[j], rwkv_w_rkv[j].astype(BF16), rwkv_w0[j],
                    _pad_cols(rwkv_w1[j], LANES).astype(BF16), _pad_rows(rwkv_w2[j], LANES).astype(BF16), rwkv_a0[j],
                    _pad_cols(rwkv_a1[j], LANES).astype(BF16), _pad_rows(rwkv_a2[j], LANES).astype(BF16),
                    rwkv_g1[j].astype(BF16), rwkv_g2[j].astype(BF16), T=T, tm=tm_s)
                wkv_par = (rwkv_k_k[j], rwkv_k_a[j], rwkv_r_k[j].reshape(-1), rwkv_ln_w[j], rwkv_ln_b[j], s0)
                if prompt:
                    o, s_new = _wkv_long(r, k, v, lw, a, g, *wkv_par, T=T, L=gp["wkv_l"], NC=WKV_STEP_CHUNKS)
                else:
                    o, s_new = _wkv_core(r, k, v, lw, a, g, *wkv_par, T=T, L=gp["wkv_l"])
                outs["wkv"][gi].append(s_new)
                outs["shift"][gi].append(tails[:, SUBLANES - 1])
                x = _res_matmul(o, rwkv_out_w[j].astype(BF16), x, gt_m, T=T, tm=tm)
            if prompt:
                ffn_tiles = jnp.zeros((B, SUBLANES, D_FF), F32)
            else:
                ffn_tiles = _state_tiles(state_ffn_conv[i])
            x, tails = _conv_ffn(x, norm_ffn[i], sc_f, sh_f, gt_f, ffn_tiles, ffn_up_w[i].astype(BF16),
                                 ffn_conv_w[i], ffn_conv_b[i], ffn_down_w[i].astype(BF16), T=T, tm=tm_s)
            outs["fconv"][gi].append(tails[:, SUBLANES - (FFN_CONV - 1):])
            xs[gi] = x

    y_prompt = _rmsnorm(xs[0], norm_final, tm=1024).reshape(BP, TP, D)
    y_sample = _rmsnorm(xs[1], norm_final, tm=1024).reshape(BS, TS, D)
    st = lambda name, gi: jnp.stack(outs[name][gi])
    ssm_sample = ssm_sample_all.reshape(state_ssm.shape)
    return (y_prompt, y_sample, st("ssm", 0), ssm_sample, st("sconv", 0), st("sconv", 1),
            st("k", 0), st("k", 1), st("v", 0), st("v", 1), st("lf", 0), st("lf", 1),
            st("wkv", 0), st("wkv", 1), st("shift", 0), st("shift", 1), st("fconv", 0), st("fconv", 1))
```

```python
import functools

import jax
import jax.numpy as jnp
from jax import lax
from jax.experimental import pallas as pl
from jax.experimental.pallas import tpu as pltpu

F32 = jnp.float32
BF16 = jnp.bfloat16

D_MODEL = 1024
DEPTH = 4
N_MIXERS = 3
RMS_EPS = 1e-6
SSD_D_INNER = 2048
SSD_HEAD_DIM = 64
SSD_HEADS = 32
SSD_GROUPS = 4
SSD_HPG = 8
SSD_STATE = 128
SSD_CONV = 4
SSD_CONV_CH = 3072
SSD_CHUNK = 128
FOX_HEAD_DIM = 64
FOX_HEADS = 16
FOX_INNER = 1024
RWKV_HEAD = 64
RWKV_HEADS = 16
RWKV_GN_EPS = 64e-5
RWKV_CHUNK = 32
WKV_ROWS = 128
WKV_GROUP_HEADS = 4
WKV_STEP_CHUNKS = 4
FOX_PAGES_PER_STEP = 16
FOX_BIAS_TERMS = 3
D_FF = 2816
FFN_CONV = 3

SUBLANES = 8
LANES = 128
VMEM_LIMIT = 56 * 1024 * 1024


def _cparams(n_grid):
    return pltpu.CompilerParams(dimension_semantics=("arbitrary",) * n_grid,
                                vmem_limit_bytes=VMEM_LIMIT)


def _nt(a, b):
    return lax.dot_general(a, b, (((1,), (1,)), ((), ())), preferred_element_type=F32)


def _tn(a, b):
    return lax.dot_general(a, b, (((0,), (0,)), ((), ())), preferred_element_type=F32)


def _split3(c):
    hi = c.astype(BF16)
    r1 = c - hi.astype(F32)
    mid = r1.astype(BF16)
    lo = (r1 - mid.astype(F32)).astype(BF16)
    return hi, mid, lo


def _sel_rows(sel, x):
    n = x.shape[1]
    r = jnp.dot(sel.astype(BF16), jnp.concatenate(_split3(x), axis=1), preferred_element_type=F32)
    return r[:, :n] + r[:, n:2 * n] + r[:, 2 * n:]


def _sel_cols(x, sel):
    m = x.shape[0]
    r = jnp.dot(jnp.concatenate(_split3(x), axis=0), sel.astype(BF16), preferred_element_type=F32)
    return r[:m] + r[m:2 * m] + r[2 * m:]


def _modnorm(x, g, sc, sh, T):
    xn = x * lax.rsqrt(jnp.mean(x * x, axis=-1, keepdims=True) + RMS_EPS) * g
    if sc.ndim == 2:
        return xn * (1.0 + sc) + sh
    tm, D = x.shape
    x3 = xn.reshape(tm // T, T, D)
    return (x3 * (1.0 + sc) + sh).reshape(tm, D)


def _gate_mul(y, gt, T):
    if gt.ndim == 2:
        return y * gt
    tm, D = y.shape
    return (y.reshape(tm // T, T, D) * gt).reshape(tm, D)


def _shift_rows(cur3, prev3, k):
    row = lax.broadcasted_iota(jnp.int32, cur3.shape, 1)
    return pltpu.roll(jnp.where(row < SUBLANES - k, cur3, prev3), k, 1)


def _mod_spec(T, tm, D):
    if T >= tm:
        per = T // tm
        return pl.BlockSpec((None, 1, D), lambda i, *_: (i // per, 0, 0))
    return pl.BlockSpec((tm // T, 1, D), lambda i, *_: (i, 0, 0))


def _tile_spec(T, tm, C):
    if T >= tm:
        per = T // tm
        return pl.BlockSpec((None, SUBLANES, C), lambda i, *_: (i // per, 0, 0))
    return pl.BlockSpec((tm // T, SUBLANES, C), lambda i, *_: (i, 0, 0))


def _ada_kernel(c_ref, w_ref, b_ref, o_ref):
    c = c_ref[...]
    a = (c * jax.nn.sigmoid(c)).astype(BF16)
    o_ref[...] = jnp.dot(a, w_ref[...].astype(BF16), preferred_element_type=F32) + b_ref[...]


def _ada_mods(c_all, ada_w, ada_b):
    R, D = c_all.shape
    L, _, N = ada_w.shape
    tn = 1536
    return pl.pallas_call(
        _ada_kernel,
        grid=(L, N // tn),
        in_specs=[pl.BlockSpec((R, D), lambda l, j: (0, 0)),
                  pl.BlockSpec((None, D, tn), lambda l, j: (l, 0, j)),
                  pl.BlockSpec((None, 1, tn), lambda l, j: (l, 0, j))],
        out_specs=pl.BlockSpec((None, R, tn), lambda l, j: (l, 0, j)),
        out_shape=jax.ShapeDtypeStruct((L, R, N), F32),
        compiler_params=_cparams(2),
        name="ada_mods",
    )(c_all, ada_w, ada_b.reshape(L, 1, N))


def _nmm_kernel(x_ref, g_ref, sc_ref, sh_ref, w_ref, o_ref, h_scr, *, T):
    @pl.when(pl.program_id(1) == 0)
    def _():
        h_scr[...] = _modnorm(x_ref[...], g_ref[...], sc_ref[...], sh_ref[...], T).astype(BF16)
    o_ref[...] = jnp.dot(h_scr[...], w_ref[...], preferred_element_type=F32).astype(o_ref.dtype)


def _norm_mod_matmul(x, g, sc, sh, w, *, T, tm, tn):
    M, D = x.shape
    N = w.shape[1]
    return pl.pallas_call(
        functools.partial(_nmm_kernel, T=T),
        grid=(M // tm, N // tn),
        in_specs=[pl.BlockSpec((tm, D), lambda i, j: (i, 0)),
                  pl.BlockSpec((1, D), lambda i, j: (0, 0)),
                  _mod_spec(T, tm, D), _mod_spec(T, tm, D),
                  pl.BlockSpec((D, tn), lambda i, j: (0, j))],
        out_specs=pl.BlockSpec((tm, tn), lambda i, j: (i, j)),
        out_shape=jax.ShapeDtypeStruct((M, N), F32),
        scratch_shapes=[pltpu.VMEM((tm, D), BF16)],
        compiler_params=_cparams(2),
        name="norm_mod_matmul",
    )(x, g.reshape(1, D), sc, sh, w)


def _nmm_multi_kernel(x_ref, g_ref, sc_ref, sh_ref, w_ref, *rest, T, n_out):
    outs, h_scr = rest[:n_out], rest[n_out]
    j = pl.program_id(1)

    @pl.when(j == 0)
    def _():
        h_scr[...] = _modnorm(x_ref[...], g_ref[...], sc_ref[...], sh_ref[...], T).astype(BF16)
    res = jnp.dot(h_scr[...], w_ref[...], preferred_element_type=F32)
    for o in range(n_out):
        @pl.when(j == o)
        def _(o=o):
            outs[o][...] = res


def _norm_mod_matmul_multi(x, g, sc, sh, w_stack, *, T, tm):
    M, D = x.shape
    n_out, _, N = w_stack.shape
    return pl.pallas_call(
        functools.partial(_nmm_multi_kernel, T=T, n_out=n_out),
        grid=(M // tm, n_out),
        in_specs=[pl.BlockSpec((tm, D), lambda i, j: (i, 0)),
                  pl.BlockSpec((1, D), lambda i, j: (0, 0)),
                  _mod_spec(T, tm, D), _mod_spec(T, tm, D),
                  pl.BlockSpec((None, D, N), lambda i, j: (j, 0, 0))],
        out_specs=[pl.BlockSpec((tm, N), lambda i, j: (i, 0)) for _ in range(n_out)],
        out_shape=[jax.ShapeDtypeStruct((M, N), F32) for _ in range(n_out)],
        scratch_shapes=[pltpu.VMEM((tm, D), BF16)],
        compiler_params=_cparams(2),
        name="norm_mod_matmul_multi",
    )(x, g.reshape(1, D), sc, sh, w_stack)


def _res_mm_kernel(a_ref, w_ref, x_ref, gt_ref, o_ref, *, T):
    y = jnp.dot(a_ref[...].astype(BF16), w_ref[...], preferred_element_type=F32)
    o_ref[...] = x_ref[...] + _gate_mul(y, gt_ref[...], T)


def _res_matmul(a, w, x, gt, *, T, tm):
    M, K = a.shape
    D = w.shape[1]
    return pl.pallas_call(
        functools.partial(_res_mm_kernel, T=T),
        grid=(M // tm,),
        in_specs=[pl.BlockSpec((tm, K), lambda i: (i, 0)),
                  pl.BlockSpec((K, D), lambda i: (0, 0)),
                  pl.BlockSpec((tm, D), lambda i: (i, 0)),
                  _mod_spec(T, tm, D)],
        out_specs=pl.BlockSpec((tm, D), lambda i: (i, 0)),
        out_shape=jax.ShapeDtypeStruct((M, D), F32),
        compiler_params=_cparams(1),
        name="res_matmul",
    )(a, w, x, gt)


def _ffn_kernel(x_ref, g_ref, sc_ref, sh_ref, gt_ref, st_ref, upw_ref, cw_ref, cb_ref, dnw_ref,
                o_ref, tail_ref, act_scr, carry_scr, *, T, F, fc):
    i = pl.program_id(0)
    tm, D = x_ref.shape
    nt = tm // SUBLANES
    long_seq = T >= tm
    x = x_ref[...]
    h = _modnorm(x, g_ref[...], sc_ref[...], sh_ref[...], T).astype(BF16)
    if long_seq:
        @pl.when(i % (T // tm) == 0)
        def _():
            carry_scr[...] = st_ref[...]
    for c in range(F // fc):
        lo, hi = c * fc, (c + 1) * fc
        gc = jnp.dot(h, upw_ref[:, lo:hi], preferred_element_type=F32)
        uc = jnp.dot(h, upw_ref[:, F + lo:F + hi], preferred_element_type=F32)
        g3 = gc.reshape(nt, SUBLANES, fc)
        if long_seq:
            prev3 = jnp.concatenate([carry_scr[:, lo:hi][None], g3[:nt - 1]], axis=0)
            carry_scr[:, lo:hi] = g3[nt - 1]
            tail_ref[:, lo:hi] = g3[nt - 1]
        else:
            prev3 = st_ref[:, :, lo:hi]
            tail_ref[:, :, lo:hi] = g3
        conv = (g3 * cw_ref[2:3, lo:hi] + _shift_rows(g3, prev3, 1) * cw_ref[1:2, lo:hi]
                + _shift_rows(g3, prev3, 2) * cw_ref[0:1, lo:hi] + cb_ref[:, lo:hi])
        act = conv * jax.nn.sigmoid(conv) * uc.reshape(nt, SUBLANES, fc)
        act_scr[:, lo:hi] = act.reshape(tm, fc).astype(BF16)
    y = jnp.dot(act_scr[...], dnw_ref[...], preferred_element_type=F32)
    o_ref[...] = x + _gate_mul(y, gt_ref[...], T)


def _conv_ffn(x, g, sc, sh, gt, st_tiles, up_w, conv_w, conv_b, down_w, *, T, tm):
    M, D = x.shape
    F = down_w.shape[0]
    B = st_tiles.shape[0]
    return pl.pallas_call(
        functools.partial(_ffn_kernel, T=T, F=F, fc=256),
        grid=(M // tm,),
        in_specs=[pl.BlockSpec((tm, D), lambda i: (i, 0)),
                  pl.BlockSpec((1, D), lambda i: (0, 0)),
                  _mod_spec(T, tm, D), _mod_spec(T, tm, D), _mod_spec(T, tm, D),
                  _tile_spec(T, tm, F),
                  pl.BlockSpec((D, 2 * F), lambda i: (0, 0)),
                  pl.BlockSpec((FFN_CONV, F), lambda i: (0, 0)),
                  pl.BlockSpec((1, F), lambda i: (0, 0)),
                  pl.BlockSpec((F, D), lambda i: (0, 0))],
        out_specs=[pl.BlockSpec((tm, D), lambda i: (i, 0)), _tile_spec(T, tm, F)],
        out_shape=[jax.ShapeDtypeStruct((M, D), F32), jax.ShapeDtypeStruct((B, SUBLANES, F), F32)],
        scratch_shapes=[pltpu.VMEM((tm, F), BF16), pltpu.VMEM((SUBLANES, F), F32)],
        compiler_params=_cparams(1),
        name="conv_ffn",
    )(x, g.reshape(1, D), sc, sh, gt, st_tiles, up_w, conv_w, conv_b.reshape(1, F), down_w)


def _rms_kernel(x_ref, g_ref, o_ref):
    x = x_ref[...]
    o_ref[...] = x * lax.rsqrt(jnp.mean(x * x, axis=-1, keepdims=True) + RMS_EPS) * g_ref[...]


def _rmsnorm(x, g, *, tm):
    M, D = x.shape
    return pl.pallas_call(
        _rms_kernel, grid=(M // tm,),
        in_specs=[pl.BlockSpec((tm, D), lambda i: (i, 0)), pl.BlockSpec((1, D), lambda i: (0, 0))],
        out_specs=pl.BlockSpec((tm, D), lambda i: (i, 0)),
        out_shape=jax.ShapeDtypeStruct((M, D), F32),
        compiler_params=_cparams(1), name="final_rmsnorm",
    )(x, g.reshape(1, D))


def _conv_silu(cur, prev_tile, w_ref, b_ref, lo, hi):
    L, C = cur.shape
    nt = L // SUBLANES
    c3 = cur.reshape(nt, SUBLANES, C)
    p3 = prev_tile[None] if nt == 1 else jnp.concatenate([prev_tile[None], c3[:nt - 1]], axis=0)
    out = c3 * w_ref[3:4, lo:hi] + b_ref[:, lo:hi]
    for k in range(1, SSD_CONV):
        out = out + _shift_rows(c3, p3, k) * w_ref[3 - k:4 - k, lo:hi]
    out = out * jax.nn.sigmoid(out)
    return out.reshape(L, C)


def _ssd_kernel(z0_ref, z1_ref, x0_ref, x1_ref, bc_ref, dt_ref, cst_ref, h0_ref, cw_ref, cb_ref, dtb_ref, alog_ref,
                dw_ref, nw_ref, ex_ref, *rest, L, n_chunks, natural):
    y_ref, ctail_ref, hout_ref, tail_scr, st_scr = rest[-5:]
    prev_ref = rest[0] if len(rest) == 6 else None
    c = pl.program_id(1)
    P, Nn, G, J = SSD_HEAD_DIM, SSD_STATE, SSD_GROUPS, SSD_HPG
    GW = J * P

    @pl.when(c == 0)
    def _():
        tail_scr[...] = cst_ref[...]
        st_scr[...] = h0_ref[...]

    conv = []
    for q, blk_ref in enumerate((x0_ref, x1_ref, bc_ref)):
        blk = blk_ref[...]
        lo, hi = q * 1024, (q + 1) * 1024
        conv.append(_conv_silu(blk, tail_scr[:, lo:hi], cw_ref, cb_ref, lo, hi))
        tail_scr[:, lo:hi] = blk[L - SUBLANES:, :]
        ctail_ref[:, lo:hi] = blk[L - SUBLANES:, :]
    xs = jnp.concatenate(conv[:2], axis=1)
    Bm = conv[2][:, :G * Nn]
    Cm = conv[2][:, G * Nn:]

    dt = jax.nn.softplus(dt_ref[...] + dtb_ref[...])
    a = dt * (-jnp.exp(alog_ref[...]))
    ri = lax.broadcasted_iota(jnp.int32, (L, L), 0)
    ci = lax.broadcasted_iota(jnp.int32, (L, L), 1)
    causal = ri >= ci
    acs = _sel_rows(causal, a)
    wide = _sel_cols(jnp.concatenate([dt, acs], axis=0), ex_ref[...])
    dt_w, acs_w = wide[:L], wide[L:]
    acs_t = acs.T
    last_w = acs_w[L - 1:L, :]
    X = xs * dt_w
    Xd = X * jnp.exp(last_w - acs_w)
    dstart = jnp.exp(acs_w)
    cdec = jnp.exp(last_w)
    zz = jnp.concatenate([z0_ref[...], z1_ref[...]], axis=1)
    zg = zz * jax.nn.sigmoid(zz)
    dwide = dw_ref[...]
    nw = nw_ref[...]
    Xb = X.astype(BF16)
    for g in range(G):
        gl = slice(g * GW, (g + 1) * GW)
        Bg = Bm[:, g * Nn:(g + 1) * Nn].astype(BF16)
        Cg = Cm[:, g * Nn:(g + 1) * Nn].astype(BF16)
        CB = _nt(Cg, Bg)
        st = st_scr[g]
        if natural:
            y_off = _nt(Cg, st.astype(BF16)) * dstart[:, gl]
            dec_h = jnp.exp(acs_t[g * J:(g + 1) * J, L - 1:L])
            dcol = jnp.broadcast_to(dec_h[:, None, :], (J, P, 1)).reshape(GW, 1)
            st_scr[g] = st * dcol + _tn(Xd[:, gl].astype(BF16), Bg)
        else:
            y_off = jnp.dot(Cg, st.astype(BF16), preferred_element_type=F32) * dstart[:, gl]
            st_scr[g] = st * cdec[:, gl] + _tn(Bg, Xd[:, gl].astype(BF16))
        ys = []
        for j in range(J):
            h = g * J + j
            diff = acs[:, h:h + 1] - acs_t[h:h + 1, :]
            dec = jnp.exp(jnp.where(causal, diff, -jnp.inf))
            Mh = (CB * dec).astype(BF16)
            ys.append(jnp.dot(Mh, Xb[:, h * P:(h + 1) * P], preferred_element_type=F32))
        yg = jnp.concatenate(ys, axis=1) + y_off + dwide[:, gl] * xs[:, gl]
        yg = yg * zg[:, gl]
        yg = yg * lax.rsqrt(jnp.mean(yg * yg, axis=-1, keepdims=True) + RMS_EPS) * nw[:, gl]
        y_ref[:, gl] = yg.astype(y_ref.dtype)

    @pl.when(c == n_chunks - 1)
    def _():
        n_prev = hout_ref.shape[0] - 1
        if prev_ref is not None:
            hout_ref[0:n_prev] = prev_ref[...]
        hout_ref[n_prev] = st_scr[...]


def _ssd_core(zx, dt, conv_tiles, h0, conv_w, conv_b, dt_bias, a_log, d_skip, norm_w, *, B, T, L, natural,
              layer=0, prev_out=None):
    M = zx.shape[0]
    nc = T // L
    DI, C = SSD_D_INNER, SSD_CONV_CH
    col = lambda q: pl.BlockSpec((L, 1024), lambda b, c, q=q: (b * nc + c, q))
    par = lambda n: pl.BlockSpec((1, n), lambda b, c: (0, 0))
    pad = lambda v: jnp.pad(v.astype(F32), (0, LANES - v.shape[0])).reshape(1, LANES)
    expand = (jnp.arange(LANES)[:, None] == (jnp.arange(DI) // SSD_HEAD_DIM)[None, :]).astype(F32)
    d_wide = jnp.repeat(d_skip.astype(F32), SSD_HEAD_DIM).reshape(1, DI)
    st_shape = h0.shape[2:]
    st_spec = pl.BlockSpec((None, None) + st_shape, lambda b, c: (layer, b, 0, 0, 0))
    tile_spec = pl.BlockSpec((None, SUBLANES, C), lambda b, c: (b, 0, 0))
    in_specs = [col(0), col(1), col(2), col(3), col(4),
                pl.BlockSpec((L, LANES), lambda b, c: (b * nc + c, 0)),
                tile_spec, st_spec,
                pl.BlockSpec((SSD_CONV, C), lambda b, c: (0, 0)), par(C), par(LANES), par(LANES),
                par(DI), par(DI), pl.BlockSpec((LANES, DI), lambda b, c: (0, 0))]
    args = [zx, zx, zx, zx, zx, dt, conv_tiles, h0, conv_w, conv_b.reshape(1, C), pad(dt_bias), pad(a_log),
            d_wide, norm_w.reshape(1, DI), expand]
    slots = lambda n: pl.BlockSpec((n, None) + st_shape, lambda b, c: (0, b, 0, 0, 0))
    if prev_out is not None:
        in_specs.append(slots(layer))
        args.append(prev_out)
    return pl.pallas_call(
        functools.partial(_ssd_kernel, L=L, n_chunks=nc, natural=natural),
        grid=(B, nc),
        in_specs=in_specs,
        out_specs=[pl.BlockSpec((L, DI), lambda b, c: (b * nc + c, 0)), tile_spec, slots(layer + 1)],
        out_shape=[jax.ShapeDtypeStruct((M, DI), BF16 if L % 16 == 0 else F32),
                   jax.ShapeDtypeStruct((B, SUBLANES, C), F32),
                   jax.ShapeDtypeStruct((layer + 1, B) + st_shape, F32)],
        scratch_shapes=[pltpu.VMEM((SUBLANES, C), F32), pltpu.VMEM(st_shape, F32)],
        compiler_params=_cparams(2), name="ssd_core",
    )(*args)


def _fox_gate_kernel(f_ref, fb_ref, lf_ref, cum_ref, carry_scr):
    c = pl.program_id(1)
    tc = f_ref.shape[0]

    @pl.when(c == 0)
    def _():
        carry_scr[...] = jnp.zeros_like(carry_scr)
    lf = jax.nn.log_sigmoid(f_ref[...] + fb_ref[...])
    ri = lax.broadcasted_iota(jnp.int32, (tc, tc), 0)
    ci = lax.broadcasted_iota(jnp.int32, (tc, tc), 1)
    cum = _sel_rows(ri >= ci, lf) + carry_scr[...]
    lf_ref[...] = lf
    cum_ref[...] = cum
    carry_scr[...] = cum[tc - 1:tc, :]


def _fox_gate(f_raw, f_b_pad, *, B, T, tc):
    M, W = f_raw.shape
    nc = T // tc
    row = pl.BlockSpec((tc, W), lambda b, c: (b * nc + c, 0))
    return pl.pallas_call(
        _fox_gate_kernel, grid=(B, nc),
        in_specs=[row, pl.BlockSpec((1, W), lambda b, c: (0, 0))],
        out_specs=[row, row],
        out_shape=[jax.ShapeDtypeStruct((M, W), F32)] * 2,
        scratch_shapes=[pltpu.VMEM((1, W), F32)],
        compiler_params=_cparams(2), name="fox_gate",
    )(f_raw, f_b_pad)


def _aug_tails(c_ref, place_ref, const_ref, which):
    pieces = _split3(c_ref[...])
    out = const_ref[which:which + 1, :]
    for t, pc in enumerate(pieces):
        out = out + jnp.dot(pc, place_ref[which, t], preferred_element_type=F32)
    return out


def _aug_placement():
    nt, Dh = FOX_BIAS_TERMS, FOX_HEAD_DIM
    src = jnp.arange(LANES)[:, None]
    lane = jnp.arange(FOX_HEADS * Dh)[None, :]
    own = (lane // Dh == src) & (src < FOX_HEADS)
    col = lane % Dh
    place_q = jnp.stack([(own & (col == t)).astype(F32) for t in range(nt)])
    place_k = jnp.stack([-(own & (col == nt + t)).astype(F32) for t in range(nt)])
    const = jnp.concatenate([((col >= nt) & (col < 2 * nt)).astype(F32), (col < nt).astype(F32)], axis=0)
    return jnp.stack([place_q, place_k]).astype(BF16), const


def _fox_proj_aug_kernel(x_ref, g_ref, sc_ref, sh_ref, w_ref, wt_ref, c_ref, place_ref, const_ref,
                         kt_ref, vt_ref, qa_ref, ka_ref, va_ref, h_scr, *, T):
    j = pl.program_id(1)
    H, Dh = FOX_HEADS, FOX_HEAD_DIM
    tm = x_ref.shape[0]

    @pl.when(j == 0)
    def _():
        h_scr[...] = _modnorm(x_ref[...], g_ref[...], sc_ref[...], sh_ref[...], T).astype(BF16)
    res = jnp.dot(h_scr[...], w_ref[...], preferred_element_type=F32)

    @pl.when(j == 0)
    def _():
        tails = _aug_tails(c_ref, place_ref, const_ref, 0)
        for h in range(H):
            sl = slice(h * Dh, (h + 1) * Dh)
            qa_ref[h] = jnp.concatenate([res[:, sl] * (Dh ** -0.5), tails[:, sl]], axis=1).astype(BF16)

    @pl.when(j == 1)
    def _():
        kt_ref[...] = _nt(wt_ref[...], h_scr[...])
        tails = _aug_tails(c_ref, place_ref, const_ref, 1)
        for h in range(H):
            sl = slice(h * Dh, (h + 1) * Dh)
            ka_ref[h] = jnp.concatenate([res[:, sl], tails[:, sl]], axis=1).astype(BF16)

    @pl.when(j == 2)
    def _():
        vt_ref[...] = _nt(wt_ref[...], h_scr[...])
        ones = (lax.broadcasted_iota(jnp.int32, (tm, LANES - Dh), 1) == 0).astype(F32)
        for h in range(H):
            va_ref[h] = jnp.concatenate([res[:, h * Dh:(h + 1) * Dh], ones], axis=1).astype(BF16)


def _fox_proj_aug(x, g, sc, sh, w_stack, cum, *, B, T, tm):
    M, D = x.shape
    per = T // tm
    H = FOX_HEADS
    row = pl.BlockSpec((tm, D), lambda i, j: (i, 0))
    wsp = pl.BlockSpec((None, D, D), lambda i, j: (j, 0, 0))
    tr = pl.BlockSpec((None, D, tm), lambda i, j: (i // per, 0, i % per))
    aug = pl.BlockSpec((None, H, tm, LANES), lambda i, j: (i // per, 0, i % per, 0))
    place, const = _aug_placement()
    return pl.pallas_call(
        functools.partial(_fox_proj_aug_kernel, T=T),
        grid=(M // tm, 3),
        in_specs=[row, pl.BlockSpec((1, D), lambda i, j: (0, 0)), _mod_spec(T, tm, D), _mod_spec(T, tm, D),
                  wsp, wsp, pl.BlockSpec((tm, cum.shape[1]), lambda i, j: (i, 0)),
                  pl.BlockSpec(place.shape, lambda i, j: (0, 0, 0, 0)), pl.BlockSpec(const.shape, lambda i, j: (0, 0))],
        out_specs=[tr, tr, aug, aug, aug],
        out_shape=[jax.ShapeDtypeStruct((B, D, T), F32)] * 2 + [jax.ShapeDtypeStruct((B, H, T, LANES), BF16)] * 3,
        scratch_shapes=[pltpu.VMEM((tm, D), BF16)],
        compiler_params=_cparams(2), name="fox_proj_aug",
    )(x, g.reshape(1, D), sc, sh, w_stack, jnp.swapaxes(w_stack, 1, 2), cum, place, const)


def _fox_flash_kernel(qa_ref, ka_ref, va_ref, o_ref, m_scr, acc_scr, *, tq, tk, nk):
    qi, ki = pl.program_id(1), pl.program_id(2)
    H, Dh = FOX_HEADS, FOX_HEAD_DIM

    @pl.when(ki == 0)
    def _():
        m_scr[...] = jnp.full_like(m_scr, -jnp.inf)
        acc_scr[...] = jnp.zeros_like(acc_scr)

    def compute(masked):
        if masked:
            rowp = qi * tq + lax.broadcasted_iota(jnp.int32, (tq, tk), 0)
            colp = ki * tk + lax.broadcasted_iota(jnp.int32, (tq, tk), 1)
            keep = colp <= rowp
        for h in range(H):
            s = _nt(qa_ref[h], ka_ref[h])
            if masked:
                s = jnp.where(keep, s, -jnp.inf)
            m_prev = m_scr[h]
            m_new = jnp.maximum(m_prev, jnp.max(s, axis=-1, keepdims=True))
            p = jnp.exp(s - m_new).astype(BF16)
            acc_scr[h] = jnp.exp(m_prev - m_new) * acc_scr[h] + jnp.dot(p, va_ref[h], preferred_element_type=F32)
            m_scr[h] = m_new

    needed = ki * tk <= qi * tq + (tq - 1)
    diag = ki * tk + (tk - 1) > qi * tq

    @pl.when(needed & diag)
    def _():
        compute(True)

    @pl.when(needed & jnp.logical_not(diag))
    def _():
        compute(False)

    @pl.when(ki == nk - 1)
    def _():
        for h in range(H):
            a = acc_scr[h]
            o_ref[:, h * Dh:(h + 1) * Dh] = (a[:, :Dh] / a[:, Dh:Dh + 1]).astype(o_ref.dtype)


def _fox_flash(qa, ka, va, *, B, T, tq, tk):
    H, Dh = FOX_HEADS, FOX_HEAD_DIM
    D = H * Dh
    nq, nk = T // tq, T // tk
    last_k = lambda qi: (qi * tq + tq - 1) // tk
    kblk = lambda b, qi, ki: (b, 0, jnp.minimum(ki, last_k(qi)), 0)
    return pl.pallas_call(
        functools.partial(_fox_flash_kernel, tq=tq, tk=tk, nk=nk),
        grid=(B, nq, nk),
        in_specs=[pl.BlockSpec((None, H, tq, LANES), lambda b, qi, ki: (b, 0, qi, 0)),
                  pl.BlockSpec((None, H, tk, LANES), kblk), pl.BlockSpec((None, H, tk, LANES), kblk)],
        out_specs=pl.BlockSpec((tq, D), lambda b, qi, ki: (b * nq + qi, 0)),
        out_shape=jax.ShapeDtypeStruct((B * T, D), BF16),
        scratch_shapes=[pltpu.VMEM((H, tq, 1), F32), pltpu.VMEM((H, tq, LANES), F32)],
        compiler_params=_cparams(3), name="fox_flash",
    )(qa, ka, va)


def _fox_sample_kernel(pt_ref, q_ref, kn_ref, vn_ref, cn_ref, cnt_ref, *rest, n_steps, pps):
    kt_refs, vt_refs, lft_refs = rest[:pps], rest[pps:2 * pps], rest[2 * pps:3 * pps]
    o_ref, qbd_scr, cq_scr, later_scr, m_scr, l_scr, acc_scr = rest[3 * pps:]
    p = pl.program_id(1)
    H, Dh = FOX_HEADS, FOX_HEAD_DIM
    TS = q_ref.shape[0]
    R = H * TS
    D = H * Dh
    KB = pps * kt_refs[0].shape[-1]

    @pl.when(p == 0)
    def _():
        row = lax.broadcasted_iota(jnp.int32, (R, D), 0)
        lane = lax.broadcasted_iota(jnp.int32, (R, D), 1)
        q_rep = jnp.broadcast_to(q_ref[...][None], (H, TS, D)).reshape(R, D) * (Dh ** -0.5)
        qbd_scr[...] = jnp.where(lane // Dh == row // TS, q_rep, 0.0).astype(BF16)
        cw = cn_ref.shape[1]
        c_rep = jnp.broadcast_to(cn_ref[...][None], (H, TS, cw)).reshape(R, cw)
        r2 = lax.broadcasted_iota(jnp.int32, (R, cw), 0)
        l2 = lax.broadcasted_iota(jnp.int32, (R, cw), 1)
        cq_scr[...] = jnp.sum(jnp.where(l2 == r2 // TS, c_rep, 0.0), axis=-1, keepdims=True)
        later_scr[...] = jnp.zeros_like(later_scr)
        m_scr[...] = jnp.full_like(m_scr, -jnp.inf)
        l_scr[...] = jnp.zeros_like(l_scr)
        acc_scr[...] = jnp.zeros_like(acc_scr)

    def online(s, pv):
        m_prev = m_scr[...]
        m_new = jnp.maximum(m_prev, jnp.max(s, axis=-1, keepdims=True))
        pr = jnp.exp(s - m_new)
        alpha = jnp.exp(m_prev - m_new)
        l_scr[...] = alpha * l_scr[...] + jnp.sum(pr, axis=-1, keepdims=True)
        acc_scr[...] = alpha * acc_scr[...] + pv(pr.astype(BF16))
        m_scr[...] = m_new

    def past_pages():
        lft = jnp.concatenate([r[...] for r in lft_refs], axis=1)
        kt = jnp.concatenate([r[...].reshape(D, -1).astype(BF16) for r in kt_refs], axis=1)
        vt = jnp.concatenate([r[...].reshape(D, -1).astype(BF16) for r in vt_refs], axis=1)
        si = lax.broadcasted_iota(jnp.int32, (KB, KB), 0)
        ki = lax.broadcasted_iota(jnp.int32, (KB, KB), 1)
        suf = _sel_cols(lft, si > ki) + later_scr[...]
        later_scr[...] = later_scr[...] + jnp.sum(lft, axis=-1, keepdims=True)
        bias = jnp.broadcast_to(suf[:, None, :], (H, TS, KB)).reshape(R, KB)
        s = jnp.dot(qbd_scr[...], kt, preferred_element_type=F32) + cq_scr[...] + bias
        online(s, lambda pr: _nt(pr, vt))

    past_pages()

    @pl.when(p == n_steps - 1)
    def _():
        cnt = cnt_ref[...]
        bias = jnp.broadcast_to(cnt[:, None, :], (H, TS, TS)).reshape(R, TS)
        s = _nt(qbd_scr[...], kn_ref[...].astype(BF16)) + cq_scr[...] - bias
        tq = lax.broadcasted_iota(jnp.int32, (R, TS), 0) % TS
        tk = lax.broadcasted_iota(jnp.int32, (R, TS), 1)
        s = jnp.where(tk <= tq, s, -jnp.inf)
        vn = vn_ref[...].astype(BF16)
        online(s, lambda pr: jnp.dot(pr, vn, preferred_element_type=F32))
        row = lax.broadcasted_iota(jnp.int32, (R, D), 0)
        lane = lax.broadcasted_iota(jnp.int32, (R, D), 1)
        own = jnp.where(lane // Dh == row // TS, acc_scr[...] / l_scr[...], 0.0)
        o_ref[...] = jnp.sum(own.reshape(H, TS, D), axis=0)


def _fox_sample(q, k_new, v_new, cum_new, cum_new_t, cache_kt, cache_vt, cache_logf_t, page_table, *, B, TS, pps):
    M, D = q.shape
    n_pages = page_table.shape[1]
    n_steps = n_pages // pps
    _, H, Dh, PG = cache_kt.shape
    R = H * TS
    row = pl.BlockSpec((TS, D), lambda b, p, pt: (b, 0))

    def page(s, nd):
        return lambda b, p, pt: (pt[b * n_pages + n_pages - (p + 1) * pps + s],) + (0,) * nd

    slots = range(pps)
    grid_spec = pltpu.PrefetchScalarGridSpec(
        num_scalar_prefetch=1, grid=(B, n_steps),
        in_specs=[row, row, row,
                  pl.BlockSpec((TS, cum_new.shape[1]), lambda b, p, pt: (b, 0)),
                  pl.BlockSpec((None, H, TS), lambda b, p, pt: (b, 0, 0))]
                 + [pl.BlockSpec((None, H, Dh, PG), page(s, 3)) for s in slots]
                 + [pl.BlockSpec((None, H, Dh, PG), page(s, 3)) for s in slots]
                 + [pl.BlockSpec((None, H, PG), page(s, 2)) for s in slots],
        out_specs=row,
        scratch_shapes=[pltpu.VMEM((R, D), BF16), pltpu.VMEM((R, 1), F32), pltpu.VMEM((H, 1), F32),
                        pltpu.VMEM((R, 1), F32), pltpu.VMEM((R, 1), F32), pltpu.VMEM((R, D), F32)])
    return pl.pallas_call(
        functools.partial(_fox_sample_kernel, n_steps=n_steps, pps=pps),
        grid_spec=grid_spec,
        out_shape=jax.ShapeDtypeStruct((M, D), F32),
        compiler_params=_cparams(2), name="fox_sample",
    )(page_table.reshape(-1), q, k_new, v_new, cum_new, cum_new_t,
      *([cache_kt] * pps), *([cache_vt] * pps), *([cache_logf_t] * pps))


def _rwkv_in_kernel(x_ref, g_ref, sc_ref, sh_ref, st_ref, mu_ref, wrkv_ref, w0_ref, w1_ref, w2_ref, a0_ref,
                    a1_ref, a2_ref, g1_ref, g2_ref,
                    r_ref, k_ref, v_ref, lw_ref, a_ref, gg_ref, tail_ref, h_scr, d_scr, carry_scr, *, T):
    i, j = pl.program_id(0), pl.program_id(1)
    tm, D = x_ref.shape
    nt = tm // SUBLANES
    long_seq = T >= tm

    @pl.when(j == 0)
    def _():
        h = _modnorm(x_ref[...], g_ref[...], sc_ref[...], sh_ref[...], T)
        h3 = h.reshape(nt, SUBLANES, D)
        if long_seq:
            @pl.when(i % (T // tm) == 0)
            def _():
                carry_scr[...] = st_ref[...]
            prev3 = jnp.concatenate([carry_scr[...][None], h3[:nt - 1]], axis=0)
            carry_scr[...] = h3[nt - 1]
            tail_ref[...] = h3[nt - 1]
        else:
            prev3 = st_ref[...]
            tail_ref[...] = h3
        prev = _shift_rows(h3, prev3, 1).reshape(tm, D)
        h_scr[...] = h
        d_scr[...] = prev - h

    def mix(s):
        return (h_scr[...] + d_scr[...] * mu_ref[s:s + 1, :]).astype(BF16)

    for s, out in enumerate((r_ref, k_ref, v_ref)):
        @pl.when(j == s)
        def _(s=s, out=out):
            out[...] = jnp.dot(mix(s), wrkv_ref[...], preferred_element_type=F32)

    @pl.when(j == 3)
    def _():
        t = jnp.tanh(jnp.dot(mix(3), w1_ref[...], preferred_element_type=F32))
        pre_w = w0_ref[...] + jnp.dot(t.astype(BF16), w2_ref[...], preferred_element_type=F32)
        w_log = -jax.nn.softplus(-pre_w) - 0.5
        lw_ref[...] = -jnp.exp(w_log)
        u = jnp.dot(mix(4), a1_ref[...], preferred_element_type=F32)
        pre_a = a0_ref[...] + jnp.dot(u.astype(BF16), a2_ref[...], preferred_element_type=F32)
        a_ref[...] = jax.nn.sigmoid(pre_a)
        sg = jax.nn.sigmoid(jnp.dot(mix(5), g1_ref[...], preferred_element_type=F32))
        gg_ref[...] = jnp.dot(sg.astype(BF16), g2_ref[...], preferred_element_type=F32)


def _rwkv_in(x, g, sc, sh, st_tiles, mu, w_rkv, w0, w1, w2, a0, a1, a2, g1, g2, *, T, tm):
    M, D = x.shape
    B = st_tiles.shape[0]
    P = LANES
    c2 = lambda shape: pl.BlockSpec(shape, lambda i, j: (0,) * len(shape))
    row = pl.BlockSpec((tm, D), lambda i, j: (i, 0))
    return pl.pallas_call(
        functools.partial(_rwkv_in_kernel, T=T),
        grid=(M // tm, 4),
        in_specs=[row, c2((1, D)), _mod_spec(T, tm, D), _mod_spec(T, tm, D), _tile_spec(T, tm, D),
                  c2((6, D)), pl.BlockSpec((None, D, D), lambda i, j: (jnp.minimum(j, 2), 0, 0)),
                  c2((1, D)), c2((D, P)), c2((P, D)), c2((1, D)), c2((D, P)), c2((P, D)), c2((D, P)), c2((P, D))],
        out_specs=[row] * 6 + [_tile_spec(T, tm, D)],
        out_shape=[jax.ShapeDtypeStruct((M, D), F32)] * 6 + [jax.ShapeDtypeStruct((B, SUBLANES, D), F32)],
        scratch_shapes=[pltpu.VMEM((tm, D), F32), pltpu.VMEM((tm, D), F32), pltpu.VMEM((SUBLANES, D), F32)],
        compiler_params=_cparams(2),
        name="rwkv_in",
    )(x, g.reshape(1, D), sc, sh, st_tiles, mu, w_rkv, w0.reshape(1, D), w1, w2, a0.reshape(1, D), a1, a2, g1, g2)


def _wkv_kernel(r_ref, k_ref, v_ref, lw_ref, a_ref, g_ref, kk_ref, ka_ref, rk_ref, lnw_ref, lnb_ref, s0_ref,
                o_ref, sout_ref, s_scr, *, L, n_chunks):
    c = pl.program_id(1)
    H, N = RWKV_HEADS, RWKV_HEAD
    G = WKV_ROWS // L
    R = G * L

    @pl.when(c == 0)
    def _():
        s_scr[...] = s0_ref[...]

    ri = lax.broadcasted_iota(jnp.int32, (L, L), 0)
    ci = lax.broadcasted_iota(jnp.int32, (L, L), 1)
    lw = lw_ref[...]
    cs = _sel_rows(ri >= ci, lw)
    e_pos = jnp.exp(cs)
    e_neg = jnp.exp(-cs)
    e_prev = jnp.exp(cs - lw)

    row = lax.broadcasted_iota(jnp.int32, (R, R), 0)
    col = lax.broadcasted_iota(jnp.int32, (R, R), 1)
    same = (row // L) == (col // L)
    strict = same & (row > col)
    incl = same & (row >= col)

    for grp in range(H // G):
        heads = range(grp * G, (grp + 1) * G)
        A_l, R_l, B_l, K_l, V_l, kp_l = [], [], [], [], [], []
        for h in heads:
            sl = slice(h * N, (h + 1) * N)
            kh, ah = k_ref[:, sl], a_ref[:, sl]
            kkh = kh * kk_ref[:, sl]
            nrm = jnp.sqrt(jnp.sum(kkh * kkh, axis=-1, keepdims=True))
            kkh = kkh / jnp.maximum(nrm, 1e-12)
            kph = kh * (1.0 + (ah - 1.0) * ka_ref[:, sl])
            en = e_neg[:, sl]
            A_l.append(-kkh * e_prev[:, sl])
            R_l.append(r_ref[:, sl] * e_pos[:, sl])
            B_l.append(kkh * ah * en)
            K_l.append(kph * en)
            V_l.append(v_ref[:, sl])
            kp_l.append(kph)
        AR = jnp.concatenate(A_l + R_l, axis=0).astype(BF16)
        BK = jnp.concatenate(B_l + K_l, axis=0).astype(BF16)
        V4 = jnp.concatenate(V_l, axis=0)
        Q = _nt(AR, BK)
        ars_a, ars_r = [], []
        for q, h in enumerate(heads):
            Sb = s_scr[h].astype(BF16)
            ars_a.append(_nt(AR[q * L:(q + 1) * L], Sb))
            ars_r.append(_nt(AR[R + q * L:R + (q + 1) * L], Sb))
        nmat = jnp.where(strict, Q[:R, :R], 0.0)
        ak = jnp.where(strict, Q[:R, R:], 0.0)
        rhs = jnp.concatenate(ars_a, axis=0) + jnp.dot(ak.astype(BF16), V4.astype(BF16), preferred_element_type=F32)
        Y = jnp.where(((row // 2) == (col // 2)) & (row > col), nmat, 0.0)
        b = 2
        while b < L:
            Ck = jnp.where(((row // (2 * b)) == (col // (2 * b))) & ((row % (2 * b)) >= b) & ((col % (2 * b)) < b),
                           nmat, 0.0)
            Yb = Y.astype(BF16)
            Tm = Ck + jnp.dot(Yb, Ck.astype(BF16), preferred_element_type=F32)
            Y = Y + Tm + jnp.dot(Tm.astype(BF16), Yb, preferred_element_type=F32)
            b *= 2
        E = rhs + jnp.dot(Y.astype(BF16), rhs.astype(BF16), preferred_element_type=F32)
        EV = jnp.concatenate([E, V4], axis=0).astype(BF16)
        rbk = jnp.concatenate([jnp.where(incl, Q[R:, :R], 0.0), jnp.where(incl, Q[R:, R:], 0.0)], axis=1)
        y4 = jnp.concatenate(ars_r, axis=0) + jnp.dot(rbk.astype(BF16), EV, preferred_element_type=F32)
        for q, h in enumerate(heads):
            sl = slice(h * N, (h + 1) * N)
            rows = slice(q * L, (q + 1) * L)
            ev_h = jnp.concatenate([EV[rows], EV[R + q * L:R + (q + 1) * L]], axis=0)
            bk_h = jnp.concatenate([BK[rows], BK[R + q * L:R + (q + 1) * L]], axis=0)
            s_scr[h] = (s_scr[h] + _tn(ev_h, bk_h)) * e_pos[L - 1:L, sl]
            y = y4[rows]
            mean = jnp.mean(y, axis=-1, keepdims=True)
            var = jnp.mean(jnp.square(y - mean), axis=-1, keepdims=True)
            yn = (y - mean) * lax.rsqrt(var + RWKV_GN_EPS) * lnw_ref[:, sl] + lnb_ref[:, sl]
            bonus = jnp.sum(r_ref[:, sl] * kp_l[q] * rk_ref[:, sl], axis=-1, keepdims=True) * V_l[q]
            o_ref[:, sl] = (yn + bonus) * g_ref[:, sl]

    @pl.when(c == n_chunks - 1)
    def _():
        sout_ref[...] = s_scr[...]


def _wkv_core(r, k, v, lw, a, g, k_k, k_a, r_k, ln_w, ln_b, s0, *, T, L):
    M, D = r.shape
    B = s0.shape[0]
    nc = T // L
    row = pl.BlockSpec((L, D), lambda b, c: (b * nc + c, 0))
    par = pl.BlockSpec((1, D), lambda b, c: (0, 0))
    st = pl.BlockSpec((None, RWKV_HEADS, RWKV_HEAD, RWKV_HEAD), lambda b, c: (b, 0, 0, 0))
    v1 = lambda t: t.reshape(1, D)
    return pl.pallas_call(
        functools.partial(_wkv_kernel, L=L, n_chunks=nc),
        grid=(B, nc),
        in_specs=[row] * 6 + [par] * 5 + [st],
        out_specs=[row, st],
        out_shape=[jax.ShapeDtypeStruct((M, D), F32), jax.ShapeDtypeStruct(s0.shape, F32)],
        scratch_shapes=[pltpu.VMEM((RWKV_HEADS, RWKV_HEAD, RWKV_HEAD), F32)],
        compiler_params=_cparams(2),
        name="wkv_core",
    )(r, k, v, lw, a, g, v1(k_k), v1(k_a), v1(r_k), v1(ln_w), v1(ln_b), s0)


def _wkv_long_kernel(r_ref, k_ref, v_ref, lw_ref, a_ref, g_ref, kk_ref, ka_ref, rk_ref, lnw_ref, lnb_ref, s0_ref,
                     o_ref, sout_ref, s_scr, *, L, NC, n_steps):
    step = pl.program_id(1)
    H, N, G = RWKV_HEADS, RWKV_HEAD, WKV_GROUP_HEADS
    GW = G * N
    NG = H // G
    R = G * L
    TB = NC * L

    @pl.when(step == 0)
    def _():
        s_scr[...] = jnp.zeros_like(s_scr)
        for h in range(H):
            q = h % G
            s_scr[h // G, q * N:(q + 1) * N, q * N:(q + 1) * N] = s0_ref[h]

    ti = lax.broadcasted_iota(jnp.int32, (TB, TB), 0)
    tj = lax.broadcasted_iota(jnp.int32, (TB, TB), 1)
    same_c = (ti // L) == (tj // L)
    lw = lw_ref[...]
    cs = _sel_rows(same_c & (ti >= tj), lw)
    ctot = jnp.concatenate([jnp.broadcast_to(cs[(c + 1) * L - 1:(c + 1) * L], (L, cs.shape[1])) for c in range(NC)],
                           axis=0)
    e_pos, e_neg, e_prev = jnp.exp(cs), jnp.exp(-cs), jnp.exp(cs - lw)
    e_hat, e_tot = jnp.exp(ctot - cs), jnp.exp(ctot)
    gi = lax.broadcasted_iota(jnp.int32, (GW, GW), 0) // N
    gj = lax.broadcasted_iota(jnp.int32, (GW, GW), 1) // N
    gones = (gi == gj).astype(F32)

    def head_sum(x):
        return jnp.concatenate([_sel_cols(x[:, g * GW:(g + 1) * GW], gones) for g in range(NG)], axis=1)

    r, k, v, a = r_ref[...], k_ref[...], v_ref[...], a_ref[...]
    kkv = k * kk_ref[...]
    kkn = kkv / jnp.maximum(jnp.sqrt(head_sum(kkv * kkv)), 1e-12)
    kp = k * (1.0 + (a - 1.0) * ka_ref[...])
    ba = kkn * a
    nat = dict(A=-kkn * e_prev, R=r * e_pos, Bt=ba * e_neg, Kt=kp * e_neg, Bh=ba * e_hat, Kh=kp * e_hat, V=v)
    bonus = head_sum(r * kp * rk_ref[...]) * v

    row = lax.broadcasted_iota(jnp.int32, (R, GW), 0)
    lane = lax.broadcasted_iota(jnp.int32, (R, GW), 1)
    hm = (row // L) == (lane // N)

    def bd(name, c, g):
        x = nat[name][c * L:(c + 1) * L, g * GW:(g + 1) * GW]
        return jnp.where(hm, jnp.concatenate([x] * G, axis=0), 0.0)

    mr = lax.broadcasted_iota(jnp.int32, (R, R), 0)
    mc = lax.broadcasted_iota(jnp.int32, (R, R), 1)
    same_h = (mr // L) == (mc // L)
    strict = same_h & (mr > mc)
    incl = same_h & (mr >= mc)

    units = [(c, g) for c in range(NC) for g in range(NG)]
    U = {}
    for u in units:
        ops = {n: bd(n, *u) for n in nat}
        AR = jnp.concatenate([ops["A"], ops["R"]], axis=0).astype(BF16)
        BK = jnp.concatenate([ops["Bt"], ops["Kt"]], axis=0).astype(BF16)
        U[u] = dict(ops=ops, Q=_nt(AR, BK))
    for u in units:
        d = U[u]
        Q = d.pop("Q")
        d["nmat"] = jnp.where(strict, Q[:R, :R], 0.0)
        d["ak"] = jnp.where(strict, Q[:R, R:], 0.0).astype(BF16)
        d["rb"] = jnp.where(incl, Q[R:, :R], 0.0).astype(BF16)
        d["rk"] = jnp.where(incl, Q[R:, R:], 0.0).astype(BF16)
        d["Y"] = jnp.where(((mr // 2) == (mc // 2)) & (mr > mc), d["nmat"], 0.0)
    b = 2
    while b < L:
        lvl = ((mr // (2 * b)) == (mc // (2 * b))) & ((mr % (2 * b)) >= b) & ((mc % (2 * b)) < b)
        for u in units:
            d = U[u]
            Ck = jnp.where(lvl, d["nmat"], 0.0)
            d["Yb"] = d["Y"].astype(BF16)
            d["Tm"] = Ck + jnp.dot(d["Yb"], Ck.astype(BF16), preferred_element_type=F32)
        for u in units:
            d = U[u]
            d["Y"] = d["Y"] + d["Tm"] + jnp.dot(d["Tm"].astype(BF16), d["Yb"], preferred_element_type=F32)
        b *= 2
    for u in units:
        d = U[u]
        vb = d["ops"]["V"].astype(BF16)
        d["akv"] = jnp.dot(d["ak"], vb, preferred_element_type=F32)
        d["rkv"] = jnp.dot(d["rk"], vb, preferred_element_type=F32)
    for u in units:
        d = U[u]
        x = jnp.concatenate([d["ops"]["A"], d["akv"]], axis=1)
        d["AE"] = x + jnp.dot(d["Y"].astype(BF16), x.astype(BF16), preferred_element_type=F32)
    for u in units:
        d = U[u]
        base = jnp.concatenate([d["ops"]["R"], d["rkv"]], axis=1)
        ry = base + jnp.dot(d["rb"], d["AE"].astype(BF16), preferred_element_type=F32)
        d["Rbar"] = ry[:, :GW].astype(BF16)
        d["y0"] = ry[:, GW:]
    for u in units:
        d = U[u]
        bh = d["ops"]["Bh"].astype(BF16)
        kh = d["ops"]["Kh"].astype(BF16)
        d["W"] = _tn(d["AE"][:, :GW].astype(BF16), bh).astype(BF16)
        ev = jnp.concatenate([d["AE"][:, GW:], d["ops"]["V"]], axis=0).astype(BF16)
        d["Z"] = _tn(ev, jnp.concatenate([bh, kh], axis=0))

    ys = []
    for c in range(NC):
        yc = []
        for g in range(NG):
            d = U[(c, g)]
            S = s_scr[g]
            Sb = S.astype(BF16)
            ybd = d["y0"] + _nt(d["Rbar"], Sb)
            yc.append(sum(ybd[q * L:(q + 1) * L] for q in range(G)))
            decay = e_tot[c * L:c * L + 1, g * GW:(g + 1) * GW]
            s_scr[g] = S * decay + jnp.dot(Sb, d["W"], preferred_element_type=F32) + d["Z"]
        ys.append(jnp.concatenate(yc, axis=1))
    y = jnp.concatenate(ys, axis=0)

    mean = head_sum(y) * (1.0 / N)
    yc = y - mean
    var = head_sum(yc * yc) * (1.0 / N)
    yn = yc * lax.rsqrt(var + RWKV_GN_EPS) * lnw_ref[...] + lnb_ref[...]
    o_ref[...] = ((yn + bonus) * g_ref[...]).astype(o_ref.dtype)

    @pl.when(step == n_steps - 1)
    def _():
        for h in range(H):
            q = h % G
            sout_ref[h] = s_scr[h // G, q * N:(q + 1) * N, q * N:(q + 1) * N]


def _wkv_long(r, k, v, lw, a, g, k_k, k_a, r_k, ln_w, ln_b, s0, *, T, L, NC):
    M, D = r.shape
    B = s0.shape[0]
    TB = NC * L
    ns = T // TB
    row = pl.BlockSpec((TB, D), lambda b, c: (b * ns + c, 0))
    par = pl.BlockSpec((1, D), lambda b, c: (0, 0))
    st = pl.BlockSpec((None, RWKV_HEADS, RWKV_HEAD, RWKV_HEAD), lambda b, c: (b, 0, 0, 0))
    v1 = lambda t: t.reshape(1, D)
    GW = WKV_GROUP_HEADS * RWKV_HEAD
    return pl.pallas_call(
        functools.partial(_wkv_long_kernel, L=L, NC=NC, n_steps=ns),
        grid=(B, ns),
        in_specs=[row] * 6 + [par] * 5 + [st],
        out_specs=[row, st],
        out_shape=[jax.ShapeDtypeStruct((M, D), BF16), jax.ShapeDtypeStruct(s0.shape, F32)],
        scratch_shapes=[pltpu.VMEM((RWKV_HEADS // WKV_GROUP_HEADS, GW, GW), F32)],
        compiler_params=_cparams(2),
        name="wkv_long",
    )(r, k, v, lw, a, g, v1(k_k), v1(k_a), v1(r_k), v1(ln_w), v1(ln_b), s0)


def _pad_cols(w, n):
    return jnp.pad(w, ((0, 0), (0, n - w.shape[1])))


def _pad_rows(w, n):
    return jnp.pad(w, ((0, n - w.shape[0]), (0, 0)))


def _state_tiles(st):
    return jnp.pad(st, ((0, 0), (SUBLANES - st.shape[1], 0), (0, 0)))


def _ssm_from_groups(s):
    B = s.shape[0]
    s = s.reshape(B, SSD_GROUPS, SSD_STATE, SSD_HPG, SSD_HEAD_DIM)
    return s.transpose(0, 1, 3, 4, 2).reshape(B, SSD_HEADS, SSD_HEAD_DIM, SSD_STATE)


def kernel(x_prompt, x_sample, c_prompt, c_sample, state_ssm, state_ssd_conv, cache_k, cache_v, cache_logf, page_table, state_wkv, state_shift, state_ffn_conv, ada_w, ada_b, norm_mix, norm_ffn, norm_final, ssd_in_w, ssd_conv_w, ssd_conv_b, ssd_dt_bias, ssd_a_log, ssd_d, ssd_norm_w, ssd_out_w, fox_in_w, fox_f_b, fox_out_w, rwkv_mu, rwkv_w_rkv, rwkv_w0, rwkv_w1, rwkv_w2, rwkv_a0, rwkv_a1, rwkv_a2, rwkv_g1, rwkv_g2, rwkv_k_k, rwkv_k_a, rwkv_r_k, rwkv_ln_w, rwkv_ln_b, rwkv_out_w, ffn_up_w, ffn_conv_w, ffn_conv_b, ffn_down_w):
    BP, TP, D = x_prompt.shape
    BS, TS, _ = x_sample.shape
    groups = [dict(B=BP, T=TP, tm=1024, tm_s=512, ssd_l=SSD_CHUNK, wkv_l=RWKV_CHUNK, gate_tc=512),
              dict(B=BS, T=TS, tm=BS * TS, tm_s=512, ssd_l=TS, wkv_l=TS, gate_tc=TS)]
    xs = [x_prompt.reshape(BP * TP, D), x_sample.reshape(BS * TS, D)]

    n_c = BP + BS
    n_c_pad = -(-n_c // SUBLANES) * SUBLANES
    c_all = jnp.pad(jnp.concatenate([c_prompt, c_sample], axis=0), ((0, n_c_pad - n_c), (0, 0)))
    mods_all = _ada_mods(c_all, ada_w, ada_b).reshape(DEPTH, n_c_pad, 6, 1, D)
    row0 = [0, BP]

    def mods_of(i, gi):
        m = mods_all[i, row0[gi]:row0[gi] + groups[gi]["B"]]
        return [m[:, k] for k in range(6)]

    outs = {k: ([], []) for k in ("ssm", "sconv", "k", "v", "lf", "wkv", "shift", "fconv")}
    ssm_sample_all = None

    for i in range(DEPTH):
        kind, j = i % N_MIXERS, i // N_MIXERS
        for gi, gp in enumerate(groups):
            B, T, tm, tm_s = gp["B"], gp["T"], gp["tm"], gp["tm_s"]
            x = xs[gi]
            prompt = gi == 0
            sh_m, sc_m, gt_m, sh_f, sc_f, gt_f = mods_of(i, gi)
            if kind == 0:
                w_main = ssd_in_w[j][:, :SSD_D_INNER + SSD_CONV_CH].astype(BF16)
                w_dt = _pad_cols(ssd_in_w[j][:, SSD_D_INNER + SSD_CONV_CH:], LANES).astype(BF16)
                zx = _norm_mod_matmul(x, norm_mix[i], sc_m, sh_m, w_main, T=T, tm=tm, tn=1024)
                dt = _norm_mod_matmul(x, norm_mix[i], sc_m, sh_m, w_dt, T=T, tm=tm, tn=LANES)
                ssd_par = (ssd_conv_w[j], ssd_conv_b[j], ssd_dt_bias[j], ssd_a_log[j], ssd_d[j], ssd_norm_w[j])
                if prompt:
                    conv_tiles = jnp.zeros((B, SUBLANES, SSD_CONV_CH), F32)
                    h0 = jnp.zeros((1, B, SSD_GROUPS, SSD_STATE, SSD_HPG * SSD_HEAD_DIM), F32)
                    y, ctail, h_new = _ssd_core(zx, dt, conv_tiles, h0, *ssd_par, B=B, T=T, L=gp["ssd_l"],
                                                natural=False)
                    outs["ssm"][gi].append(_ssm_from_groups(h_new[0]))
                else:
                    conv_tiles = _state_tiles(state_ssd_conv[j])
                    h0 = state_ssm.reshape(-1, B, SSD_GROUPS, SSD_HPG * SSD_HEAD_DIM, SSD_STATE)
                    y, ctail, ssm_sample_all = _ssd_core(zx, dt, conv_tiles, h0, *ssd_par, B=B, T=T, L=gp["ssd_l"],
                                                         natural=True, layer=j, prev_out=ssm_sample_all)
                outs["sconv"][gi].append(ctail[:, SUBLANES - (SSD_CONV - 1):])
                x = _res_matmul(y, ssd_out_w[j].astype(BF16), x, gt_m, T=T, tm=tm)
            elif kind == 1:
                w_qkv = fox_in_w[j][:, :3 * FOX_INNER].reshape(D, 3, FOX_INNER).transpose(1, 0, 2).astype(BF16)
                w_f = _pad_cols(fox_in_w[j][:, 3 * FOX_INNER:], LANES).astype(BF16)
                f = _norm_mod_matmul(x, norm_mix[i], sc_m, sh_m, w_f, T=T, tm=tm, tn=LANES)
                f_b = jnp.pad(fox_f_b[j], (0, LANES - FOX_HEADS)).reshape(1, LANES)
                lf, cum = _fox_gate(f, f_b, B=B, T=T, tc=gp["gate_tc"])
                if prompt:
                    kt, vt, qa, ka, va = _fox_proj_aug(x, norm_mix[i], sc_m, sh_m, w_qkv, cum, B=B, T=T, tm=tm_s)
                    o = _fox_flash(qa, ka, va, B=B, T=T, tq=512, tk=512)
                    rows = lambda t: t.reshape(B, FOX_HEADS, FOX_HEAD_DIM, T).transpose(0, 3, 1, 2)
                    k_out, v_out = rows(kt), rows(vt)
                else:
                    q, k, v = _norm_mod_matmul_multi(x, norm_mix[i], sc_m, sh_m, w_qkv, T=T, tm=tm_s)
                    cum_t = jnp.swapaxes(cum[:, :FOX_HEADS].reshape(B, T, FOX_HEADS), 1, 2)
                    o = _fox_sample(q, k, v, cum, cum_t, jnp.transpose(cache_k[j], (0, 2, 3, 1)),
                                    jnp.transpose(cache_v[j], (0, 2, 3, 1)), jnp.swapaxes(cache_logf[j], 1, 2),
                                    page_table, B=B, TS=T, pps=FOX_PAGES_PER_STEP)
                    hd = lambda t: t.reshape(B, T, FOX_HEADS, FOX_HEAD_DIM)
                    k_out, v_out = hd(k), hd(v)
                outs["k"][gi].append(k_out)
                outs["v"][gi].append(v_out)
                outs["lf"][gi].append(lf[:, :FOX_HEADS].reshape(B, T, FOX_HEADS))
                x = _res_matmul(o, fox_out_w[j].astype(BF16), x, gt_m, T=T, tm=tm)
            else:
                if prompt:
                    st_tiles = jnp.zeros((B, SUBLANES, D), F32)
                    s0 = jnp.zeros((B, RWKV_HEADS, RWKV_HEAD, RWKV_HEAD), F32)
                else:
                    st_tiles = _state_tiles(state_shift[j][:, None, :])
                    s0 = state_wkv[j]
                r, k, v, lw, a, g, tails = _rwkv_in(
                    x, norm_mix[i], sc_m, sh_m, st_tiles, rwkv_mu[j], rwkv_w_rkv[j].astype(BF16), rwkv_w0[j],
                    _pad_cols(rwkv_w1[j], LANES).astype(BF16), _pad_rows(rwkv_w2[j], LANES).astype(BF16), rwkv_a0[j],
                    _pad_cols(rwkv_a1[j], LANES).astype(BF16), _pad_rows(rwkv_a2[j], LANES).astype(BF16),
                    rwkv_g1[j].astype(BF16), rwkv_g2[j].astype(BF16), T=T, tm=tm_s)
                wkv_par = (rwkv_k_k[j], rwkv_k_a[j], rwkv_r_k[j].reshape(-1), rwkv_ln_w[j], rwkv_ln_b[j], s0)
                if prompt:
                    o, s_new = _wkv_long(r, k, v, lw, a, g, *wkv_par, T=T, L=gp["wkv_l"], NC=WKV_STEP_CHUNKS)
                else:
                    o, s_new = _wkv_core(r, k, v, lw, a, g, *wkv_par, T=T, L=gp["wkv_l"])
                outs["wkv"][gi].append(s_new)
                outs["shift"][gi].append(tails[:, SUBLANES - 1])
                x = _res_matmul(o, rwkv_out_w[j].astype(BF16), x, gt_m, T=T, tm=tm)
            if prompt:
                ffn_tiles = jnp.zeros((B, SUBLANES, D_FF), F32)
            else:
                ffn_tiles = _state_tiles(state_ffn_conv[i])
            x, tails = _conv_ffn(x, norm_ffn[i], sc_f, sh_f, gt_f, ffn_tiles, ffn_up_w[i].astype(BF16),
                                 ffn_conv_w[i], ffn_conv_b[i], ffn_down_w[i].astype(BF16), T=T, tm=tm_s)
            outs["fconv"][gi].append(tails[:, SUBLANES - (FFN_CONV - 1):])
            xs[gi] = x

    y_prompt = _rmsnorm(xs[0], norm_final, tm=1024).reshape(BP, TP, D)
    y_sample = _rmsnorm(xs[1], norm_final, tm=1024).reshape(BS, TS, D)
    st = lambda name, gi: jnp.stack(outs[name][gi])
    ssm_sample = ssm_sample_all.reshape(state_ssm.shape)
    return (y_prompt, y_sample, st("ssm", 0), ssm_sample, st("sconv", 0), st("sconv", 1),
            st("k", 0), st("k", 1), st("v", 0), st("v", 1), st("lf", 0), st("lf", 1),
            st("wkv", 0), st("wkv", 1), st("shift", 0), st("shift", 1), st("fconv", 0), st("fconv", 1))
```
